```python
import jax, jax.numpy as jnp
from jax import lax
import numpy as np

D_MODEL = 1024
BATCH = 8
SEQ = 2048
DEPTH = 1
DEC_BATCH = 32
DEC_SEQ = 1
PAST_LEN = 16384
PAGE_SIZE = 128

HEAD_DIM = 64
ATTN_WIDTH = D_MODEL // 2
ATTN_HEADS = ATTN_WIDTH // HEAD_DIM
IDX_HEADS = 8
IDX_DIM = 64
TOPK_MAX = 256
Q_BLOCK = 128
POOL_WIDTH = D_MODEL - ATTN_WIDTH
POOL_WINDOWS = (2, 4, 8, 16)
POOL_GROUPS = len(POOL_WINDOWS)
POOL_CG = POOL_WIDTH // POOL_GROUPS
POOL_MAX = max(POOL_WINDOWS)
MIX_WIDTH = ATTN_WIDTH + POOL_WIDTH
IN_SPLITS = (ATTN_WIDTH, HEAD_DIM, HEAD_DIM, POOL_WIDTH, IDX_HEADS * IDX_DIM, IDX_DIM, IDX_HEADS)
IN_WIDTH = sum(IN_SPLITS)
N_MEM = 256
MEM_HEADS = 4
MEM_HEAD_DIM = D_MODEL // 8
N_EXPERTS = 32
TOP_K = 4
D_FF = D_MODEL
SWIGLU_LIMIT = 7.0
SWIGLU_ALPHA = 1.702

ROPE_THETA = 10000.0
EPS = 1e-6
NEG = -1e30

kernel_name = 'hymba_dsa_pool_moe_decode_step'


def rms_norm(x, g):
    xf = x.astype(jnp.float32)
    y = xf * lax.rsqrt(jnp.mean(xf * xf, axis=-1, keepdims=True) + EPS)
    return (y * g.astype(jnp.float32)).astype(x.dtype)


def rope(x, pos):
    half = x.shape[-1] // 2
    inv = ROPE_THETA ** (-jnp.arange(half, dtype=jnp.float32) / half)
    ang = pos.astype(jnp.float32)[:, None] * inv[None, :]
    shape = (1, x.shape[1]) + (1,) * (x.ndim - 3) + (half,)
    cos = jnp.cos(ang).reshape(shape)
    sin = jnp.sin(ang).reshape(shape)
    x1 = x[..., :half].astype(jnp.float32)
    x2 = x[..., half:].astype(jnp.float32)
    return jnp.concatenate([x1 * cos - x2 * sin, x2 * cos + x1 * sin], axis=-1).astype(x.dtype)


def project_in(xn, w_in, pos):
    Bn, Tn = xn.shape[:2]
    z = jnp.einsum('btd,dn->btn', xn, w_in)
    cuts = [int(c) for c in np.cumsum(IN_SPLITS)[:-1]]
    q, k, v, u, qi, ki, wi = jnp.split(z, cuts, axis=-1)
    q = rope(q.reshape(Bn, Tn, ATTN_HEADS, HEAD_DIM), pos)
    k = rope(k, pos)
    qi = rope(qi.reshape(Bn, Tn, IDX_HEADS, IDX_DIM), pos)
    ki = rope(ki, pos)
    return q, k, v, u, qi, ki, wi


def indexer_scores(qi, wi, ki, pos_q, pos_k):
    s = jnp.einsum('bqhd,bsd->bqhs', qi, ki).astype(jnp.float32) * (IDX_DIM ** -0.5)
    w = wi.astype(jnp.float32) * (IDX_HEADS ** -0.5)
    score = jnp.einsum('bqh,bqhs->bqs', w, jax.nn.relu(s))
    return jnp.where(pos_k[None, None, :] <= pos_q[None, :, None], score, NEG)


def attend_selected(q, ksel, vsel, valid):
    logits = jnp.einsum('bqhd,bqkd->bqhk', q, ksel).astype(jnp.float32) * (HEAD_DIM ** -0.5)
    logits = jnp.where(valid[:, :, None, :], logits, NEG)
    p = jax.nn.softmax(logits, axis=-1)
    return jnp.einsum('bqhk,bqkd->bqhd', p.astype(vsel.dtype), vsel)


def gather_rows(rows, idx):
    return jax.vmap(lambda r, i: r[i])(rows, idx)


def dsa_prompt(q, k, v, qi, ki, wi):
    Bn, Tn = q.shape[:2]
    topk = min(TOPK_MAX, Tn // 4)
    pos_k = jnp.arange(Tn, dtype=jnp.int32)

    def block(start):
        qb = lax.dynamic_slice_in_dim(q, start, Q_BLOCK, axis=1)
        qib = lax.dynamic_slice_in_dim(qi, start, Q_BLOCK, axis=1)
        wib = lax.dynamic_slice_in_dim(wi, start, Q_BLOCK, axis=1)
        pos_q = start + jnp.arange(Q_BLOCK, dtype=jnp.int32)
        score = indexer_scores(qib, wib, ki, pos_q, pos_k)
        _, sel = lax.top_k(score, topk)
        valid = sel <= pos_q[None, :, None]
        return attend_selected(qb, gather_rows(k, sel), gather_rows(v, sel), valid)

    starts = jnp.arange(Tn // Q_BLOCK, dtype=jnp.int32) * Q_BLOCK
    out = lax.map(block, starts)
    return out.transpose(1, 0, 2, 3, 4).reshape(Bn, Tn, ATTN_WIDTH)


def dsa_sample(q, k_new, v_new, qi, ki_new, wi, cache_k, cache_v, cache_idx_k, page_table):
    DB, DS = q.shape[:2]
    past = page_table.shape[1] * PAGE_SIZE
    L = past + DS
    topk = min(TOPK_MAX, L // 4)
    ki_past = cache_idx_k[page_table].reshape(DB, past, IDX_DIM)
    ki_all = jnp.concatenate([ki_past, ki_new.astype(ki_past.dtype)], axis=1)
    pos_q = past + jnp.arange(DS, dtype=jnp.int32)
    pos_k = jnp.arange(L, dtype=jnp.int32)
    score = indexer_scores(qi, wi, ki_all, pos_q, pos_k)
    _, sel = lax.top_k(score, topk)
    valid = sel <= pos_q[None, :, None]
    is_past = sel < past
    psel = jnp.minimum(sel, past - 1)
    phys = jax.vmap(lambda pt, i: pt[i])(page_table, psel // PAGE_SIZE)
    off = psel % PAGE_SIZE
    nsel = jnp.clip(sel - past, 0, DS - 1)
    ksel = jnp.where(is_past[..., None], cache_k[phys, off], gather_rows(k_new, nsel).astype(cache_k.dtype))
    vsel = jnp.where(is_past[..., None], cache_v[phys, off], gather_rows(v_new, nsel).astype(cache_v.dtype))
    return attend_selected(q, ksel, vsel, valid).reshape(DB, DS, ATTN_WIDTH)


def pool_mixer(u, prefix, pos, pool_w, pool_scale):
    Bn, Tn, C = u.shape
    P = prefix.shape[1]
    buf = jnp.concatenate([prefix, u.astype(prefix.dtype)], axis=1)
    cs = jnp.cumsum(buf.astype(jnp.float32), axis=1)
    cs = jnp.pad(cs, ((0, 0), (1, 0), (0, 0)))
    end = P + 1 + jnp.arange(Tn)
    groups = []
    for g, w in enumerate(POOL_WINDOWS):
        lo, hi = g * POOL_CG, (g + 1) * POOL_CG
        s = cs[:, end, lo:hi] - cs[:, end - w, lo:hi]
        cnt = jnp.minimum(w, pos + 1).astype(jnp.float32)
        groups.append(s / cnt[None, :, None])
    pooled = jnp.stack(groups, axis=2) - u.astype(jnp.float32).reshape(Bn, Tn, POOL_GROUPS, POOL_CG)
    y = jnp.einsum('btgc,gce->btge', pooled.astype(u.dtype), pool_w).reshape(Bn, Tn, C) * pool_scale
    return y, buf[:, -(POOL_MAX - 1):]


def mem_kv(mem, w_mk, w_mv):
    return jnp.einsum('bmd,dhe->bmhe', mem, w_mk), jnp.einsum('bmd,dhe->bmhe', mem, w_mv)


def mem_attend(xn, mk, mv, w_mq, w_mo):
    q = jnp.einsum('btd,dhe->bthe', xn, w_mq)
    logits = jnp.einsum('bthe,bmhe->bhtm', q, mk).astype(jnp.float32) * (MEM_HEAD_DIM ** -0.5)
    p = jax.nn.softmax(logits, axis=-1)
    o = jnp.einsum('bhtm,bmhe->bthe', p.astype(mv.dtype), mv)
    return jnp.einsum('bthe,hed->btd', o, w_mo)


def moe(tok, router_w, router_b, w_up, b_up, w_down, b_down):
    logits = (tok @ router_w + router_b).astype(jnp.float32)
    vals, idx = lax.top_k(logits, TOP_K)
    gates = jax.nn.softmax(vals, axis=-1)
    combine = jnp.sum(jax.nn.one_hot(idx, N_EXPERTS, dtype=jnp.float32) * gates[..., None], axis=1)

    def expert(acc, xs):
        wu, bu, wd, bd, c = xs
        h = tok @ wu + bu
        gate = jnp.minimum(h[:, :D_FF], SWIGLU_LIMIT)
        up = jnp.clip(h[:, D_FF:], -SWIGLU_LIMIT, SWIGLU_LIMIT)
        act = gate * jax.nn.sigmoid(SWIGLU_ALPHA * gate) * (up + 1.0)
        out = act @ wd + bd
        return acc + c[:, None] * out.astype(jnp.float32), None

    acc0 = jnp.zeros(tok.shape, jnp.float32)
    acc, _ = lax.scan(expert, acc0, (w_up, b_up, w_down, b_down, combine.T))
    return acc.astype(tok.dtype)


def setup_inputs(seed: int = 0) -> dict:
    key = jax.random.key(seed)
    ks = jax.random.split(key, 32)
    n_pages = PAST_LEN // PAGE_SIZE
    n_used = DEC_BATCH * n_pages
    n_pool = n_used + n_used // 4
    f32 = jnp.float32

    def nrm(k, shape, scale=1.0):
        return jax.random.normal(k, shape, f32) * scale

    def gain(k, shape):
        return 1.0 + 0.05 * jax.random.normal(k, shape, f32)

    page_table = jax.random.permutation(ks[0], n_pool)[:n_used].reshape(DEC_BATCH, n_pages).astype(jnp.int32)
    return {
        'x_prompt': nrm(ks[1], (BATCH, SEQ, D_MODEL)),
        'x_sample': nrm(ks[2], (DEC_BATCH, DEC_SEQ, D_MODEL)),
        'mem_prompt': nrm(ks[3], (BATCH, N_MEM, D_MODEL)),
        'cache_k': nrm(ks[4], (DEPTH, n_pool, PAGE_SIZE, HEAD_DIM)),
        'cache_v': nrm(ks[5], (DEPTH, n_pool, PAGE_SIZE, HEAD_DIM)),
        'cache_idx_k': nrm(ks[6], (DEPTH, n_pool, PAGE_SIZE, IDX_DIM)),
        'state_pool': nrm(ks[7], (DEPTH, DEC_BATCH, POOL_MAX - 1, POOL_WIDTH)),
        'cache_mem_k': nrm(ks[8], (DEPTH, DEC_BATCH, N_MEM, MEM_HEADS, MEM_HEAD_DIM)),
        'cache_mem_v': nrm(ks[9], (DEPTH, DEC_BATCH, N_MEM, MEM_HEADS, MEM_HEAD_DIM)),
        'page_table': page_table,
        'norm_mix': gain(ks[10], (DEPTH, D_MODEL)),
        'w_in': nrm(ks[11], (DEPTH, D_MODEL, IN_WIDTH), D_MODEL ** -0.5),
        'pool_w': nrm(ks[12], (DEPTH, POOL_GROUPS, POOL_CG, POOL_CG), POOL_CG ** -0.5),
        'pool_scale': gain(ks[13], (DEPTH, POOL_WIDTH)),
        'w_o': nrm(ks[14], (DEPTH, MIX_WIDTH, D_MODEL), MIX_WIDTH ** -0.5),
        'norm_mem': gain(ks[15], (DEPTH, D_MODEL)),
        'w_mq': nrm(ks[16], (DEPTH, D_MODEL, MEM_HEADS, MEM_HEAD_DIM), D_MODEL ** -0.5),
        'w_mk': nrm(ks[17], (DEPTH, D_MODEL, MEM_HEADS, MEM_HEAD_DIM), D_MODEL ** -0.5),
        'w_mv': nrm(ks[18], (DEPTH, D_MODEL, MEM_HEADS, MEM_HEAD_DIM), D_MODEL ** -0.5),
        'w_mo': nrm(ks[19], (DEPTH, MEM_HEADS, MEM_HEAD_DIM, D_MODEL), (MEM_HEADS * MEM_HEAD_DIM) ** -0.5),
        'norm_ffn': gain(ks[20], (DEPTH, D_MODEL)),
        'router_w': nrm(ks[21], (DEPTH, D_MODEL, N_EXPERTS), D_MODEL ** -0.5),
        'router_b': nrm(ks[22], (DEPTH, N_EXPERTS), 0.01),
        'w_up': nrm(ks[23], (DEPTH, N_EXPERTS, D_MODEL, 2 * D_FF), D_MODEL ** -0.5),
        'b_up': nrm(ks[24], (DEPTH, N_EXPERTS, 2 * D_FF), 0.01),
        'w_down': nrm(ks[25], (DEPTH, N_EXPERTS, D_FF, D_MODEL), D_FF ** -0.5),
        'b_down': nrm(ks[26], (DEPTH, N_EXPERTS, D_MODEL), 0.01),
        'norm_final': gain(ks[27], (D_MODEL,)),
    }


def reference(x_prompt, x_sample, mem_prompt, cache_k, cache_v, cache_idx_k, state_pool,
              cache_mem_k, cache_mem_v, page_table, norm_mix, w_in, pool_w, pool_scale, w_o,
              norm_mem, w_mq, w_mk, w_mv, w_mo, norm_ffn, router_w, router_b, w_up, b_up,
              w_down, b_down, norm_final):
    B, T = x_prompt.shape[:2]
    DB, DS = x_sample.shape[:2]
    past = page_table.shape[1] * PAGE_SIZE
    pos_p = jnp.arange(T, dtype=jnp.int32)
    pos_s = past + jnp.arange(DS, dtype=jnp.int32)
    h_p, h_s = x_prompt, x_sample
    k_p_l, v_p_l, ki_p_l, pool_p_l, mk_p_l, mv_p_l = [], [], [], [], [], []
    k_s_l, v_s_l, ki_s_l, pool_s_l = [], [], [], []
    for l in range(DEPTH):
        q, k, v, u, qi, ki, wi = project_in(rms_norm(h_p, norm_mix[l]), w_in[l], pos_p)
        attn = dsa_prompt(q, k, v, qi, ki, wi)
        prefix = jnp.zeros((B, POOL_MAX - 1, POOL_WIDTH), u.dtype)
        pool, pool_state = pool_mixer(u, prefix, pos_p, pool_w[l], pool_scale[l])
        h_p = h_p + jnp.einsum('btc,cd->btd', jnp.concatenate([attn, pool], axis=-1), w_o[l])
        k_p_l.append(k)
        v_p_l.append(v)
        ki_p_l.append(ki)
        pool_p_l.append(pool_state)

        q, k, v, u, qi, ki, wi = project_in(rms_norm(h_s, norm_mix[l]), w_in[l], pos_s)
        attn = dsa_sample(q, k, v, qi, ki, wi, cache_k[l], cache_v[l], cache_idx_k[l], page_table)
        pool, pool_state = pool_mixer(u, state_pool[l], pos_s, pool_w[l], pool_scale[l])
        h_s = h_s + jnp.einsum('btc,cd->btd', jnp.concatenate([attn, pool], axis=-1), w_o[l])
        k_s_l.append(k)
        v_s_l.append(v)
        ki_s_l.append(ki)
        pool_s_l.append(pool_state)

        mk, mv = mem_kv(mem_prompt, w_mk[l], w_mv[l])
        h_p = h_p + mem_attend(rms_norm(h_p, norm_mem[l]), mk, mv, w_mq[l], w_mo[l])
        h_s = h_s + mem_attend(rms_norm(h_s, norm_mem[l]), cache_mem_k[l], cache_mem_v[l], w_mq[l], w_mo[l])
        mk_p_l.append(mk)
        mv_p_l.append(mv)

        tok = jnp.concatenate([rms_norm(h_p, norm_ffn[l]).reshape(B * T, -1),
                               rms_norm(h_s, norm_ffn[l]).reshape(DB * DS, -1)], axis=0)
        ffn = moe(tok, router_w[l], router_b[l], w_up[l], b_up[l], w_down[l], b_down[l])
        h_p = h_p + ffn[:B * T].reshape(h_p.shape)
        h_s = h_s + ffn[B * T:].reshape(h_s.shape)

    y_prompt = rms_norm(h_p, norm_final)
    y_sample = rms_norm(h_s, norm_final)
    new_k_p = jnp.stack(k_p_l)
    new_v_p = jnp.stack(v_p_l)
    new_idx_k_p = jnp.stack(ki_p_l)
    new_pool_p = jnp.stack(pool_p_l)
    new_mem_k_p = jnp.stack(mk_p_l)
    new_mem_v_p = jnp.stack(mv_p_l)
    new_k_s = jnp.stack(k_s_l)
    new_v_s = jnp.stack(v_s_l)
    new_idx_k_s = jnp.stack(ki_s_l)
    new_pool_s = jnp.stack(pool_s_l)
    return (y_prompt, y_sample, new_k_p, new_v_p, new_idx_k_p, new_pool_p, new_mem_k_p, new_mem_v_p, new_k_s, new_v_s, new_idx_k_s, new_pool_s)
```

```python
import functools

import jax
import jax.numpy as jnp
from jax import lax
from jax.experimental import pallas as pl
from jax.experimental.pallas import tpu as pltpu

D_MODEL = 1024
HEAD_DIM = 64
ATTN_WIDTH = 512
ATTN_HEADS = 8
IDX_HEADS = 8
IDX_DIM = 64
TOPK_MAX = 256
Q_BLOCK = 128
PAGE_SIZE = 128
POOL_WIDTH = 512
POOL_WINDOWS = (2, 4, 8, 16)
POOL_CG = 128
POOL_MAX = 16
N_MEM = 256
MEM_HEADS = 4
MEM_HEAD_DIM = 128
MEM_WIDTH = MEM_HEADS * MEM_HEAD_DIM
N_EXPERTS = 32
TOP_K = 4
D_FF = 1024
SWIGLU_LIMIT = 7.0
SWIGLU_ALPHA = 1.702
ROPE_THETA = 10000.0
EPS = 1e-6
NEG = -1e30

LANES = 128
IN_COLS = 3 * 512 + 4 * LANES
VMEM_LIMIT = 56 * 1024 * 1024

BF16 = jnp.bfloat16
F32 = jnp.float32


def _dot(a, b):
    return jnp.dot(a, b, preferred_element_type=F32)


def _dot_nt(a, b):
    return lax.dot_general(a, b, (((1,), (1,)), ((), ())), preferred_element_type=F32)


def _dot_tn(a, b):
    return lax.dot_general(a, b, (((0,), (0,)), ((), ())), preferred_element_type=F32)


def _rms(x, g):
    return x * lax.rsqrt(jnp.mean(x * x, axis=-1, keepdims=True) + EPS) * g


def _params(*sem):
    return pltpu.CompilerParams(dimension_semantics=sem, vmem_limit_bytes=VMEM_LIMIT)


def _in_kernel(x_ref, g_ref, w_ref, cos_ref, sin_ref,
               q_ref, qi_ref, u_ref, kk_ref, kiki_ref, vv_ref, k_ref, ki_ref, v_ref, wi_ref):
    xn = _rms(x_ref[...], g_ref[...]).astype(BF16)
    z = _dot(xn, w_ref[...])
    cos = cos_ref[...]
    sin = sin_ref[...]
    lane = lax.broadcasted_iota(jnp.int32, cos.shape, 1)
    first_half = (lane % HEAD_DIM) < (HEAD_DIM // 2)

    def rope(zc):
        swapped = jnp.where(first_half,
                            pltpu.roll(zc, LANES - HEAD_DIM // 2, 1),
                            pltpu.roll(zc, HEAD_DIM // 2, 1))
        return zc * cos + swapped * sin

    for c in range(4):
        q_ref[:, c * LANES:(c + 1) * LANES] = rope(z[:, c * LANES:(c + 1) * LANES]).astype(BF16)
        qi_ref[:, c * LANES:(c + 1) * LANES] = rope(z[:, 512 + c * LANES:512 + (c + 1) * LANES]).astype(BF16)
    u_ref[...] = z[:, 1024:1536]
    kk = rope(z[:, 1536:1664])
    kiki = rope(z[:, 1664:1792])
    vv = z[:, 1792:1920]
    kk_ref[...] = kk.astype(BF16)
    kiki_ref[...] = kiki.astype(BF16)
    vv_ref[...] = vv.astype(BF16)
    k_ref[...] = kk[:, :HEAD_DIM]
    ki_ref[...] = kiki[:, :IDX_DIM]
    v_ref[...] = vv[:, :HEAD_DIM]
    wi_ref[...] = z[:, 1920:2048]


def _in_project(x, g, w_all, cos, sin, tm, table_blocks):
    n = x.shape[0]
    row = lambda i: (i, 0)
    tab = lambda i: (i % table_blocks, 0)
    fixed = lambda i: (0, 0)
    outs = [
        jax.ShapeDtypeStruct((n, 512), BF16),
        jax.ShapeDtypeStruct((n, 512), BF16),
        jax.ShapeDtypeStruct((n, 512), F32),
        jax.ShapeDtypeStruct((n, LANES), BF16),
        jax.ShapeDtypeStruct((n, LANES), BF16),
        jax.ShapeDtypeStruct((n, LANES), BF16),
        jax.ShapeDtypeStruct((n, HEAD_DIM), F32),
        jax.ShapeDtypeStruct((n, IDX_DIM), F32),
        jax.ShapeDtypeStruct((n, HEAD_DIM), F32),
        jax.ShapeDtypeStruct((n, LANES), F32),
    ]
    out_specs = [pl.BlockSpec((tm, s.shape[1]), row) for s in outs]
    return pl.pallas_call(
        _in_kernel,
        grid=(n // tm,),
        in_specs=[pl.BlockSpec((tm, D_MODEL), row),
                  pl.BlockSpec((1, D_MODEL), fixed),
                  pl.BlockSpec((D_MODEL, IN_COLS), fixed),
                  pl.BlockSpec((tm, LANES), tab),
                  pl.BlockSpec((tm, LANES), tab)],
        out_specs=out_specs,
        out_shape=outs,
        compiler_params=_params("parallel"),
        name="in_project",
    )(x, g, w_all, cos, sin)


def _sort_key(score):
    bits = lax.bitcast_convert_type(score, jnp.int32)
    return jnp.where(bits < 0, bits ^ jnp.int32(0x7FFFFFFF), bits)


def _kth_largest(count_ge, k, shape):
    int_min = jnp.int32(-2 ** 31)

    def body(it, tau):
        step = jnp.left_shift(jnp.int32(1), jnp.int32(31) - it)
        cand = tau + step
        return jnp.where(count_ge(cand) >= k, cand, tau)

    return lax.fori_loop(0, 32, body, jnp.full(shape, int_min, jnp.int32))


def _strict_upper(n):
    r = lax.broadcasted_iota(jnp.int32, (n, n), 0)
    c = lax.broadcasted_iota(jnp.int32, (n, n), 1)
    return jnp.where(r < c, 1.0, 0.0).astype(BF16)


def _dsa_prompt_kernel(q_ref, qi_ref, wi_ref, kk_ref, kiki_ref, vv_ref, o_ref, *, topk):
    i = pl.program_id(1)
    t_len = kk_ref.shape[0]
    kiki = kiki_ref[...]
    kk = kk_ref[...]
    vv = vv_ref[...]
    wi = wi_ref[...]
    lane = lax.broadcasted_iota(jnp.int32, (Q_BLOCK, LANES), 1)
    low = lane < HEAD_DIM

    def head_operand(ref, h):
        pair = ref[:, (h // 2) * LANES:(h // 2 + 1) * LANES]
        keep = low if h % 2 == 0 else jnp.logical_not(low)
        return jnp.where(keep, pair, jnp.zeros_like(pair))

    score = jnp.zeros((Q_BLOCK, t_len), F32)
    for h in range(IDX_HEADS):
        s = _dot_nt(head_operand(qi_ref, h), kiki) * (IDX_DIM ** -0.5)
        score = score + jnp.maximum(s, 0.0) * (wi[:, h:h + 1] * (IDX_HEADS ** -0.5))
    pos_q = i * Q_BLOCK + lax.broadcasted_iota(jnp.int32, (Q_BLOCK, t_len), 0)
    pos_k = lax.broadcasted_iota(jnp.int32, (Q_BLOCK, t_len), 1)
    causal = pos_k <= pos_q
    key = _sort_key(jnp.where(causal, score, NEG))

    def count_ge(cand):
        return jnp.sum(jnp.where(key >= cand, 1, 0), axis=1, keepdims=True)

    tau = _kth_largest(count_ge, topk, (Q_BLOCK, 1))
    gt = key > tau
    eq = key == tau
    need = topk - jnp.sum(jnp.where(gt, 1, 0), axis=1, keepdims=True)
    need_f = need.astype(F32)
    upper = _strict_upper(LANES)
    carry = jnp.zeros((Q_BLOCK, 1), F32)
    parts = []
    for c in range(t_len // LANES):
        eq_c = eq[:, c * LANES:(c + 1) * LANES]
        eq_b = jnp.where(eq_c, 1.0, 0.0).astype(BF16)
        rank = _dot(eq_b, upper) + carry
        parts.append(jnp.logical_and(eq_c, rank < need_f))
        carry = carry + jnp.sum(eq_b.astype(F32), axis=1, keepdims=True)
    take_eq = jnp.concatenate(parts, axis=1)
    mask = jnp.logical_and(jnp.logical_or(gt, take_eq), causal)

    for j in range(ATTN_HEADS // 2):
        outs = []
        for h in (2 * j, 2 * j + 1):
            logits = _dot_nt(head_operand(q_ref, h), kk) * (HEAD_DIM ** -0.5)
            logits = jnp.where(mask, logits, NEG)
            m = jnp.max(logits, axis=1, keepdims=True)
            p = jnp.exp(logits - m)
            p = p / jnp.sum(p, axis=1, keepdims=True)
            outs.append(_dot(p.astype(BF16), vv))
        o_ref[:, j * LANES:(j + 1) * LANES] = jnp.where(low, outs[0], outs[1]).astype(BF16)


def _dsa_prompt(q, qi, wi, kk, kiki, vv, batch, t_len):
    nb = t_len // Q_BLOCK
    topk = min(TOPK_MAX, t_len // 4)
    blk = lambda b, i: (b * nb + i, 0)
    seq = lambda b, i: (b, 0)
    return pl.pallas_call(
        functools.partial(_dsa_prompt_kernel, topk=topk),
        grid=(batch, nb),
        in_specs=[pl.BlockSpec((Q_BLOCK, 512), blk),
                  pl.BlockSpec((Q_BLOCK, 512), blk),
                  pl.BlockSpec((Q_BLOCK, LANES), blk),
                  pl.BlockSpec((t_len, LANES), seq),
                  pl.BlockSpec((t_len, LANES), seq),
                  pl.BlockSpec((t_len, LANES), seq)],
        out_specs=pl.BlockSpec((Q_BLOCK, 512), blk),
        out_shape=jax.ShapeDtypeStruct((batch * t_len, 512), BF16),
        compiler_params=_params("parallel", "parallel"),
        name="dsa_prompt",
    )(q, qi, wi, kk, kiki, vv)


def _mem_kv_kernel(m_ref, w_ref, k_ref, v_ref):
    z = _dot(m_ref[...].astype(BF16), w_ref[...])
    k_ref[...] = z[:, :MEM_WIDTH]
    v_ref[...] = z[:, MEM_WIDTH:]


def _mem_kv(mem, w_kv):
    n = mem.shape[0]
    tm = min(n, 512)
    row = lambda i: (i, 0)
    return pl.pallas_call(
        _mem_kv_kernel,
        grid=(n // tm,),
        in_specs=[pl.BlockSpec((tm, D_MODEL), row),
                  pl.BlockSpec((D_MODEL, 2 * MEM_WIDTH), lambda i: (0, 0))],
        out_specs=[pl.BlockSpec((tm, MEM_WIDTH), row), pl.BlockSpec((tm, MEM_WIDTH), row)],
        out_shape=[jax.ShapeDtypeStruct((n, MEM_WIDTH), F32)] * 2,
        compiler_params=_params("parallel"),
        name="mem_kv",
    )(mem, w_kv)


def _pool_project(window_sums, u, cnts, pw_ref, ps_ref):
    ys = []
    for g in range(len(POOL_WINDOWS)):
        sl = slice(g * POOL_CG, (g + 1) * POOL_CG)
        pooled = window_sums[g] / cnts[g] - u[:, sl]
        ys.append(_dot(pooled.astype(BF16), pw_ref[g]) * ps_ref[:, sl])
    return jnp.concatenate(ys, axis=1)


def _mid_prompt_kernel(x_ref, attn_ref, u_ref, halo_ref, mk_ref, mv_ref, pw_ref, ps_ref, wo_ref,
                       gm_ref, wmq_ref, wmo_ref, gf_ref, h2_ref, tok_ref, e_ref):
    i = pl.program_id(1)
    tm = x_ref.shape[0]
    u = u_ref[...]
    halo = halo_ref[...]
    e_ref[0:POOL_MAX, :] = jnp.where(i == 0, jnp.zeros_like(halo), halo)
    e_ref[POOL_MAX:POOL_MAX + tm, :] = u
    pos = i * tm + lax.broadcasted_iota(jnp.int32, (tm, 1), 0)
    sums, cnts = [], []
    for g, w in enumerate(POOL_WINDOWS):
        sl = slice(g * POOL_CG, (g + 1) * POOL_CG)
        s = u[:, sl]
        for j in range(1, w):
            s = s + e_ref[POOL_MAX - j:POOL_MAX - j + tm, sl]
        sums.append(s)
        cnts.append(jnp.minimum(w, pos + 1).astype(F32))
    pool = _pool_project(sums, u, cnts, pw_ref, ps_ref)
    h1 = x_ref[...] + _dot(attn_ref[...], wo_ref[0:ATTN_WIDTH, :]) \
        + _dot(pool.astype(BF16), wo_ref[ATTN_WIDTH:, :])

    qm = _dot(_rms(h1, gm_ref[...]).astype(BF16), wmq_ref[...]).astype(BF16)
    mk = mk_ref[0].astype(BF16)
    mv = mv_ref[0].astype(BF16)
    outs = []
    for h in range(MEM_HEADS):
        sl = slice(h * MEM_HEAD_DIM, (h + 1) * MEM_HEAD_DIM)
        logits = _dot_nt(qm[:, sl], mk[:, sl]) * (MEM_HEAD_DIM ** -0.5)
        m = jnp.max(logits, axis=1, keepdims=True)
        p = jnp.exp(logits - m)
        p = p / jnp.sum(p, axis=1, keepdims=True)
        outs.append(_dot(p.astype(BF16), mv[:, sl]))
    o = jnp.concatenate(outs, axis=1)
    h2 = h1 + _dot(o.astype(BF16), wmo_ref[...])
    h2_ref[...] = h2
    tok_ref[...] = _rms(h2, gf_ref[...]).astype(BF16)


def _mid_prompt(x, attn, u, mk, mv, pool_w, pool_scale, w_o, g_mem, w_mq, w_mo, g_ffn, batch, t_len, tm):
    nt = t_len // tm
    hb = tm // POOL_MAX
    row = lambda b, i: (b * nt + i, 0)
    halo = lambda b, i: (jnp.maximum((b * nt + i) * hb - 1, 0), 0)
    fixed2 = lambda b, i: (0, 0)
    fixed3 = lambda b, i: (0, 0, 0)
    perb = lambda b, i: (b, 0, 0)
    n = batch * t_len
    return pl.pallas_call(
        _mid_prompt_kernel,
        grid=(batch, nt),
        in_specs=[pl.BlockSpec((tm, D_MODEL), row),
                  pl.BlockSpec((tm, ATTN_WIDTH), row),
                  pl.BlockSpec((tm, POOL_WIDTH), row),
                  pl.BlockSpec((POOL_MAX, POOL_WIDTH), halo),
                  pl.BlockSpec((1, N_MEM, MEM_WIDTH), perb),
                  pl.BlockSpec((1, N_MEM, MEM_WIDTH), perb),
                  pl.BlockSpec((len(POOL_WINDOWS), POOL_CG, POOL_CG), fixed3),
                  pl.BlockSpec((1, POOL_WIDTH), fixed2),
                  pl.BlockSpec((D_MODEL, D_MODEL), fixed2),
                  pl.BlockSpec((1, D_MODEL), fixed2),
                  pl.BlockSpec((D_MODEL, MEM_WIDTH), fixed2),
                  pl.BlockSpec((MEM_WIDTH, D_MODEL), fixed2),
                  pl.BlockSpec((1, D_MODEL), fixed2)],
        out_specs=[pl.BlockSpec((tm, D_MODEL), row), pl.BlockSpec((tm, D_MODEL), row)],
        out_shape=[jax.ShapeDtypeStruct((n, D_MODEL), F32), jax.ShapeDtypeStruct((n, D_MODEL), BF16)],
        scratch_shapes=[pltpu.VMEM((POOL_MAX + tm, POOL_WIDTH), F32)],
        compiler_params=_params("parallel", "arbitrary"),
        name="mid_prompt",
    )(x, attn, u, u, mk, mv, pool_w, pool_scale, w_o, g_mem, w_mq, w_mo, g_ffn)


def _router_kernel(tok_ref, rw_ref, rb_ref, rank_ref, gate_ref, cnt_ref):
    tm = tok_ref.shape[0]
    logits = _dot_nt(rw_ref[...], tok_ref[...]) + rb_ref[...]
    eid = lax.broadcasted_iota(jnp.int32, (N_EXPERTS, tm), 0)
    vals, hots = [], []
    cur = logits
    for _ in range(TOP_K):
        m = jnp.max(cur, axis=0, keepdims=True)
        idx = jnp.min(jnp.where(cur == m, eid, N_EXPERTS), axis=0, keepdims=True)
        hot = eid == idx
        vals.append(m)
        hots.append(hot)
        cur = jnp.where(hot, -jnp.inf, cur)
    exps = [jnp.exp(v - vals[0]) for v in vals]
    den = exps[0] + exps[1] + exps[2] + exps[3]
    gate = jnp.zeros((N_EXPERTS, tm), F32)
    sel = jnp.zeros((N_EXPERTS, tm), jnp.bool_)
    for hot, ex in zip(hots, exps):
        gate = jnp.where(hot, ex / den, gate)
        sel = jnp.logical_or(sel, hot)
    gate_ref[0] = gate

    upper = _strict_upper(LANES)
    carry = jnp.zeros((N_EXPERTS, 1), F32)
    for c in range(tm // LANES):
        sl = slice(c * LANES, (c + 1) * LANES)
        sel_c = sel[:, sl]
        sel_b = jnp.where(sel_c, 1.0, 0.0).astype(BF16)
        rank = _dot(sel_b, upper) + carry
        rank_ref[0, :, sl] = jnp.where(sel_c, rank, -1.0).astype(jnp.int32)
        carry = carry + jnp.sum(sel_b.astype(F32), axis=1, keepdims=True)
    cnt_ref[0] = carry.astype(jnp.int32)


def _router(tok, rw_t, rb, tm):
    n = tok.shape[0]
    nt = n // tm
    t3 = lambda i: (i, 0, 0)
    return pl.pallas_call(
        _router_kernel,
        grid=(nt,),
        in_specs=[pl.BlockSpec((tm, D_MODEL), lambda i: (i, 0)),
                  pl.BlockSpec((N_EXPERTS, D_MODEL), lambda i: (0, 0)),
                  pl.BlockSpec((N_EXPERTS, 1), lambda i: (0, 0))],
        out_specs=[pl.BlockSpec((1, N_EXPERTS, tm), t3),
                   pl.BlockSpec((1, N_EXPERTS, tm), t3),
                   pl.BlockSpec((1, N_EXPERTS, 1), t3)],
        out_shape=[jax.ShapeDtypeStruct((nt, N_EXPERTS, tm), jnp.int32),
                   jax.ShapeDtypeStruct((nt, N_EXPERTS, tm), F32),
                   jax.ShapeDtypeStruct((nt, N_EXPERTS, 1), jnp.int32)],
        compiler_params=_params("parallel"),
        name="router",
    )(tok, rw_t, rb)


def _swiglu(h):
    gate = jnp.minimum(h[:, :D_FF], SWIGLU_LIMIT)
    up = jnp.clip(h[:, D_FF:], -SWIGLU_LIMIT, SWIGLU_LIMIT)
    return gate * jax.nn.sigmoid(SWIGLU_ALPHA * gate) * (up + 1.0)


def _moe_kernel(cnt_ref, x_ref, rank_ref, gate_ref, wu_ref, bu_ref, wd_ref, bd_ref, o_ref, *, chunk):
    i = pl.program_id(0)
    e = pl.program_id(1)
    tm = x_ref.shape[0]

    @pl.when(e == 0)
    def _():
        o_ref[...] = jnp.zeros_like(o_ref)

    n_routed = cnt_ref[i * N_EXPERTS + e]
    rank = rank_ref[0, pl.ds(e, 1), :]
    gate = gate_ref[0, pl.ds(e, 1), :]

    def body(c, carry):
        slot = c * chunk + lax.broadcasted_iota(jnp.int32, (chunk, tm), 0)
        hit = rank == slot
        onehot = jnp.where(hit, 1.0, 0.0).astype(BF16)
        xg = _dot(onehot, x_ref[...]).astype(BF16)
        act = _swiglu(_dot(xg, wu_ref[0]) + bu_ref[0])
        out = _dot(act.astype(BF16), wd_ref[0]) + bd_ref[0]
        g_slot = jnp.sum(jnp.where(hit, gate, 0.0), axis=1, keepdims=True)
        o_ref[...] += _dot_tn(onehot, (out * g_slot).astype(BF16))
        return carry

    lax.fori_loop(0, (n_routed + chunk - 1) // chunk, body, 0)


def _moe(tok, rank, gate, cnt, wu, bu, wd, bd, tm, chunk):
    n = tok.shape[0]
    nt = n // tm
    grid_spec = pltpu.PrefetchScalarGridSpec(
        num_scalar_prefetch=1,
        grid=(nt, N_EXPERTS),
        in_specs=[pl.BlockSpec((tm, D_MODEL), lambda i, e, c: (i, 0)),
                  pl.BlockSpec((1, N_EXPERTS, tm), lambda i, e, c: (i, 0, 0)),
                  pl.BlockSpec((1, N_EXPERTS, tm), lambda i, e, c: (i, 0, 0)),
                  pl.BlockSpec((1, D_MODEL, 2 * D_FF), lambda i, e, c: (e, 0, 0)),
                  pl.BlockSpec((1, 1, 2 * D_FF), lambda i, e, c: (e, 0, 0)),
                  pl.BlockSpec((1, D_FF, D_MODEL), lambda i, e, c: (e, 0, 0)),
                  pl.BlockSpec((1, 1, D_MODEL), lambda i, e, c: (e, 0, 0))],
        out_specs=pl.BlockSpec((tm, D_MODEL), lambda i, e, c: (i, 0)),
    )
    return pl.pallas_call(
        functools.partial(_moe_kernel, chunk=chunk),
        grid_spec=grid_spec,
        out_shape=jax.ShapeDtypeStruct((n, D_MODEL), F32),
        compiler_params=_params("parallel", "arbitrary"),
        name="moe",
    )(cnt, tok, rank, gate, wu, bu, wd, bd)


def _final_kernel(h_ref, f_ref, g_ref, y_ref):
    y_ref[...] = _rms(h_ref[...] + f_ref[...], g_ref[...])


def _final(h2, ffn, g, tm):
    n = h2.shape[0]
    row = lambda i: (i, 0)
    return pl.pallas_call(
        _final_kernel,
        grid=(n // tm,),
        in_specs=[pl.BlockSpec((tm, D_MODEL), row), pl.BlockSpec((tm, D_MODEL), row),
                  pl.BlockSpec((1, D_MODEL), lambda i: (0, 0))],
        out_specs=pl.BlockSpec((tm, D_MODEL), row),
        out_shape=jax.ShapeDtypeStruct((n, D_MODEL), F32),
        compiler_params=_params("parallel"),
        name="final_norm",
    )(h2, ffn, g)


def _page_copy(cache_ref, buf_ref, sem_ref, pt_ref, b, p, slot):
    return pltpu.make_async_copy(cache_ref.at[pt_ref[b, p]], buf_ref.at[slot, p], sem_ref.at[slot])


def _fetch_pages(cache_ref, buf_ref, sem_ref, pt_ref, b, slot, n_pages):
    def body(p, c):
        _page_copy(cache_ref, buf_ref, sem_ref, pt_ref, b, p, slot).start()
        return c
    lax.fori_loop(0, n_pages, body, 0)


def _wait_pages(cache_ref, buf_ref, sem_ref, pt_ref, b, slot, n_pages):
    def body(p, c):
        _page_copy(cache_ref, buf_ref, sem_ref, pt_ref, b, p, slot).wait()
        return c
    lax.fori_loop(0, n_pages, body, 0)


def _dsa_sample_mask_kernel(pt_ref, qi_ref, wi_ref, kin_ref, cache_ref, mask_ref, new_ref,
                            buf_ref, sem_ref, *, topk):
    b = pl.program_id(0)
    nb = pl.num_programs(0)
    n_pages = buf_ref.shape[1]
    past = n_pages * PAGE_SIZE
    slot = b % 2

    @pl.when(b == 0)
    def _():
        _fetch_pages(cache_ref, buf_ref, sem_ref, pt_ref, b, slot, n_pages)

    @pl.when(b + 1 < nb)
    def _():
        _fetch_pages(cache_ref, buf_ref, sem_ref, pt_ref, b + 1, 1 - slot, n_pages)

    _wait_pages(cache_ref, buf_ref, sem_ref, pt_ref, b, slot, n_pages)

    qi = qi_ref[0]
    w = wi_ref[0] * (IDX_HEADS ** -0.5)
    keys = buf_ref[slot].reshape(past, IDX_DIM).astype(BF16)
    s = _dot_nt(qi, keys) * (IDX_DIM ** -0.5)
    score = jnp.sum(jnp.maximum(s, 0.0) * w, axis=0, keepdims=True)
    s_new = jnp.sum(qi.astype(F32) * kin_ref[0].astype(F32), axis=1, keepdims=True) * (IDX_DIM ** -0.5)
    score_new = jnp.sum(jnp.maximum(s_new, 0.0) * w, axis=0, keepdims=True)
    key = _sort_key(score)
    key_new = _sort_key(score_new)

    def count_ge(cand):
        return (jnp.sum(jnp.where(key >= cand, 1, 0), axis=1, keepdims=True)
                + jnp.where(key_new >= cand, 1, 0))

    tau = _kth_largest(count_ge, topk, (1, 1))
    gt = key > tau
    eq = key == tau
    n_gt = jnp.sum(jnp.where(gt, 1, 0), axis=1, keepdims=True) + jnp.where(key_new > tau, 1, 0)
    need_f = (topk - n_gt).astype(F32)
    upper = _strict_upper(LANES)
    carry = jnp.zeros((1, 1), F32)
    for c in range(past // LANES):
        sl = slice(c * LANES, (c + 1) * LANES)
        eq_c = eq[:, sl]
        eq_b = jnp.where(eq_c, 1.0, 0.0).astype(BF16)
        rank = _dot(eq_b, upper) + carry
        take = jnp.logical_or(gt[:, sl], jnp.logical_and(eq_c, rank < need_f))
        mask_ref[0, :, sl] = jnp.where(take, 1.0, 0.0)
        carry = carry + jnp.sum(eq_b.astype(F32), axis=1, keepdims=True)
    take_new = jnp.logical_or(key_new > tau, jnp.logical_and(key_new == tau, carry < need_f))
    new_ref[0] = jnp.where(take_new, 1.0, 0.0)


def _dsa_sample_mask(page_table, qi_heads, wi_col, ki_new, cache_idx_k):
    db, n_pages = page_table.shape
    past = n_pages * PAGE_SIZE
    topk = min(TOPK_MAX, (past + 1) // 4)
    per = lambda b, pt: (b, 0, 0)
    grid_spec = pltpu.PrefetchScalarGridSpec(
        num_scalar_prefetch=1,
        grid=(db,),
        in_specs=[pl.BlockSpec((1, IDX_HEADS, IDX_DIM), per),
                  pl.BlockSpec((1, IDX_HEADS, 1), per),
                  pl.BlockSpec((1, 1, IDX_DIM), per),
                  pl.BlockSpec(memory_space=pl.ANY)],
        out_specs=[pl.BlockSpec((1, 1, past), per), pl.BlockSpec((1, 1, 1), per)],
        scratch_shapes=[pltpu.VMEM((2, n_pages, PAGE_SIZE, IDX_DIM), F32),
                        pltpu.SemaphoreType.DMA((2,))],
    )
    return pl.pallas_call(
        functools.partial(_dsa_sample_mask_kernel, topk=topk),
        grid_spec=grid_spec,
        out_shape=[jax.ShapeDtypeStruct((db, 1, past), F32), jax.ShapeDtypeStruct((db, 1, 1), F32)],
        compiler_params=_params("arbitrary"),
        name="dsa_sample_mask",
    )(page_table, qi_heads, wi_col, ki_new, cache_idx_k)


def _dsa_sample_attend_kernel(pt_ref, q_ref, kn_ref, vn_ref, mask_ref, new_ref, ck_ref, cv_ref, o_ref,
                              kbuf_ref, vbuf_ref, ksem_ref, vsem_ref):
    b = pl.program_id(0)
    nb = pl.num_programs(0)
    n_pages = kbuf_ref.shape[1]
    past = n_pages * PAGE_SIZE
    slot = b % 2

    def fetch(bb, sl):
        _fetch_pages(ck_ref, kbuf_ref, ksem_ref, pt_ref, bb, sl, n_pages)
        _fetch_pages(cv_ref, vbuf_ref, vsem_ref, pt_ref, bb, sl, n_pages)

    @pl.when(b == 0)
    def _():
        fetch(b, slot)

    @pl.when(b + 1 < nb)
    def _():
        fetch(b + 1, 1 - slot)

    _wait_pages(ck_ref, kbuf_ref, ksem_ref, pt_ref, b, slot, n_pages)
    _wait_pages(cv_ref, vbuf_ref, vsem_ref, pt_ref, b, slot, n_pages)

    q = q_ref[0]
    keys = kbuf_ref[slot].reshape(past, HEAD_DIM).astype(BF16)
    vals = vbuf_ref[slot].reshape(past, HEAD_DIM).astype(BF16)
    logits = _dot_nt(q, keys) * (HEAD_DIM ** -0.5)
    logits = jnp.where(mask_ref[0] > 0.5, logits, NEG)
    l_new = jnp.sum(q.astype(F32) * kn_ref[0].astype(F32), axis=1, keepdims=True) * (HEAD_DIM ** -0.5)
    l_new = jnp.where(new_ref[0] > 0.5, l_new, NEG)
    m = jnp.maximum(jnp.max(logits, axis=1, keepdims=True), l_new)
    p = jnp.exp(logits - m)
    p_new = jnp.exp(l_new - m)
    den = jnp.sum(p, axis=1, keepdims=True) + p_new
    p = p / den
    p_new = p_new / den
    o_ref[0] = _dot(p.astype(BF16), vals) + p_new * vn_ref[0].astype(F32)


def _dsa_sample_attend(page_table, q_heads, k_new, v_new, mask, take_new, cache_k, cache_v):
    db, n_pages = page_table.shape
    past = n_pages * PAGE_SIZE
    per = lambda b, pt: (b, 0, 0)
    grid_spec = pltpu.PrefetchScalarGridSpec(
        num_scalar_prefetch=1,
        grid=(db,),
        in_specs=[pl.BlockSpec((1, ATTN_HEADS, HEAD_DIM), per),
                  pl.BlockSpec((1, 1, HEAD_DIM), per),
                  pl.BlockSpec((1, 1, HEAD_DIM), per),
                  pl.BlockSpec((1, 1, past), per),
                  pl.BlockSpec((1, 1, 1), per),
                  pl.BlockSpec(memory_space=pl.ANY),
                  pl.BlockSpec(memory_space=pl.ANY)],
        out_specs=pl.BlockSpec((1, ATTN_HEADS, HEAD_DIM), per),
        scratch_shapes=[pltpu.VMEM((2, n_pages, PAGE_SIZE, HEAD_DIM), F32),
                        pltpu.VMEM((2, n_pages, PAGE_SIZE, HEAD_DIM), F32),
                        pltpu.SemaphoreType.DMA((2,)),
                        pltpu.SemaphoreType.DMA((2,))],
    )
    return pl.pallas_call(
        _dsa_sample_attend_kernel,
        grid_spec=grid_spec,
        out_shape=jax.ShapeDtypeStruct((db, ATTN_HEADS, HEAD_DIM), F32),
        compiler_params=_params("arbitrary"),
        name="dsa_sample_attend",
    )(page_table, q_heads, k_new, v_new, mask, take_new, cache_k, cache_v)


def _mid_sample_a_kernel(x_ref, attn_ref, u_ref, st_ref, pw_ref, ps_ref, wo_ref, gm_ref, wmq_ref,
                         h1_ref, qm_ref, *, past):
    u = u_ref[...]
    rows = u.shape[0]
    sums, cnts = [], []
    for g, w in enumerate(POOL_WINDOWS):
        sl = slice(g * POOL_CG, (g + 1) * POOL_CG)
        s = u[:, sl]
        for j in range(1, w):
            s = s + st_ref[POOL_MAX - 1 - j][:, sl]
        sums.append(s)
        cnts.append(jnp.full((rows, 1), float(min(w, past + 1)), F32))
    pool = _pool_project(sums, u, cnts, pw_ref, ps_ref)
    h1 = x_ref[...] + _dot(attn_ref[...], wo_ref[0:ATTN_WIDTH, :]) \
        + _dot(pool.astype(BF16), wo_ref[ATTN_WIDTH:, :])
    h1_ref[...] = h1
    qm_ref[...] = _dot(_rms(h1, gm_ref[...]).astype(BF16), wmq_ref[...]).astype(BF16)


def _mid_sample_a(x, attn, u, state_t, pool_w, pool_scale, w_o, g_mem, w_mq, past):
    db = x.shape[0]
    full = lambda a: pl.BlockSpec(a.shape, lambda i, nd=a.ndim: (0,) * nd)
    args = (x, attn, u, state_t, pool_w, pool_scale, w_o, g_mem, w_mq)
    return pl.pallas_call(
        functools.partial(_mid_sample_a_kernel, past=past),
        grid=(1,),
        in_specs=[full(a) for a in args],
        out_specs=[pl.BlockSpec((db, D_MODEL), lambda i: (0, 0)),
                   pl.BlockSpec((db, MEM_WIDTH), lambda i: (0, 0))],
        out_shape=[jax.ShapeDtypeStruct((db, D_MODEL), F32), jax.ShapeDtypeStruct((db, MEM_WIDTH), BF16)],
        compiler_params=_params("arbitrary"),
        name="mid_sample_a",
    )(*args)


def _mem_sample_kernel(q_ref, mk_ref, mv_ref, o_ref):
    rows = 8
    head_of_row = lax.broadcasted_iota(jnp.int32, (rows, MEM_WIDTH), 0)
    head_of_lane = lax.broadcasted_iota(jnp.int32, (rows, MEM_WIDTH), 1) // MEM_HEAD_DIM
    own = head_of_row == head_of_lane
    q = jnp.broadcast_to(q_ref[0].astype(F32), (rows, MEM_WIDTH))
    qh = jnp.where(own, q, 0.0).astype(BF16)
    logits = _dot_nt(qh, mk_ref[0].astype(BF16)) * (MEM_HEAD_DIM ** -0.5)
    m = jnp.max(logits, axis=1, keepdims=True)
    p = jnp.exp(logits - m)
    p = p / jnp.sum(p, axis=1, keepdims=True)
    o = _dot(p.astype(BF16), mv_ref[0].astype(BF16))
    o_ref[0] = jnp.sum(jnp.where(own, o, 0.0), axis=0, keepdims=True)


def _mem_sample(qm, mk, mv):
    db = qm.shape[0]
    per = lambda b: (b, 0, 0)
    return pl.pallas_call(
        _mem_sample_kernel,
        grid=(db,),
        in_specs=[pl.BlockSpec((1, 1, MEM_WIDTH), per),
                  pl.BlockSpec((1, N_MEM, MEM_WIDTH), per),
                  pl.BlockSpec((1, N_MEM, MEM_WIDTH), per)],
        out_specs=pl.BlockSpec((1, 1, MEM_WIDTH), per),
        out_shape=jax.ShapeDtypeStruct((db, 1, MEM_WIDTH), F32),
        compiler_params=_params("parallel"),
        name="mem_sample",
    )(qm.reshape(db, 1, MEM_WIDTH), mk, mv)


def _tail_sample_kernel(h1_ref, o_ref, wmo_ref, gf_ref, rw_ref, rb_ref, wu_ref, bu_ref, wd_ref, bd_ref,
                        gfin_ref, y_ref, h2_ref, tok_ref, comb_ref, acc_ref):
    e = pl.program_id(0)
    rows = h1_ref.shape[0]

    @pl.when(e == 0)
    def _():
        h2 = h1_ref[...] + _dot(o_ref[...].astype(BF16), wmo_ref[...])
        tok = _rms(h2, gf_ref[...]).astype(BF16)
        h2_ref[...] = h2
        tok_ref[...] = tok
        logits = _dot(tok, rw_ref[...]) + rb_ref[...]
        eid = lax.broadcasted_iota(jnp.int32, (rows, N_EXPERTS), 1)
        vals, hots = [], []
        cur = logits
        for _ in range(TOP_K):
            m = jnp.max(cur, axis=1, keepdims=True)
            idx = jnp.min(jnp.where(cur == m, eid, N_EXPERTS), axis=1, keepdims=True)
            hot = eid == idx
            vals.append(m)
            hots.append(hot)
            cur = jnp.where(hot, -jnp.inf, cur)
        exps = [jnp.exp(v - vals[0]) for v in vals]
        den = exps[0] + exps[1] + exps[2] + exps[3]
        comb = jnp.zeros((rows, N_EXPERTS), F32)
        for hot, ex in zip(hots, exps):
            comb = jnp.where(hot, ex / den, comb)
        comb_ref[...] = comb
        acc_ref[...] = jnp.zeros_like(acc_ref)

    act = _swiglu(_dot(tok_ref[...], wu_ref[0]) + bu_ref[0])
    out = _dot(act.astype(BF16), wd_ref[0]) + bd_ref[0]
    eid = lax.broadcasted_iota(jnp.int32, (rows, N_EXPERTS), 1)
    c_e = jnp.sum(jnp.where(eid == e, comb_ref[...], 0.0), axis=1, keepdims=True)
    acc_ref[...] += c_e * out

    @pl.when(e == pl.num_programs(0) - 1)
    def _():
        y_ref[...] = _rms(h2_ref[...] + acc_ref[...], gfin_ref[...])


def _tail_sample(h1, o, w_mo, g_ffn, rw, rb, wu, bu, wd, bd, g_final):
    db = h1.shape[0]
    f2 = lambda e: (0, 0)
    pe = lambda e: (e, 0, 0)
    return pl.pallas_call(
        _tail_sample_kernel,
        grid=(N_EXPERTS,),
        in_specs=[pl.BlockSpec((db, D_MODEL), f2),
                  pl.BlockSpec((db, MEM_WIDTH), f2),
                  pl.BlockSpec((MEM_WIDTH, D_MODEL), f2),
                  pl.BlockSpec((1, D_MODEL), f2),
                  pl.BlockSpec((D_MODEL, N_EXPERTS), f2),
                  pl.BlockSpec((1, N_EXPERTS), f2),
                  pl.BlockSpec((1, D_MODEL, 2 * D_FF), pe),
                  pl.BlockSpec((1, 1, 2 * D_FF), pe),
                  pl.BlockSpec((1, D_FF, D_MODEL), pe),
                  pl.BlockSpec((1, 1, D_MODEL), pe),
                  pl.BlockSpec((1, D_MODEL), f2)],
        out_specs=pl.BlockSpec((db, D_MODEL), f2),
        out_shape=jax.ShapeDtypeStruct((db, D_MODEL), F32),
        scratch_shapes=[pltpu.VMEM((db, D_MODEL), F32), pltpu.VMEM((db, D_MODEL), BF16),
                        pltpu.VMEM((db, N_EXPERTS), F32), pltpu.VMEM((db, D_MODEL), F32)],
        compiler_params=_params("arbitrary"),
        name="tail_sample",
    )(h1, o, w_mo, g_ffn, rw, rb, wu, bu, wd, bd, g_final)


def _rope_tables(pos):
    half = HEAD_DIM // 2
    inv = ROPE_THETA ** (-jnp.arange(half, dtype=F32) / half)
    ang = pos.astype(F32)[:, None] * inv[None, :]
    cos = jnp.cos(ang)
    sin = jnp.sin(ang)
    return jnp.tile(cos, (1, 4)), jnp.concatenate([-sin, sin, -sin, sin], axis=1)


def _pick_tile(n, pref):
    t = min(n, pref)
    while n % t:
        t //= 2
    return t


def kernel(x_prompt, x_sample, mem_prompt, cache_k, cache_v, cache_idx_k, state_pool, cache_mem_k, cache_mem_v, page_table, norm_mix, w_in, pool_w, pool_scale, w_o, norm_mem, w_mq, w_mk, w_mv, w_mo, norm_ffn, router_w, router_b, w_up, b_up, w_down, b_down, norm_final):
    B, T = x_prompt.shape[:2]
    DB, DS = x_sample.shape[:2]
    assert DS == 1 and w_in.shape[0] == 1
    n_pages = page_table.shape[1]
    past = n_pages * PAGE_SIZE
    NP = B * T

    wi_ = w_in[0]
    wq, wk, wv, wu_, wqi, wki, wwi = jnp.split(wi_, [512, 576, 640, 1152, 1664, 1728], axis=1)
    w_all = jnp.concatenate(
        [wq, wqi, wu_, wk, wk, wki, wki, wv, wv, wwi, jnp.zeros((D_MODEL, LANES - IDX_HEADS), F32)],
        axis=1).astype(BF16)
    g_mix = norm_mix[0][None, :]
    g_mem = norm_mem[0][None, :]
    g_ffn = norm_ffn[0][None, :]
    g_fin = norm_final[None, :]
    pw = pool_w[0].astype(BF16)
    ps = pool_scale[0][None, :]
    wo = w_o[0].astype(BF16)
    wmq = w_mq[0].reshape(D_MODEL, MEM_WIDTH).astype(BF16)
    wkv = jnp.concatenate([w_mk[0].reshape(D_MODEL, MEM_WIDTH), w_mv[0].reshape(D_MODEL, MEM_WIDTH)],
                          axis=1).astype(BF16)
    wmo = w_mo[0].reshape(MEM_WIDTH, D_MODEL).astype(BF16)
    rw = router_w[0].astype(BF16)
    rb = router_b[0]
    wup = w_up[0].astype(BF16)
    wdn = w_down[0].astype(BF16)
    bup = b_up[0][:, None, :]
    bdn = b_down[0][:, None, :]

    xp = x_prompt.reshape(NP, D_MODEL)
    tm_in = _pick_tile(T, 512)
    cos_p, sin_p = _rope_tables(jnp.arange(T, dtype=jnp.int32))
    q, qi, u, kk, kiki, vv, k_p, ki_p, v_p, wi_p = _in_project(xp, g_mix, w_all, cos_p, sin_p, tm_in, T // tm_in)
    attn = _dsa_prompt(q, qi, wi_p, kk, kiki, vv, B, T)
    mk_p, mv_p = _mem_kv(mem_prompt.reshape(B * N_MEM, D_MODEL), wkv)
    h2_p, tok_p = _mid_prompt(xp, attn, u, mk_p.reshape(B, N_MEM, MEM_WIDTH), mv_p.reshape(B, N_MEM, MEM_WIDTH),
                              pw, ps, wo, g_mem, wmq, wmo, g_ffn, B, T, _pick_tile(T, 512))
    tm_moe = _pick_tile(NP, 2048)
    rank, gate, cnt = _router(tok_p, rw.T, rb[:, None], tm_moe)
    ffn_p = _moe(tok_p, rank, gate, cnt.reshape(-1), wup, bup, wdn, bdn, tm_moe, 128)
    y_prompt = _final(h2_p, ffn_p, g_fin, _pick_tile(NP, 1024)).reshape(B, T, D_MODEL)

    xs = x_sample.reshape(DB, D_MODEL)
    cos_s, sin_s = _rope_tables(jnp.full((DB,), past, jnp.int32))
    q_s, qi_s, u_s, kk_s, kiki_s, vv_s, k_s, ki_s, v_s, wi_s = _in_project(xs, g_mix, w_all, cos_s, sin_s, DB, 1)
    mask, take_new = _dsa_sample_mask(
        page_table, qi_s.reshape(DB, IDX_HEADS, IDX_DIM), wi_s[:, :IDX_HEADS, None],
        kiki_s[:, None, :IDX_DIM], cache_idx_k[0])
    attn_s = _dsa_sample_attend(
        page_table, q_s.reshape(DB, ATTN_HEADS, HEAD_DIM), kk_s[:, None, :HEAD_DIM], vv_s[:, None, :HEAD_DIM],
        mask, take_new, cache_k[0], cache_v[0])
    attn_s = attn_s.reshape(DB, ATTN_WIDTH).astype(BF16)
    h1_s, qm_s = _mid_sample_a(xs, attn_s, u_s, jnp.transpose(state_pool[0], (1, 0, 2)), pw, ps, wo, g_mem, wmq, past)
    o_s = _mem_sample(qm_s, cache_mem_k[0].reshape(DB, N_MEM, MEM_WIDTH), cache_mem_v[0].reshape(DB, N_MEM, MEM_WIDTH))
    y_sample = _tail_sample(h1_s, o_s.reshape(DB, MEM_WIDTH), wmo, g_ffn, rw, rb[None, :], wup, bup, wdn, bdn,
                            g_fin).reshape(DB, 1, D_MODEL)

    new_pool_p = u.reshape(B, T, POOL_WIDTH)[:, T - (POOL_MAX - 1):][None]
    new_pool_s = jnp.concatenate([state_pool[0][:, 1:], u_s[:, None, :]], axis=1)[None]
    return (y_prompt, y_sample,
            k_p.reshape(1, B, T, HEAD_DIM), v_p.reshape(1, B, T, HEAD_DIM), ki_p.reshape(1, B, T, IDX_DIM),
            new_pool_p,
            mk_p.reshape(1, B, N_MEM, MEM_HEADS, MEM_HEAD_DIM), mv_p.reshape(1, B, N_MEM, MEM_HEADS, MEM_HEAD_DIM),
            k_s.reshape(1, DB, 1, HEAD_DIM), v_s.reshape(1, DB, 1, HEAD_DIM), ki_s.reshape(1, DB, 1, IDX_DIM),
            new_pool_s)
```

```python
import functools

import jax
import jax.numpy as jnp
from jax import lax
from jax.experimental import pallas as pl
from jax.experimental.pallas import tpu as pltpu

D_MODEL = 1024
HEAD_DIM = 64
ATTN_WIDTH = 512
ATTN_HEADS = 8
IDX_HEADS = 8
IDX_DIM = 64
TOPK_MAX = 256
Q_BLOCK = 128
PAGE_SIZE = 128
POOL_WIDTH = 512
POOL_WINDOWS = (2, 4, 8, 16)
POOL_CG = 128
POOL_MAX = 16
N_MEM = 256
MEM_HEADS = 4
MEM_HEAD_DIM = 128
MEM_WIDTH = MEM_HEADS * MEM_HEAD_DIM
N_EXPERTS = 32
TOP_K = 4
D_FF = 1024
SWIGLU_LIMIT = 7.0
SWIGLU_ALPHA = 1.702
ROPE_THETA = 10000.0
EPS = 1e-6
NEG = -1e30

LANES = 128
IN_COLS = 3 * 512 + 4 * LANES
VMEM_LIMIT = 56 * 1024 * 1024

BF16 = jnp.bfloat16
F32 = jnp.float32


def _dot(a, b):
    return jnp.dot(a, b, preferred_element_type=F32)


def _dot_nt(a, b):
    return lax.dot_general(a, b, (((1,), (1,)), ((), ())), preferred_element_type=F32)


def _dot_tn(a, b):
    return lax.dot_general(a, b, (((0,), (0,)), ((), ())), preferred_element_type=F32)


def _rms(x, g):
    return x * lax.rsqrt(jnp.mean(x * x, axis=-1, keepdims=True) + EPS) * g


def _params(*sem):
    return pltpu.CompilerParams(dimension_semantics=sem, vmem_limit_bytes=VMEM_LIMIT)


def _in_kernel(x_ref, g_ref, w_ref, cos_ref, sin_ref,
               q_ref, qi_ref, u_ref, kk_ref, kiki_ref, vv_ref, k_ref, ki_ref, v_ref, wi_ref):
    xn = _rms(x_ref[...], g_ref[...]).astype(BF16)
    z = _dot(xn, w_ref[...])
    cos = cos_ref[...]
    sin = sin_ref[...]
    lane = lax.broadcasted_iota(jnp.int32, cos.shape, 1)
    first_half = (lane % HEAD_DIM) < (HEAD_DIM // 2)

    def rope(zc):
        swapped = jnp.where(first_half,
                            pltpu.roll(zc, LANES - HEAD_DIM // 2, 1),
                            pltpu.roll(zc, HEAD_DIM // 2, 1))
        return zc * cos + swapped * sin

    for c in range(4):
        q_c = rope(z[:, c * LANES:(c + 1) * LANES]) * (HEAD_DIM ** -0.5)
        qi_c = rope(z[:, 512 + c * LANES:512 + (c + 1) * LANES]) * (IDX_DIM ** -0.5)
        q_ref[:, c * LANES:(c + 1) * LANES] = q_c.astype(BF16)
        qi_ref[:, c * LANES:(c + 1) * LANES] = qi_c.astype(BF16)
    u_ref[...] = z[:, 1024:1536]
    kk = rope(z[:, 1536:1664])
    kiki = rope(z[:, 1664:1792])
    vv = z[:, 1792:1920]
    kk_ref[...] = kk.astype(BF16)
    kiki_ref[...] = kiki.astype(BF16)
    vv_ref[...] = vv.astype(BF16)
    k_ref[...] = kk[:, :HEAD_DIM]
    ki_ref[...] = kiki[:, :IDX_DIM]
    v_ref[...] = vv[:, :HEAD_DIM]
    wi_ref[...] = z[:, 1920:2048]


def _in_project(x, g, w_all, cos, sin, tm, table_blocks):
    n = x.shape[0]
    row = lambda i: (i, 0)
    tab = lambda i: (i % table_blocks, 0)
    fixed = lambda i: (0, 0)
    outs = [
        jax.ShapeDtypeStruct((n, 512), BF16),
        jax.ShapeDtypeStruct((n, 512), BF16),
        jax.ShapeDtypeStruct((n, 512), F32),
        jax.ShapeDtypeStruct((n, LANES), BF16),
        jax.ShapeDtypeStruct((n, LANES), BF16),
        jax.ShapeDtypeStruct((n, LANES), BF16),
        jax.ShapeDtypeStruct((n, HEAD_DIM), F32),
        jax.ShapeDtypeStruct((n, IDX_DIM), F32),
        jax.ShapeDtypeStruct((n, HEAD_DIM), F32),
        jax.ShapeDtypeStruct((n, LANES), F32),
    ]
    out_specs = [pl.BlockSpec((tm, s.shape[1]), row) for s in outs]
    return pl.pallas_call(
        _in_kernel,
        grid=(n // tm,),
        in_specs=[pl.BlockSpec((tm, D_MODEL), row),
                  pl.BlockSpec((1, D_MODEL), fixed),
                  pl.BlockSpec((D_MODEL, IN_COLS), fixed),
                  pl.BlockSpec((tm, LANES), tab),
                  pl.BlockSpec((tm, LANES), tab)],
        out_specs=out_specs,
        out_shape=outs,
        compiler_params=_params("parallel"),
        name="in_project",
    )(x, g, w_all, cos, sin)


def _f32_to_key(x):
    bits = lax.bitcast_convert_type(x, jnp.int32)
    return jnp.where(bits < 0, bits ^ jnp.int32(0x7FFFFFFF), bits)


def _key_to_f32(key):
    bits = jnp.where(key < 0, key ^ jnp.int32(0x7FFFFFFF), key)
    return lax.bitcast_convert_type(bits, F32)


def _search_kth(count_ge, k, lo, hi, c_lo, c_hi):
    def finished(lo, hi, c_lo):
        return jnp.logical_or(c_lo == k, lo + 1 == hi)

    def cond(st):
        return st[1] > 0

    def body(st):
        it, _, lo, hi, c_lo, c_hi = st
        done = finished(lo, hi, c_lo)
        mid = lo + lax.shift_right_logical(hi - lo, 1)
        f_lo = _key_to_f32(lo)
        f_hi = _key_to_f32(hi)
        frac = ((c_lo - k).astype(F32) + 0.5) / (c_lo - c_hi).astype(F32)
        guess = _f32_to_key(f_lo + (f_hi - f_lo) * frac)
        guess = jnp.minimum(jnp.maximum(guess, lo + 1), hi - 1)
        cand = jnp.where(it % 3 == 2, mid, guess)
        c = count_ge(_key_to_f32(cand))
        move_lo = jnp.logical_and(c >= k, jnp.logical_not(done))
        move_hi = jnp.logical_and(c < k, jnp.logical_not(done))
        lo = jnp.where(move_lo, cand, lo)
        c_lo = jnp.where(move_lo, c, c_lo)
        hi = jnp.where(move_hi, cand, hi)
        c_hi = jnp.where(move_hi, c, c_hi)
        active = jnp.max(jnp.where(finished(lo, hi, c_lo), 0, 1))
        return it + 1, active, lo, hi, c_lo, c_hi

    active0 = jnp.max(jnp.where(finished(lo, hi, c_lo), 0, 1))
    _, _, lo, hi, c_lo, c_hi = lax.while_loop(cond, body, (jnp.int32(0), active0, lo, hi, c_lo, c_hi))
    return lo, c_lo, c_hi


def _strict_upper(n):
    r = lax.broadcasted_iota(jnp.int32, (n, n), 0)
    c = lax.broadcasted_iota(jnp.int32, (n, n), 1)
    return jnp.where(r < c, 1.0, 0.0).astype(BF16)


def _take_first_ties(eq, need_f, tied_any):
    rows, n = eq.shape

    def ranked():
        upper = _strict_upper(LANES)
        carry = jnp.zeros((rows, 1), F32)
        parts = []
        for c in range(n // LANES):
            eq_c = eq[:, c * LANES:(c + 1) * LANES]
            eq_f = jnp.where(eq_c, 1.0, 0.0)
            rank = _dot(eq_f.astype(BF16), upper) + carry
            parts.append(jnp.where(rank < need_f, eq_f, 0.0))
            carry = carry + jnp.sum(eq_f, axis=1, keepdims=True)
        return jnp.concatenate(parts, axis=1), carry

    def plain():
        return jnp.where(eq, 1.0, 0.0), jnp.zeros((rows, 1), F32)

    return lax.cond(jnp.max(jnp.where(tied_any, 1, 0)) > 0, ranked, plain)


def _dsa_prompt_kernel(q_ref, qi_ref, wi_ref, kk_ref, kiki_ref, vv_ref, o_ref, *, topk, first_block, select):
    i = first_block + pl.program_id(1)
    lk = kk_ref.shape[1]
    kiki = kiki_ref[0]
    kk = kk_ref[0]
    vv = vv_ref[0]
    w = wi_ref[...] * (IDX_HEADS ** -0.5)
    lane = lax.broadcasted_iota(jnp.int32, (Q_BLOCK, LANES), 1)
    low = lane < HEAD_DIM

    def head_operand(ref, h):
        pair = ref[:, (h // 2) * LANES:(h // 2 + 1) * LANES]
        keep = low if h % 2 == 0 else jnp.logical_not(low)
        return jnp.where(keep, pair, jnp.zeros_like(pair))

    pos_q = i * Q_BLOCK + lax.broadcasted_iota(jnp.int32, (Q_BLOCK, 1), 0)
    pos_k = lax.broadcasted_iota(jnp.int32, (Q_BLOCK, lk), 1)
    causal = pos_k <= pos_q

    if select:
        score = jnp.zeros((Q_BLOCK, lk), F32)
        for h in range(IDX_HEADS):
            score = score + jnp.maximum(_dot_nt(head_operand(qi_ref, h), kiki), 0.0) * w[:, h:h + 1]
        sm = jnp.where(causal, score, -jnp.inf)
        lo0 = _f32_to_key(jnp.min(jnp.where(causal, score, jnp.inf), axis=1, keepdims=True))
        hi0 = _f32_to_key(jnp.max(sm, axis=1, keepdims=True)) + 1

        def count_ge(cand):
            return jnp.sum(jnp.where(sm >= cand, 1, 0), axis=1, keepdims=True)

        lo, c_lo, c_hi = _search_kth(count_ge, topk, lo0, hi0, pos_q + 1, jnp.zeros_like(pos_q))
        tau = _key_to_f32(lo)
        gt = sm > tau
        eq = sm == tau
        tied = c_lo > topk
        need_f = jnp.where(tied, topk - c_hi, lk + 1).astype(F32)
        take_eq, _ = _take_first_ties(eq, need_f, tied)
        mask = jnp.logical_or(gt, take_eq > 0.5)
    else:
        mask = causal
    bias = jnp.where(mask, 0.0, NEG)

    for j in range(ATTN_HEADS // 2):
        outs = []
        for h in (2 * j, 2 * j + 1):
            logits = _dot_nt(head_operand(q_ref, h), kk) + bias
            p = jnp.exp(logits - jnp.max(logits, axis=1, keepdims=True))
            inv = 1.0 / jnp.sum(p, axis=1, keepdims=True)
            outs.append(_dot(p.astype(BF16), vv) * inv)
        o_ref[0, :, j * LANES:(j + 1) * LANES] = jnp.where(low, outs[0], outs[1]).astype(BF16)


def _dsa_prompt_group(q, qi, wi, kk, kiki, vv, batch, t_len, first_block, n_blocks, topk):
    nb = t_len // Q_BLOCK
    lk = (first_block + n_blocks) * Q_BLOCK
    blk = lambda b, j: (b * nb + first_block + j, 0)
    seq = lambda b, j: (b, 0, 0)
    return pl.pallas_call(
        functools.partial(_dsa_prompt_kernel, topk=topk, first_block=first_block, select=lk > topk),
        grid=(batch, n_blocks),
        in_specs=[pl.BlockSpec((Q_BLOCK, 512), blk),
                  pl.BlockSpec((Q_BLOCK, 512), blk),
                  pl.BlockSpec((Q_BLOCK, LANES), blk),
                  pl.BlockSpec((1, lk, LANES), seq),
                  pl.BlockSpec((1, lk, LANES), seq),
                  pl.BlockSpec((1, lk, LANES), seq)],
        out_specs=pl.BlockSpec((1, Q_BLOCK, 512), lambda b, j: (b, j, 0)),
        out_shape=jax.ShapeDtypeStruct((batch, n_blocks * Q_BLOCK, 512), BF16),
        compiler_params=_params("parallel", "parallel"),
        name="dsa_prompt_%d" % lk,
    )(q, qi, wi, kk, kiki, vv)


def _dsa_prompt(q, qi, wi, kk, kiki, vv, batch, t_len):
    nb = t_len // Q_BLOCK
    topk = min(TOPK_MAX, t_len // 4)
    kk, kiki, vv = (a.reshape(batch, t_len, LANES) for a in (kk, kiki, vv))
    edges = [0, min(topk // Q_BLOCK, nb)]
    while edges[-1] < nb:
        edges.append(min(edges[-1] + (2 if edges[-1] < 4 else 4), nb))
    parts = [_dsa_prompt_group(q, qi, wi, kk, kiki, vv, batch, t_len, a, b - a, topk)
             for a, b in zip(edges[:-1], edges[1:]) if b > a]
    return jnp.concatenate(parts, axis=1).reshape(batch * t_len, 512)


def _mem_kv_kernel(m_ref, w_ref, k_ref, v_ref):
    z = _dot(m_ref[...].astype(BF16), w_ref[...])
    k_ref[...] = z[:, :MEM_WIDTH]
    v_ref[...] = z[:, MEM_WIDTH:]


def _mem_kv(mem, w_kv):
    n = mem.shape[0]
    tm = min(n, 512)
    row = lambda i: (i, 0)
    return pl.pallas_call(
        _mem_kv_kernel,
        grid=(n // tm,),
        in_specs=[pl.BlockSpec((tm, D_MODEL), row),
                  pl.BlockSpec((D_MODEL, 2 * MEM_WIDTH), lambda i: (0, 0))],
        out_specs=[pl.BlockSpec((tm, MEM_WIDTH), row), pl.BlockSpec((tm, MEM_WIDTH), row)],
        out_shape=[jax.ShapeDtypeStruct((n, MEM_WIDTH), F32)] * 2,
        compiler_params=_params("parallel"),
        name="mem_kv",
    )(mem, w_kv)


def _pool_project(window_sums, u, cnts, pw_ref, ps_ref):
    ys = []
    for g in range(len(POOL_WINDOWS)):
        sl = slice(g * POOL_CG, (g + 1) * POOL_CG)
        pooled = window_sums[g] / cnts[g] - u[:, sl]
        ys.append(_dot(pooled.astype(BF16), pw_ref[g]) * ps_ref[:, sl])
    return jnp.concatenate(ys, axis=1)


def _mid_prompt_kernel(x_ref, attn_ref, u_ref, halo_ref, mk_ref, mv_ref, pw_ref, ps_ref, wo_ref,
                       gm_ref, wmq_ref, wmo_ref, gf_ref, h2_ref, tok_ref, e_ref):
    i = pl.program_id(1)
    tm = x_ref.shape[0]
    u = u_ref[...]
    halo = halo_ref[...]
    e_ref[0:POOL_MAX, :] = jnp.where(i == 0, jnp.zeros_like(halo), halo)
    e_ref[POOL_MAX:POOL_MAX + tm, :] = u
    pos = i * tm + lax.broadcasted_iota(jnp.int32, (tm, 1), 0)
    sums, cnts = [], []
    for g, w in enumerate(POOL_WINDOWS):
        sl = slice(g * POOL_CG, (g + 1) * POOL_CG)
        s = u[:, sl]
        for j in range(1, w):
            s = s + e_ref[POOL_MAX - j:POOL_MAX - j + tm, sl]
        sums.append(s)
        cnts.append(jnp.minimum(w, pos + 1).astype(F32))
    pool = _pool_project(sums, u, cnts, pw_ref, ps_ref)
    h1 = x_ref[...] + _dot(attn_ref[...], wo_ref[0:ATTN_WIDTH, :]) \
        + _dot(pool.astype(BF16), wo_ref[ATTN_WIDTH:, :])

    qm = _dot(_rms(h1, gm_ref[...]).astype(BF16), wmq_ref[...]).astype(BF16)
    mk = mk_ref[0].astype(BF16)
    mv = mv_ref[0].astype(BF16)
    outs = []
    for h in range(MEM_HEADS):
        sl = slice(h * MEM_HEAD_DIM, (h + 1) * MEM_HEAD_DIM)
        logits = _dot_nt(qm[:, sl], mk[:, sl]) * (MEM_HEAD_DIM ** -0.5)
        m = jnp.max(logits, axis=1, keepdims=True)
        p = jnp.exp(logits - m)
        p = p / jnp.sum(p, axis=1, keepdims=True)
        outs.append(_dot(p.astype(BF16), mv[:, sl]))
    o = jnp.concatenate(outs, axis=1)
    h2 = h1 + _dot(o.astype(BF16), wmo_ref[...])
    h2_ref[...] = h2
    tok_ref[...] = _rms(h2, gf_ref[...]).astype(BF16)


def _mid_prompt(x, attn, u, mk, mv, pool_w, pool_scale, w_o, g_mem, w_mq, w_mo, g_ffn, batch, t_len, tm):
    nt = t_len // tm
    hb = tm // POOL_MAX
    row = lambda b, i: (b * nt + i, 0)
    halo = lambda b, i: (jnp.maximum((b * nt + i) * hb - 1, 0), 0)
    fixed2 = lambda b, i: (0, 0)
    fixed3 = lambda b, i: (0, 0, 0)
    perb = lambda b, i: (b, 0, 0)
    n = batch * t_len
    return pl.pallas_call(
        _mid_prompt_kernel,
        grid=(batch, nt),
        in_specs=[pl.BlockSpec((tm, D_MODEL), row),
                  pl.BlockSpec((tm, ATTN_WIDTH), row),
                  pl.BlockSpec((tm, POOL_WIDTH), row),
                  pl.BlockSpec((POOL_MAX, POOL_WIDTH), halo),
                  pl.BlockSpec((1, N_MEM, MEM_WIDTH), perb),
                  pl.BlockSpec((1, N_MEM, MEM_WIDTH), perb),
                  pl.BlockSpec((len(POOL_WINDOWS), POOL_CG, POOL_CG), fixed3),
                  pl.BlockSpec((1, POOL_WIDTH), fixed2),
                  pl.BlockSpec((D_MODEL, D_MODEL), fixed2),
                  pl.BlockSpec((1, D_MODEL), fixed2),
                  pl.BlockSpec((D_MODEL, MEM_WIDTH), fixed2),
                  pl.BlockSpec((MEM_WIDTH, D_MODEL), fixed2),
                  pl.BlockSpec((1, D_MODEL), fixed2)],
        out_specs=[pl.BlockSpec((tm, D_MODEL), row), pl.BlockSpec((tm, D_MODEL), row)],
        out_shape=[jax.ShapeDtypeStruct((n, D_MODEL), F32), jax.ShapeDtypeStruct((n, D_MODEL), BF16)],
        scratch_shapes=[pltpu.VMEM((POOL_MAX + tm, POOL_WIDTH), F32)],
        compiler_params=_params("parallel", "arbitrary"),
        name="mid_prompt",
    )(x, attn, u, u, mk, mv, pool_w, pool_scale, w_o, g_mem, w_mq, w_mo, g_ffn)


def _router_kernel(tok_ref, rw_ref, rb_ref, rank_ref, gate_ref, cnt_ref):
    tm = tok_ref.shape[0]
    logits = _dot_nt(rw_ref[...], tok_ref[...]) + rb_ref[...]
    eid = lax.broadcasted_iota(jnp.int32, (N_EXPERTS, tm), 0)
    vals, hots = [], []
    cur = logits
    for _ in range(TOP_K):
        m = jnp.max(cur, axis=0, keepdims=True)
        idx = jnp.min(jnp.where(cur == m, eid, N_EXPERTS), axis=0, keepdims=True)
        hot = eid == idx
        vals.append(m)
        hots.append(hot)
        cur = jnp.where(hot, -jnp.inf, cur)
    exps = [jnp.exp(v - vals[0]) for v in vals]
    den = exps[0] + exps[1] + exps[2] + exps[3]
    gate = jnp.zeros((N_EXPERTS, tm), F32)
    sel = jnp.zeros((N_EXPERTS, tm), jnp.bool_)
    for hot, ex in zip(hots, exps):
        gate = jnp.where(hot, ex / den, gate)
        sel = jnp.logical_or(sel, hot)
    gate_ref[0] = gate

    upper = _strict_upper(LANES)
    carry = jnp.zeros((N_EXPERTS, 1), F32)
    for c in range(tm // LANES):
        sl = slice(c * LANES, (c + 1) * LANES)
        sel_c = sel[:, sl]
        sel_b = jnp.where(sel_c, 1.0, 0.0).astype(BF16)
        rank = _dot(sel_b, upper) + carry
        rank_ref[0, :, sl] = jnp.where(sel_c, rank, -1.0).astype(jnp.int32)
        carry = carry + jnp.sum(sel_b.astype(F32), axis=1, keepdims=True)
    cnt_ref[0] = carry.astype(jnp.int32)


def _router(tok, rw_t, rb, tm):
    n = tok.shape[0]
    nt = n // tm
    t3 = lambda i: (i, 0, 0)
    return pl.pallas_call(
        _router_kernel,
        grid=(nt,),
        in_specs=[pl.BlockSpec((tm, D_MODEL), lambda i: (i, 0)),
                  pl.BlockSpec((N_EXPERTS, D_MODEL), lambda i: (0, 0)),
                  pl.BlockSpec((N_EXPERTS, 1), lambda i: (0, 0))],
        out_specs=[pl.BlockSpec((1, N_EXPERTS, tm), t3),
                   pl.BlockSpec((1, N_EXPERTS, tm), t3),
                   pl.BlockSpec((1, N_EXPERTS, 1), t3)],
        out_shape=[jax.ShapeDtypeStruct((nt, N_EXPERTS, tm), jnp.int32),
                   jax.ShapeDtypeStruct((nt, N_EXPERTS, tm), F32),
                   jax.ShapeDtypeStruct((nt, N_EXPERTS, 1), jnp.int32)],
        compiler_params=_params("parallel"),
        name="router",
    )(tok, rw_t, rb)


def _swiglu(h):
    gate = jnp.minimum(h[:, :D_FF], SWIGLU_LIMIT)
    up = jnp.clip(h[:, D_FF:], -SWIGLU_LIMIT, SWIGLU_LIMIT)
    return gate * jax.nn.sigmoid(SWIGLU_ALPHA * gate) * (up + 1.0)


def _moe_kernel(cnt_ref, x_ref, rank_ref, gate_ref, wu_ref, bu_ref, wd_ref, bd_ref, o_ref, *,
                first_chunk, chunk):
    i = pl.program_id(0)
    e = pl.program_id(1)
    tm = x_ref.shape[0]

    @pl.when(e == 0)
    def _():
        o_ref[...] = jnp.zeros_like(o_ref)

    n_routed = cnt_ref[i * N_EXPERTS + e]
    rank = rank_ref[0, pl.ds(e, 1), :]
    gate = gate_ref[0, pl.ds(e, 1), :]

    def run_slots(first, size):
        slot = first + lax.broadcasted_iota(jnp.int32, (size, tm), 0)
        hit = rank == slot
        onehot = jnp.where(hit, 1.0, 0.0).astype(BF16)
        xg = _dot(onehot, x_ref[...]).astype(BF16)
        act = _swiglu(_dot(xg, wu_ref[0]) + bu_ref[0])
        out = _dot(act.astype(BF16), wd_ref[0]) + bd_ref[0]
        g_slot = jnp.sum(jnp.where(hit, gate, 0.0), axis=1, keepdims=True)
        o_ref[...] += _dot_tn(onehot, (out * g_slot).astype(BF16))

    @pl.when(n_routed > 0)
    def _():
        run_slots(0, first_chunk)

    def body(c, carry):
        run_slots(first_chunk + c * chunk, chunk)
        return carry

    lax.fori_loop(0, (jnp.maximum(n_routed - first_chunk, 0) + chunk - 1) // chunk, body, 0)


def _moe(tok, rank, gate, cnt, wu, bu, wd, bd, tm, first_chunk, chunk):
    n = tok.shape[0]
    nt = n // tm
    grid_spec = pltpu.PrefetchScalarGridSpec(
        num_scalar_prefetch=1,
        grid=(nt, N_EXPERTS),
        in_specs=[pl.BlockSpec((tm, D_MODEL), lambda i, e, c: (i, 0)),
                  pl.BlockSpec((1, N_EXPERTS, tm), lambda i, e, c: (i, 0, 0)),
                  pl.BlockSpec((1, N_EXPERTS, tm), lambda i, e, c: (i, 0, 0)),
                  pl.BlockSpec((1, D_MODEL, 2 * D_FF), lambda i, e, c: (e, 0, 0)),
                  pl.BlockSpec((1, 1, 2 * D_FF), lambda i, e, c: (e, 0, 0)),
                  pl.BlockSpec((1, D_FF, D_MODEL), lambda i, e, c: (e, 0, 0)),
                  pl.BlockSpec((1, 1, D_MODEL), lambda i, e, c: (e, 0, 0))],
        out_specs=pl.BlockSpec((tm, D_MODEL), lambda i, e, c: (i, 0)),
    )
    return pl.pallas_call(
        functools.partial(_moe_kernel, first_chunk=first_chunk, chunk=chunk),
        grid_spec=grid_spec,
        out_shape=jax.ShapeDtypeStruct((n, D_MODEL), F32),
        compiler_params=_params("parallel", "arbitrary"),
        name="moe",
    )(cnt, tok, rank, gate, wu, bu, wd, bd)


def _final_kernel(h_ref, f_ref, g_ref, y_ref):
    y_ref[...] = _rms(h_ref[...] + f_ref[...], g_ref[...])


def _final(h2, ffn, g, tm):
    n = h2.shape[0]
    row = lambda i: (i, 0)
    return pl.pallas_call(
        _final_kernel,
        grid=(n // tm,),
        in_specs=[pl.BlockSpec((tm, D_MODEL), row), pl.BlockSpec((tm, D_MODEL), row),
                  pl.BlockSpec((1, D_MODEL), lambda i: (0, 0))],
        out_specs=pl.BlockSpec((tm, D_MODEL), row),
        out_shape=jax.ShapeDtypeStruct((n, D_MODEL), F32),
        compiler_params=_params("parallel"),
        name="final_norm",
    )(h2, ffn, g)


def _page_copy(cache_ref, buf_ref, sem_ref, pt_ref, b, p, slot):
    return pltpu.make_async_copy(cache_ref.at[0, pt_ref[b, p]], buf_ref.at[slot, p], sem_ref.at[slot])


def _fetch_pages(cache_ref, buf_ref, sem_ref, pt_ref, b, slot, n_pages):
    def body(p, c):
        _page_copy(cache_ref, buf_ref, sem_ref, pt_ref, b, p, slot).start()
        return c
    lax.fori_loop(0, n_pages, body, 0)


def _wait_pages(cache_ref, buf_ref, sem_ref, pt_ref, b, slot, n_pages):
    def body(p, c):
        _page_copy(cache_ref, buf_ref, sem_ref, pt_ref, b, p, slot).wait()
        return c
    lax.fori_loop(0, n_pages, body, 0)


def _dsa_sample_mask_kernel(pt_ref, qi_ref, wi_ref, kin_ref, cache_ref, mask_ref, new_ref,
                            buf_ref, sem_ref, *, topk):
    b = pl.program_id(0)
    nb = pl.num_programs(0)
    n_pages = buf_ref.shape[1]
    past = n_pages * PAGE_SIZE
    slot = b % 2

    @pl.when(b == 0)
    def _():
        _fetch_pages(cache_ref, buf_ref, sem_ref, pt_ref, b, slot, n_pages)

    @pl.when(b + 1 < nb)
    def _():
        _fetch_pages(cache_ref, buf_ref, sem_ref, pt_ref, b + 1, 1 - slot, n_pages)

    _wait_pages(cache_ref, buf_ref, sem_ref, pt_ref, b, slot, n_pages)

    qi = qi_ref[0]
    w = wi_ref[0] * (IDX_HEADS ** -0.5)
    keys = buf_ref[slot].reshape(past, IDX_DIM).astype(BF16)
    score = jnp.sum(jnp.maximum(_dot_nt(qi, keys), 0.0) * w, axis=0, keepdims=True)
    s_new = jnp.sum(qi.astype(F32) * kin_ref[0].astype(F32), axis=1, keepdims=True)
    score_new = jnp.sum(jnp.maximum(s_new, 0.0) * w, axis=0, keepdims=True)

    def count_ge(cand):
        return (jnp.sum(jnp.where(score >= cand, 1, 0), axis=1, keepdims=True)
                + jnp.where(score_new >= cand, 1, 0))

    lo0 = _f32_to_key(jnp.minimum(jnp.min(score, axis=1, keepdims=True), score_new))
    hi0 = _f32_to_key(jnp.maximum(jnp.max(score, axis=1, keepdims=True), score_new)) + 1
    lo, c_lo, c_hi = _search_kth(count_ge, topk, lo0, hi0,
                                 jnp.full((1, 1), past + 1, jnp.int32), jnp.zeros((1, 1), jnp.int32))
    tau = _key_to_f32(lo)
    gt = score > tau
    eq = score == tau
    tied = c_lo > topk
    need_f = jnp.where(tied, topk - c_hi, past + 2).astype(F32)
    take_eq, n_eq = _take_first_ties(eq, need_f, tied)
    mask_ref[0] = jnp.where(jnp.logical_or(gt, take_eq > 0.5), 0.0, NEG)
    take_new = jnp.logical_or(score_new > tau, jnp.logical_and(score_new == tau, n_eq < need_f))
    new_ref[0] = jnp.where(take_new, 0.0, NEG)


def _dsa_sample_mask(page_table, qi_heads, wi_col, ki_new, cache_idx_k):
    db, n_pages = page_table.shape
    past = n_pages * PAGE_SIZE
    topk = min(TOPK_MAX, (past + 1) // 4)
    per = lambda b, pt: (b, 0, 0)
    grid_spec = pltpu.PrefetchScalarGridSpec(
        num_scalar_prefetch=1,
        grid=(db,),
        in_specs=[pl.BlockSpec((1, IDX_HEADS, IDX_DIM), per),
                  pl.BlockSpec((1, IDX_HEADS, 1), per),
                  pl.BlockSpec((1, 1, IDX_DIM), per),
                  pl.BlockSpec(memory_space=pl.ANY)],
        out_specs=[pl.BlockSpec((1, 1, past), per), pl.BlockSpec((1, 1, 1), per)],
        scratch_shapes=[pltpu.VMEM((2, n_pages, PAGE_SIZE, IDX_DIM), F32),
                        pltpu.SemaphoreType.DMA((2,))],
    )
    return pl.pallas_call(
        functools.partial(_dsa_sample_mask_kernel, topk=topk),
        grid_spec=grid_spec,
        out_shape=[jax.ShapeDtypeStruct((db, 1, past), F32), jax.ShapeDtypeStruct((db, 1, 1), F32)],
        compiler_params=_params("arbitrary"),
        name="dsa_sample_mask",
    )(page_table, qi_heads, wi_col, ki_new, cache_idx_k)


def _dsa_sample_attend_kernel(pt_ref, q_ref, kn_ref, vn_ref, mask_ref, new_ref, ck_ref, cv_ref, o_ref,
                              kbuf_ref, vbuf_ref, ksem_ref, vsem_ref):
    b = pl.program_id(0)
    nb = pl.num_programs(0)
    n_pages = kbuf_ref.shape[1]
    past = n_pages * PAGE_SIZE
    slot = b % 2

    def fetch(bb, sl):
        _fetch_pages(ck_ref, kbuf_ref, ksem_ref, pt_ref, bb, sl, n_pages)
        _fetch_pages(cv_ref, vbuf_ref, vsem_ref, pt_ref, bb, sl, n_pages)

    @pl.when(b == 0)
    def _():
        fetch(b, slot)

    @pl.when(b + 1 < nb)
    def _():
        fetch(b + 1, 1 - slot)

    _wait_pages(ck_ref, kbuf_ref, ksem_ref, pt_ref, b, slot, n_pages)
    _wait_pages(cv_ref, vbuf_ref, vsem_ref, pt_ref, b, slot, n_pages)

    q = q_ref[0]
    keys = kbuf_ref[slot].reshape(past, HEAD_DIM).astype(BF16)
    vals = vbuf_ref[slot].reshape(past, HEAD_DIM).astype(BF16)
    logits = _dot_nt(q, keys) + mask_ref[0]
    l_new = jnp.sum(q.astype(F32) * kn_ref[0].astype(F32), axis=1, keepdims=True) + new_ref[0]
    m = jnp.maximum(jnp.max(logits, axis=1, keepdims=True), l_new)
    p = jnp.exp(logits - m)
    p_new = jnp.exp(l_new - m)
    den = jnp.sum(p, axis=1, keepdims=True) + p_new
    p = p / den
    p_new = p_new / den
    o_ref[0] = _dot(p.astype(BF16), vals) + p_new * vn_ref[0].astype(F32)


def _dsa_sample_attend(page_table, q_heads, k_new, v_new, mask, take_new, cache_k, cache_v):
    db, n_pages = page_table.shape
    past = n_pages * PAGE_SIZE
    per = lambda b, pt: (b, 0, 0)
    grid_spec = pltpu.PrefetchScalarGridSpec(
        num_scalar_prefetch=1,
        grid=(db,),
        in_specs=[pl.BlockSpec((1, ATTN_HEADS, HEAD_DIM), per),
                  pl.BlockSpec((1, 1, HEAD_DIM), per),
                  pl.BlockSpec((1, 1, HEAD_DIM), per),
                  pl.BlockSpec((1, 1, past), per),
                  pl.BlockSpec((1, 1, 1), per),
                  pl.BlockSpec(memory_space=pl.ANY),
                  pl.BlockSpec(memory_space=pl.ANY)],
        out_specs=pl.BlockSpec((1, ATTN_HEADS, HEAD_DIM), per),
        scratch_shapes=[pltpu.VMEM((2, n_pages, PAGE_SIZE, HEAD_DIM), F32),
                        pltpu.VMEM((2, n_pages, PAGE_SIZE, HEAD_DIM), F32),
                        pltpu.SemaphoreType.DMA((2,)),
                        pltpu.SemaphoreType.DMA((2,))],
    )
    return pl.pallas_call(
        _dsa_sample_attend_kernel,
        grid_spec=grid_spec,
        out_shape=jax.ShapeDtypeStruct((db, ATTN_HEADS, HEAD_DIM), F32),
        compiler_params=_params("arbitrary"),
        name="dsa_sample_attend",
    )(page_table, q_heads, k_new, v_new, mask, take_new, cache_k, cache_v)


def _mid_sample_a_kernel(x_ref, attn_ref, u_ref, st_ref, pw_ref, ps_ref, wo_ref, gm_ref, wmq_ref,
                         h1_ref, qm_ref, *, past):
    u = u_ref[...]
    rows = u.shape[0]
    sums, cnts = [], []
    for g, w in enumerate(POOL_WINDOWS):
        sl = slice(g * POOL_CG, (g + 1) * POOL_CG)
        s = u[:, sl]
        for j in range(1, w):
            s = s + st_ref[POOL_MAX - 1 - j][:, sl]
        sums.append(s)
        cnts.append(jnp.full((rows, 1), float(min(w, past + 1)), F32))
    pool = _pool_project(sums, u, cnts, pw_ref, ps_ref)
    h1 = x_ref[...] + _dot(attn_ref[...], wo_ref[0:ATTN_WIDTH, :]) \
        + _dot(pool.astype(BF16), wo_ref[ATTN_WIDTH:, :])
    h1_ref[...] = h1
    qm_ref[...] = _dot(_rms(h1, gm_ref[...]).astype(BF16), wmq_ref[...]).astype(BF16)


def _mid_sample_a(x, attn, u, state_t, pool_w, pool_scale, w_o, g_mem, w_mq, past):
    db = x.shape[0]
    full = lambda a: pl.BlockSpec(a.shape, lambda i, nd=a.ndim: (0,) * nd)
    args = (x, attn, u, state_t, pool_w, pool_scale, w_o, g_mem, w_mq)
    return pl.pallas_call(
        functools.partial(_mid_sample_a_kernel, past=past),
        grid=(1,),
        in_specs=[full(a) for a in args],
        out_specs=[pl.BlockSpec((db, D_MODEL), lambda i: (0, 0)),
                   pl.BlockSpec((db, MEM_WIDTH), lambda i: (0, 0))],
        out_shape=[jax.ShapeDtypeStruct((db, D_MODEL), F32), jax.ShapeDtypeStruct((db, MEM_WIDTH), BF16)],
        compiler_params=_params("arbitrary"),
        name="mid_sample_a",
    )(*args)


def _mem_sample_kernel(q_ref, mk_ref, mv_ref, o_ref):
    rows = 8
    head_of_row = lax.broadcasted_iota(jnp.int32, (rows, MEM_WIDTH), 0)
    head_of_lane = lax.broadcasted_iota(jnp.int32, (rows, MEM_WIDTH), 1) // MEM_HEAD_DIM
    own = head_of_row == head_of_lane
    q = jnp.broadcast_to(q_ref[0].astype(F32), (rows, MEM_WIDTH))
    qh = jnp.where(own, q, 0.0).astype(BF16)
    logits = _dot_nt(qh, mk_ref[0].astype(BF16)) * (MEM_HEAD_DIM ** -0.5)
    m = jnp.max(logits, axis=1, keepdims=True)
    p = jnp.exp(logits - m)
    p = p / jnp.sum(p, axis=1, keepdims=True)
    o = _dot(p.astype(BF16), mv_ref[0].astype(BF16))
    o_ref[0] = jnp.sum(jnp.where(own, o, 0.0), axis=0, keepdims=True)


def _mem_sample(qm, mk, mv):
    db = qm.shape[0]
    per = lambda b: (b, 0, 0)
    return pl.pallas_call(
        _mem_sample_kernel,
        grid=(db,),
        in_specs=[pl.BlockSpec((1, 1, MEM_WIDTH), per),
                  pl.BlockSpec((1, N_MEM, MEM_WIDTH), per),
                  pl.BlockSpec((1, N_MEM, MEM_WIDTH), per)],
        out_specs=pl.BlockSpec((1, 1, MEM_WIDTH), per),
        out_shape=jax.ShapeDtypeStruct((db, 1, MEM_WIDTH), F32),
        compiler_params=_params("parallel"),
        name="mem_sample",
    )(qm.reshape(db, 1, MEM_WIDTH), mk, mv)


def _tail_sample_kernel(h1_ref, o_ref, wmo_ref, gf_ref, rw_ref, rb_ref, wu_ref, bu_ref, wd_ref, bd_ref,
                        gfin_ref, y_ref, h2_ref, tok_ref, comb_ref, acc_ref):
    e = pl.program_id(0)
    rows = h1_ref.shape[0]

    @pl.when(e == 0)
    def _():
        h2 = h1_ref[...] + _dot(o_ref[...].astype(BF16), wmo_ref[...])
        tok = _rms(h2, gf_ref[...]).astype(BF16)
        h2_ref[...] = h2
        tok_ref[...] = tok
        logits = _dot(tok, rw_ref[...]) + rb_ref[...]
        eid = lax.broadcasted_iota(jnp.int32, (rows, N_EXPERTS), 1)
        vals, hots = [], []
        cur = logits
        for _ in range(TOP_K):
            m = jnp.max(cur, axis=1, keepdims=True)
            idx = jnp.min(jnp.where(cur == m, eid, N_EXPERTS), axis=1, keepdims=True)
            hot = eid == idx
            vals.append(m)
            hots.append(hot)
            cur = jnp.where(hot, -jnp.inf, cur)
        exps = [jnp.exp(v - vals[0]) for v in vals]
        den = exps[0] + exps[1] + exps[2] + exps[3]
        comb = jnp.zeros((rows, N_EXPERTS), F32)
        for hot, ex in zip(hots, exps):
            comb = jnp.where(hot, ex / den, comb)
        comb_ref[...] = comb
        acc_ref[...] = jnp.zeros_like(acc_ref)

    act = _swiglu(_dot(tok_ref[...], wu_ref[0]) + bu_ref[0])
    out = _dot(act.astype(BF16), wd_ref[0]) + bd_ref[0]
    eid = lax.broadcasted_iota(jnp.int32, (rows, N_EXPERTS), 1)
    c_e = jnp.sum(jnp.where(eid == e, comb_ref[...], 0.0), axis=1, keepdims=True)
    acc_ref[...] += c_e * out

    @pl.when(e == pl.num_programs(0) - 1)
    def _():
        y_ref[...] = _rms(h2_ref[...] + acc_ref[...], gfin_ref[...])


def _tail_sample(h1, o, w_mo, g_ffn, rw, rb, wu, bu, wd, bd, g_final):
    db = h1.shape[0]
    f2 = lambda e: (0, 0)
    pe = lambda e: (e, 0, 0)
    return pl.pallas_call(
        _tail_sample_kernel,
        grid=(N_EXPERTS,),
        in_specs=[pl.BlockSpec((db, D_MODEL), f2),
                  pl.BlockSpec((db, MEM_WIDTH), f2),
                  pl.BlockSpec((MEM_WIDTH, D_MODEL), f2),
                  pl.BlockSpec((1, D_MODEL), f2),
                  pl.BlockSpec((D_MODEL, N_EXPERTS), f2),
                  pl.BlockSpec((1, N_EXPERTS), f2),
                  pl.BlockSpec((1, D_MODEL, 2 * D_FF), pe),
                  pl.BlockSpec((1, 1, 2 * D_FF), pe),
                  pl.BlockSpec((1, D_FF, D_MODEL), pe),
                  pl.BlockSpec((1, 1, D_MODEL), pe),
                  pl.BlockSpec((1, D_MODEL), f2)],
        out_specs=pl.BlockSpec((db, D_MODEL), f2),
        out_shape=jax.ShapeDtypeStruct((db, D_MODEL), F32),
        scratch_shapes=[pltpu.VMEM((db, D_MODEL), F32), pltpu.VMEM((db, D_MODEL), BF16),
                        pltpu.VMEM((db, N_EXPERTS), F32), pltpu.VMEM((db, D_MODEL), F32)],
        compiler_params=_params("arbitrary"),
        name="tail_sample",
    )(h1, o, w_mo, g_ffn, rw, rb, wu, bu, wd, bd, g_final)


def _rope_tables(pos):
    half = HEAD_DIM // 2
    inv = ROPE_THETA ** (-jnp.arange(half, dtype=F32) / half)
    ang = pos.astype(F32)[:, None] * inv[None, :]
    cos = jnp.cos(ang)
    sin = jnp.sin(ang)
    return jnp.tile(cos, (1, 4)), jnp.concatenate([-sin, sin, -sin, sin], axis=1)


def _pick_tile(n, pref):
    t = min(n, pref)
    while n % t:
        t //= 2
    return t


def kernel(x_prompt, x_sample, mem_prompt, cache_k, cache_v, cache_idx_k, state_pool, cache_mem_k, cache_mem_v, page_table, norm_mix, w_in, pool_w, pool_scale, w_o, norm_mem, w_mq, w_mk, w_mv, w_mo, norm_ffn, router_w, router_b, w_up, b_up, w_down, b_down, norm_final):
    B, T = x_prompt.shape[:2]
    DB, DS = x_sample.shape[:2]
    assert DS == 1 and w_in.shape[0] == 1
    n_pages = page_table.shape[1]
    past = n_pages * PAGE_SIZE
    NP = B * T

    wi_ = w_in[0]
    wq, wk, wv, wu_, wqi, wki, wwi = jnp.split(wi_, [512, 576, 640, 1152, 1664, 1728], axis=1)
    w_all = jnp.concatenate(
        [wq, wqi, wu_, wk, wk, wki, wki, wv, wv, wwi, jnp.zeros((D_MODEL, LANES - IDX_HEADS), F32)],
        axis=1).astype(BF16)
    g_mix = norm_mix[0][None, :]
    g_mem = norm_mem[0][None, :]
    g_ffn = norm_ffn[0][None, :]
    g_fin = norm_final[None, :]
    pw = pool_w[0].astype(BF16)
    ps = pool_scale[0][None, :]
    wo = w_o[0].astype(BF16)
    wmq = w_mq[0].reshape(D_MODEL, MEM_WIDTH).astype(BF16)
    wkv = jnp.concatenate([w_mk[0].reshape(D_MODEL, MEM_WIDTH), w_mv[0].reshape(D_MODEL, MEM_WIDTH)],
                          axis=1).astype(BF16)
    wmo = w_mo[0].reshape(MEM_WIDTH, D_MODEL).astype(BF16)
    rw = router_w[0].astype(BF16)
    rb = router_b[0]
    wup = w_up[0].astype(BF16)
    wdn = w_down[0].astype(BF16)
    bup = b_up[0][:, None, :]
    bdn = b_down[0][:, None, :]

    xp = x_prompt.reshape(NP, D_MODEL)
    tm_in = _pick_tile(T, 512)
    cos_p, sin_p = _rope_tables(jnp.arange(T, dtype=jnp.int32))
    q, qi, u, kk, kiki, vv, k_p, ki_p, v_p, wi_p = _in_project(xp, g_mix, w_all, cos_p, sin_p, tm_in, T // tm_in)
    attn = _dsa_prompt(q, qi, wi_p, kk, kiki, vv, B, T)
    mk_p, mv_p = _mem_kv(mem_prompt.reshape(B * N_MEM, D_MODEL), wkv)
    h2_p, tok_p = _mid_prompt(xp, attn, u, mk_p.reshape(B, N_MEM, MEM_WIDTH), mv_p.reshape(B, N_MEM, MEM_WIDTH),
                              pw, ps, wo, g_mem, wmq, wmo, g_ffn, B, T, _pick_tile(T, 512))
    tm_moe = _pick_tile(NP, 2048)
    rank, gate, cnt = _router(tok_p, rw.T, rb[:, None], tm_moe)
    ffn_p = _moe(tok_p, rank, gate, cnt.reshape(-1), wup, bup, wdn, bdn, tm_moe, 256, 128)
    y_prompt = _final(h2_p, ffn_p, g_fin, _pick_tile(NP, 1024)).reshape(B, T, D_MODEL)

    xs = x_sample.reshape(DB, D_MODEL)
    cos_s, sin_s = _rope_tables(jnp.full((DB,), past, jnp.int32))
    q_s, qi_s, u_s, kk_s, kiki_s, vv_s, k_s, ki_s, v_s, wi_s = _in_project(xs, g_mix, w_all, cos_s, sin_s, DB, 1)
    mask, take_new = _dsa_sample_mask(
        page_table, qi_s.reshape(DB, IDX_HEADS, IDX_DIM), wi_s[:, :IDX_HEADS, None],
        kiki_s[:, None, :IDX_DIM], cache_idx_k)
    attn_s = _dsa_sample_attend(
        page_table, q_s.reshape(DB, ATTN_HEADS, HEAD_DIM), kk_s[:, None, :HEAD_DIM], vv_s[:, None, :HEAD_DIM],
        mask, take_new, cache_k, cache_v)
    attn_s = attn_s.reshape(DB, ATTN_WIDTH).astype(BF16)
    h1_s, qm_s = _mid_sample_a(xs, attn_s, u_s, jnp.transpose(state_pool[0], (1, 0, 2)), pw, ps, wo, g_mem, wmq, past)
    o_s = _mem_sample(qm_s, cache_mem_k[0].reshape(DB, N_MEM, MEM_WIDTH), cache_mem_v[0].reshape(DB, N_MEM, MEM_WIDTH))
    y_sample = _tail_sample(h1_s, o_s.reshape(DB, MEM_WIDTH), wmo, g_ffn, rw, rb[None, :], wup, bup, wdn, bdn,
                            g_fin).reshape(DB, 1, D_MODEL)

    new_pool_p = u.reshape(B, T, POOL_WIDTH)[:, T - (POOL_MAX - 1):][None]
    new_pool_s = jnp.concatenate([state_pool[0][:, 1:], u_s[:, None, :]], axis=1)[None]
    return (y_prompt, y_sample,
            k_p.reshape(1, B, T, HEAD_DIM), v_p.reshape(1, B, T, HEAD_DIM), ki_p.reshape(1, B, T, IDX_DIM),
            new_pool_p,
            mk_p.reshape(1, B, N_MEM, MEM_HEADS, MEM_HEAD_DIM), mv_p.reshape(1, B, N_MEM, MEM_HEADS, MEM_HEAD_DIM),
            k_s.reshape(1, DB, 1, HEAD_DIM), v_s.reshape(1, DB, 1, HEAD_DIM), ki_s.reshape(1, DB, 1, IDX_DIM),
            new_pool_s)
```

```python
import functools

import jax
import jax.numpy as jnp
from jax import lax
from jax.experimental import pallas as pl
from jax.experimental.pallas import tpu as pltpu

D_MODEL = 1024
HEAD_DIM = 64
ATTN_WIDTH = 512
ATTN_HEADS = 8
IDX_HEADS = 8
IDX_DIM = 64
TOPK_MAX = 256
Q_BLOCK = 128
PAGE_SIZE = 128
POOL_WIDTH = 512
POOL_WINDOWS = (2, 4, 8, 16)
POOL_CG = 128
POOL_MAX = 16
N_MEM = 256
MEM_HEADS = 4
MEM_HEAD_DIM = 128
MEM_WIDTH = MEM_HEADS * MEM_HEAD_DIM
N_EXPERTS = 32
TOP_K = 4
D_FF = 1024
SWIGLU_LIMIT = 7.0
SWIGLU_ALPHA = 1.702
ROPE_THETA = 10000.0
EPS = 1e-6
NEG = -1e30

LANES = 128
IN_COLS = 3 * 512 + 4 * LANES
VMEM_LIMIT = 56 * 1024 * 1024

BF16 = jnp.bfloat16
F32 = jnp.float32


def _dot(a, b):
    return jnp.dot(a, b, preferred_element_type=F32)


def _dot_nt(a, b):
    return lax.dot_general(a, b, (((1,), (1,)), ((), ())), preferred_element_type=F32)


def _dot_tn(a, b):
    return lax.dot_general(a, b, (((0,), (0,)), ((), ())), preferred_element_type=F32)


def _rms(x, g):
    return x * lax.rsqrt(jnp.mean(x * x, axis=-1, keepdims=True) + EPS) * g


def _params(*sem):
    return pltpu.CompilerParams(dimension_semantics=sem, vmem_limit_bytes=VMEM_LIMIT)


def _in_kernel(x_ref, g_ref, w_ref, cos_ref, sin_ref,
               q_ref, qi_ref, u_ref, kk_ref, kiki_ref, vv_ref, k_ref, ki_ref, v_ref, wi_ref):
    xn = _rms(x_ref[...], g_ref[...]).astype(BF16)
    z = _dot(xn, w_ref[...])
    cos = cos_ref[...]
    sin = sin_ref[...]
    lane = lax.broadcasted_iota(jnp.int32, cos.shape, 1)
    first_half = (lane % HEAD_DIM) < (HEAD_DIM // 2)

    def rope(zc):
        swapped = jnp.where(first_half,
                            pltpu.roll(zc, LANES - HEAD_DIM // 2, 1),
                            pltpu.roll(zc, HEAD_DIM // 2, 1))
        return zc * cos + swapped * sin

    for c in range(4):
        q_c = rope(z[:, c * LANES:(c + 1) * LANES]) * (HEAD_DIM ** -0.5)
        qi_c = rope(z[:, 512 + c * LANES:512 + (c + 1) * LANES]) * (IDX_DIM ** -0.5)
        q_ref[:, c * LANES:(c + 1) * LANES] = q_c.astype(BF16)
        qi_ref[:, c * LANES:(c + 1) * LANES] = qi_c.astype(BF16)
    u_ref[...] = z[:, 1024:1536]
    kk = rope(z[:, 1536:1664])
    kiki = rope(z[:, 1664:1792])
    vv = z[:, 1792:1920]
    kk_ref[...] = kk.astype(BF16)
    kiki_ref[...] = kiki.astype(BF16)
    vv_ref[...] = vv.astype(BF16)
    k_ref[...] = kk[:, :HEAD_DIM]
    ki_ref[...] = kiki[:, :IDX_DIM]
    v_ref[...] = vv[:, :HEAD_DIM]
    wi_ref[...] = z[:, 1920:2048]


def _in_project(x, g, w_all, cos, sin, tm, table_blocks):
    n = x.shape[0]
    row = lambda i: (i, 0)
    tab = lambda i: (i % table_blocks, 0)
    fixed = lambda i: (0, 0)
    outs = [
        jax.ShapeDtypeStruct((n, 512), BF16),
        jax.ShapeDtypeStruct((n, 512), BF16),
        jax.ShapeDtypeStruct((n, 512), F32),
        jax.ShapeDtypeStruct((n, LANES), BF16),
        jax.ShapeDtypeStruct((n, LANES), BF16),
        jax.ShapeDtypeStruct((n, LANES), BF16),
        jax.ShapeDtypeStruct((n, HEAD_DIM), F32),
        jax.ShapeDtypeStruct((n, IDX_DIM), F32),
        jax.ShapeDtypeStruct((n, HEAD_DIM), F32),
        jax.ShapeDtypeStruct((n, LANES), F32),
    ]
    out_specs = [pl.BlockSpec((tm, s.shape[1]), row) for s in outs]
    return pl.pallas_call(
        _in_kernel,
        grid=(n // tm,),
        in_specs=[pl.BlockSpec((tm, D_MODEL), row),
                  pl.BlockSpec((1, D_MODEL), fixed),
                  pl.BlockSpec((D_MODEL, IN_COLS), fixed),
                  pl.BlockSpec((tm, LANES), tab),
                  pl.BlockSpec((tm, LANES), tab)],
        out_specs=out_specs,
        out_shape=outs,
        compiler_params=_params("parallel"),
        name="in_project",
    )(x, g, w_all, cos, sin)


def _f32_to_key(x):
    bits = lax.bitcast_convert_type(x, jnp.int32)
    return jnp.where(bits < 0, bits ^ jnp.int32(0x7FFFFFFF), bits)


def _key_to_f32(key):
    bits = jnp.where(key < 0, key ^ jnp.int32(0x7FFFFFFF), key)
    return lax.bitcast_convert_type(bits, F32)


def _kth_largest(count_ge, k, shape):
    def body(it, tau):
        cand = tau + jnp.left_shift(jnp.int32(1), jnp.int32(31) - it)
        return jnp.where(count_ge(_key_to_f32(cand)) >= k, cand, tau)

    return _key_to_f32(lax.fori_loop(0, 32, body, jnp.full(shape, -2 ** 31, jnp.int32)))


def _strict_triangle(n, lower):
    r = lax.broadcasted_iota(jnp.int32, (n, n), 0)
    c = lax.broadcasted_iota(jnp.int32, (n, n), 1)
    return jnp.where((r > c) if lower else (r < c), 1.0, 0.0).astype(BF16)


def _take_first_ties(eq, need_f, tied, axis):
    n = eq.shape[axis]
    cshape = (1, eq.shape[1]) if axis == 0 else (eq.shape[0], 1)

    def ranked():
        tri = _strict_triangle(LANES, lower=(axis == 0))
        carry = jnp.zeros(cshape, F32)
        parts = []
        for c in range(n // LANES):
            sl = slice(c * LANES, (c + 1) * LANES)
            eq_f = jnp.where(eq[sl, :] if axis == 0 else eq[:, sl], 1.0, 0.0)
            eq_b = eq_f.astype(BF16)
            rank = (_dot(tri, eq_b) if axis == 0 else _dot(eq_b, tri)) + carry
            parts.append(jnp.where(rank < need_f, eq_f, 0.0))
            carry = carry + jnp.sum(eq_f, axis=axis, keepdims=True)
        return jnp.concatenate(parts, axis=axis), carry

    def plain():
        return jnp.where(eq, 1.0, 0.0), jnp.zeros(cshape, F32)

    return lax.cond(jnp.max(jnp.where(tied, 1, 0)) > 0, ranked, plain)


def _dsa_prompt_kernel(q_ref, qi_ref, wi_ref, kk_ref, kiki_ref, vvt_ref, o_ref, *, topk, first_block, select):
    i = first_block + pl.program_id(1)
    qb = q_ref.shape[0]
    lk = kk_ref.shape[1]
    kiki = kiki_ref[0]
    kk = kk_ref[0]
    vvt = vvt_ref[0]
    low = lax.broadcasted_iota(jnp.int32, (qb, LANES), 1) < HEAD_DIM

    def head_operand(ref, h):
        pair = ref[:, (h // 2) * LANES:(h // 2 + 1) * LANES]
        keep = low if h % 2 == 0 else jnp.logical_not(low)
        return jnp.where(keep, pair, jnp.zeros_like(pair))

    pos_q = i * qb + lax.broadcasted_iota(jnp.int32, (1, qb), 1)
    pos_k = lax.broadcasted_iota(jnp.int32, (lk, 1), 0)
    causal = pos_k <= pos_q

    if select:
        w_t = jnp.transpose(wi_ref[...])[0:IDX_HEADS, :] * (IDX_HEADS ** -0.5)
        score = jnp.zeros((lk, qb), F32)
        for h in range(IDX_HEADS):
            score = score + jnp.maximum(_dot_nt(kiki, head_operand(qi_ref, h)), 0.0) * w_t[h:h + 1, :]
        sm = jnp.where(causal, score, -jnp.inf)

        def count_ge(cand):
            return jnp.sum(jnp.where(sm >= cand, 1, 0), axis=0, keepdims=True)

        tau = _kth_largest(count_ge, topk, (1, qb))
        gt = sm > tau
        eq = sm == tau
        n_gt = jnp.sum(jnp.where(gt, 1, 0), axis=0, keepdims=True)
        n_eq = jnp.sum(jnp.where(eq, 1, 0), axis=0, keepdims=True)
        tied = n_gt + n_eq > topk
        need_f = jnp.where(tied, topk - n_gt, lk + 1).astype(F32)
        take_eq, _ = _take_first_ties(eq, need_f, tied, axis=0)
        mask = jnp.logical_or(gt, take_eq > 0.5)
    else:
        mask = causal
    bias = jnp.where(mask, 0.0, NEG)

    upper_rows = lax.broadcasted_iota(jnp.int32, (LANES, qb), 0) < HEAD_DIM
    for j in range(ATTN_HEADS // 2):
        outs = []
        for h in (2 * j, 2 * j + 1):
            logits = _dot_nt(kk, head_operand(q_ref, h)) + bias
            p = jnp.exp(logits - jnp.max(logits, axis=0, keepdims=True))
            inv = 1.0 / jnp.sum(p, axis=0, keepdims=True)
            outs.append(_dot(vvt, p.astype(BF16)) * inv)
        pair_t = jnp.where(upper_rows, outs[0], outs[1])
        o_ref[0, :, j * LANES:(j + 1) * LANES] = jnp.transpose(pair_t).astype(BF16)


DSA_Q = 256


def _dsa_prompt_group(q, qi, wi, kk, kiki, vvt, batch, t_len, first_block, n_blocks, topk):
    nb = t_len // DSA_Q
    lk = (first_block + n_blocks) * DSA_Q
    blk = lambda b, j: (b * nb + first_block + j, 0)
    return pl.pallas_call(
        functools.partial(_dsa_prompt_kernel, topk=topk, first_block=first_block, select=lk > topk),
        grid=(batch, n_blocks),
        in_specs=[pl.BlockSpec((DSA_Q, 512), blk),
                  pl.BlockSpec((DSA_Q, 512), blk),
                  pl.BlockSpec((DSA_Q, LANES), blk),
                  pl.BlockSpec((1, lk, LANES), lambda b, j: (b, 0, 0)),
                  pl.BlockSpec((1, lk, LANES), lambda b, j: (b, 0, 0)),
                  pl.BlockSpec((1, LANES, lk), lambda b, j: (b, 0, 0))],
        out_specs=pl.BlockSpec((1, DSA_Q, 512), lambda b, j: (b, j, 0)),
        out_shape=jax.ShapeDtypeStruct((batch, n_blocks * DSA_Q, 512), BF16),
        compiler_params=_params("parallel", "parallel"),
        name="dsa_prompt_%d" % lk,
    )(q, qi, wi, kk, kiki, vvt)


def _dsa_prompt(q, qi, wi, kk, kiki, vv, batch, t_len):
    nb = t_len // DSA_Q
    topk = min(TOPK_MAX, t_len // 4)
    assert topk % DSA_Q == 0 and t_len % DSA_Q == 0
    kk, kiki, vv = (a.reshape(batch, t_len, LANES) for a in (kk, kiki, vv))
    vvt = jnp.swapaxes(vv, 1, 2)
    edges = [0, topk // DSA_Q]
    while edges[-1] < nb:
        edges.append(min(edges[-1] + (1 if edges[-1] < 2 else 2), nb))
    parts = [_dsa_prompt_group(q, qi, wi, kk, kiki, vvt, batch, t_len, a, b - a, topk)
             for a, b in zip(edges[:-1], edges[1:]) if b > a]
    return jnp.concatenate(parts, axis=1).reshape(batch * t_len, 512)


def _mem_kv_kernel(m_ref, w_ref, k_ref, v_ref):
    z = _dot(m_ref[...].astype(BF16), w_ref[...])
    k_ref[...] = z[:, :MEM_WIDTH]
    v_ref[...] = z[:, MEM_WIDTH:]


def _mem_kv(mem, w_kv):
    n = mem.shape[0]
    tm = min(n, 512)
    row = lambda i: (i, 0)
    return pl.pallas_call(
        _mem_kv_kernel,
        grid=(n // tm,),
        in_specs=[pl.BlockSpec((tm, D_MODEL), row),
                  pl.BlockSpec((D_MODEL, 2 * MEM_WIDTH), lambda i: (0, 0))],
        out_specs=[pl.BlockSpec((tm, MEM_WIDTH), row), pl.BlockSpec((tm, MEM_WIDTH), row)],
        out_shape=[jax.ShapeDtypeStruct((n, MEM_WIDTH), F32)] * 2,
        compiler_params=_params("parallel"),
        name="mem_kv",
    )(mem, w_kv)


def _pool_project(window_sums, u, cnts, pw_ref, ps_ref):
    ys = []
    for g in range(len(POOL_WINDOWS)):
        sl = slice(g * POOL_CG, (g + 1) * POOL_CG)
        pooled = window_sums[g] / cnts[g] - u[:, sl]
        ys.append(_dot(pooled.astype(BF16), pw_ref[g]) * ps_ref[:, sl])
    return jnp.concatenate(ys, axis=1)


def _mid_prompt_kernel(x_ref, attn_ref, u_ref, halo_ref, mk_ref, mv_ref, pw_ref, ps_ref, wo_ref,
                       gm_ref, wmq_ref, wmo_ref, gf_ref, h2_ref, tok_ref, e_ref):
    i = pl.program_id(1)
    tm = x_ref.shape[0]
    u = u_ref[...]
    halo = halo_ref[...]
    e_ref[0:POOL_MAX, :] = jnp.where(i == 0, jnp.zeros_like(halo), halo)
    e_ref[POOL_MAX:POOL_MAX + tm, :] = u
    pos = i * tm + lax.broadcasted_iota(jnp.int32, (tm, 1), 0)
    sums, cnts = [], []
    for g, w in enumerate(POOL_WINDOWS):
        sl = slice(g * POOL_CG, (g + 1) * POOL_CG)
        s = u[:, sl]
        for j in range(1, w):
            s = s + e_ref[POOL_MAX - j:POOL_MAX - j + tm, sl]
        sums.append(s)
        cnts.append(jnp.minimum(w, pos + 1).astype(F32))
    pool = _pool_project(sums, u, cnts, pw_ref, ps_ref)
    h1 = x_ref[...] + _dot(attn_ref[...], wo_ref[0:ATTN_WIDTH, :]) \
        + _dot(pool.astype(BF16), wo_ref[ATTN_WIDTH:, :])

    qm = _dot(_rms(h1, gm_ref[...]).astype(BF16), wmq_ref[...]).astype(BF16)
    mk = mk_ref[0].astype(BF16)
    mv = mv_ref[0].astype(BF16)
    outs = []
    for h in range(MEM_HEADS):
        sl = slice(h * MEM_HEAD_DIM, (h + 1) * MEM_HEAD_DIM)
        logits = _dot_nt(qm[:, sl], mk[:, sl]) * (MEM_HEAD_DIM ** -0.5)
        m = jnp.max(logits, axis=1, keepdims=True)
        p = jnp.exp(logits - m)
        p = p / jnp.sum(p, axis=1, keepdims=True)
        outs.append(_dot(p.astype(BF16), mv[:, sl]))
    o = jnp.concatenate(outs, axis=1)
    h2 = h1 + _dot(o.astype(BF16), wmo_ref[...])
    h2_ref[...] = h2
    tok_ref[...] = _rms(h2, gf_ref[...]).astype(BF16)


def _mid_prompt(x, attn, u, mk, mv, pool_w, pool_scale, w_o, g_mem, w_mq, w_mo, g_ffn, batch, t_len, tm):
    nt = t_len // tm
    hb = tm // POOL_MAX
    row = lambda b, i: (b * nt + i, 0)
    halo = lambda b, i: (jnp.maximum((b * nt + i) * hb - 1, 0), 0)
    fixed2 = lambda b, i: (0, 0)
    fixed3 = lambda b, i: (0, 0, 0)
    perb = lambda b, i: (b, 0, 0)
    n = batch * t_len
    return pl.pallas_call(
        _mid_prompt_kernel,
        grid=(batch, nt),
        in_specs=[pl.BlockSpec((tm, D_MODEL), row),
                  pl.BlockSpec((tm, ATTN_WIDTH), row),
                  pl.BlockSpec((tm, POOL_WIDTH), row),
                  pl.BlockSpec((POOL_MAX, POOL_WIDTH), halo),
                  pl.BlockSpec((1, N_MEM, MEM_WIDTH), perb),
                  pl.BlockSpec((1, N_MEM, MEM_WIDTH), perb),
                  pl.BlockSpec((len(POOL_WINDOWS), POOL_CG, POOL_CG), fixed3),
                  pl.BlockSpec((1, POOL_WIDTH), fixed2),
                  pl.BlockSpec((D_MODEL, D_MODEL), fixed2),
                  pl.BlockSpec((1, D_MODEL), fixed2),
                  pl.BlockSpec((D_MODEL, MEM_WIDTH), fixed2),
                  pl.BlockSpec((MEM_WIDTH, D_MODEL), fixed2),
                  pl.BlockSpec((1, D_MODEL), fixed2)],
        out_specs=[pl.BlockSpec((tm, D_MODEL), row), pl.BlockSpec((tm, D_MODEL), row)],
        out_shape=[jax.ShapeDtypeStruct((n, D_MODEL), F32), jax.ShapeDtypeStruct((n, D_MODEL), BF16)],
        scratch_shapes=[pltpu.VMEM((POOL_MAX + tm, POOL_WIDTH), F32)],
        compiler_params=_params("parallel", "arbitrary"),
        name="mid_prompt",
    )(x, attn, u, u, mk, mv, pool_w, pool_scale, w_o, g_mem, w_mq, w_mo, g_ffn)


def _router_kernel(tok_ref, rw_ref, rb_ref, rank_ref, gate_ref, cnt_ref):
    tm = tok_ref.shape[0]
    logits = _dot_nt(rw_ref[...], tok_ref[...]) + rb_ref[...]
    eid = lax.broadcasted_iota(jnp.int32, (N_EXPERTS, tm), 0)
    vals, hots = [], []
    cur = logits
    for _ in range(TOP_K):
        m = jnp.max(cur, axis=0, keepdims=True)
        idx = jnp.min(jnp.where(cur == m, eid, N_EXPERTS), axis=0, keepdims=True)
        hot = eid == idx
        vals.append(m)
        hots.append(hot)
        cur = jnp.where(hot, -jnp.inf, cur)
    exps = [jnp.exp(v - vals[0]) for v in vals]
    den = exps[0] + exps[1] + exps[2] + exps[3]
    gate = jnp.zeros((N_EXPERTS, tm), F32)
    sel = jnp.zeros((N_EXPERTS, tm), jnp.bool_)
    for hot, ex in zip(hots, exps):
        gate = jnp.where(hot, ex / den, gate)
        sel = jnp.logical_or(sel, hot)
    gate_ref[0] = gate

    upper = _strict_triangle(LANES, lower=False)
    carry = jnp.zeros((N_EXPERTS, 1), F32)
    for c in range(tm // LANES):
        sl = slice(c * LANES, (c + 1) * LANES)
        sel_c = sel[:, sl]
        sel_b = jnp.where(sel_c, 1.0, 0.0).astype(BF16)
        rank = _dot(sel_b, upper) + carry
        rank_ref[0, :, sl] = jnp.where(sel_c, rank, -1.0).astype(jnp.int32)
        carry = carry + jnp.sum(sel_b.astype(F32), axis=1, keepdims=True)
    cnt_ref[0] = carry.astype(jnp.int32)


def _router(tok, rw_t, rb, tm):
    n = tok.shape[0]
    nt = n // tm
    t3 = lambda i: (i, 0, 0)
    return pl.pallas_call(
        _router_kernel,
        grid=(nt,),
        in_specs=[pl.BlockSpec((tm, D_MODEL), lambda i: (i, 0)),
                  pl.BlockSpec((N_EXPERTS, D_MODEL), lambda i: (0, 0)),
                  pl.BlockSpec((N_EXPERTS, 1), lambda i: (0, 0))],
        out_specs=[pl.BlockSpec((1, N_EXPERTS, tm), t3),
                   pl.BlockSpec((1, N_EXPERTS, tm), t3),
                   pl.BlockSpec((1, N_EXPERTS, 1), t3)],
        out_shape=[jax.ShapeDtypeStruct((nt, N_EXPERTS, tm), jnp.int32),
                   jax.ShapeDtypeStruct((nt, N_EXPERTS, tm), F32),
                   jax.ShapeDtypeStruct((nt, N_EXPERTS, 1), jnp.int32)],
        compiler_params=_params("parallel"),
        name="router",
    )(tok, rw_t, rb)


def _swiglu(h):
    gate = jnp.minimum(h[:, :D_FF], SWIGLU_LIMIT)
    up = jnp.clip(h[:, D_FF:], -SWIGLU_LIMIT, SWIGLU_LIMIT)
    return gate * jax.nn.sigmoid(SWIGLU_ALPHA * gate) * (up + 1.0)


def _moe_kernel(cnt_ref, x_ref, rank_ref, gate_ref, wu_ref, bu_ref, wd_ref, bd_ref, o_ref, *,
                first_chunk, chunk):
    i = pl.program_id(0)
    e = pl.program_id(1)
    tm = x_ref.shape[0]

    @pl.when(e == 0)
    def _():
        o_ref[...] = jnp.zeros_like(o_ref)

    n_routed = cnt_ref[i * N_EXPERTS + e]
    rank = rank_ref[0, pl.ds(e, 1), :]
    gate = gate_ref[0, pl.ds(e, 1), :]

    def run_slots(first, size):
        slot = first + lax.broadcasted_iota(jnp.int32, (size, tm), 0)
        hit = rank == slot
        onehot = jnp.where(hit, 1.0, 0.0).astype(BF16)
        xg = _dot(onehot, x_ref[...]).astype(BF16)
        act = _swiglu(_dot(xg, wu_ref[0]) + bu_ref[0])
        out = _dot(act.astype(BF16), wd_ref[0]) + bd_ref[0]
        g_slot = jnp.sum(jnp.where(hit, gate, 0.0), axis=1, keepdims=True)
        o_ref[...] += _dot_tn(onehot, (out * g_slot).astype(BF16))

    @pl.when(n_routed > 0)
    def _():
        run_slots(0, first_chunk)

    def body(c, carry):
        run_slots(first_chunk + c * chunk, chunk)
        return carry

    lax.fori_loop(0, (jnp.maximum(n_routed - first_chunk, 0) + chunk - 1) // chunk, body, 0)


def _moe(tok, rank, gate, cnt, wu, bu, wd, bd, tm, first_chunk, chunk):
    n = tok.shape[0]
    nt = n // tm
    grid_spec = pltpu.PrefetchScalarGridSpec(
        num_scalar_prefetch=1,
        grid=(nt, N_EXPERTS),
        in_specs=[pl.BlockSpec((tm, D_MODEL), lambda i, e, c: (i, 0)),
                  pl.BlockSpec((1, N_EXPERTS, tm), lambda i, e, c: (i, 0, 0)),
                  pl.BlockSpec((1, N_EXPERTS, tm), lambda i, e, c: (i, 0, 0)),
                  pl.BlockSpec((1, D_MODEL, 2 * D_FF), lambda i, e, c: (e, 0, 0)),
                  pl.BlockSpec((1, 1, 2 * D_FF), lambda i, e, c: (e, 0, 0)),
                  pl.BlockSpec((1, D_FF, D_MODEL), lambda i, e, c: (e, 0, 0)),
                  pl.BlockSpec((1, 1, D_MODEL), lambda i, e, c: (e, 0, 0))],
        out_specs=pl.BlockSpec((tm, D_MODEL), lambda i, e, c: (i, 0)),
    )
    return pl.pallas_call(
        functools.partial(_moe_kernel, first_chunk=first_chunk, chunk=chunk),
        grid_spec=grid_spec,
        out_shape=jax.ShapeDtypeStruct((n, D_MODEL), F32),
        compiler_params=_params("parallel", "arbitrary"),
        name="moe",
    )(cnt, tok, rank, gate, wu, bu, wd, bd)


def _final_kernel(h_ref, f_ref, g_ref, y_ref):
    y_ref[...] = _rms(h_ref[...] + f_ref[...], g_ref[...])


def _final(h2, ffn, g, tm):
    n = h2.shape[0]
    row = lambda i: (i, 0)
    return pl.pallas_call(
        _final_kernel,
        grid=(n // tm,),
        in_specs=[pl.BlockSpec((tm, D_MODEL), row), pl.BlockSpec((tm, D_MODEL), row),
                  pl.BlockSpec((1, D_MODEL), lambda i: (0, 0))],
        out_specs=pl.BlockSpec((tm, D_MODEL), row),
        out_shape=jax.ShapeDtypeStruct((n, D_MODEL), F32),
        compiler_params=_params("parallel"),
        name="final_norm",
    )(h2, ffn, g)


def _page_copy(cache_ref, buf_ref, sem_ref, pt_ref, b, p, slot):
    dst = buf_ref.at[slot, :, pl.ds(pl.multiple_of(p * PAGE_SIZE, PAGE_SIZE), PAGE_SIZE)]
    return pltpu.make_async_copy(cache_ref.at[0, pt_ref[b, p]], dst, sem_ref.at[slot])


def _fetch_pages(cache_ref, buf_ref, sem_ref, pt_ref, b, slot, n_pages):
    def body(p, c):
        _page_copy(cache_ref, buf_ref, sem_ref, pt_ref, b, p, slot).start()
        return c
    lax.fori_loop(0, n_pages, body, 0)


def _wait_pages(cache_ref, buf_ref, sem_ref, pt_ref, b, slot, n_pages):
    def body(p, c):
        _page_copy(cache_ref, buf_ref, sem_ref, pt_ref, b, p, slot).wait()
        return c
    lax.fori_loop(0, n_pages, body, 0)


def _dsa_sample_mask_kernel(pt_ref, qi_ref, wi_ref, kin_ref, cache_ref, mask_ref, new_ref,
                            buf_ref, sem_ref, *, topk):
    b = pl.program_id(0)
    nb = pl.num_programs(0)
    past = buf_ref.shape[2]
    n_pages = past // PAGE_SIZE
    slot = b % 2

    @pl.when(b == 0)
    def _():
        _fetch_pages(cache_ref, buf_ref, sem_ref, pt_ref, b, slot, n_pages)

    @pl.when(b + 1 < nb)
    def _():
        _fetch_pages(cache_ref, buf_ref, sem_ref, pt_ref, b + 1, 1 - slot, n_pages)

    _wait_pages(cache_ref, buf_ref, sem_ref, pt_ref, b, slot, n_pages)

    qi = qi_ref[0]
    w = wi_ref[0] * (IDX_HEADS ** -0.5)
    keys_t = buf_ref[slot].astype(BF16)
    score = jnp.sum(jnp.maximum(_dot(qi, keys_t), 0.0) * w, axis=0, keepdims=True)
    s_new = jnp.sum(qi.astype(F32) * kin_ref[0].astype(F32), axis=1, keepdims=True)
    score_new = jnp.sum(jnp.maximum(s_new, 0.0) * w, axis=0, keepdims=True)

    def count_ge(cand):
        return (jnp.sum(jnp.where(score >= cand, 1, 0), axis=1, keepdims=True)
                + jnp.where(score_new >= cand, 1, 0))

    tau = _kth_largest(count_ge, topk, (1, 1))
    gt = score > tau
    eq = score == tau
    n_gt = jnp.sum(jnp.where(gt, 1, 0), axis=1, keepdims=True) + jnp.where(score_new > tau, 1, 0)
    n_eq_all = jnp.sum(jnp.where(eq, 1, 0), axis=1, keepdims=True) + jnp.where(score_new == tau, 1, 0)
    tied = n_gt + n_eq_all > topk
    need_f = jnp.where(tied, topk - n_gt, past + 2).astype(F32)
    take_eq, n_eq = _take_first_ties(eq, need_f, tied, axis=1)
    mask_ref[0] = jnp.where(jnp.logical_or(gt, take_eq > 0.5), 0.0, NEG)
    take_new = jnp.logical_or(score_new > tau, jnp.logical_and(score_new == tau, n_eq < need_f))
    new_ref[0] = jnp.where(take_new, 0.0, NEG)


def _dsa_sample_mask(page_table, qi_heads, wi_col, ki_new, cache_idx_k):
    db, n_pages = page_table.shape
    past = n_pages * PAGE_SIZE
    topk = min(TOPK_MAX, (past + 1) // 4)
    per = lambda b, pt: (b, 0, 0)
    grid_spec = pltpu.PrefetchScalarGridSpec(
        num_scalar_prefetch=1,
        grid=(db,),
        in_specs=[pl.BlockSpec((1, IDX_HEADS, IDX_DIM), per),
                  pl.BlockSpec((1, IDX_HEADS, 1), per),
                  pl.BlockSpec((1, 1, IDX_DIM), per),
                  pl.BlockSpec(memory_space=pl.ANY)],
        out_specs=[pl.BlockSpec((1, 1, past), per), pl.BlockSpec((1, 1, 1), per)],
        scratch_shapes=[pltpu.VMEM((2, IDX_DIM, past), F32),
                        pltpu.SemaphoreType.DMA((2,))],
    )
    return pl.pallas_call(
        functools.partial(_dsa_sample_mask_kernel, topk=topk),
        grid_spec=grid_spec,
        out_shape=[jax.ShapeDtypeStruct((db, 1, past), F32), jax.ShapeDtypeStruct((db, 1, 1), F32)],
        compiler_params=_params("arbitrary"),
        name="dsa_sample_mask",
    )(page_table, qi_heads, wi_col, ki_new, cache_idx_k)


def _dsa_sample_attend_kernel(pt_ref, q_ref, kn_ref, vn_ref, mask_ref, new_ref, ck_ref, cv_ref, o_ref,
                              kbuf_ref, vbuf_ref, ksem_ref, vsem_ref):
    b = pl.program_id(0)
    nb = pl.num_programs(0)
    past = kbuf_ref.shape[2]
    n_pages = past // PAGE_SIZE
    slot = b % 2

    def fetch(bb, sl):
        _fetch_pages(ck_ref, kbuf_ref, ksem_ref, pt_ref, bb, sl, n_pages)
        _fetch_pages(cv_ref, vbuf_ref, vsem_ref, pt_ref, bb, sl, n_pages)

    @pl.when(b == 0)
    def _():
        fetch(b, slot)

    @pl.when(b + 1 < nb)
    def _():
        fetch(b + 1, 1 - slot)

    _wait_pages(ck_ref, kbuf_ref, ksem_ref, pt_ref, b, slot, n_pages)
    _wait_pages(cv_ref, vbuf_ref, vsem_ref, pt_ref, b, slot, n_pages)

    q = q_ref[0]
    keys_t = kbuf_ref[slot].astype(BF16)
    vals_t = vbuf_ref[slot].astype(BF16)
    logits = _dot(q, keys_t) + mask_ref[0]
    l_new = jnp.sum(q.astype(F32) * kn_ref[0].astype(F32), axis=1, keepdims=True) + new_ref[0]
    m = jnp.maximum(jnp.max(logits, axis=1, keepdims=True), l_new)
    p = jnp.exp(logits - m)
    p_new = jnp.exp(l_new - m)
    den = jnp.sum(p, axis=1, keepdims=True) + p_new
    p = p / den
    p_new = p_new / den
    o_ref[0] = _dot_nt(p.astype(BF16), vals_t) + p_new * vn_ref[0].astype(F32)


def _dsa_sample_attend(page_table, q_heads, k_new, v_new, mask, take_new, cache_k, cache_v):
    db, n_pages = page_table.shape
    past = n_pages * PAGE_SIZE
    per = lambda b, pt: (b, 0, 0)
    grid_spec = pltpu.PrefetchScalarGridSpec(
        num_scalar_prefetch=1,
        grid=(db,),
        in_specs=[pl.BlockSpec((1, ATTN_HEADS, HEAD_DIM), per),
                  pl.BlockSpec((1, 1, HEAD_DIM), per),
                  pl.BlockSpec((1, 1, HEAD_DIM), per),
                  pl.BlockSpec((1, 1, past), per),
                  pl.BlockSpec((1, 1, 1), per),
                  pl.BlockSpec(memory_space=pl.ANY),
                  pl.BlockSpec(memory_space=pl.ANY)],
        out_specs=pl.BlockSpec((1, ATTN_HEADS, HEAD_DIM), per),
        scratch_shapes=[pltpu.VMEM((2, HEAD_DIM, past), F32),
                        pltpu.VMEM((2, HEAD_DIM, past), F32),
                        pltpu.SemaphoreType.DMA((2,)),
                        pltpu.SemaphoreType.DMA((2,))],
    )
    return pl.pallas_call(
        _dsa_sample_attend_kernel,
        grid_spec=grid_spec,
        out_shape=jax.ShapeDtypeStruct((db, ATTN_HEADS, HEAD_DIM), F32),
        compiler_params=_params("arbitrary"),
        name="dsa_sample_attend",
    )(page_table, q_heads, k_new, v_new, mask, take_new, cache_k, cache_v)


def _mid_sample_a_kernel(x_ref, attn_ref, u_ref, st_ref, pw_ref, ps_ref, wo_ref, gm_ref, wmq_ref,
                         h1_ref, qm_ref, *, past):
    u = u_ref[...]
    rows = u.shape[0]
    sums, cnts = [], []
    for g, w in enumerate(POOL_WINDOWS):
        sl = slice(g * POOL_CG, (g + 1) * POOL_CG)
        s = u[:, sl]
        for j in range(1, w):
            s = s + st_ref[POOL_MAX - 1 - j][:, sl]
        sums.append(s)
        cnts.append(jnp.full((rows, 1), float(min(w, past + 1)), F32))
    pool = _pool_project(sums, u, cnts, pw_ref, ps_ref)
    h1 = x_ref[...] + _dot(attn_ref[...], wo_ref[0:ATTN_WIDTH, :]) \
        + _dot(pool.astype(BF16), wo_ref[ATTN_WIDTH:, :])
    h1_ref[...] = h1
    qm_ref[...] = _dot(_rms(h1, gm_ref[...]).astype(BF16), wmq_ref[...]).astype(BF16)


def _mid_sample_a(x, attn, u, state_t, pool_w, pool_scale, w_o, g_mem, w_mq, past):
    db = x.shape[0]
    full = lambda a: pl.BlockSpec(a.shape, lambda i, nd=a.ndim: (0,) * nd)
    args = (x, attn, u, state_t, pool_w, pool_scale, w_o, g_mem, w_mq)
    return pl.pallas_call(
        functools.partial(_mid_sample_a_kernel, past=past),
        grid=(1,),
        in_specs=[full(a) for a in args],
        out_specs=[pl.BlockSpec((db, D_MODEL), lambda i: (0, 0)),
                   pl.BlockSpec((db, MEM_WIDTH), lambda i: (0, 0))],
        out_shape=[jax.ShapeDtypeStruct((db, D_MODEL), F32), jax.ShapeDtypeStruct((db, MEM_WIDTH), BF16)],
        compiler_params=_params("arbitrary"),
        name="mid_sample_a",
    )(*args)


def _mem_sample_kernel(q_ref, mk_ref, mv_ref, o_ref):
    rows = 8
    head_of_row = lax.broadcasted_iota(jnp.int32, (rows, MEM_WIDTH), 0)
    head_of_lane = lax.broadcasted_iota(jnp.int32, (rows, MEM_WIDTH), 1) // MEM_HEAD_DIM
    own = head_of_row == head_of_lane
    q = jnp.broadcast_to(q_ref[0].astype(F32), (rows, MEM_WIDTH))
    qh = jnp.where(own, q, 0.0).astype(BF16)
    logits = _dot_nt(qh, mk_ref[0].astype(BF16)) * (MEM_HEAD_DIM ** -0.5)
    m = jnp.max(logits, axis=1, keepdims=True)
    p = jnp.exp(logits - m)
    p = p / jnp.sum(p, axis=1, keepdims=True)
    o = _dot(p.astype(BF16), mv_ref[0].astype(BF16))
    o_ref[0] = jnp.sum(jnp.where(own, o, 0.0), axis=0, keepdims=True)


def _mem_sample(qm, mk, mv):
    db = qm.shape[0]
    per = lambda b: (b, 0, 0)
    return pl.pallas_call(
        _mem_sample_kernel,
        grid=(db,),
        in_specs=[pl.BlockSpec((1, 1, MEM_WIDTH), per),
                  pl.BlockSpec((1, N_MEM, MEM_WIDTH), per),
                  pl.BlockSpec((1, N_MEM, MEM_WIDTH), per)],
        out_specs=pl.BlockSpec((1, 1, MEM_WIDTH), per),
        out_shape=jax.ShapeDtypeStruct((db, 1, MEM_WIDTH), F32),
        compiler_params=_params("parallel"),
        name="mem_sample",
    )(qm.reshape(db, 1, MEM_WIDTH), mk, mv)


def _tail_sample_kernel(h1_ref, o_ref, wmo_ref, gf_ref, rw_ref, rb_ref, wu_ref, bu_ref, wd_ref, bd_ref,
                        gfin_ref, y_ref, h2_ref, tok_ref, comb_ref, acc_ref):
    e = pl.program_id(0)
    rows = h1_ref.shape[0]

    @pl.when(e == 0)
    def _():
        h2 = h1_ref[...] + _dot(o_ref[...].astype(BF16), wmo_ref[...])
        tok = _rms(h2, gf_ref[...]).astype(BF16)
        h2_ref[...] = h2
        tok_ref[...] = tok
        logits = _dot(tok, rw_ref[...]) + rb_ref[...]
        eid = lax.broadcasted_iota(jnp.int32, (rows, N_EXPERTS), 1)
        vals, hots = [], []
        cur = logits
        for _ in range(TOP_K):
            m = jnp.max(cur, axis=1, keepdims=True)
            idx = jnp.min(jnp.where(cur == m, eid, N_EXPERTS), axis=1, keepdims=True)
            hot = eid == idx
            vals.append(m)
            hots.append(hot)
            cur = jnp.where(hot, -jnp.inf, cur)
        exps = [jnp.exp(v - vals[0]) for v in vals]
        den = exps[0] + exps[1] + exps[2] + exps[3]
        comb = jnp.zeros((rows, N_EXPERTS), F32)
        for hot, ex in zip(hots, exps):
            comb = jnp.where(hot, ex / den, comb)
        comb_ref[...] = comb
        acc_ref[...] = jnp.zeros_like(acc_ref)

    act = _swiglu(_dot(tok_ref[...], wu_ref[0]) + bu_ref[0])
    out = _dot(act.astype(BF16), wd_ref[0]) + bd_ref[0]
    eid = lax.broadcasted_iota(jnp.int32, (rows, N_EXPERTS), 1)
    c_e = jnp.sum(jnp.where(eid == e, comb_ref[...], 0.0), axis=1, keepdims=True)
    acc_ref[...] += c_e * out

    @pl.when(e == pl.num_programs(0) - 1)
    def _():
        y_ref[...] = _rms(h2_ref[...] + acc_ref[...], gfin_ref[...])


def _tail_sample(h1, o, w_mo, g_ffn, rw, rb, wu, bu, wd, bd, g_final):
    db = h1.shape[0]
    f2 = lambda e: (0, 0)
    pe = lambda e: (e, 0, 0)
    return pl.pallas_call(
        _tail_sample_kernel,
        grid=(N_EXPERTS,),
        in_specs=[pl.BlockSpec((db, D_MODEL), f2),
                  pl.BlockSpec((db, MEM_WIDTH), f2),
                  pl.BlockSpec((MEM_WIDTH, D_MODEL), f2),
                  pl.BlockSpec((1, D_MODEL), f2),
                  pl.BlockSpec((D_MODEL, N_EXPERTS), f2),
                  pl.BlockSpec((1, N_EXPERTS), f2),
                  pl.BlockSpec((1, D_MODEL, 2 * D_FF), pe),
                  pl.BlockSpec((1, 1, 2 * D_FF), pe),
                  pl.BlockSpec((1, D_FF, D_MODEL), pe),
                  pl.BlockSpec((1, 1, D_MODEL), pe),
                  pl.BlockSpec((1, D_MODEL), f2)],
        out_specs=pl.BlockSpec((db, D_MODEL), f2),
        out_shape=jax.ShapeDtypeStruct((db, D_MODEL), F32),
        scratch_shapes=[pltpu.VMEM((db, D_MODEL), F32), pltpu.VMEM((db, D_MODEL), BF16),
                        pltpu.VMEM((db, N_EXPERTS), F32), pltpu.VMEM((db, D_MODEL), F32)],
        compiler_params=_params("arbitrary"),
        name="tail_sample",
    )(h1, o, w_mo, g_ffn, rw, rb, wu, bu, wd, bd, g_final)


def _rope_tables(pos):
    half = HEAD_DIM // 2
    inv = ROPE_THETA ** (-jnp.arange(half, dtype=F32) / half)
    ang = pos.astype(F32)[:, None] * inv[None, :]
    cos = jnp.cos(ang)
    sin = jnp.sin(ang)
    return jnp.tile(cos, (1, 4)), jnp.concatenate([-sin, sin, -sin, sin], axis=1)


def _pick_tile(n, pref):
    t = min(n, pref)
    while n % t:
        t //= 2
    return t


def kernel(x_prompt, x_sample, mem_prompt, cache_k, cache_v, cache_idx_k, state_pool, cache_mem_k, cache_mem_v, page_table, norm_mix, w_in, pool_w, pool_scale, w_o, norm_mem, w_mq, w_mk, w_mv, w_mo, norm_ffn, router_w, router_b, w_up, b_up, w_down, b_down, norm_final):
    B, T = x_prompt.shape[:2]
    DB, DS = x_sample.shape[:2]
    assert DS == 1 and w_in.shape[0] == 1
    n_pages = page_table.shape[1]
    past = n_pages * PAGE_SIZE
    NP = B * T

    wi_ = w_in[0]
    wq, wk, wv, wu_, wqi, wki, wwi = jnp.split(wi_, [512, 576, 640, 1152, 1664, 1728], axis=1)
    w_all = jnp.concatenate(
        [wq, wqi, wu_, wk, wk, wki, wki, wv, wv, wwi, jnp.zeros((D_MODEL, LANES - IDX_HEADS), F32)],
        axis=1).astype(BF16)
    g_mix = norm_mix[0][None, :]
    g_mem = norm_mem[0][None, :]
    g_ffn = norm_ffn[0][None, :]
    g_fin = norm_final[None, :]
    pw = pool_w[0].astype(BF16)
    ps = pool_scale[0][None, :]
    wo = w_o[0].astype(BF16)
    wmq = w_mq[0].reshape(D_MODEL, MEM_WIDTH).astype(BF16)
    wkv = jnp.concatenate([w_mk[0].reshape(D_MODEL, MEM_WIDTH), w_mv[0].reshape(D_MODEL, MEM_WIDTH)],
                          axis=1).astype(BF16)
    wmo = w_mo[0].reshape(MEM_WIDTH, D_MODEL).astype(BF16)
    rw = router_w[0].astype(BF16)
    rb = router_b[0]
    wup = w_up[0].astype(BF16)
    wdn = w_down[0].astype(BF16)
    bup = b_up[0][:, None, :]
    bdn = b_down[0][:, None, :]

    xp = x_prompt.reshape(NP, D_MODEL)
    tm_in = _pick_tile(T, 512)
    cos_p, sin_p = _rope_tables(jnp.arange(T, dtype=jnp.int32))
    q, qi, u, kk, kiki, vv, k_p, ki_p, v_p, wi_p = _in_project(xp, g_mix, w_all, cos_p, sin_p, tm_in, T // tm_in)
    attn = _dsa_prompt(q, qi, wi_p, kk, kiki, vv, B, T)
    mk_p, mv_p = _mem_kv(mem_prompt.reshape(B * N_MEM, D_MODEL), wkv)
    h2_p, tok_p = _mid_prompt(xp, attn, u, mk_p.reshape(B, N_MEM, MEM_WIDTH), mv_p.reshape(B, N_MEM, MEM_WIDTH),
                              pw, ps, wo, g_mem, wmq, wmo, g_ffn, B, T, _pick_tile(T, 512))
    tm_moe = _pick_tile(NP, 2048)
    rank, gate, cnt = _router(tok_p, rw.T, rb[:, None], tm_moe)
    ffn_p = _moe(tok_p, rank, gate, cnt.reshape(-1), wup, bup, wdn, bdn, tm_moe, 256, 128)
    y_prompt = _final(h2_p, ffn_p, g_fin, _pick_tile(NP, 1024)).reshape(B, T, D_MODEL)

    xs = x_sample.reshape(DB, D_MODEL)
    cos_s, sin_s = _rope_tables(jnp.full((DB,), past, jnp.int32))
    q_s, qi_s, u_s, kk_s, kiki_s, vv_s, k_s, ki_s, v_s, wi_s = _in_project(xs, g_mix, w_all, cos_s, sin_s, DB, 1)
    mask, take_new = _dsa_sample_mask(
        page_table, qi_s.reshape(DB, IDX_HEADS, IDX_DIM), wi_s[:, :IDX_HEADS, None],
        kiki_s[:, None, :IDX_DIM], jnp.swapaxes(cache_idx_k, 2, 3))
    attn_s = _dsa_sample_attend(
        page_table, q_s.reshape(DB, ATTN_HEADS, HEAD_DIM), kk_s[:, None, :HEAD_DIM], vv_s[:, None, :HEAD_DIM],
        mask, take_new, jnp.swapaxes(cache_k, 2, 3), jnp.swapaxes(cache_v, 2, 3))
    attn_s = attn_s.reshape(DB, ATTN_WIDTH).astype(BF16)
    h1_s, qm_s = _mid_sample_a(xs, attn_s, u_s, jnp.transpose(state_pool[0], (1, 0, 2)), pw, ps, wo, g_mem, wmq, past)
    o_s = _mem_sample(qm_s, cache_mem_k[0].reshape(DB, N_MEM, MEM_WIDTH), cache_mem_v[0].reshape(DB, N_MEM, MEM_WIDTH))
    y_sample = _tail_sample(h1_s, o_s.reshape(DB, MEM_WIDTH), wmo, g_ffn, rw, rb[None, :], wup, bup, wdn, bdn,
                            g_fin).reshape(DB, 1, D_MODEL)

    new_pool_p = u.reshape(B, T, POOL_WIDTH)[:, T - (POOL_MAX - 1):][None]
    new_pool_s = jnp.concatenate([state_pool[0][:, 1:], u_s[:, None, :]], axis=1)[None]
    return (y_prompt, y_sample,
            k_p.reshape(1, B, T, HEAD_DIM), v_p.reshape(1, B, T, HEAD_DIM), ki_p.reshape(1, B, T, IDX_DIM),
            new_pool_p,
            mk_p.reshape(1, B, N_MEM, MEM_HEADS, MEM_HEAD_DIM), mv_p.reshape(1, B, N_MEM, MEM_HEADS, MEM_HEAD_DIM),
            k_s.reshape(1, DB, 1, HEAD_DIM), v_s.reshape(1, DB, 1, HEAD_DIM), ki_s.reshape(1, DB, 1, IDX_DIM),
            new_pool_s)
```

```python
import functools

import jax
import jax.numpy as jnp
from jax import lax
from jax.experimental import pallas as pl
from jax.experimental.pallas import tpu as pltpu

D_MODEL = 1024
HEAD_DIM = 64
ATTN_WIDTH = 512
ATTN_HEADS = 8
IDX_HEADS = 8
IDX_DIM = 64
TOPK_MAX = 256
Q_BLOCK = 128
PAGE_SIZE = 128
POOL_WIDTH = 512
POOL_WINDOWS = (2, 4, 8, 16)
POOL_CG = 128
POOL_MAX = 16
N_MEM = 256
MEM_HEADS = 4
MEM_HEAD_DIM = 128
MEM_WIDTH = MEM_HEADS * MEM_HEAD_DIM
N_EXPERTS = 32
TOP_K = 4
D_FF = 1024
SWIGLU_LIMIT = 7.0
SWIGLU_ALPHA = 1.702
ROPE_THETA = 10000.0
EPS = 1e-6
NEG = -1e30

LANES = 128
MOE_TILE = 13 * LANES
DSA_Q = 256
IN_COLS = 3 * 512 + 4 * LANES
VMEM_LIMIT = 56 * 1024 * 1024

BF16 = jnp.bfloat16
F32 = jnp.float32


def _dot(a, b):
    return jnp.dot(a, b, preferred_element_type=F32)


def _dot_nt(a, b):
    return lax.dot_general(a, b, (((1,), (1,)), ((), ())), preferred_element_type=F32)


def _dot_tn(a, b):
    return lax.dot_general(a, b, (((0,), (0,)), ((), ())), preferred_element_type=F32)


def _rms(x, g):
    return x * lax.rsqrt(jnp.mean(x * x, axis=-1, keepdims=True) + EPS) * g


def _params(*sem):
    return pltpu.CompilerParams(dimension_semantics=sem, vmem_limit_bytes=VMEM_LIMIT)


def _in_kernel(x_ref, g_ref, w_ref, cos_ref, sin_ref,
               q_ref, qi_ref, u_ref, kk_ref, kiki_ref, vv_ref, k_ref, ki_ref, v_ref, wi_ref):
    xn = _rms(x_ref[...], g_ref[...]).astype(BF16)
    z = _dot(xn, w_ref[...])
    cos = cos_ref[...]
    sin = sin_ref[...]
    lane = lax.broadcasted_iota(jnp.int32, cos.shape, 1)
    first_half = (lane % HEAD_DIM) < (HEAD_DIM // 2)

    def rope(zc):
        swapped = jnp.where(first_half,
                            pltpu.roll(zc, LANES - HEAD_DIM // 2, 1),
                            pltpu.roll(zc, HEAD_DIM // 2, 1))
        return zc * cos + swapped * sin

    for c in range(4):
        q_c = rope(z[:, c * LANES:(c + 1) * LANES]) * (HEAD_DIM ** -0.5)
        qi_c = rope(z[:, 512 + c * LANES:512 + (c + 1) * LANES]) * (IDX_DIM ** -0.5)
        q_ref[:, c * LANES:(c + 1) * LANES] = q_c.astype(BF16)
        qi_ref[:, c * LANES:(c + 1) * LANES] = qi_c.astype(BF16)
    u_ref[...] = z[:, 1024:1536]
    kk = rope(z[:, 1536:1664])
    kiki = rope(z[:, 1664:1792])
    vv = z[:, 1792:1920]
    kk_ref[...] = kk.astype(BF16)
    kiki_ref[...] = kiki.astype(BF16)
    vv_ref[...] = vv.astype(BF16)
    k_ref[...] = kk[:, :HEAD_DIM]
    ki_ref[...] = kiki[:, :IDX_DIM]
    v_ref[...] = vv[:, :HEAD_DIM]
    wi_ref[...] = z[:, 1920:2048]


def _in_project(x, g, w_all, cos, sin, tm, table_blocks):
    n = x.shape[0]
    row = lambda i: (i, 0)
    tab = lambda i: (i % table_blocks, 0)
    fixed = lambda i: (0, 0)
    outs = [
        jax.ShapeDtypeStruct((n, 512), BF16),
        jax.ShapeDtypeStruct((n, 512), BF16),
        jax.ShapeDtypeStruct((n, 512), F32),
        jax.ShapeDtypeStruct((n, LANES), BF16),
        jax.ShapeDtypeStruct((n, LANES), BF16),
        jax.ShapeDtypeStruct((n, LANES), BF16),
        jax.ShapeDtypeStruct((n, HEAD_DIM), F32),
        jax.ShapeDtypeStruct((n, IDX_DIM), F32),
        jax.ShapeDtypeStruct((n, HEAD_DIM), F32),
        jax.ShapeDtypeStruct((n, LANES), F32),
    ]
    out_specs = [pl.BlockSpec((tm, s.shape[1]), row) for s in outs]
    return pl.pallas_call(
        _in_kernel,
        grid=(n // tm,),
        in_specs=[pl.BlockSpec((tm, D_MODEL), row),
                  pl.BlockSpec((1, D_MODEL), fixed),
                  pl.BlockSpec((D_MODEL, IN_COLS), fixed),
                  pl.BlockSpec((tm, LANES), tab),
                  pl.BlockSpec((tm, LANES), tab)],
        out_specs=out_specs,
        out_shape=outs,
        compiler_params=_params("parallel"),
        name="in_project",
    )(x, g, w_all, cos, sin)


def _f32_to_key(x):
    bits = lax.bitcast_convert_type(x, jnp.int32)
    return jnp.where(bits < 0, bits ^ jnp.int32(0x7FFFFFFF), bits)


def _key_to_f32(key):
    bits = jnp.where(key < 0, key ^ jnp.int32(0x7FFFFFFF), key)
    return lax.bitcast_convert_type(bits, F32)


def _kth_largest(count_ge, k, shape):
    def body(it, tau):
        cand = tau + jnp.left_shift(jnp.int32(1), jnp.int32(31) - it)
        return jnp.where(count_ge(_key_to_f32(cand)) >= k, cand, tau)

    return _key_to_f32(lax.fori_loop(0, 32, body, jnp.full(shape, -2 ** 31, jnp.int32)))


def _strict_triangle(n, lower):
    r = lax.broadcasted_iota(jnp.int32, (n, n), 0)
    c = lax.broadcasted_iota(jnp.int32, (n, n), 1)
    return jnp.where((r > c) if lower else (r < c), 1.0, 0.0).astype(BF16)


def _take_first_ties(eq, need_f, tied, axis):
    n = eq.shape[axis]
    cshape = (1, eq.shape[1]) if axis == 0 else (eq.shape[0], 1)

    def ranked():
        tri = _strict_triangle(LANES, lower=(axis == 0))
        carry = jnp.zeros(cshape, F32)
        parts = []
        for c in range(n // LANES):
            sl = slice(c * LANES, (c + 1) * LANES)
            eq_f = jnp.where(eq[sl, :] if axis == 0 else eq[:, sl], 1.0, 0.0)
            eq_b = eq_f.astype(BF16)
            rank = (_dot(tri, eq_b) if axis == 0 else _dot(eq_b, tri)) + carry
            parts.append(jnp.where(rank < need_f, eq_f, 0.0))
            carry = carry + jnp.sum(eq_f, axis=axis, keepdims=True)
        return jnp.concatenate(parts, axis=axis), carry

    def plain():
        return jnp.where(eq, 1.0, 0.0), jnp.zeros(cshape, F32)

    return lax.cond(jnp.max(jnp.where(tied, 1, 0)) > 0, ranked, plain)


def _dsa_prompt_kernel(q_ref, qi_ref, wi_ref, kk_ref, kiki_ref, vvt_ref, o_ref, *, topk, first_block, select):
    i = first_block + pl.program_id(1)
    qb = q_ref.shape[0]
    lk = kk_ref.shape[1]
    kiki = kiki_ref[0]
    kk = kk_ref[0]
    vvt = vvt_ref[0]
    low = lax.broadcasted_iota(jnp.int32, (qb, LANES), 1) < HEAD_DIM

    def head_operand(ref, h):
        pair = ref[:, (h // 2) * LANES:(h // 2 + 1) * LANES]
        keep = low if h % 2 == 0 else jnp.logical_not(low)
        return jnp.where(keep, pair, jnp.zeros_like(pair))

    pos_q = i * qb + lax.broadcasted_iota(jnp.int32, (1, qb), 1)
    pos_k = lax.broadcasted_iota(jnp.int32, (lk, 1), 0)
    causal = pos_k <= pos_q

    if select:
        w_t = jnp.transpose(wi_ref[...])[0:IDX_HEADS, :] * (IDX_HEADS ** -0.5)
        score = jnp.zeros((lk, qb), F32)
        for h in range(IDX_HEADS):
            score = score + jnp.maximum(_dot_nt(kiki, head_operand(qi_ref, h)), 0.0) * w_t[h:h + 1, :]
        sm = jnp.where(causal, score, -jnp.inf)

        def count_ge(cand):
            return jnp.sum(jnp.where(sm >= cand, 1, 0), axis=0, keepdims=True)

        tau = _kth_largest(count_ge, topk, (1, qb))
        gt = sm > tau
        eq = sm == tau
        n_gt = jnp.sum(jnp.where(gt, 1, 0), axis=0, keepdims=True)
        n_eq = jnp.sum(jnp.where(eq, 1, 0), axis=0, keepdims=True)
        tied = n_gt + n_eq > topk
        need_f = jnp.where(tied, topk - n_gt, lk + 1).astype(F32)
        take_eq, _ = _take_first_ties(eq, need_f, tied, axis=0)
        mask = jnp.logical_or(gt, take_eq > 0.5)
    else:
        mask = causal
    bias = jnp.where(mask, 0.0, NEG)

    upper_rows = lax.broadcasted_iota(jnp.int32, (LANES, qb), 0) < HEAD_DIM
    for j in range(ATTN_HEADS // 2):
        outs = []
        for h in (2 * j, 2 * j + 1):
            logits = _dot_nt(kk, head_operand(q_ref, h)) + bias
            p = jnp.exp(logits - jnp.max(logits, axis=0, keepdims=True))
            inv = 1.0 / jnp.sum(p, axis=0, keepdims=True)
            outs.append(_dot(vvt, p.astype(BF16)) * inv)
        pair_t = jnp.where(upper_rows, outs[0], outs[1])
        o_ref[0, :, j * LANES:(j + 1) * LANES] = jnp.transpose(pair_t).astype(BF16)


def _dsa_prompt_group(q, qi, wi, kk, kiki, vvt, batch, t_len, first_block, n_blocks, topk):
    nb = t_len // DSA_Q
    lk = (first_block + n_blocks) * DSA_Q
    blk = lambda b, j: (b * nb + first_block + j, 0)
    return pl.pallas_call(
        functools.partial(_dsa_prompt_kernel, topk=topk, first_block=first_block, select=lk > topk),
        grid=(batch, n_blocks),
        in_specs=[pl.BlockSpec((DSA_Q, 512), blk),
                  pl.BlockSpec((DSA_Q, 512), blk),
                  pl.BlockSpec((DSA_Q, LANES), blk),
                  pl.BlockSpec((1, lk, LANES), lambda b, j: (b, 0, 0)),
                  pl.BlockSpec((1, lk, LANES), lambda b, j: (b, 0, 0)),
                  pl.BlockSpec((1, LANES, lk), lambda b, j: (b, 0, 0))],
        out_specs=pl.BlockSpec((1, DSA_Q, 512), lambda b, j: (b, j, 0)),
        out_shape=jax.ShapeDtypeStruct((batch, n_blocks * DSA_Q, 512), BF16),
        compiler_params=_params("parallel", "parallel"),
        name="dsa_prompt_%d" % lk,
    )(q, qi, wi, kk, kiki, vvt)


def _dsa_prompt(q, qi, wi, kk, kiki, vv, batch, t_len):
    nb = t_len // DSA_Q
    topk = min(TOPK_MAX, t_len // 4)
    assert topk % DSA_Q == 0 and t_len % DSA_Q == 0
    kk, kiki, vv = (a.reshape(batch, t_len, LANES) for a in (kk, kiki, vv))
    vvt = jnp.swapaxes(vv, 1, 2)
    edges = [0, topk // DSA_Q]
    while edges[-1] < nb:
        edges.append(min(edges[-1] + (1 if edges[-1] < 2 else 2), nb))
    parts = [_dsa_prompt_group(q, qi, wi, kk, kiki, vvt, batch, t_len, a, b - a, topk)
             for a, b in zip(edges[:-1], edges[1:]) if b > a]
    return jnp.concatenate(parts, axis=1).reshape(batch * t_len, 512)


def _mem_kv_kernel(m_ref, w_ref, k_ref, v_ref):
    z = _dot(m_ref[...].astype(BF16), w_ref[...])
    k_ref[...] = z[:, :MEM_WIDTH]
    v_ref[...] = z[:, MEM_WIDTH:]


def _mem_kv(mem, w_kv):
    n = mem.shape[0]
    tm = min(n, 512)
    row = lambda i: (i, 0)
    return pl.pallas_call(
        _mem_kv_kernel,
        grid=(n // tm,),
        in_specs=[pl.BlockSpec((tm, D_MODEL), row),
                  pl.BlockSpec((D_MODEL, 2 * MEM_WIDTH), lambda i: (0, 0))],
        out_specs=[pl.BlockSpec((tm, MEM_WIDTH), row), pl.BlockSpec((tm, MEM_WIDTH), row)],
        out_shape=[jax.ShapeDtypeStruct((n, MEM_WIDTH), F32)] * 2,
        compiler_params=_params("parallel"),
        name="mem_kv",
    )(mem, w_kv)


def _pool_project(window_sums, u, cnts, pw_ref, ps_ref):
    ys = []
    for g in range(len(POOL_WINDOWS)):
        sl = slice(g * POOL_CG, (g + 1) * POOL_CG)
        pooled = window_sums[g] / cnts[g] - u[:, sl]
        ys.append(_dot(pooled.astype(BF16), pw_ref[g]) * ps_ref[:, sl])
    return jnp.concatenate(ys, axis=1)


def _mid_prompt_kernel(x_ref, attn_ref, u_ref, halo_ref, mk_ref, mv_ref, pw_ref, ps_ref, wo_ref,
                       gm_ref, wmq_ref, wmo_ref, gf_ref, h2_ref, tok_ref, e_ref):
    i = pl.program_id(1)
    tm = x_ref.shape[0]
    u = u_ref[...]
    halo = halo_ref[...]
    e_ref[0:POOL_MAX, :] = jnp.where(i == 0, jnp.zeros_like(halo), halo)
    e_ref[POOL_MAX:POOL_MAX + tm, :] = u
    pos = i * tm + lax.broadcasted_iota(jnp.int32, (tm, 1), 0)
    sums, cnts = [], []
    for g, w in enumerate(POOL_WINDOWS):
        sl = slice(g * POOL_CG, (g + 1) * POOL_CG)
        s = u[:, sl]
        for j in range(1, w):
            s = s + e_ref[POOL_MAX - j:POOL_MAX - j + tm, sl]
        sums.append(s)
        cnts.append(jnp.minimum(w, pos + 1).astype(F32))
    pool = _pool_project(sums, u, cnts, pw_ref, ps_ref)
    h1 = x_ref[...] + _dot(attn_ref[...], wo_ref[0:ATTN_WIDTH, :]) \
        + _dot(pool.astype(BF16), wo_ref[ATTN_WIDTH:, :])

    qm = _dot(_rms(h1, gm_ref[...]).astype(BF16), wmq_ref[...]).astype(BF16)
    mk = mk_ref[0].astype(BF16)
    mv = mv_ref[0].astype(BF16)
    outs = []
    for h in range(MEM_HEADS):
        sl = slice(h * MEM_HEAD_DIM, (h + 1) * MEM_HEAD_DIM)
        logits = _dot_nt(qm[:, sl], mk[:, sl]) * (MEM_HEAD_DIM ** -0.5)
        m = jnp.max(logits, axis=1, keepdims=True)
        p = jnp.exp(logits - m)
        p = p / jnp.sum(p, axis=1, keepdims=True)
        outs.append(_dot(p.astype(BF16), mv[:, sl]))
    o = jnp.concatenate(outs, axis=1)
    h2 = h1 + _dot(o.astype(BF16), wmo_ref[...])
    h2_ref[...] = h2
    tok_ref[...] = _rms(h2, gf_ref[...]).astype(BF16)


def _mid_prompt(x, attn, u, mk, mv, pool_w, pool_scale, w_o, g_mem, w_mq, w_mo, g_ffn, batch, t_len, tm):
    nt = t_len // tm
    hb = tm // POOL_MAX
    row = lambda b, i: (b * nt + i, 0)
    halo = lambda b, i: (jnp.maximum((b * nt + i) * hb - 1, 0), 0)
    fixed2 = lambda b, i: (0, 0)
    fixed3 = lambda b, i: (0, 0, 0)
    perb = lambda b, i: (b, 0, 0)
    n = batch * t_len
    return pl.pallas_call(
        _mid_prompt_kernel,
        grid=(batch, nt),
        in_specs=[pl.BlockSpec((tm, D_MODEL), row),
                  pl.BlockSpec((tm, ATTN_WIDTH), row),
                  pl.BlockSpec((tm, POOL_WIDTH), row),
                  pl.BlockSpec((POOL_MAX, POOL_WIDTH), halo),
                  pl.BlockSpec((1, N_MEM, MEM_WIDTH), perb),
                  pl.BlockSpec((1, N_MEM, MEM_WIDTH), perb),
                  pl.BlockSpec((len(POOL_WINDOWS), POOL_CG, POOL_CG), fixed3),
                  pl.BlockSpec((1, POOL_WIDTH), fixed2),
                  pl.BlockSpec((D_MODEL, D_MODEL), fixed2),
                  pl.BlockSpec((1, D_MODEL), fixed2),
                  pl.BlockSpec((D_MODEL, MEM_WIDTH), fixed2),
                  pl.BlockSpec((MEM_WIDTH, D_MODEL), fixed2),
                  pl.BlockSpec((1, D_MODEL), fixed2)],
        out_specs=[pl.BlockSpec((tm, D_MODEL), row), pl.BlockSpec((tm, D_MODEL), row)],
        out_shape=[jax.ShapeDtypeStruct((n, D_MODEL), F32), jax.ShapeDtypeStruct((n, D_MODEL), BF16)],
        scratch_shapes=[pltpu.VMEM((POOL_MAX + tm, POOL_WIDTH), F32)],
        compiler_params=_params("parallel", "arbitrary"),
        name="mid_prompt",
    )(x, attn, u, u, mk, mv, pool_w, pool_scale, w_o, g_mem, w_mq, w_mo, g_ffn)


def _router_kernel(tok_ref, rw_ref, rb_ref, rank_ref, gate_ref, cnt_ref, *, n_valid):
    tm = tok_ref.shape[0]
    token = pl.program_id(0) * tm + lax.broadcasted_iota(jnp.int32, (1, tm), 1)
    valid = token < n_valid
    logits = _dot_nt(rw_ref[...], tok_ref[...]) + rb_ref[...]
    eid = lax.broadcasted_iota(jnp.int32, (N_EXPERTS, tm), 0)
    vals, hots = [], []
    cur = logits
    for _ in range(TOP_K):
        m = jnp.max(cur, axis=0, keepdims=True)
        idx = jnp.min(jnp.where(cur == m, eid, N_EXPERTS), axis=0, keepdims=True)
        hot = eid == idx
        vals.append(m)
        hots.append(hot)
        cur = jnp.where(hot, -jnp.inf, cur)
    exps = [jnp.exp(v - vals[0]) for v in vals]
    den = exps[0] + exps[1] + exps[2] + exps[3]
    gate = jnp.zeros((N_EXPERTS, tm), F32)
    sel = jnp.zeros((N_EXPERTS, tm), jnp.bool_)
    for hot, ex in zip(hots, exps):
        gate = jnp.where(hot, ex / den, gate)
        sel = jnp.logical_or(sel, hot)
    sel = jnp.logical_and(sel, valid)
    gate_ref[0] = gate

    upper = _strict_triangle(LANES, lower=False)
    carry = jnp.zeros((N_EXPERTS, 1), F32)
    for c in range(tm // LANES):
        sl = slice(c * LANES, (c + 1) * LANES)
        sel_c = sel[:, sl]
        sel_b = jnp.where(sel_c, 1.0, 0.0).astype(BF16)
        rank = _dot(sel_b, upper) + carry
        rank_ref[0, :, sl] = jnp.where(sel_c, rank, -1.0).astype(jnp.int32)
        carry = carry + jnp.sum(sel_b.astype(F32), axis=1, keepdims=True)
    cnt_ref[0] = carry.astype(jnp.int32)


def _router(tok, rw_t, rb, tm, n_valid):
    n = tok.shape[0]
    nt = n // tm
    t3 = lambda i: (i, 0, 0)
    return pl.pallas_call(
        functools.partial(_router_kernel, n_valid=n_valid),
        grid=(nt,),
        in_specs=[pl.BlockSpec((tm, D_MODEL), lambda i: (i, 0)),
                  pl.BlockSpec((N_EXPERTS, D_MODEL), lambda i: (0, 0)),
                  pl.BlockSpec((N_EXPERTS, 1), lambda i: (0, 0))],
        out_specs=[pl.BlockSpec((1, N_EXPERTS, tm), t3),
                   pl.BlockSpec((1, N_EXPERTS, tm), t3),
                   pl.BlockSpec((1, N_EXPERTS, 1), t3)],
        out_shape=[jax.ShapeDtypeStruct((nt, N_EXPERTS, tm), jnp.int32),
                   jax.ShapeDtypeStruct((nt, N_EXPERTS, tm), F32),
                   jax.ShapeDtypeStruct((nt, N_EXPERTS, 1), jnp.int32)],
        compiler_params=_params("parallel"),
        name="router",
    )(tok, rw_t, rb)


def _swiglu(h):
    gate = jnp.minimum(h[:, :D_FF], SWIGLU_LIMIT)
    up = jnp.clip(h[:, D_FF:], -SWIGLU_LIMIT, SWIGLU_LIMIT)
    return gate * jax.nn.sigmoid(SWIGLU_ALPHA * gate) * (up + 1.0)


def _moe_kernel(cnt_ref, x_ref, rank_ref, gate_ref, wu_ref, bu_ref, wd_ref, bd_ref, o_ref, *,
                first_chunk, chunk):
    i = pl.program_id(0)
    e = pl.program_id(1)
    tm = x_ref.shape[0]

    @pl.when(e == 0)
    def _():
        o_ref[...] = jnp.zeros_like(o_ref)

    n_routed = cnt_ref[i * N_EXPERTS + e]
    rank = rank_ref[0, pl.ds(e, 1), :]
    gate = gate_ref[0, pl.ds(e, 1), :]

    def run_slots(first, size):
        slot = first + lax.broadcasted_iota(jnp.int32, (size, tm), 0)
        hit = rank == slot
        onehot = jnp.where(hit, 1.0, 0.0).astype(BF16)
        xg = _dot(onehot, x_ref[...]).astype(BF16)
        act = _swiglu(_dot(xg, wu_ref[0]) + bu_ref[0])
        out = _dot(act.astype(BF16), wd_ref[0]) + bd_ref[0]
        g_slot = jnp.sum(jnp.where(hit, gate, 0.0), axis=1, keepdims=True)
        o_ref[...] += _dot_tn(onehot, (out * g_slot).astype(BF16))

    @pl.when(n_routed > 0)
    def _():
        run_slots(0, first_chunk)

    def body(c, carry):
        run_slots(first_chunk + c * chunk, chunk)
        return carry

    lax.fori_loop(0, (jnp.maximum(n_routed - first_chunk, 0) + chunk - 1) // chunk, body, 0)


def _moe(tok, rank, gate, cnt, wu, bu, wd, bd, tm, first_chunk, chunk):
    n = tok.shape[0]
    nt = n // tm
    grid_spec = pltpu.PrefetchScalarGridSpec(
        num_scalar_prefetch=1,
        grid=(nt, N_EXPERTS),
        in_specs=[pl.BlockSpec((tm, D_MODEL), lambda i, e, c: (i, 0)),
                  pl.BlockSpec((1, N_EXPERTS, tm), lambda i, e, c: (i, 0, 0)),
                  pl.BlockSpec((1, N_EXPERTS, tm), lambda i, e, c: (i, 0, 0)),
                  pl.BlockSpec((1, D_MODEL, 2 * D_FF), lambda i, e, c: (e, 0, 0)),
                  pl.BlockSpec((1, 1, 2 * D_FF), lambda i, e, c: (e, 0, 0)),
                  pl.BlockSpec((1, D_FF, D_MODEL), lambda i, e, c: (e, 0, 0)),
                  pl.BlockSpec((1, 1, D_MODEL), lambda i, e, c: (e, 0, 0))],
        out_specs=pl.BlockSpec((tm, D_MODEL), lambda i, e, c: (i, 0)),
    )
    return pl.pallas_call(
        functools.partial(_moe_kernel, first_chunk=first_chunk, chunk=chunk),
        grid_spec=grid_spec,
        out_shape=jax.ShapeDtypeStruct((n, D_MODEL), F32),
        compiler_params=_params("parallel", "arbitrary"),
        name="moe",
    )(cnt, tok, rank, gate, wu, bu, wd, bd)


def _final_kernel(h_ref, f_ref, g_ref, y_ref):
    y_ref[...] = _rms(h_ref[...] + f_ref[...], g_ref[...])


def _final(h2, ffn, g, tm):
    n = h2.shape[0]
    row = lambda i: (i, 0)
    return pl.pallas_call(
        _final_kernel,
        grid=(n // tm,),
        in_specs=[pl.BlockSpec((tm, D_MODEL), row), pl.BlockSpec((tm, D_MODEL), row),
                  pl.BlockSpec((1, D_MODEL), lambda i: (0, 0))],
        out_specs=pl.BlockSpec((tm, D_MODEL), row),
        out_shape=jax.ShapeDtypeStruct((n, D_MODEL), F32),
        compiler_params=_params("parallel"),
        name="final_norm",
    )(h2, ffn, g)


def _page_copy(cache_ref, buf_ref, sem_ref, pt_ref, b, p, slot):
    dst = buf_ref.at[slot, :, pl.ds(pl.multiple_of(p * PAGE_SIZE, PAGE_SIZE), PAGE_SIZE)]
    return pltpu.make_async_copy(cache_ref.at[0, pt_ref[b, p]], dst, sem_ref.at[slot])


def _fetch_pages(cache_ref, buf_ref, sem_ref, pt_ref, b, slot, n_pages):
    def body(p, c):
        _page_copy(cache_ref, buf_ref, sem_ref, pt_ref, b, p, slot).start()
        return c
    lax.fori_loop(0, n_pages, body, 0, unroll=8)


def _wait_pages(cache_ref, buf_ref, sem_ref, pt_ref, b, slot, n_pages):
    def body(p, c):
        _page_copy(cache_ref, buf_ref, sem_ref, pt_ref, b, p, slot).wait()
        return c
    lax.fori_loop(0, n_pages, body, 0, unroll=8)


def _dsa_sample_score_kernel(pt_ref, qi_ref, wi_ref, kin_ref, cache_ref, score_ref, new_ref, buf_ref, sem_ref):
    b = pl.program_id(0)
    nb = pl.num_programs(0)
    past = buf_ref.shape[2]
    n_pages = past // PAGE_SIZE
    slot = b % 2

    @pl.when(b == 0)
    def _():
        _fetch_pages(cache_ref, buf_ref, sem_ref, pt_ref, b, slot, n_pages)

    @pl.when(b + 1 < nb)
    def _():
        _fetch_pages(cache_ref, buf_ref, sem_ref, pt_ref, b + 1, 1 - slot, n_pages)

    _wait_pages(cache_ref, buf_ref, sem_ref, pt_ref, b, slot, n_pages)

    qi = qi_ref[0]
    w = wi_ref[0] * (IDX_HEADS ** -0.5)
    keys_t = buf_ref[slot].astype(BF16)
    score_ref[0] = jnp.sum(jnp.maximum(_dot(qi, keys_t), 0.0) * w, axis=0, keepdims=True)
    s_new = jnp.sum(qi.astype(F32) * kin_ref[0].astype(F32), axis=1, keepdims=True)
    new_ref[0] = jnp.sum(jnp.maximum(s_new, 0.0) * w, axis=0, keepdims=True)


def _dsa_sample_select_kernel(score_ref, new_ref, mask_ref, take_new_ref, *, topk):
    score = score_ref[...]
    score_new = new_ref[...]
    past = score.shape[1]

    def count_ge(cand):
        return (jnp.sum(jnp.where(score >= cand, 1, 0), axis=1, keepdims=True)
                + jnp.where(score_new >= cand, 1, 0))

    tau = _kth_largest(count_ge, topk, score_new.shape)
    gt = score > tau
    eq = score == tau
    n_gt = jnp.sum(jnp.where(gt, 1, 0), axis=1, keepdims=True) + jnp.where(score_new > tau, 1, 0)
    n_eq_all = jnp.sum(jnp.where(eq, 1, 0), axis=1, keepdims=True) + jnp.where(score_new == tau, 1, 0)
    tied = n_gt + n_eq_all > topk
    need_f = jnp.where(tied, topk - n_gt, past + 2).astype(F32)
    take_eq, n_eq = _take_first_ties(eq, need_f, tied, axis=1)
    mask_ref[...] = jnp.where(jnp.logical_or(gt, take_eq > 0.5), 0.0, NEG)
    take_new = jnp.logical_or(score_new > tau, jnp.logical_and(score_new == tau, n_eq < need_f))
    take_new_ref[...] = jnp.where(take_new, 0.0, NEG)


def _dsa_sample_select(score, score_new, topk):
    db, past = score.shape
    full = lambda shape: pl.BlockSpec(shape, lambda i: (0, 0))
    return pl.pallas_call(
        functools.partial(_dsa_sample_select_kernel, topk=topk),
        grid=(1,),
        in_specs=[full((db, past)), full((db, 1))],
        out_specs=[full((db, past)), full((db, 1))],
        out_shape=[jax.ShapeDtypeStruct((db, past), F32), jax.ShapeDtypeStruct((db, 1), F32)],
        compiler_params=_params("arbitrary"),
        name="dsa_sample_select",
    )(score, score_new)


def _dsa_sample_mask(page_table, qi_heads, wi_col, ki_new, cache_idx_k):
    db, n_pages = page_table.shape
    past = n_pages * PAGE_SIZE
    topk = min(TOPK_MAX, (past + 1) // 4)
    per = lambda b, pt: (b, 0, 0)
    grid_spec = pltpu.PrefetchScalarGridSpec(
        num_scalar_prefetch=1,
        grid=(db,),
        in_specs=[pl.BlockSpec((1, IDX_HEADS, IDX_DIM), per),
                  pl.BlockSpec((1, IDX_HEADS, 1), per),
                  pl.BlockSpec((1, 1, IDX_DIM), per),
                  pl.BlockSpec(memory_space=pl.ANY)],
        out_specs=[pl.BlockSpec((1, 1, past), per), pl.BlockSpec((1, 1, 1), per)],
        scratch_shapes=[pltpu.VMEM((2, IDX_DIM, past), F32),
                        pltpu.SemaphoreType.DMA((2,))],
    )
    score, score_new = pl.pallas_call(
        _dsa_sample_score_kernel,
        grid_spec=grid_spec,
        out_shape=[jax.ShapeDtypeStruct((db, 1, past), F32), jax.ShapeDtypeStruct((db, 1, 1), F32)],
        compiler_params=_params("arbitrary"),
        name="dsa_sample_score",
    )(page_table, qi_heads, wi_col, ki_new, cache_idx_k)
    mask, take_new = _dsa_sample_select(score.reshape(db, past), score_new.reshape(db, 1), topk)
    return mask.reshape(db, 1, past), take_new.reshape(db, 1, 1)


def _dsa_sample_attend_kernel(pt_ref, q_ref, kn_ref, vn_ref, mask_ref, new_ref, ck_ref, cv_ref, o_ref,
                              kbuf_ref, vbuf_ref, ksem_ref, vsem_ref):
    b = pl.program_id(0)
    nb = pl.num_programs(0)
    past = kbuf_ref.shape[2]
    n_pages = past // PAGE_SIZE
    slot = b % 2

    def fetch(bb, sl):
        _fetch_pages(ck_ref, kbuf_ref, ksem_ref, pt_ref, bb, sl, n_pages)
        _fetch_pages(cv_ref, vbuf_ref, vsem_ref, pt_ref, bb, sl, n_pages)

    @pl.when(b == 0)
    def _():
        fetch(b, slot)

    @pl.when(b + 1 < nb)
    def _():
        fetch(b + 1, 1 - slot)

    _wait_pages(ck_ref, kbuf_ref, ksem_ref, pt_ref, b, slot, n_pages)
    _wait_pages(cv_ref, vbuf_ref, vsem_ref, pt_ref, b, slot, n_pages)

    q = q_ref[0]
    keys_t = kbuf_ref[slot].astype(BF16)
    vals_t = vbuf_ref[slot].astype(BF16)
    logits = _dot(q, keys_t) + mask_ref[0]
    l_new = jnp.sum(q.astype(F32) * kn_ref[0].astype(F32), axis=1, keepdims=True) + new_ref[0]
    m = jnp.maximum(jnp.max(logits, axis=1, keepdims=True), l_new)
    p = jnp.exp(logits - m)
    p_new = jnp.exp(l_new - m)
    den = jnp.sum(p, axis=1, keepdims=True) + p_new
    p = p / den
    p_new = p_new / den
    o_ref[0] = _dot_nt(p.astype(BF16), vals_t) + p_new * vn_ref[0].astype(F32)


def _dsa_sample_attend(page_table, q_heads, k_new, v_new, mask, take_new, cache_k, cache_v):
    db, n_pages = page_table.shape
    past = n_pages * PAGE_SIZE
    per = lambda b, pt: (b, 0, 0)
    grid_spec = pltpu.PrefetchScalarGridSpec(
        num_scalar_prefetch=1,
        grid=(db,),
        in_specs=[pl.BlockSpec((1, ATTN_HEADS, HEAD_DIM), per),
                  pl.BlockSpec((1, 1, HEAD_DIM), per),
                  pl.BlockSpec((1, 1, HEAD_DIM), per),
                  pl.BlockSpec((1, 1, past), per),
                  pl.BlockSpec((1, 1, 1), per),
                  pl.BlockSpec(memory_space=pl.ANY),
                  pl.BlockSpec(memory_space=pl.ANY)],
        out_specs=pl.BlockSpec((1, ATTN_HEADS, HEAD_DIM), per),
        scratch_shapes=[pltpu.VMEM((2, HEAD_DIM, past), F32),
                        pltpu.VMEM((2, HEAD_DIM, past), F32),
                        pltpu.SemaphoreType.DMA((2,)),
                        pltpu.SemaphoreType.DMA((2,))],
    )
    return pl.pallas_call(
        _dsa_sample_attend_kernel,
        grid_spec=grid_spec,
        out_shape=jax.ShapeDtypeStruct((db, ATTN_HEADS, HEAD_DIM), F32),
        compiler_params=_params("arbitrary"),
        name="dsa_sample_attend",
    )(page_table, q_heads, k_new, v_new, mask, take_new, cache_k, cache_v)


def _mid_sample_a_kernel(x_ref, attn_ref, u_ref, st_ref, pw_ref, ps_ref, wo_ref, gm_ref, wmq_ref,
                         h1_ref, qm_ref, *, past):
    u = u_ref[...]
    rows = u.shape[0]
    sums, cnts = [], []
    for g, w in enumerate(POOL_WINDOWS):
        sl = slice(g * POOL_CG, (g + 1) * POOL_CG)
        s = u[:, sl]
        for j in range(1, w):
            s = s + st_ref[POOL_MAX - 1 - j][:, sl]
        sums.append(s)
        cnts.append(jnp.full((rows, 1), float(min(w, past + 1)), F32))
    pool = _pool_project(sums, u, cnts, pw_ref, ps_ref)
    h1 = x_ref[...] + _dot(attn_ref[...], wo_ref[0:ATTN_WIDTH, :]) \
        + _dot(pool.astype(BF16), wo_ref[ATTN_WIDTH:, :])
    h1_ref[...] = h1
    qm_ref[...] = _dot(_rms(h1, gm_ref[...]).astype(BF16), wmq_ref[...]).astype(BF16)


def _mid_sample_a(x, attn, u, state_t, pool_w, pool_scale, w_o, g_mem, w_mq, past):
    db = x.shape[0]
    full = lambda a: pl.BlockSpec(a.shape, lambda i, nd=a.ndim: (0,) * nd)
    args = (x, attn, u, state_t, pool_w, pool_scale, w_o, g_mem, w_mq)
    return pl.pallas_call(
        functools.partial(_mid_sample_a_kernel, past=past),
        grid=(1,),
        in_specs=[full(a) for a in args],
        out_specs=[pl.BlockSpec((db, D_MODEL), lambda i: (0, 0)),
                   pl.BlockSpec((db, MEM_WIDTH), lambda i: (0, 0))],
        out_shape=[jax.ShapeDtypeStruct((db, D_MODEL), F32), jax.ShapeDtypeStruct((db, MEM_WIDTH), BF16)],
        compiler_params=_params("arbitrary"),
        name="mid_sample_a",
    )(*args)


def _mem_sample_kernel(q_ref, mk_ref, mv_ref, o_ref):
    rows = 8
    head_of_row = lax.broadcasted_iota(jnp.int32, (rows, MEM_WIDTH), 0)
    head_of_lane = lax.broadcasted_iota(jnp.int32, (rows, MEM_WIDTH), 1) // MEM_HEAD_DIM
    own = head_of_row == head_of_lane
    q = jnp.broadcast_to(q_ref[0].astype(F32), (rows, MEM_WIDTH))
    qh = jnp.where(own, q, 0.0).astype(BF16)
    logits = _dot_nt(qh, mk_ref[0].astype(BF16)) * (MEM_HEAD_DIM ** -0.5)
    m = jnp.max(logits, axis=1, keepdims=True)
    p = jnp.exp(logits - m)
    p = p / jnp.sum(p, axis=1, keepdims=True)
    o = _dot(p.astype(BF16), mv_ref[0].astype(BF16))
    o_ref[0] = jnp.sum(jnp.where(own, o, 0.0), axis=0, keepdims=True)


def _mem_sample(qm, mk, mv):
    db = qm.shape[0]
    per = lambda b: (b, 0, 0)
    return pl.pallas_call(
        _mem_sample_kernel,
        grid=(db,),
        in_specs=[pl.BlockSpec((1, 1, MEM_WIDTH), per),
                  pl.BlockSpec((1, N_MEM, MEM_WIDTH), per),
                  pl.BlockSpec((1, N_MEM, MEM_WIDTH), per)],
        out_specs=pl.BlockSpec((1, 1, MEM_WIDTH), per),
        out_shape=jax.ShapeDtypeStruct((db, 1, MEM_WIDTH), F32),
        compiler_params=_params("parallel"),
        name="mem_sample",
    )(qm.reshape(db, 1, MEM_WIDTH), mk, mv)


def _tail_sample_kernel(h1_ref, o_ref, wmo_ref, gf_ref, rw_ref, rb_ref, wu_ref, bu_ref, wd_ref, bd_ref,
                        gfin_ref, y_ref, h2_ref, tok_ref, comb_ref, acc_ref):
    e = pl.program_id(0)
    rows = h1_ref.shape[0]

    @pl.when(e == 0)
    def _():
        h2 = h1_ref[...] + _dot(o_ref[...].astype(BF16), wmo_ref[...])
        tok = _rms(h2, gf_ref[...]).astype(BF16)
        h2_ref[...] = h2
        tok_ref[...] = tok
        logits = _dot(tok, rw_ref[...]) + rb_ref[...]
        eid = lax.broadcasted_iota(jnp.int32, (rows, N_EXPERTS), 1)
        vals, hots = [], []
        cur = logits
        for _ in range(TOP_K):
            m = jnp.max(cur, axis=1, keepdims=True)
            idx = jnp.min(jnp.where(cur == m, eid, N_EXPERTS), axis=1, keepdims=True)
            hot = eid == idx
            vals.append(m)
            hots.append(hot)
            cur = jnp.where(hot, -jnp.inf, cur)
        exps = [jnp.exp(v - vals[0]) for v in vals]
        den = exps[0] + exps[1] + exps[2] + exps[3]
        comb = jnp.zeros((rows, N_EXPERTS), F32)
        for hot, ex in zip(hots, exps):
            comb = jnp.where(hot, ex / den, comb)
        comb_ref[...] = comb
        acc_ref[...] = jnp.zeros_like(acc_ref)

    act = _swiglu(_dot(tok_ref[...], wu_ref[0]) + bu_ref[0])
    out = _dot(act.astype(BF16), wd_ref[0]) + bd_ref[0]
    eid = lax.broadcasted_iota(jnp.int32, (rows, N_EXPERTS), 1)
    c_e = jnp.sum(jnp.where(eid == e, comb_ref[...], 0.0), axis=1, keepdims=True)
    acc_ref[...] += c_e * out

    @pl.when(e == pl.num_programs(0) - 1)
    def _():
        y_ref[...] = _rms(h2_ref[...] + acc_ref[...], gfin_ref[...])


def _tail_sample(h1, o, w_mo, g_ffn, rw, rb, wu, bu, wd, bd, g_final):
    db = h1.shape[0]
    f2 = lambda e: (0, 0)
    pe = lambda e: (e, 0, 0)
    return pl.pallas_call(
        _tail_sample_kernel,
        grid=(N_EXPERTS,),
        in_specs=[pl.BlockSpec((db, D_MODEL), f2),
                  pl.BlockSpec((db, MEM_WIDTH), f2),
                  pl.BlockSpec((MEM_WIDTH, D_MODEL), f2),
                  pl.BlockSpec((1, D_MODEL), f2),
                  pl.BlockSpec((D_MODEL, N_EXPERTS), f2),
                  pl.BlockSpec((1, N_EXPERTS), f2),
                  pl.BlockSpec((1, D_MODEL, 2 * D_FF), pe),
                  pl.BlockSpec((1, 1, 2 * D_FF), pe),
                  pl.BlockSpec((1, D_FF, D_MODEL), pe),
                  pl.BlockSpec((1, 1, D_MODEL), pe),
                  pl.BlockSpec((1, D_MODEL), f2)],
        out_specs=pl.BlockSpec((db, D_MODEL), f2),
        out_shape=jax.ShapeDtypeStruct((db, D_MODEL), F32),
        scratch_shapes=[pltpu.VMEM((db, D_MODEL), F32), pltpu.VMEM((db, D_MODEL), BF16),
                        pltpu.VMEM((db, N_EXPERTS), F32), pltpu.VMEM((db, D_MODEL), F32)],
        compiler_params=_params("arbitrary"),
        name="tail_sample",
    )(h1, o, w_mo, g_ffn, rw, rb, wu, bu, wd, bd, g_final)


def _rope_tables(pos):
    half = HEAD_DIM // 2
    inv = ROPE_THETA ** (-jnp.arange(half, dtype=F32) / half)
    ang = pos.astype(F32)[:, None] * inv[None, :]
    cos = jnp.cos(ang)
    sin = jnp.sin(ang)
    return jnp.tile(cos, (1, 4)), jnp.concatenate([-sin, sin, -sin, sin], axis=1)


def _pick_tile(n, pref):
    t = min(n, pref)
    while n % t:
        t //= 2
    return t


def kernel(x_prompt, x_sample, mem_prompt, cache_k, cache_v, cache_idx_k, state_pool, cache_mem_k, cache_mem_v, page_table, norm_mix, w_in, pool_w, pool_scale, w_o, norm_mem, w_mq, w_mk, w_mv, w_mo, norm_ffn, router_w, router_b, w_up, b_up, w_down, b_down, norm_final):
    B, T = x_prompt.shape[:2]
    DB, DS = x_sample.shape[:2]
    assert DS == 1 and w_in.shape[0] == 1
    n_pages = page_table.shape[1]
    past = n_pages * PAGE_SIZE
    NP = B * T

    wi_ = w_in[0]
    wq, wk, wv, wu_, wqi, wki, wwi = jnp.split(wi_, [512, 576, 640, 1152, 1664, 1728], axis=1)
    w_all = jnp.concatenate(
        [wq, wqi, wu_, wk, wk, wki, wki, wv, wv, wwi, jnp.zeros((D_MODEL, LANES - IDX_HEADS), F32)],
        axis=1).astype(BF16)
    g_mix = norm_mix[0][None, :]
    g_mem = norm_mem[0][None, :]
    g_ffn = norm_ffn[0][None, :]
    g_fin = norm_final[None, :]
    pw = pool_w[0].astype(BF16)
    ps = pool_scale[0][None, :]
    wo = w_o[0].astype(BF16)
    wmq = w_mq[0].reshape(D_MODEL, MEM_WIDTH).astype(BF16)
    wkv = jnp.concatenate([w_mk[0].reshape(D_MODEL, MEM_WIDTH), w_mv[0].reshape(D_MODEL, MEM_WIDTH)],
                          axis=1).astype(BF16)
    wmo = w_mo[0].reshape(MEM_WIDTH, D_MODEL).astype(BF16)
    rw = router_w[0].astype(BF16)
    rb = router_b[0]
    wup = w_up[0].astype(BF16)
    wdn = w_down[0].astype(BF16)
    bup = b_up[0][:, None, :]
    bdn = b_down[0][:, None, :]

    xp = x_prompt.reshape(NP, D_MODEL)
    tm_in = _pick_tile(T, 512)
    cos_p, sin_p = _rope_tables(jnp.arange(T, dtype=jnp.int32))
    q, qi, u, kk, kiki, vv, k_p, ki_p, v_p, wi_p = _in_project(xp, g_mix, w_all, cos_p, sin_p, tm_in, T // tm_in)
    attn = _dsa_prompt(q, qi, wi_p, kk, kiki, vv, B, T)
    mk_p, mv_p = _mem_kv(mem_prompt.reshape(B * N_MEM, D_MODEL), wkv)
    h2_p, tok_p = _mid_prompt(xp, attn, u, mk_p.reshape(B, N_MEM, MEM_WIDTH), mv_p.reshape(B, N_MEM, MEM_WIDTH),
                              pw, ps, wo, g_mem, wmq, wmo, g_ffn, B, T, _pick_tile(T, 512))
    tm_moe = min(MOE_TILE, NP)
    tok_pad = jnp.pad(tok_p, ((0, -NP % tm_moe), (0, 0)))
    rank, gate, cnt = _router(tok_pad, rw.T, rb[:, None], tm_moe, NP)
    ffn_p = _moe(tok_pad, rank, gate, cnt.reshape(-1), wup, bup, wdn, bdn, tm_moe, 256, 128)
    y_prompt = _final(h2_p, ffn_p, g_fin, _pick_tile(NP, 1024)).reshape(B, T, D_MODEL)

    xs = x_sample.reshape(DB, D_MODEL)
    cos_s, sin_s = _rope_tables(jnp.full((DB,), past, jnp.int32))
    q_s, qi_s, u_s, kk_s, kiki_s, vv_s, k_s, ki_s, v_s, wi_s = _in_project(xs, g_mix, w_all, cos_s, sin_s, DB, 1)
    mask, take_new = _dsa_sample_mask(
        page_table, qi_s.reshape(DB, IDX_HEADS, IDX_DIM), wi_s[:, :IDX_HEADS, None],
        kiki_s[:, None, :IDX_DIM], jnp.swapaxes(cache_idx_k, 2, 3))
    attn_s = _dsa_sample_attend(
        page_table, q_s.reshape(DB, ATTN_HEADS, HEAD_DIM), kk_s[:, None, :HEAD_DIM], vv_s[:, None, :HEAD_DIM],
        mask, take_new, jnp.swapaxes(cache_k, 2, 3), jnp.swapaxes(cache_v, 2, 3))
    attn_s = attn_s.reshape(DB, ATTN_WIDTH).astype(BF16)
    h1_s, qm_s = _mid_sample_a(xs, attn_s, u_s, jnp.transpose(state_pool[0], (1, 0, 2)), pw, ps, wo, g_mem, wmq, past)
    o_s = _mem_sample(qm_s, cache_mem_k[0].reshape(DB, N_MEM, MEM_WIDTH), cache_mem_v[0].reshape(DB, N_MEM, MEM_WIDTH))
    y_sample = _tail_sample(h1_s, o_s.reshape(DB, MEM_WIDTH), wmo, g_ffn, rw, rb[None, :], wup, bup, wdn, bdn,
                            g_fin).reshape(DB, 1, D_MODEL)

    new_pool_p = u.reshape(B, T, POOL_WIDTH)[:, T - (POOL_MAX - 1):][None]
    new_pool_s = jnp.concatenate([state_pool[0][:, 1:], u_s[:, None, :]], axis=1)[None]
    return (y_prompt, y_sample,
            k_p.reshape(1, B, T, HEAD_DIM), v_p.reshape(1, B, T, HEAD_DIM), ki_p.reshape(1, B, T, IDX_DIM),
            new_pool_p,
            mk_p.reshape(1, B, N_MEM, MEM_HEADS, MEM_HEAD_DIM), mv_p.reshape(1, B, N_MEM, MEM_HEADS, MEM_HEAD_DIM),
            k_s.reshape(1, DB, 1, HEAD_DIM), v_s.reshape(1, DB, 1, HEAD_DIM), ki_s.reshape(1, DB, 1, IDX_DIM),
            new_pool_s)
```

```python
import functools

import jax
import jax.numpy as jnp
from jax import lax
from jax.experimental import pallas as pl
from jax.experimental.pallas import tpu as pltpu

D_MODEL = 1024
HEAD_DIM = 64
ATTN_WIDTH = 512
ATTN_HEADS = 8
IDX_HEADS = 8
IDX_DIM = 64
TOPK_MAX = 256
Q_BLOCK = 128
PAGE_SIZE = 128
POOL_WIDTH = 512
POOL_WINDOWS = (2, 4, 8, 16)
POOL_CG = 128
POOL_MAX = 16
N_MEM = 256
MEM_HEADS = 4
MEM_HEAD_DIM = 128
MEM_WIDTH = MEM_HEADS * MEM_HEAD_DIM
N_EXPERTS = 32
TOP_K = 4
D_FF = 1024
SWIGLU_LIMIT = 7.0
SWIGLU_ALPHA = 1.702
ROPE_THETA = 10000.0
EPS = 1e-6
NEG = -1e30

LANES = 128
MOE_TILE = 13 * LANES
SAMPLE_ROWS = 128
DSA_Q = 256
DSA_KCHUNK = 256
IN_COLS = 3 * 512 + 4 * LANES
VMEM_LIMIT = 56 * 1024 * 1024

BF16 = jnp.bfloat16
F32 = jnp.float32


def _dot(a, b):
    return jnp.dot(a, b, preferred_element_type=F32)


def _dot_nt(a, b):
    return lax.dot_general(a, b, (((1,), (1,)), ((), ())), preferred_element_type=F32)


def _dot_tn(a, b):
    return lax.dot_general(a, b, (((0,), (0,)), ((), ())), preferred_element_type=F32)


def _rms(x, g):
    return x * lax.rsqrt(jnp.mean(x * x, axis=-1, keepdims=True) + EPS) * g


def _params(*sem):
    return pltpu.CompilerParams(dimension_semantics=sem, vmem_limit_bytes=VMEM_LIMIT)


def _in_kernel(x_ref, g_ref, w_ref, cos_ref, sin_ref,
               q_ref, qi_ref, u_ref, kk_ref, kiki_ref, vv_ref, k_ref, ki_ref, v_ref, wi_ref):
    xn = _rms(x_ref[...], g_ref[...]).astype(BF16)
    z = _dot(xn, w_ref[...])
    cos = cos_ref[...]
    sin = sin_ref[...]
    lane = lax.broadcasted_iota(jnp.int32, cos.shape, 1)
    first_half = (lane % HEAD_DIM) < (HEAD_DIM // 2)

    def rope(zc):
        swapped = jnp.where(first_half,
                            pltpu.roll(zc, LANES - HEAD_DIM // 2, 1),
                            pltpu.roll(zc, HEAD_DIM // 2, 1))
        return zc * cos + swapped * sin

    for c in range(4):
        q_c = rope(z[:, c * LANES:(c + 1) * LANES]) * (HEAD_DIM ** -0.5)
        qi_c = rope(z[:, 512 + c * LANES:512 + (c + 1) * LANES]) * (IDX_DIM ** -0.5)
        q_ref[:, c * LANES:(c + 1) * LANES] = q_c.astype(BF16)
        qi_ref[:, c * LANES:(c + 1) * LANES] = qi_c.astype(BF16)
    u_ref[...] = z[:, 1024:1536]
    kk = rope(z[:, 1536:1664])
    kiki = rope(z[:, 1664:1792])
    vv = z[:, 1792:1920]
    kk_ref[...] = kk.astype(BF16)
    kiki_ref[...] = kiki.astype(BF16)
    vv_ref[...] = vv.astype(BF16)
    k_ref[...] = kk[:, :HEAD_DIM]
    ki_ref[...] = kiki[:, :IDX_DIM]
    v_ref[...] = vv[:, :HEAD_DIM]
    wi_ref[...] = z[:, 1920:2048]


def _in_project(x, g, w_all, cos, sin, tm, table_blocks):
    n = x.shape[0]
    row = lambda i: (i, 0)
    tab = lambda i: (i % table_blocks, 0)
    fixed = lambda i: (0, 0)
    outs = [
        jax.ShapeDtypeStruct((n, 512), BF16),
        jax.ShapeDtypeStruct((n, 512), BF16),
        jax.ShapeDtypeStruct((n, 512), F32),
        jax.ShapeDtypeStruct((n, LANES), BF16),
        jax.ShapeDtypeStruct((n, LANES), BF16),
        jax.ShapeDtypeStruct((n, LANES), BF16),
        jax.ShapeDtypeStruct((n, HEAD_DIM), F32),
        jax.ShapeDtypeStruct((n, IDX_DIM), F32),
        jax.ShapeDtypeStruct((n, HEAD_DIM), F32),
        jax.ShapeDtypeStruct((n, LANES), F32),
    ]
    out_specs = [pl.BlockSpec((tm, s.shape[1]), row) for s in outs]
    return pl.pallas_call(
        _in_kernel,
        grid=(n // tm,),
        in_specs=[pl.BlockSpec((tm, D_MODEL), row),
                  pl.BlockSpec((1, D_MODEL), fixed),
                  pl.BlockSpec((D_MODEL, IN_COLS), fixed),
                  pl.BlockSpec((tm, LANES), tab),
                  pl.BlockSpec((tm, LANES), tab)],
        out_specs=out_specs,
        out_shape=outs,
        compiler_params=_params("parallel"),
        name="in_project",
    )(x, g, w_all, cos, sin)


def _f32_to_key(x):
    bits = lax.bitcast_convert_type(x, jnp.int32)
    return jnp.where(bits < 0, bits ^ jnp.int32(0x7FFFFFFF), bits)


def _key_to_f32(key):
    bits = jnp.where(key < 0, key ^ jnp.int32(0x7FFFFFFF), key)
    return lax.bitcast_convert_type(bits, F32)


def _kth_largest(count_ge, k, shape):
    def body(it, tau):
        cand = tau + jnp.left_shift(jnp.int32(1), jnp.int32(31) - it)
        return jnp.where(count_ge(_key_to_f32(cand)) >= k, cand, tau)

    return _key_to_f32(lax.fori_loop(0, 32, body, jnp.full(shape, -2 ** 31, jnp.int32)))


def _strict_triangle(n, lower):
    r = lax.broadcasted_iota(jnp.int32, (n, n), 0)
    c = lax.broadcasted_iota(jnp.int32, (n, n), 1)
    return jnp.where((r > c) if lower else (r < c), 1.0, 0.0).astype(BF16)


def _take_first_ties(eq, need_f, tied, axis):
    n = eq.shape[axis]
    cshape = (1, eq.shape[1]) if axis == 0 else (eq.shape[0], 1)

    def ranked():
        tri = _strict_triangle(LANES, lower=(axis == 0))
        carry = jnp.zeros(cshape, F32)
        parts = []
        for c in range(n // LANES):
            sl = slice(c * LANES, (c + 1) * LANES)
            eq_f = jnp.where(eq[sl, :] if axis == 0 else eq[:, sl], 1.0, 0.0)
            eq_b = eq_f.astype(BF16)
            rank = (_dot(tri, eq_b) if axis == 0 else _dot(eq_b, tri)) + carry
            parts.append(jnp.where(rank < need_f, eq_f, 0.0))
            carry = carry + jnp.sum(eq_f, axis=axis, keepdims=True)
        return jnp.concatenate(parts, axis=axis), carry

    def plain():
        return jnp.where(eq, 1.0, 0.0), jnp.zeros(cshape, F32)

    return lax.cond(jnp.max(jnp.where(tied, 1, 0)) > 0, ranked, plain)


def _dsa_prompt_kernel(q_ref, qi_ref, wi_ref, kk_ref, kiki_ref, vvt_ref, o_ref, sm_ref, bias_ref, *,
                       topk, first_block, select):
    i = first_block + pl.program_id(1)
    qb = q_ref.shape[0]
    lk = kk_ref.shape[1]
    ck = min(DSA_KCHUNK, lk)
    chunks = [slice(c * ck, (c + 1) * ck) for c in range(lk // ck)]
    upper_rows = lax.broadcasted_iota(jnp.int32, (LANES, qb), 0) < HEAD_DIM

    def head_operand(ref, h):
        pair_t = jnp.transpose(ref[:, (h // 2) * LANES:(h // 2 + 1) * LANES].astype(F32))
        keep = upper_rows if h % 2 == 0 else jnp.logical_not(upper_rows)
        return jnp.where(keep, pair_t, 0.0).astype(BF16)

    pos_q = i * qb + lax.broadcasted_iota(jnp.int32, (1, qb), 1)

    def causal(sl):
        return sl.start + lax.broadcasted_iota(jnp.int32, (ck, 1), 0) <= pos_q

    if select:
        w_t = jnp.transpose(wi_ref[...])[0:IDX_HEADS, :] * (IDX_HEADS ** -0.5)
        qi_ops = [head_operand(qi_ref, h) for h in range(IDX_HEADS)]
        for sl in chunks:
            keys = kiki_ref[0, sl, :]
            score = jnp.zeros((ck, qb), F32)
            for h in range(IDX_HEADS):
                score = score + jnp.maximum(_dot(keys, qi_ops[h]), 0.0) * w_t[h:h + 1, :]
            sm_ref[sl, :] = jnp.where(causal(sl), score, -jnp.inf)

        def count_ge(cand):
            return jnp.sum(jnp.where(sm_ref[...] >= cand, 1, 0), axis=0, keepdims=True)

        tau = _kth_largest(count_ge, topk, (1, qb))
        sm = sm_ref[...]
        gt = sm > tau
        eq = sm == tau
        n_gt = jnp.sum(jnp.where(gt, 1, 0), axis=0, keepdims=True)
        n_eq = jnp.sum(jnp.where(eq, 1, 0), axis=0, keepdims=True)
        tied = n_gt + n_eq > topk
        need_f = jnp.where(tied, topk - n_gt, lk + 1).astype(F32)
        take_eq, _ = _take_first_ties(eq, need_f, tied, axis=0)
        bias_ref[...] = jnp.where(jnp.logical_or(gt, take_eq > 0.5), 0.0, NEG)

    def chunk_bias(sl):
        if select:
            return bias_ref[sl, :]
        return jnp.where(causal(sl), 0.0, NEG)

    def fold8(x, op):
        return op(x.reshape(ck // 8, 8, qb), axis=0)

    for j in range(ATTN_HEADS // 2):
        outs = []
        for h in (2 * j, 2 * j + 1):
            q_op = head_operand(q_ref, h)
            mx8 = jnp.full((8, qb), -jnp.inf, F32)
            for sl in chunks:
                mx8 = jnp.maximum(mx8, fold8(_dot(kk_ref[0, sl, :], q_op) + chunk_bias(sl), jnp.max))
            m = jnp.max(mx8, axis=0, keepdims=True)
            den8 = jnp.zeros((8, qb), F32)
            acc = jnp.zeros((LANES, qb), F32)
            for sl in chunks:
                p = jnp.exp(_dot(kk_ref[0, sl, :], q_op) + chunk_bias(sl) - m)
                den8 = den8 + fold8(p, jnp.sum)
                acc = acc + _dot(vvt_ref[0, :, sl], p.astype(BF16))
            outs.append(acc * (1.0 / jnp.sum(den8, axis=0, keepdims=True)))
        pair_t = jnp.where(upper_rows, outs[0], outs[1])
        o_ref[0, :, j * LANES:(j + 1) * LANES] = jnp.transpose(pair_t).astype(BF16)


def _dsa_prompt_group(q, qi, wi, kk, kiki, vvt, batch, t_len, first_block, n_blocks, topk):
    nb = t_len // DSA_Q
    lk = (first_block + n_blocks) * DSA_Q
    blk = lambda b, j: (b * nb + first_block + j, 0)
    return pl.pallas_call(
        functools.partial(_dsa_prompt_kernel, topk=topk, first_block=first_block, select=lk > topk),
        grid=(batch, n_blocks),
        in_specs=[pl.BlockSpec((DSA_Q, 512), blk),
                  pl.BlockSpec((DSA_Q, 512), blk),
                  pl.BlockSpec((DSA_Q, LANES), blk),
                  pl.BlockSpec((1, lk, LANES), lambda b, j: (b, 0, 0)),
                  pl.BlockSpec((1, lk, LANES), lambda b, j: (b, 0, 0)),
                  pl.BlockSpec((1, LANES, lk), lambda b, j: (b, 0, 0))],
        out_specs=pl.BlockSpec((1, DSA_Q, 512), lambda b, j: (b, j, 0)),
        out_shape=jax.ShapeDtypeStruct((batch, n_blocks * DSA_Q, 512), BF16),
        scratch_shapes=[pltpu.VMEM((lk, DSA_Q), F32), pltpu.VMEM((lk, DSA_Q), F32)],
        compiler_params=_params("parallel", "parallel"),
        name="dsa_prompt_%d" % lk,
    )(q, qi, wi, kk, kiki, vvt)


def _dsa_prompt(q, qi, wi, kk, kiki, vv, batch, t_len):
    nb = t_len // DSA_Q
    topk = min(TOPK_MAX, t_len // 4)
    assert topk % DSA_Q == 0 and t_len % DSA_Q == 0
    kk, kiki, vv = (a.reshape(batch, t_len, LANES) for a in (kk, kiki, vv))
    vvt = jnp.swapaxes(vv, 1, 2)
    edges = [0, topk // DSA_Q]
    while edges[-1] < nb:
        edges.append(min(edges[-1] + (1 if edges[-1] < 2 else 2), nb))
    parts = [_dsa_prompt_group(q, qi, wi, kk, kiki, vvt, batch, t_len, a, b - a, topk)
             for a, b in zip(edges[:-1], edges[1:]) if b > a]
    return jnp.concatenate(parts, axis=1).reshape(batch * t_len, 512)


def _mem_kv_kernel(m_ref, w_ref, k_ref, v_ref):
    z = _dot(m_ref[...].astype(BF16), w_ref[...])
    k_ref[...] = z[:, :MEM_WIDTH]
    v_ref[...] = z[:, MEM_WIDTH:]


def _mem_kv(mem, w_kv):
    n = mem.shape[0]
    tm = min(n, 512)
    row = lambda i: (i, 0)
    return pl.pallas_call(
        _mem_kv_kernel,
        grid=(n // tm,),
        in_specs=[pl.BlockSpec((tm, D_MODEL), row),
                  pl.BlockSpec((D_MODEL, 2 * MEM_WIDTH), lambda i: (0, 0))],
        out_specs=[pl.BlockSpec((tm, MEM_WIDTH), row), pl.BlockSpec((tm, MEM_WIDTH), row)],
        out_shape=[jax.ShapeDtypeStruct((n, MEM_WIDTH), F32)] * 2,
        compiler_params=_params("parallel"),
        name="mem_kv",
    )(mem, w_kv)


def _pool_project(window_sums, u, cnts, pw_ref, ps_ref):
    ys = []
    for g in range(len(POOL_WINDOWS)):
        sl = slice(g * POOL_CG, (g + 1) * POOL_CG)
        pooled = window_sums[g] / cnts[g] - u[:, sl]
        ys.append(_dot(pooled.astype(BF16), pw_ref[g]) * ps_ref[:, sl])
    return jnp.concatenate(ys, axis=1)


def _mid_prompt_kernel(x_ref, attn_ref, u_ref, halo_ref, mk_ref, mv_ref, pw_ref, ps_ref, wo_ref,
                       gm_ref, wmq_ref, wmo_ref, gf_ref, h2_ref, tok_ref, e_ref, *, nt, n_real):
    step = pl.program_id(0)

    @pl.when(step >= n_real)
    def _():
        h2_ref[...] = jnp.zeros_like(h2_ref)
        tok_ref[...] = jnp.zeros_like(tok_ref)

    @pl.when(step < n_real)
    def _():
        _mid_prompt_rows(x_ref, attn_ref, u_ref, halo_ref, mk_ref, mv_ref, pw_ref, ps_ref, wo_ref,
                         gm_ref, wmq_ref, wmo_ref, gf_ref, h2_ref, tok_ref, e_ref, step % nt)


def _mid_prompt_rows(x_ref, attn_ref, u_ref, halo_ref, mk_ref, mv_ref, pw_ref, ps_ref, wo_ref,
                     gm_ref, wmq_ref, wmo_ref, gf_ref, h2_ref, tok_ref, e_ref, i):
    tm = x_ref.shape[0]
    u = u_ref[...]
    halo = halo_ref[...]
    e_ref[0:POOL_MAX, :] = jnp.where(i == 0, jnp.zeros_like(halo), halo)
    e_ref[POOL_MAX:POOL_MAX + tm, :] = u
    pos = i * tm + lax.broadcasted_iota(jnp.int32, (tm, 1), 0)
    sums, cnts = [], []
    for g, w in enumerate(POOL_WINDOWS):
        sl = slice(g * POOL_CG, (g + 1) * POOL_CG)
        s = u[:, sl]
        for j in range(1, w):
            s = s + e_ref[POOL_MAX - j:POOL_MAX - j + tm, sl]
        sums.append(s)
        cnts.append(jnp.minimum(w, pos + 1).astype(F32))
    pool = _pool_project(sums, u, cnts, pw_ref, ps_ref)
    h1 = x_ref[...] + _dot(attn_ref[...], wo_ref[0:ATTN_WIDTH, :]) \
        + _dot(pool.astype(BF16), wo_ref[ATTN_WIDTH:, :])

    qm = _dot(_rms(h1, gm_ref[...]).astype(BF16), wmq_ref[...]).astype(BF16)
    mk = mk_ref[0].astype(BF16)
    mv = mv_ref[0].astype(BF16)
    outs = []
    for h in range(MEM_HEADS):
        sl = slice(h * MEM_HEAD_DIM, (h + 1) * MEM_HEAD_DIM)
        logits = _dot_nt(qm[:, sl], mk[:, sl]) * (MEM_HEAD_DIM ** -0.5)
        m = jnp.max(logits, axis=1, keepdims=True)
        p = jnp.exp(logits - m)
        p = p / jnp.sum(p, axis=1, keepdims=True)
        outs.append(_dot(p.astype(BF16), mv[:, sl]))
    o = jnp.concatenate(outs, axis=1)
    h2 = h1 + _dot(o.astype(BF16), wmo_ref[...])
    h2_ref[...] = h2
    tok_ref[...] = _rms(h2, gf_ref[...]).astype(BF16)


def _mid_prompt(x, attn, u, mk, mv, pool_w, pool_scale, w_o, g_mem, w_mq, w_mo, g_ffn, batch, t_len, tm, n_rows):
    nt = t_len // tm
    hb = tm // POOL_MAX
    n_real = batch * nt
    n_steps = -(-n_rows // tm)
    last = n_real - 1
    row = lambda s: (jnp.minimum(s, last), 0)
    halo = lambda s: (jnp.maximum(jnp.minimum(s, last) * hb - 1, 0), 0)
    fixed2 = lambda s: (0, 0)
    fixed3 = lambda s: (0, 0, 0)
    perb = lambda s: (jnp.minimum(s, last) // nt, 0, 0)
    out_row = lambda s: (s, 0)
    n = n_rows
    return pl.pallas_call(
        functools.partial(_mid_prompt_kernel, nt=nt, n_real=n_real),
        grid=(n_steps,),
        in_specs=[pl.BlockSpec((tm, D_MODEL), row),
                  pl.BlockSpec((tm, ATTN_WIDTH), row),
                  pl.BlockSpec((tm, POOL_WIDTH), row),
                  pl.BlockSpec((POOL_MAX, POOL_WIDTH), halo),
                  pl.BlockSpec((1, N_MEM, MEM_WIDTH), perb),
                  pl.BlockSpec((1, N_MEM, MEM_WIDTH), perb),
                  pl.BlockSpec((len(POOL_WINDOWS), POOL_CG, POOL_CG), fixed3),
                  pl.BlockSpec((1, POOL_WIDTH), fixed2),
                  pl.BlockSpec((D_MODEL, D_MODEL), fixed2),
                  pl.BlockSpec((1, D_MODEL), fixed2),
                  pl.BlockSpec((D_MODEL, MEM_WIDTH), fixed2),
                  pl.BlockSpec((MEM_WIDTH, D_MODEL), fixed2),
                  pl.BlockSpec((1, D_MODEL), fixed2)],
        out_specs=[pl.BlockSpec((tm, D_MODEL), out_row), pl.BlockSpec((tm, D_MODEL), out_row)],
        out_shape=[jax.ShapeDtypeStruct((n, D_MODEL), F32), jax.ShapeDtypeStruct((n, D_MODEL), BF16)],
        scratch_shapes=[pltpu.VMEM((POOL_MAX + tm, POOL_WIDTH), F32)],
        compiler_params=_params("arbitrary"),
        name="mid_prompt",
    )(x, attn, u, u, mk, mv, pool_w, pool_scale, w_o, g_mem, w_mq, w_mo, g_ffn)


def _router_kernel(tok_ref, rw_ref, rb_ref, rank_ref, gate_ref, cnt_ref, *, n_valid):
    tm = tok_ref.shape[0]
    token = pl.program_id(0) * tm + lax.broadcasted_iota(jnp.int32, (1, tm), 1)
    valid = token < n_valid
    logits = _dot_nt(rw_ref[...], tok_ref[...]) + rb_ref[...]
    eid = lax.broadcasted_iota(jnp.int32, (N_EXPERTS, tm), 0)
    vals, hots = [], []
    cur = logits
    for _ in range(TOP_K):
        m = jnp.max(cur, axis=0, keepdims=True)
        idx = jnp.min(jnp.where(cur == m, eid, N_EXPERTS), axis=0, keepdims=True)
        hot = eid == idx
        vals.append(m)
        hots.append(hot)
        cur = jnp.where(hot, -jnp.inf, cur)
    exps = [jnp.exp(v - vals[0]) for v in vals]
    den = exps[0] + exps[1] + exps[2] + exps[3]
    gate = jnp.zeros((N_EXPERTS, tm), F32)
    sel = jnp.zeros((N_EXPERTS, tm), jnp.bool_)
    for hot, ex in zip(hots, exps):
        gate = jnp.where(hot, ex / den, gate)
        sel = jnp.logical_or(sel, hot)
    sel = jnp.logical_and(sel, valid)
    gate_ref[0] = gate

    upper = _strict_triangle(LANES, lower=False)
    carry = jnp.zeros((N_EXPERTS, 1), F32)
    for c in range(tm // LANES):
        sl = slice(c * LANES, (c + 1) * LANES)
        sel_c = sel[:, sl]
        sel_b = jnp.where(sel_c, 1.0, 0.0).astype(BF16)
        rank = _dot(sel_b, upper) + carry
        rank_ref[0, :, sl] = jnp.where(sel_c, rank, -1.0).astype(jnp.int32)
        carry = carry + jnp.sum(sel_b.astype(F32), axis=1, keepdims=True)
    cnt_ref[0] = carry.astype(jnp.int32)


def _router(tok, rw_t, rb, tm, n_valid):
    n = tok.shape[0]
    nt = n // tm
    t3 = lambda i: (i, 0, 0)
    return pl.pallas_call(
        functools.partial(_router_kernel, n_valid=n_valid),
        grid=(nt,),
        in_specs=[pl.BlockSpec((tm, D_MODEL), lambda i: (i, 0)),
                  pl.BlockSpec((N_EXPERTS, D_MODEL), lambda i: (0, 0)),
                  pl.BlockSpec((N_EXPERTS, 1), lambda i: (0, 0))],
        out_specs=[pl.BlockSpec((1, N_EXPERTS, tm), t3),
                   pl.BlockSpec((1, N_EXPERTS, tm), t3),
                   pl.BlockSpec((1, N_EXPERTS, 1), t3)],
        out_shape=[jax.ShapeDtypeStruct((nt, N_EXPERTS, tm), jnp.int32),
                   jax.ShapeDtypeStruct((nt, N_EXPERTS, tm), F32),
                   jax.ShapeDtypeStruct((nt, N_EXPERTS, 1), jnp.int32)],
        compiler_params=_params("parallel"),
        name="router",
    )(tok, rw_t, rb)


def _swiglu(h):
    gate = jnp.minimum(h[:, :D_FF], SWIGLU_LIMIT)
    up = jnp.clip(h[:, D_FF:], -SWIGLU_LIMIT, SWIGLU_LIMIT)
    return gate * jax.nn.sigmoid(SWIGLU_ALPHA * gate) * (up + 1.0)


def _moe_kernel(cnt_ref, x_ref, h2_ref, rank_ref, gate_ref, wu_ref, bu_ref, wd_ref, bd_ref, gfin_ref,
                o_ref, os_ref, *, first_chunk, chunk, sample_row):
    i = pl.program_id(0)
    e = pl.program_id(1)
    tm = x_ref.shape[0]

    @pl.when(e == 0)
    def _():
        o_ref[...] = jnp.zeros_like(o_ref)

    n_routed = cnt_ref[i * N_EXPERTS + e]
    rank = rank_ref[0, pl.ds(e, 1), :]
    gate = gate_ref[0, pl.ds(e, 1), :]

    def run_slots(first, size):
        slot = first + lax.broadcasted_iota(jnp.int32, (size, tm), 0)
        hit = rank == slot
        onehot = jnp.where(hit, 1.0, 0.0).astype(BF16)
        xg = _dot(onehot, x_ref[...]).astype(BF16)
        act = _swiglu(_dot(xg, wu_ref[0]) + bu_ref[0])
        out = _dot(act.astype(BF16), wd_ref[0]) + bd_ref[0]
        g_slot = jnp.sum(jnp.where(hit, gate, 0.0), axis=1, keepdims=True)
        o_ref[...] += _dot_tn(onehot, (out * g_slot).astype(BF16))

    @pl.when(n_routed > 0)
    def _():
        run_slots(0, first_chunk)

    def body(c, carry):
        run_slots(first_chunk + c * chunk, chunk)
        return carry

    lax.fori_loop(0, (jnp.maximum(n_routed - first_chunk, 0) + chunk - 1) // chunk, body, 0)

    @pl.when(e == pl.num_programs(1) - 1)
    def _():
        o_ref[...] = _rms(h2_ref[...] + o_ref[...], gfin_ref[...])

        @pl.when(i == pl.num_programs(0) - 1)
        def _():
            os_ref[...] = o_ref[sample_row:sample_row + os_ref.shape[0], :]


def _moe(tok, h2, rank, gate, cnt, wu, bu, wd, bd, g_final, tm, n_prompt, first_chunk, chunk):
    nt = tok.shape[0] // tm
    tile = lambda i, e, c: (i, 0)
    tile3 = lambda i, e, c: (i, 0, 0)
    expert = lambda i, e, c: (e, 0, 0)
    grid_spec = pltpu.PrefetchScalarGridSpec(
        num_scalar_prefetch=1,
        grid=(nt, N_EXPERTS),
        in_specs=[pl.BlockSpec((tm, D_MODEL), tile),
                  pl.BlockSpec((tm, D_MODEL), tile),
                  pl.BlockSpec((1, N_EXPERTS, tm), tile3),
                  pl.BlockSpec((1, N_EXPERTS, tm), tile3),
                  pl.BlockSpec((1, D_MODEL, 2 * D_FF), expert),
                  pl.BlockSpec((1, 1, 2 * D_FF), expert),
                  pl.BlockSpec((1, D_FF, D_MODEL), expert),
                  pl.BlockSpec((1, 1, D_MODEL), expert),
                  pl.BlockSpec((1, D_MODEL), lambda i, e, c: (0, 0))],
        out_specs=[pl.BlockSpec((tm, D_MODEL), tile),
                   pl.BlockSpec((SAMPLE_ROWS, D_MODEL), lambda i, e, c: (0, 0))],
    )
    return pl.pallas_call(
        functools.partial(_moe_kernel, first_chunk=first_chunk, chunk=chunk,
                          sample_row=n_prompt - (nt - 1) * tm),
        grid_spec=grid_spec,
        out_shape=[jax.ShapeDtypeStruct((n_prompt, D_MODEL), F32),
                   jax.ShapeDtypeStruct((SAMPLE_ROWS, D_MODEL), F32)],
        compiler_params=_params("arbitrary", "arbitrary"),
        name="moe",
    )(cnt, tok, h2, rank, gate, wu, bu, wd, bd, g_final)


def _page_copy(cache_ref, buf_ref, sem_ref, pt_ref, b, p, slot):
    dst = buf_ref.at[slot, :, pl.ds(pl.multiple_of(p * PAGE_SIZE, PAGE_SIZE), PAGE_SIZE)]
    return pltpu.make_async_copy(cache_ref.at[0, pt_ref[b, p]], dst, sem_ref.at[slot])


def _fetch_pages(cache_ref, buf_ref, sem_ref, pt_ref, b, slot, n_pages):
    def body(p, c):
        _page_copy(cache_ref, buf_ref, sem_ref, pt_ref, b, p, slot).start()
        return c
    lax.fori_loop(0, n_pages, body, 0, unroll=8)


def _wait_pages(cache_ref, buf_ref, sem_ref, pt_ref, b, slot, n_pages):
    def body(p, c):
        _page_copy(cache_ref, buf_ref, sem_ref, pt_ref, b, p, slot).wait()
        return c
    lax.fori_loop(0, n_pages, body, 0, unroll=8)


def _dsa_sample_score_kernel(pt_ref, qi_ref, wi_ref, kin_ref, cache_ref, score_ref, new_ref, buf_ref, sem_ref):
    b = pl.program_id(0)
    nb = pl.num_programs(0)
    past = buf_ref.shape[2]
    n_pages = past // PAGE_SIZE
    slot = b % 2

    @pl.when(b == 0)
    def _():
        _fetch_pages(cache_ref, buf_ref, sem_ref, pt_ref, b, slot, n_pages)

    @pl.when(b + 1 < nb)
    def _():
        _fetch_pages(cache_ref, buf_ref, sem_ref, pt_ref, b + 1, 1 - slot, n_pages)

    _wait_pages(cache_ref, buf_ref, sem_ref, pt_ref, b, slot, n_pages)

    qi = qi_ref[0]
    w = wi_ref[0] * (IDX_HEADS ** -0.5)
    keys_t = buf_ref[slot].astype(BF16)
    score_ref[0] = jnp.sum(jnp.maximum(_dot(qi, keys_t), 0.0) * w, axis=0, keepdims=True)
    s_new = jnp.sum(qi.astype(F32) * kin_ref[0].astype(F32), axis=1, keepdims=True)
    new_ref[0] = jnp.sum(jnp.maximum(s_new, 0.0) * w, axis=0, keepdims=True)


def _dsa_sample_select_kernel(score_ref, new_ref, mask_ref, take_new_ref, *, topk):
    score = score_ref[...]
    score_new = new_ref[...]
    past = score.shape[1]

    def count_ge(cand):
        return (jnp.sum(jnp.where(score >= cand, 1, 0), axis=1, keepdims=True)
                + jnp.where(score_new >= cand, 1, 0))

    tau = _kth_largest(count_ge, topk, score_new.shape)
    gt = score > tau
    eq = score == tau
    n_gt = jnp.sum(jnp.where(gt, 1, 0), axis=1, keepdims=True) + jnp.where(score_new > tau, 1, 0)
    n_eq_all = jnp.sum(jnp.where(eq, 1, 0), axis=1, keepdims=True) + jnp.where(score_new == tau, 1, 0)
    tied = n_gt + n_eq_all > topk
    need_f = jnp.where(tied, topk - n_gt, past + 2).astype(F32)
    take_eq, n_eq = _take_first_ties(eq, need_f, tied, axis=1)
    mask_ref[...] = jnp.where(jnp.logical_or(gt, take_eq > 0.5), 0.0, NEG)
    take_new = jnp.logical_or(score_new > tau, jnp.logical_and(score_new == tau, n_eq < need_f))
    take_new_ref[...] = jnp.where(take_new, 0.0, NEG)


def _dsa_sample_select(score, score_new, topk):
    db, past = score.shape
    full = lambda shape: pl.BlockSpec(shape, lambda i: (0, 0))
    return pl.pallas_call(
        functools.partial(_dsa_sample_select_kernel, topk=topk),
        grid=(1,),
        in_specs=[full((db, past)), full((db, 1))],
        out_specs=[full((db, past)), full((db, 1))],
        out_shape=[jax.ShapeDtypeStruct((db, past), F32), jax.ShapeDtypeStruct((db, 1), F32)],
        compiler_params=_params("arbitrary"),
        name="dsa_sample_select",
    )(score, score_new)


def _dsa_sample_mask(page_table, qi_heads, wi_col, ki_new, cache_idx_k):
    db, n_pages = page_table.shape
    past = n_pages * PAGE_SIZE
    topk = min(TOPK_MAX, (past + 1) // 4)
    per = lambda b, pt: (b, 0, 0)
    grid_spec = pltpu.PrefetchScalarGridSpec(
        num_scalar_prefetch=1,
        grid=(db,),
        in_specs=[pl.BlockSpec((1, IDX_HEADS, IDX_DIM), per),
                  pl.BlockSpec((1, IDX_HEADS, 1), per),
                  pl.BlockSpec((1, 1, IDX_DIM), per),
                  pl.BlockSpec(memory_space=pl.ANY)],
        out_specs=[pl.BlockSpec((1, 1, past), per), pl.BlockSpec((1, 1, 1), per)],
        scratch_shapes=[pltpu.VMEM((2, IDX_DIM, past), F32),
                        pltpu.SemaphoreType.DMA((2,))],
    )
    score, score_new = pl.pallas_call(
        _dsa_sample_score_kernel,
        grid_spec=grid_spec,
        out_shape=[jax.ShapeDtypeStruct((db, 1, past), F32), jax.ShapeDtypeStruct((db, 1, 1), F32)],
        compiler_params=_params("arbitrary"),
        name="dsa_sample_score",
    )(page_table, qi_heads, wi_col, ki_new, cache_idx_k)
    mask, take_new = _dsa_sample_select(score.reshape(db, past), score_new.reshape(db, 1), topk)
    return mask.reshape(db, 1, past), take_new.reshape(db, 1, 1)


def _dsa_sample_attend_kernel(pt_ref, q_ref, kn_ref, vn_ref, mask_ref, new_ref, ck_ref, cv_ref, o_ref,
                              kbuf_ref, vbuf_ref, ksem_ref, vsem_ref):
    b = pl.program_id(0)
    nb = pl.num_programs(0)
    past = kbuf_ref.shape[2]
    n_pages = past // PAGE_SIZE
    slot = b % 2

    def fetch(bb, sl):
        _fetch_pages(ck_ref, kbuf_ref, ksem_ref, pt_ref, bb, sl, n_pages)
        _fetch_pages(cv_ref, vbuf_ref, vsem_ref, pt_ref, bb, sl, n_pages)

    @pl.when(b == 0)
    def _():
        fetch(b, slot)

    @pl.when(b + 1 < nb)
    def _():
        fetch(b + 1, 1 - slot)

    _wait_pages(ck_ref, kbuf_ref, ksem_ref, pt_ref, b, slot, n_pages)
    _wait_pages(cv_ref, vbuf_ref, vsem_ref, pt_ref, b, slot, n_pages)

    q = q_ref[0]
    keys_t = kbuf_ref[slot].astype(BF16)
    vals_t = vbuf_ref[slot].astype(BF16)
    logits = _dot(q, keys_t) + mask_ref[0]
    l_new = jnp.sum(q.astype(F32) * kn_ref[0].astype(F32), axis=1, keepdims=True) + new_ref[0]
    m = jnp.maximum(jnp.max(logits, axis=1, keepdims=True), l_new)
    p = jnp.exp(logits - m)
    p_new = jnp.exp(l_new - m)
    den = jnp.sum(p, axis=1, keepdims=True) + p_new
    p = p / den
    p_new = p_new / den
    o_ref[0] = _dot_nt(p.astype(BF16), vals_t) + p_new * vn_ref[0].astype(F32)


def _dsa_sample_attend(page_table, q_heads, k_new, v_new, mask, take_new, cache_k, cache_v):
    db, n_pages = page_table.shape
    past = n_pages * PAGE_SIZE
    per = lambda b, pt: (b, 0, 0)
    grid_spec = pltpu.PrefetchScalarGridSpec(
        num_scalar_prefetch=1,
        grid=(db,),
        in_specs=[pl.BlockSpec((1, ATTN_HEADS, HEAD_DIM), per),
                  pl.BlockSpec((1, 1, HEAD_DIM), per),
                  pl.BlockSpec((1, 1, HEAD_DIM), per),
                  pl.BlockSpec((1, 1, past), per),
                  pl.BlockSpec((1, 1, 1), per),
                  pl.BlockSpec(memory_space=pl.ANY),
                  pl.BlockSpec(memory_space=pl.ANY)],
        out_specs=pl.BlockSpec((1, ATTN_HEADS, HEAD_DIM), per),
        scratch_shapes=[pltpu.VMEM((2, HEAD_DIM, past), F32),
                        pltpu.VMEM((2, HEAD_DIM, past), F32),
                        pltpu.SemaphoreType.DMA((2,)),
                        pltpu.SemaphoreType.DMA((2,))],
    )
    return pl.pallas_call(
        _dsa_sample_attend_kernel,
        grid_spec=grid_spec,
        out_shape=jax.ShapeDtypeStruct((db, ATTN_HEADS, HEAD_DIM), F32),
        compiler_params=_params("arbitrary"),
        name="dsa_sample_attend",
    )(page_table, q_heads, k_new, v_new, mask, take_new, cache_k, cache_v)


def _mid_sample_a_kernel(x_ref, attn_ref, u_ref, st_ref, pw_ref, ps_ref, wo_ref, gm_ref, wmq_ref,
                         h1_ref, qm_ref, *, past):
    u = u_ref[...]
    rows = u.shape[0]
    sums, cnts = [], []
    for g, w in enumerate(POOL_WINDOWS):
        sl = slice(g * POOL_CG, (g + 1) * POOL_CG)
        s = u[:, sl]
        for j in range(1, w):
            s = s + st_ref[POOL_MAX - 1 - j][:, sl]
        sums.append(s)
        cnts.append(jnp.full((rows, 1), float(min(w, past + 1)), F32))
    pool = _pool_project(sums, u, cnts, pw_ref, ps_ref)
    h1 = x_ref[...] + _dot(attn_ref[...], wo_ref[0:ATTN_WIDTH, :]) \
        + _dot(pool.astype(BF16), wo_ref[ATTN_WIDTH:, :])
    h1_ref[...] = h1
    qm_ref[...] = _dot(_rms(h1, gm_ref[...]).astype(BF16), wmq_ref[...]).astype(BF16)


def _mid_sample_a(x, attn, u, state_t, pool_w, pool_scale, w_o, g_mem, w_mq, past):
    db = x.shape[0]
    full = lambda a: pl.BlockSpec(a.shape, lambda i, nd=a.ndim: (0,) * nd)
    args = (x, attn, u, state_t, pool_w, pool_scale, w_o, g_mem, w_mq)
    return pl.pallas_call(
        functools.partial(_mid_sample_a_kernel, past=past),
        grid=(1,),
        in_specs=[full(a) for a in args],
        out_specs=[pl.BlockSpec((db, D_MODEL), lambda i: (0, 0)),
                   pl.BlockSpec((db, MEM_WIDTH), lambda i: (0, 0))],
        out_shape=[jax.ShapeDtypeStruct((db, D_MODEL), F32), jax.ShapeDtypeStruct((db, MEM_WIDTH), BF16)],
        compiler_params=_params("arbitrary"),
        name="mid_sample_a",
    )(*args)


def _mem_sample_kernel(q_ref, mk_ref, mv_ref, o_ref):
    rows = 8
    head_of_row = lax.broadcasted_iota(jnp.int32, (rows, MEM_WIDTH), 0)
    head_of_lane = lax.broadcasted_iota(jnp.int32, (rows, MEM_WIDTH), 1) // MEM_HEAD_DIM
    own = head_of_row == head_of_lane
    q = jnp.broadcast_to(q_ref[0].astype(F32), (rows, MEM_WIDTH))
    qh = jnp.where(own, q, 0.0).astype(BF16)
    logits = _dot_nt(qh, mk_ref[0].astype(BF16)) * (MEM_HEAD_DIM ** -0.5)
    m = jnp.max(logits, axis=1, keepdims=True)
    p = jnp.exp(logits - m)
    p = p / jnp.sum(p, axis=1, keepdims=True)
    o = _dot(p.astype(BF16), mv_ref[0].astype(BF16))
    o_ref[0] = jnp.sum(jnp.where(own, o, 0.0), axis=0, keepdims=True)


def _mem_sample(qm, mk, mv):
    db = qm.shape[0]
    per = lambda b: (b, 0, 0)
    return pl.pallas_call(
        _mem_sample_kernel,
        grid=(db,),
        in_specs=[pl.BlockSpec((1, 1, MEM_WIDTH), per),
                  pl.BlockSpec((1, N_MEM, MEM_WIDTH), per),
                  pl.BlockSpec((1, N_MEM, MEM_WIDTH), per)],
        out_specs=pl.BlockSpec((1, 1, MEM_WIDTH), per),
        out_shape=jax.ShapeDtypeStruct((db, 1, MEM_WIDTH), F32),
        compiler_params=_params("parallel"),
        name="mem_sample",
    )(qm.reshape(db, 1, MEM_WIDTH), mk, mv)


def _sample_join_kernel(h1_ref, o_ref, wmo_ref, gf_ref, h2_any, tok_any, h2_ref, tok_ref):
    del h2_any, tok_any
    db = h1_ref.shape[0]
    h2_ref[...] = jnp.zeros_like(h2_ref)
    tok_ref[...] = jnp.zeros_like(tok_ref)

    @pl.when(pl.program_id(0) == 0)
    def _():
        h2 = h1_ref[...] + _dot(o_ref[...].astype(BF16), wmo_ref[...])
        h2_ref[0:db, :] = h2
        tok_ref[0:db, :] = _rms(h2, gf_ref[...]).astype(BF16)


def _sample_join(h1, o, w_mo, g_ffn, h2_all, tok_all, n_prompt):
    db = h1.shape[0]
    n_rows = h2_all.shape[0]
    assert db <= SAMPLE_ROWS and n_prompt % SAMPLE_ROWS == 0 and n_rows % SAMPLE_ROWS == 0
    first = n_prompt // SAMPLE_ROWS
    f2 = lambda j: (0, 0)
    tail = lambda j: (first + j, 0)
    return pl.pallas_call(
        _sample_join_kernel,
        grid=((n_rows - n_prompt) // SAMPLE_ROWS,),
        in_specs=[pl.BlockSpec((db, D_MODEL), f2),
                  pl.BlockSpec((db, MEM_WIDTH), f2),
                  pl.BlockSpec((MEM_WIDTH, D_MODEL), f2),
                  pl.BlockSpec((1, D_MODEL), f2),
                  pl.BlockSpec(memory_space=pl.ANY),
                  pl.BlockSpec(memory_space=pl.ANY)],
        out_specs=[pl.BlockSpec((SAMPLE_ROWS, D_MODEL), tail), pl.BlockSpec((SAMPLE_ROWS, D_MODEL), tail)],
        out_shape=[jax.ShapeDtypeStruct(h2_all.shape, F32), jax.ShapeDtypeStruct(tok_all.shape, BF16)],
        input_output_aliases={4: 0, 5: 1},
        compiler_params=_params("arbitrary"),
        name="sample_join",
    )(h1, o, w_mo, g_ffn, h2_all, tok_all)


def _rope_tables(pos):
    half = HEAD_DIM // 2
    inv = ROPE_THETA ** (-jnp.arange(half, dtype=F32) / half)
    ang = pos.astype(F32)[:, None] * inv[None, :]
    cos = jnp.cos(ang)
    sin = jnp.sin(ang)
    return jnp.tile(cos, (1, 4)), jnp.concatenate([-sin, sin, -sin, sin], axis=1)


def _pick_tile(n, pref):
    t = min(n, pref)
    while n % t:
        t //= 2
    return t


def kernel(x_prompt, x_sample, mem_prompt, cache_k, cache_v, cache_idx_k, state_pool, cache_mem_k, cache_mem_v, page_table, norm_mix, w_in, pool_w, pool_scale, w_o, norm_mem, w_mq, w_mk, w_mv, w_mo, norm_ffn, router_w, router_b, w_up, b_up, w_down, b_down, norm_final):
    B, T = x_prompt.shape[:2]
    DB, DS = x_sample.shape[:2]
    assert DS == 1 and w_in.shape[0] == 1
    n_pages = page_table.shape[1]
    past = n_pages * PAGE_SIZE
    NP = B * T

    wi_ = w_in[0]
    wq, wk, wv, wu_, wqi, wki, wwi = jnp.split(wi_, [512, 576, 640, 1152, 1664, 1728], axis=1)
    w_all = jnp.concatenate(
        [wq, wqi, wu_, wk, wk, wki, wki, wv, wv, wwi, jnp.zeros((D_MODEL, LANES - IDX_HEADS), F32)],
        axis=1).astype(BF16)
    g_mix = norm_mix[0][None, :]
    g_mem = norm_mem[0][None, :]
    g_ffn = norm_ffn[0][None, :]
    g_fin = norm_final[None, :]
    pw = pool_w[0].astype(BF16)
    ps = pool_scale[0][None, :]
    wo = w_o[0].astype(BF16)
    wmq = w_mq[0].reshape(D_MODEL, MEM_WIDTH).astype(BF16)
    wkv = jnp.concatenate([w_mk[0].reshape(D_MODEL, MEM_WIDTH), w_mv[0].reshape(D_MODEL, MEM_WIDTH)],
                          axis=1).astype(BF16)
    wmo = w_mo[0].reshape(MEM_WIDTH, D_MODEL).astype(BF16)
    rw = router_w[0].astype(BF16)
    rb = router_b[0]
    wup = w_up[0].astype(BF16)
    wdn = w_down[0].astype(BF16)
    bup = b_up[0][:, None, :]
    bdn = b_down[0][:, None, :]

    xp = x_prompt.reshape(NP, D_MODEL)
    tm_in = _pick_tile(T, 512)
    cos_p, sin_p = _rope_tables(jnp.arange(T, dtype=jnp.int32))
    q, qi, u, kk, kiki, vv, k_p, ki_p, v_p, wi_p = _in_project(xp, g_mix, w_all, cos_p, sin_p, tm_in, T // tm_in)
    attn = _dsa_prompt(q, qi, wi_p, kk, kiki, vv, B, T)
    mk_p, mv_p = _mem_kv(mem_prompt.reshape(B * N_MEM, D_MODEL), wkv)
    tm_moe = min(MOE_TILE, NP)
    n_tiles = -(-(NP + SAMPLE_ROWS) // tm_moe)
    assert (n_tiles - 1) * tm_moe <= NP and NP + SAMPLE_ROWS <= n_tiles * tm_moe
    h2_all, tok_all = _mid_prompt(xp, attn, u, mk_p.reshape(B, N_MEM, MEM_WIDTH), mv_p.reshape(B, N_MEM, MEM_WIDTH),
                                  pw, ps, wo, g_mem, wmq, wmo, g_ffn, B, T, _pick_tile(T, 512), n_tiles * tm_moe)

    xs = x_sample.reshape(DB, D_MODEL)
    cos_s, sin_s = _rope_tables(jnp.full((DB,), past, jnp.int32))
    q_s, qi_s, u_s, kk_s, kiki_s, vv_s, k_s, ki_s, v_s, wi_s = _in_project(xs, g_mix, w_all, cos_s, sin_s, DB, 1)
    mask, take_new = _dsa_sample_mask(
        page_table, qi_s.reshape(DB, IDX_HEADS, IDX_DIM), wi_s[:, :IDX_HEADS, None],
        kiki_s[:, None, :IDX_DIM], jnp.swapaxes(cache_idx_k, 2, 3))
    attn_s = _dsa_sample_attend(
        page_table, q_s.reshape(DB, ATTN_HEADS, HEAD_DIM), kk_s[:, None, :HEAD_DIM], vv_s[:, None, :HEAD_DIM],
        mask, take_new, jnp.swapaxes(cache_k, 2, 3), jnp.swapaxes(cache_v, 2, 3))
    attn_s = attn_s.reshape(DB, ATTN_WIDTH).astype(BF16)
    h1_s, qm_s = _mid_sample_a(xs, attn_s, u_s, jnp.transpose(state_pool[0], (1, 0, 2)), pw, ps, wo, g_mem, wmq, past)
    o_s = _mem_sample(qm_s, cache_mem_k[0].reshape(DB, N_MEM, MEM_WIDTH), cache_mem_v[0].reshape(DB, N_MEM, MEM_WIDTH))
    h2_all, tok_all = _sample_join(h1_s, o_s.reshape(DB, MEM_WIDTH), wmo, g_ffn, h2_all, tok_all, NP)

    rank, gate, cnt = _router(tok_all, rw.T, rb[:, None], tm_moe, NP + DB)
    y_p, y_s = _moe(tok_all, h2_all, rank, gate, cnt.reshape(-1), wup, bup, wdn, bdn, g_fin, tm_moe, NP, 256, 128)
    y_prompt = y_p.reshape(B, T, D_MODEL)
    y_sample = y_s[:DB].reshape(DB, 1, D_MODEL)

    new_pool_p = u.reshape(B, T, POOL_WIDTH)[:, T - (POOL_MAX - 1):][None]
    new_pool_s = jnp.concatenate([state_pool[0][:, 1:], u_s[:, None, :]], axis=1)[None]
    return (y_prompt, y_sample,
            k_p.reshape(1, B, T, HEAD_DIM), v_p.reshape(1, B, T, HEAD_DIM), ki_p.reshape(1, B, T, IDX_DIM),
            new_pool_p,
            mk_p.reshape(1, B, N_MEM, MEM_HEADS, MEM_HEAD_DIM), mv_p.reshape(1, B, N_MEM, MEM_HEADS, MEM_HEAD_DIM),
            k_s.reshape(1, DB, 1, HEAD_DIM), v_s.reshape(1, DB, 1, HEAD_DIM), ki_s.reshape(1, DB, 1, IDX_DIM),
            new_pool_s)
```

```python
import functools

import jax
import jax.numpy as jnp
from jax import lax
from jax.experimental import pallas as pl
from jax.experimental.pallas import tpu as pltpu

D_MODEL = 1024
HEAD_DIM = 64
ATTN_WIDTH = 512
ATTN_HEADS = 8
IDX_HEADS = 8
IDX_DIM = 64
TOPK_MAX = 256
Q_BLOCK = 128
PAGE_SIZE = 128
POOL_WIDTH = 512
POOL_WINDOWS = (2, 4, 8, 16)
POOL_CG = 128
POOL_MAX = 16
N_MEM = 256
MEM_HEADS = 4
MEM_HEAD_DIM = 128
MEM_WIDTH = MEM_HEADS * MEM_HEAD_DIM
N_EXPERTS = 32
TOP_K = 4
D_FF = 1024
SWIGLU_LIMIT = 7.0
SWIGLU_ALPHA = 1.702
ROPE_THETA = 10000.0
EPS = 1e-6
NEG = -1e30

LANES = 128
MOE_TILE = 13 * LANES
MOE_FIRST_SIZES = (208, 224)
SAMPLE_ROWS = 128
DSA_Q = 256
DSA_KCHUNK = 256
IN_COLS = 3 * 512 + 4 * LANES
VMEM_LIMIT = 56 * 1024 * 1024

BF16 = jnp.bfloat16
F32 = jnp.float32


def _dot(a, b):
    return jnp.dot(a, b, preferred_element_type=F32)


def _dot_nt(a, b):
    return lax.dot_general(a, b, (((1,), (1,)), ((), ())), preferred_element_type=F32)


def _dot_tn(a, b):
    return lax.dot_general(a, b, (((0,), (0,)), ((), ())), preferred_element_type=F32)


def _rms(x, g):
    return x * lax.rsqrt(jnp.mean(x * x, axis=-1, keepdims=True) + EPS) * g


def _params(*sem):
    return pltpu.CompilerParams(dimension_semantics=sem, vmem_limit_bytes=VMEM_LIMIT)


def _in_kernel(x_ref, g_ref, w_ref, cos_ref, sin_ref,
               q_ref, qi_ref, u_ref, kk_ref, kiki_ref, vv_ref, k_ref, ki_ref, v_ref, wi_ref):
    xn = _rms(x_ref[...], g_ref[...]).astype(BF16)
    z = _dot(xn, w_ref[...])
    cos = cos_ref[...]
    sin = sin_ref[...]
    lane = lax.broadcasted_iota(jnp.int32, cos.shape, 1)
    first_half = (lane % HEAD_DIM) < (HEAD_DIM // 2)

    def rope(zc):
        swapped = jnp.where(first_half,
                            pltpu.roll(zc, LANES - HEAD_DIM // 2, 1),
                            pltpu.roll(zc, HEAD_DIM // 2, 1))
        return zc * cos + swapped * sin

    for c in range(4):
        q_c = rope(z[:, c * LANES:(c + 1) * LANES]) * (HEAD_DIM ** -0.5)
        qi_c = rope(z[:, 512 + c * LANES:512 + (c + 1) * LANES]) * (IDX_DIM ** -0.5)
        q_ref[:, c * LANES:(c + 1) * LANES] = q_c.astype(BF16)
        qi_ref[:, c * LANES:(c + 1) * LANES] = qi_c.astype(BF16)
    u_ref[...] = z[:, 1024:1536]
    kk = rope(z[:, 1536:1664])
    kiki = rope(z[:, 1664:1792])
    vv = z[:, 1792:1920]
    kk_ref[...] = kk.astype(BF16)
    kiki_ref[...] = kiki.astype(BF16)
    vv_ref[...] = vv.astype(BF16)
    k_ref[...] = kk[:, :HEAD_DIM]
    ki_ref[...] = kiki[:, :IDX_DIM]
    v_ref[...] = vv[:, :HEAD_DIM]
    wi_ref[...] = z[:, 1920:2048]


def _in_project(x, g, w_all, cos, sin, tm, table_blocks):
    n = x.shape[0]
    row = lambda i: (i, 0)
    tab = lambda i: (i % table_blocks, 0)
    fixed = lambda i: (0, 0)
    outs = [
        jax.ShapeDtypeStruct((n, 512), BF16),
        jax.ShapeDtypeStruct((n, 512), BF16),
        jax.ShapeDtypeStruct((n, 512), F32),
        jax.ShapeDtypeStruct((n, LANES), BF16),
        jax.ShapeDtypeStruct((n, LANES), BF16),
        jax.ShapeDtypeStruct((n, LANES), BF16),
        jax.ShapeDtypeStruct((n, HEAD_DIM), F32),
        jax.ShapeDtypeStruct((n, IDX_DIM), F32),
        jax.ShapeDtypeStruct((n, HEAD_DIM), F32),
        jax.ShapeDtypeStruct((n, LANES), F32),
    ]
    out_specs = [pl.BlockSpec((tm, s.shape[1]), row) for s in outs]
    return pl.pallas_call(
        _in_kernel,
        grid=(n // tm,),
        in_specs=[pl.BlockSpec((tm, D_MODEL), row),
                  pl.BlockSpec((1, D_MODEL), fixed),
                  pl.BlockSpec((D_MODEL, IN_COLS), fixed),
                  pl.BlockSpec((tm, LANES), tab),
                  pl.BlockSpec((tm, LANES), tab)],
        out_specs=out_specs,
        out_shape=outs,
        compiler_params=_params("parallel"),
        name="in_project",
    )(x, g, w_all, cos, sin)


def _f32_to_key(x):
    bits = lax.bitcast_convert_type(x, jnp.int32)
    return jnp.where(bits < 0, bits ^ jnp.int32(0x7FFFFFFF), bits)


def _key_to_f32(key):
    bits = jnp.where(key < 0, key ^ jnp.int32(0x7FFFFFFF), key)
    return lax.bitcast_convert_type(bits, F32)


def _kth_largest(count_ge, k, shape):
    def body(it, tau):
        cand = tau + jnp.left_shift(jnp.int32(1), jnp.int32(31) - it)
        return jnp.where(count_ge(_key_to_f32(cand)) >= k, cand, tau)

    return _key_to_f32(lax.fori_loop(0, 32, body, jnp.full(shape, -2 ** 31, jnp.int32)))


def _strict_triangle(n, lower):
    r = lax.broadcasted_iota(jnp.int32, (n, n), 0)
    c = lax.broadcasted_iota(jnp.int32, (n, n), 1)
    return jnp.where((r > c) if lower else (r < c), 1.0, 0.0).astype(BF16)


def _take_first_ties(eq, need_f, tied, axis):
    n = eq.shape[axis]
    cshape = (1, eq.shape[1]) if axis == 0 else (eq.shape[0], 1)

    def ranked():
        tri = _strict_triangle(LANES, lower=(axis == 0))
        carry = jnp.zeros(cshape, F32)
        parts = []
        for c in range(n // LANES):
            sl = slice(c * LANES, (c + 1) * LANES)
            eq_f = jnp.where(eq[sl, :] if axis == 0 else eq[:, sl], 1.0, 0.0)
            eq_b = eq_f.astype(BF16)
            rank = (_dot(tri, eq_b) if axis == 0 else _dot(eq_b, tri)) + carry
            parts.append(jnp.where(rank < need_f, eq_f, 0.0))
            carry = carry + jnp.sum(eq_f, axis=axis, keepdims=True)
        return jnp.concatenate(parts, axis=axis), carry

    def plain():
        return jnp.where(eq, 1.0, 0.0), jnp.zeros(cshape, F32)

    return lax.cond(jnp.max(jnp.where(tied, 1, 0)) > 0, ranked, plain)


def _dsa_prompt_kernel(q_ref, qi_ref, wi_ref, kk_ref, kiki_ref, vvt_ref, o_ref, sm_ref, bias_ref, *,
                       topk, first_block, select):
    i = first_block + pl.program_id(1)
    qb = q_ref.shape[0]
    lk = kk_ref.shape[1]
    ck = min(DSA_KCHUNK, lk)
    chunks = [slice(c * ck, (c + 1) * ck) for c in range(lk // ck)]
    upper_rows = lax.broadcasted_iota(jnp.int32, (LANES, qb), 0) < HEAD_DIM

    def head_operand(ref, h):
        pair_t = jnp.transpose(ref[:, (h // 2) * LANES:(h // 2 + 1) * LANES].astype(F32))
        keep = upper_rows if h % 2 == 0 else jnp.logical_not(upper_rows)
        return jnp.where(keep, pair_t, 0.0).astype(BF16)

    pos_q = i * qb + lax.broadcasted_iota(jnp.int32, (1, qb), 1)

    def causal(sl):
        return sl.start + lax.broadcasted_iota(jnp.int32, (ck, 1), 0) <= pos_q

    if select:
        w_t = jnp.transpose(wi_ref[...])[0:IDX_HEADS, :] * (IDX_HEADS ** -0.5)
        qi_ops = [head_operand(qi_ref, h) for h in range(IDX_HEADS)]
        for sl in chunks:
            keys = kiki_ref[0, sl, :]
            score = jnp.zeros((ck, qb), F32)
            for h in range(IDX_HEADS):
                score = score + jnp.maximum(_dot(keys, qi_ops[h]), 0.0) * w_t[h:h + 1, :]
            sm_ref[sl, :] = jnp.where(causal(sl), score, -jnp.inf)

        def count_ge(cand):
            return jnp.sum(jnp.where(sm_ref[...] >= cand, 1, 0), axis=0, keepdims=True)

        tau = _kth_largest(count_ge, topk, (1, qb))
        sm = sm_ref[...]
        gt = sm > tau
        eq = sm == tau
        n_gt = jnp.sum(jnp.where(gt, 1, 0), axis=0, keepdims=True)
        n_eq = jnp.sum(jnp.where(eq, 1, 0), axis=0, keepdims=True)
        tied = n_gt + n_eq > topk
        need_f = jnp.where(tied, topk - n_gt, lk + 1).astype(F32)
        take_eq, _ = _take_first_ties(eq, need_f, tied, axis=0)
        bias_ref[...] = jnp.where(jnp.logical_or(gt, take_eq > 0.5), 0.0, NEG)

    def chunk_bias(sl):
        if select:
            return bias_ref[sl, :]
        return jnp.where(causal(sl), 0.0, NEG)

    def fold8(x, op):
        return op(x.reshape(ck // 8, 8, qb), axis=0)

    for j in range(ATTN_HEADS // 2):
        outs = []
        for h in (2 * j, 2 * j + 1):
            q_op = head_operand(q_ref, h)
            mx8 = jnp.full((8, qb), -jnp.inf, F32)
            for sl in chunks:
                mx8 = jnp.maximum(mx8, fold8(_dot(kk_ref[0, sl, :], q_op) + chunk_bias(sl), jnp.max))
            m = jnp.max(mx8, axis=0, keepdims=True)
            den8 = jnp.zeros((8, qb), F32)
            acc = jnp.zeros((LANES, qb), F32)
            for sl in chunks:
                p = jnp.exp(_dot(kk_ref[0, sl, :], q_op) + chunk_bias(sl) - m)
                den8 = den8 + fold8(p, jnp.sum)
                acc = acc + _dot(vvt_ref[0, :, sl], p.astype(BF16))
            outs.append(acc * (1.0 / jnp.sum(den8, axis=0, keepdims=True)))
        pair_t = jnp.where(upper_rows, outs[0], outs[1])
        o_ref[0, :, j * LANES:(j + 1) * LANES] = jnp.transpose(pair_t).astype(BF16)


def _dsa_prompt_group(q, qi, wi, kk, kiki, vvt, batch, t_len, first_block, n_blocks, topk):
    nb = t_len // DSA_Q
    lk = (first_block + n_blocks) * DSA_Q
    blk = lambda b, j: (b * nb + first_block + j, 0)
    return pl.pallas_call(
        functools.partial(_dsa_prompt_kernel, topk=topk, first_block=first_block, select=lk > topk),
        grid=(batch, n_blocks),
        in_specs=[pl.BlockSpec((DSA_Q, 512), blk),
                  pl.BlockSpec((DSA_Q, 512), blk),
                  pl.BlockSpec((DSA_Q, LANES), blk),
                  pl.BlockSpec((1, lk, LANES), lambda b, j: (b, 0, 0)),
                  pl.BlockSpec((1, lk, LANES), lambda b, j: (b, 0, 0)),
                  pl.BlockSpec((1, LANES, lk), lambda b, j: (b, 0, 0))],
        out_specs=pl.BlockSpec((1, DSA_Q, 512), lambda b, j: (b, j, 0)),
        out_shape=jax.ShapeDtypeStruct((batch, n_blocks * DSA_Q, 512), BF16),
        scratch_shapes=[pltpu.VMEM((lk, DSA_Q), F32), pltpu.VMEM((lk, DSA_Q), F32)],
        compiler_params=_params("parallel", "parallel"),
        name="dsa_prompt_%d" % lk,
    )(q, qi, wi, kk, kiki, vvt)


def _dsa_prompt(q, qi, wi, kk, kiki, vv, batch, t_len):
    nb = t_len // DSA_Q
    topk = min(TOPK_MAX, t_len // 4)
    assert topk % DSA_Q == 0 and t_len % DSA_Q == 0
    kk, kiki, vv = (a.reshape(batch, t_len, LANES) for a in (kk, kiki, vv))
    vvt = jnp.swapaxes(vv, 1, 2)
    edges = [0, topk // DSA_Q]
    while edges[-1] < nb:
        edges.append(min(edges[-1] + (1 if edges[-1] < 2 else 2), nb))
    parts = [_dsa_prompt_group(q, qi, wi, kk, kiki, vvt, batch, t_len, a, b - a, topk)
             for a, b in zip(edges[:-1], edges[1:]) if b > a]
    return jnp.concatenate(parts, axis=1).reshape(batch * t_len, 512)


def _mem_kv_kernel(m_ref, w_ref, k_ref, v_ref):
    z = _dot(m_ref[...].astype(BF16), w_ref[...])
    k_ref[...] = z[:, :MEM_WIDTH]
    v_ref[...] = z[:, MEM_WIDTH:]


def _mem_kv(mem, w_kv):
    n = mem.shape[0]
    tm = min(n, 512)
    row = lambda i: (i, 0)
    return pl.pallas_call(
        _mem_kv_kernel,
        grid=(n // tm,),
        in_specs=[pl.BlockSpec((tm, D_MODEL), row),
                  pl.BlockSpec((D_MODEL, 2 * MEM_WIDTH), lambda i: (0, 0))],
        out_specs=[pl.BlockSpec((tm, MEM_WIDTH), row), pl.BlockSpec((tm, MEM_WIDTH), row)],
        out_shape=[jax.ShapeDtypeStruct((n, MEM_WIDTH), F32)] * 2,
        compiler_params=_params("parallel"),
        name="mem_kv",
    )(mem, w_kv)


def _pool_project(window_sums, u, cnts, pw_ref, ps_ref):
    ys = []
    for g in range(len(POOL_WINDOWS)):
        sl = slice(g * POOL_CG, (g + 1) * POOL_CG)
        pooled = window_sums[g] / cnts[g] - u[:, sl]
        ys.append(_dot(pooled.astype(BF16), pw_ref[g]) * ps_ref[:, sl])
    return jnp.concatenate(ys, axis=1)


def _mid_prompt_kernel(x_ref, attn_ref, u_ref, halo_ref, mk_ref, mv_ref, pw_ref, ps_ref, wo_ref,
                       gm_ref, wmq_ref, wmo_ref, gf_ref, h2_ref, tok_ref, e_ref, *, nt, n_real):
    step = pl.program_id(0)

    @pl.when(step >= n_real)
    def _():
        h2_ref[...] = jnp.zeros_like(h2_ref)
        tok_ref[...] = jnp.zeros_like(tok_ref)

    @pl.when(step < n_real)
    def _():
        _mid_prompt_rows(x_ref, attn_ref, u_ref, halo_ref, mk_ref, mv_ref, pw_ref, ps_ref, wo_ref,
                         gm_ref, wmq_ref, wmo_ref, gf_ref, h2_ref, tok_ref, e_ref, step % nt)


def _mid_prompt_rows(x_ref, attn_ref, u_ref, halo_ref, mk_ref, mv_ref, pw_ref, ps_ref, wo_ref,
                     gm_ref, wmq_ref, wmo_ref, gf_ref, h2_ref, tok_ref, e_ref, i):
    tm = x_ref.shape[0]
    u = u_ref[...]
    halo = halo_ref[...]
    e_ref[0:POOL_MAX, :] = jnp.where(i == 0, jnp.zeros_like(halo), halo)
    e_ref[POOL_MAX:POOL_MAX + tm, :] = u
    pos = i * tm + lax.broadcasted_iota(jnp.int32, (tm, 1), 0)
    sums, cnts = [], []
    for g, w in enumerate(POOL_WINDOWS):
        sl = slice(g * POOL_CG, (g + 1) * POOL_CG)
        s = u[:, sl]
        for j in range(1, w):
            s = s + e_ref[POOL_MAX - j:POOL_MAX - j + tm, sl]
        sums.append(s)
        cnts.append(jnp.minimum(w, pos + 1).astype(F32))
    pool = _pool_project(sums, u, cnts, pw_ref, ps_ref)
    h1 = x_ref[...] + _dot(attn_ref[...], wo_ref[0:ATTN_WIDTH, :]) \
        + _dot(pool.astype(BF16), wo_ref[ATTN_WIDTH:, :])

    qm = _dot(_rms(h1, gm_ref[...]).astype(BF16), wmq_ref[...]).astype(BF16)
    mk = mk_ref[0].astype(BF16)
    mv = mv_ref[0].astype(BF16)
    outs = []
    for h in range(MEM_HEADS):
        sl = slice(h * MEM_HEAD_DIM, (h + 1) * MEM_HEAD_DIM)
        logits = _dot_nt(qm[:, sl], mk[:, sl]) * (MEM_HEAD_DIM ** -0.5)
        m = jnp.max(logits, axis=1, keepdims=True)
        p = jnp.exp(logits - m)
        p = p / jnp.sum(p, axis=1, keepdims=True)
        outs.append(_dot(p.astype(BF16), mv[:, sl]))
    o = jnp.concatenate(outs, axis=1)
    h2 = h1 + _dot(o.astype(BF16), wmo_ref[...])
    h2_ref[...] = h2
    tok_ref[...] = _rms(h2, gf_ref[...]).astype(BF16)


def _mid_prompt(x, attn, u, mk, mv, pool_w, pool_scale, w_o, g_mem, w_mq, w_mo, g_ffn, batch, t_len, tm, n_rows):
    nt = t_len // tm
    hb = tm // POOL_MAX
    n_real = batch * nt
    n_steps = -(-n_rows // tm)
    last = n_real - 1
    row = lambda s: (jnp.minimum(s, last), 0)
    halo = lambda s: (jnp.maximum(jnp.minimum(s, last) * hb - 1, 0), 0)
    fixed2 = lambda s: (0, 0)
    fixed3 = lambda s: (0, 0, 0)
    perb = lambda s: (jnp.minimum(s, last) // nt, 0, 0)
    out_row = lambda s: (s, 0)
    n = n_rows
    return pl.pallas_call(
        functools.partial(_mid_prompt_kernel, nt=nt, n_real=n_real),
        grid=(n_steps,),
        in_specs=[pl.BlockSpec((tm, D_MODEL), row),
                  pl.BlockSpec((tm, ATTN_WIDTH), row),
                  pl.BlockSpec((tm, POOL_WIDTH), row),
                  pl.BlockSpec((POOL_MAX, POOL_WIDTH), halo),
                  pl.BlockSpec((1, N_MEM, MEM_WIDTH), perb),
                  pl.BlockSpec((1, N_MEM, MEM_WIDTH), perb),
                  pl.BlockSpec((len(POOL_WINDOWS), POOL_CG, POOL_CG), fixed3),
                  pl.BlockSpec((1, POOL_WIDTH), fixed2),
                  pl.BlockSpec((D_MODEL, D_MODEL), fixed2),
                  pl.BlockSpec((1, D_MODEL), fixed2),
                  pl.BlockSpec((D_MODEL, MEM_WIDTH), fixed2),
                  pl.BlockSpec((MEM_WIDTH, D_MODEL), fixed2),
                  pl.BlockSpec((1, D_MODEL), fixed2)],
        out_specs=[pl.BlockSpec((tm, D_MODEL), out_row), pl.BlockSpec((tm, D_MODEL), out_row)],
        out_shape=[jax.ShapeDtypeStruct((n, D_MODEL), F32), jax.ShapeDtypeStruct((n, D_MODEL), BF16)],
        scratch_shapes=[pltpu.VMEM((POOL_MAX + tm, POOL_WIDTH), F32)],
        compiler_params=_params("arbitrary"),
        name="mid_prompt",
    )(x, attn, u, u, mk, mv, pool_w, pool_scale, w_o, g_mem, w_mq, w_mo, g_ffn)


def _router_kernel(tok_ref, rw_ref, rb_ref, rank_ref, gate_ref, cnt_ref, *, n_valid):
    tm = tok_ref.shape[0]
    token = pl.program_id(0) * tm + lax.broadcasted_iota(jnp.int32, (1, tm), 1)
    valid = token < n_valid
    logits = _dot_nt(rw_ref[...], tok_ref[...]) + rb_ref[...]
    eid = lax.broadcasted_iota(jnp.int32, (N_EXPERTS, tm), 0)
    vals, hots = [], []
    cur = logits
    for _ in range(TOP_K):
        m = jnp.max(cur, axis=0, keepdims=True)
        idx = jnp.min(jnp.where(cur == m, eid, N_EXPERTS), axis=0, keepdims=True)
        hot = eid == idx
        vals.append(m)
        hots.append(hot)
        cur = jnp.where(hot, -jnp.inf, cur)
    exps = [jnp.exp(v - vals[0]) for v in vals]
    den = exps[0] + exps[1] + exps[2] + exps[3]
    gate = jnp.zeros((N_EXPERTS, tm), F32)
    sel = jnp.zeros((N_EXPERTS, tm), jnp.bool_)
    for hot, ex in zip(hots, exps):
        gate = jnp.where(hot, ex / den, gate)
        sel = jnp.logical_or(sel, hot)
    sel = jnp.logical_and(sel, valid)
    gate_ref[0] = gate

    upper = _strict_triangle(LANES, lower=False)
    carry = jnp.zeros((N_EXPERTS, 1), F32)
    for c in range(tm // LANES):
        sl = slice(c * LANES, (c + 1) * LANES)
        sel_c = sel[:, sl]
        sel_b = jnp.where(sel_c, 1.0, 0.0).astype(BF16)
        rank = _dot(sel_b, upper) + carry
        rank_ref[0, :, sl] = jnp.where(sel_c, rank, -1.0).astype(jnp.int32)
        carry = carry + jnp.sum(sel_b.astype(F32), axis=1, keepdims=True)
    cnt_ref[0] = carry.astype(jnp.int32)


def _router(tok, rw_t, rb, tm, n_valid):
    n = tok.shape[0]
    nt = n // tm
    t3 = lambda i: (i, 0, 0)
    return pl.pallas_call(
        functools.partial(_router_kernel, n_valid=n_valid),
        grid=(nt,),
        in_specs=[pl.BlockSpec((tm, D_MODEL), lambda i: (i, 0)),
                  pl.BlockSpec((N_EXPERTS, D_MODEL), lambda i: (0, 0)),
                  pl.BlockSpec((N_EXPERTS, 1), lambda i: (0, 0))],
        out_specs=[pl.BlockSpec((1, N_EXPERTS, tm), t3),
                   pl.BlockSpec((1, N_EXPERTS, tm), t3),
                   pl.BlockSpec((1, N_EXPERTS, 1), t3)],
        out_shape=[jax.ShapeDtypeStruct((nt, N_EXPERTS, tm), jnp.int32),
                   jax.ShapeDtypeStruct((nt, N_EXPERTS, tm), F32),
                   jax.ShapeDtypeStruct((nt, N_EXPERTS, 1), jnp.int32)],
        compiler_params=_params("parallel"),
        name="router",
    )(tok, rw_t, rb)


def _swiglu(h):
    gate = jnp.minimum(h[:, :D_FF], SWIGLU_LIMIT)
    up = jnp.clip(h[:, D_FF:], -SWIGLU_LIMIT, SWIGLU_LIMIT)
    return gate * jax.nn.sigmoid(SWIGLU_ALPHA * gate) * (up + 1.0)


def _moe_kernel(cnt_ref, x_ref, h2_ref, rank_ref, gate_ref, wu_ref, bu_ref, wd_ref, bd_ref, gfin_ref,
                o_ref, os_ref, *, first_chunk, chunk, sample_row):
    i = pl.program_id(0)
    e = pl.program_id(1)
    tm = x_ref.shape[0]

    @pl.when(e == 0)
    def _():
        o_ref[...] = jnp.zeros_like(o_ref)

    n_routed = cnt_ref[i * N_EXPERTS + e]
    rank = rank_ref[0, pl.ds(e, 1), :]
    gate = gate_ref[0, pl.ds(e, 1), :]

    def run_slots(first, size):
        slot = first + lax.broadcasted_iota(jnp.int32, (size, tm), 0)
        hit = rank == slot
        onehot = jnp.where(hit, 1.0, 0.0).astype(BF16)
        xg = _dot(onehot, x_ref[...]).astype(BF16)
        act = _swiglu(_dot(xg, wu_ref[0]) + bu_ref[0])
        out = _dot(act.astype(BF16), wd_ref[0]) + bd_ref[0]
        g_slot = jnp.sum(jnp.where(hit, gate, 0.0), axis=1, keepdims=True)
        o_ref[...] += _dot_tn(onehot, (out * g_slot).astype(BF16))

    sizes = [s for s in MOE_FIRST_SIZES if s < first_chunk] + [first_chunk]
    for lo, size in zip([0] + sizes[:-1], sizes):
        fits = n_routed <= size if size < first_chunk else n_routed > lo
        @pl.when(jnp.logical_and(n_routed > lo, fits))
        def _():
            run_slots(0, size)

    def body(c, carry):
        run_slots(first_chunk + c * chunk, chunk)
        return carry

    lax.fori_loop(0, (jnp.maximum(n_routed - first_chunk, 0) + chunk - 1) // chunk, body, 0)

    @pl.when(e == pl.num_programs(1) - 1)
    def _():
        o_ref[...] = _rms(h2_ref[...] + o_ref[...], gfin_ref[...])

        @pl.when(i == pl.num_programs(0) - 1)
        def _():
            os_ref[...] = o_ref[sample_row:sample_row + os_ref.shape[0], :]


def _moe(tok, h2, rank, gate, cnt, wu, bu, wd, bd, g_final, tm, n_prompt, first_chunk, chunk):
    nt = tok.shape[0] // tm
    tile = lambda i, e, c: (i, 0)
    tile3 = lambda i, e, c: (i, 0, 0)
    expert = lambda i, e, c: (e, 0, 0)
    grid_spec = pltpu.PrefetchScalarGridSpec(
        num_scalar_prefetch=1,
        grid=(nt, N_EXPERTS),
        in_specs=[pl.BlockSpec((tm, D_MODEL), tile),
                  pl.BlockSpec((tm, D_MODEL), tile),
                  pl.BlockSpec((1, N_EXPERTS, tm), tile3),
                  pl.BlockSpec((1, N_EXPERTS, tm), tile3),
                  pl.BlockSpec((1, D_MODEL, 2 * D_FF), expert),
                  pl.BlockSpec((1, 1, 2 * D_FF), expert),
                  pl.BlockSpec((1, D_FF, D_MODEL), expert),
                  pl.BlockSpec((1, 1, D_MODEL), expert),
                  pl.BlockSpec((1, D_MODEL), lambda i, e, c: (0, 0))],
        out_specs=[pl.BlockSpec((tm, D_MODEL), tile),
                   pl.BlockSpec((SAMPLE_ROWS, D_MODEL), lambda i, e, c: (0, 0))],
    )
    return pl.pallas_call(
        functools.partial(_moe_kernel, first_chunk=first_chunk, chunk=chunk,
                          sample_row=n_prompt - (nt - 1) * tm),
        grid_spec=grid_spec,
        out_shape=[jax.ShapeDtypeStruct((n_prompt, D_MODEL), F32),
                   jax.ShapeDtypeStruct((SAMPLE_ROWS, D_MODEL), F32)],
        compiler_params=_params("arbitrary", "arbitrary"),
        name="moe",
    )(cnt, tok, h2, rank, gate, wu, bu, wd, bd, g_final)


def _page_copy(cache_ref, buf_ref, sem_ref, pt_ref, b, p, slot):
    dst = buf_ref.at[slot, :, pl.ds(pl.multiple_of(p * PAGE_SIZE, PAGE_SIZE), PAGE_SIZE)]
    return pltpu.make_async_copy(cache_ref.at[0, pt_ref[b, p]], dst, sem_ref.at[slot])


def _fetch_pages(cache_ref, buf_ref, sem_ref, pt_ref, b, slot, n_pages):
    def body(p, c):
        _page_copy(cache_ref, buf_ref, sem_ref, pt_ref, b, p, slot).start()
        return c
    lax.fori_loop(0, n_pages, body, 0, unroll=8)


def _wait_pages(cache_ref, buf_ref, sem_ref, pt_ref, b, slot, n_pages):
    def body(p, c):
        _page_copy(cache_ref, buf_ref, sem_ref, pt_ref, b, p, slot).wait()
        return c
    lax.fori_loop(0, n_pages, body, 0, unroll=8)


def _dsa_sample_score_kernel(pt_ref, qi_ref, wi_ref, kin_ref, cache_ref, score_ref, new_ref, buf_ref, sem_ref):
    b = pl.program_id(0)
    nb = pl.num_programs(0)
    past = buf_ref.shape[2]
    n_pages = past // PAGE_SIZE
    slot = b % 2

    @pl.when(b == 0)
    def _():
        _fetch_pages(cache_ref, buf_ref, sem_ref, pt_ref, b, slot, n_pages)

    @pl.when(b + 1 < nb)
    def _():
        _fetch_pages(cache_ref, buf_ref, sem_ref, pt_ref, b + 1, 1 - slot, n_pages)

    _wait_pages(cache_ref, buf_ref, sem_ref, pt_ref, b, slot, n_pages)

    qi = qi_ref[0]
    w = wi_ref[0] * (IDX_HEADS ** -0.5)
    keys_t = buf_ref[slot].astype(BF16)
    score_ref[0] = jnp.sum(jnp.maximum(_dot(qi, keys_t), 0.0) * w, axis=0, keepdims=True)
    s_new = jnp.sum(qi.astype(F32) * kin_ref[0].astype(F32), axis=1, keepdims=True)
    new_ref[0] = jnp.sum(jnp.maximum(s_new, 0.0) * w, axis=0, keepdims=True)


def _dsa_sample_select_kernel(score_ref, new_ref, mask_ref, take_new_ref, *, topk):
    score = score_ref[...]
    score_new = new_ref[...]
    past = score.shape[1]

    def count_ge(cand):
        return (jnp.sum(jnp.where(score >= cand, 1, 0), axis=1, keepdims=True)
                + jnp.where(score_new >= cand, 1, 0))

    tau = _kth_largest(count_ge, topk, score_new.shape)
    gt = score > tau
    eq = score == tau
    n_gt = jnp.sum(jnp.where(gt, 1, 0), axis=1, keepdims=True) + jnp.where(score_new > tau, 1, 0)
    n_eq_all = jnp.sum(jnp.where(eq, 1, 0), axis=1, keepdims=True) + jnp.where(score_new == tau, 1, 0)
    tied = n_gt + n_eq_all > topk
    need_f = jnp.where(tied, topk - n_gt, past + 2).astype(F32)
    take_eq, n_eq = _take_first_ties(eq, need_f, tied, axis=1)
    mask_ref[...] = jnp.where(jnp.logical_or(gt, take_eq > 0.5), 0.0, NEG)
    take_new = jnp.logical_or(score_new > tau, jnp.logical_and(score_new == tau, n_eq < need_f))
    take_new_ref[...] = jnp.where(take_new, 0.0, NEG)


def _dsa_sample_select(score, score_new, topk):
    db, past = score.shape
    full = lambda shape: pl.BlockSpec(shape, lambda i: (0, 0))
    return pl.pallas_call(
        functools.partial(_dsa_sample_select_kernel, topk=topk),
        grid=(1,),
        in_specs=[full((db, past)), full((db, 1))],
        out_specs=[full((db, past)), full((db, 1))],
        out_shape=[jax.ShapeDtypeStruct((db, past), F32), jax.ShapeDtypeStruct((db, 1), F32)],
        compiler_params=_params("arbitrary"),
        name="dsa_sample_select",
    )(score, score_new)


def _dsa_sample_mask(page_table, qi_heads, wi_col, ki_new, cache_idx_k):
    db, n_pages = page_table.shape
    past = n_pages * PAGE_SIZE
    topk = min(TOPK_MAX, (past + 1) // 4)
    per = lambda b, pt: (b, 0, 0)
    grid_spec = pltpu.PrefetchScalarGridSpec(
        num_scalar_prefetch=1,
        grid=(db,),
        in_specs=[pl.BlockSpec((1, IDX_HEADS, IDX_DIM), per),
                  pl.BlockSpec((1, IDX_HEADS, 1), per),
                  pl.BlockSpec((1, 1, IDX_DIM), per),
                  pl.BlockSpec(memory_space=pl.ANY)],
        out_specs=[pl.BlockSpec((1, 1, past), per), pl.BlockSpec((1, 1, 1), per)],
        scratch_shapes=[pltpu.VMEM((2, IDX_DIM, past), F32),
                        pltpu.SemaphoreType.DMA((2,))],
    )
    score, score_new = pl.pallas_call(
        _dsa_sample_score_kernel,
        grid_spec=grid_spec,
        out_shape=[jax.ShapeDtypeStruct((db, 1, past), F32), jax.ShapeDtypeStruct((db, 1, 1), F32)],
        compiler_params=_params("arbitrary"),
        name="dsa_sample_score",
    )(page_table, qi_heads, wi_col, ki_new, cache_idx_k)
    mask, take_new = _dsa_sample_select(score.reshape(db, past), score_new.reshape(db, 1), topk)
    return mask.reshape(db, 1, past), take_new.reshape(db, 1, 1)


def _dsa_sample_attend_kernel(pt_ref, q_ref, kn_ref, vn_ref, mask_ref, new_ref, ck_ref, cv_ref, o_ref,
                              kbuf_ref, vbuf_ref, ksem_ref, vsem_ref):
    b = pl.program_id(0)
    nb = pl.num_programs(0)
    past = kbuf_ref.shape[2]
    n_pages = past // PAGE_SIZE
    slot = b % 2

    def fetch(bb, sl):
        _fetch_pages(ck_ref, kbuf_ref, ksem_ref, pt_ref, bb, sl, n_pages)
        _fetch_pages(cv_ref, vbuf_ref, vsem_ref, pt_ref, bb, sl, n_pages)

    @pl.when(b == 0)
    def _():
        fetch(b, slot)

    @pl.when(b + 1 < nb)
    def _():
        fetch(b + 1, 1 - slot)

    _wait_pages(ck_ref, kbuf_ref, ksem_ref, pt_ref, b, slot, n_pages)
    _wait_pages(cv_ref, vbuf_ref, vsem_ref, pt_ref, b, slot, n_pages)

    q = q_ref[0]
    keys_t = kbuf_ref[slot].astype(BF16)
    vals_t = vbuf_ref[slot].astype(BF16)
    logits = _dot(q, keys_t) + mask_ref[0]
    l_new = jnp.sum(q.astype(F32) * kn_ref[0].astype(F32), axis=1, keepdims=True) + new_ref[0]
    m = jnp.maximum(jnp.max(logits, axis=1, keepdims=True), l_new)
    p = jnp.exp(logits - m)
    p_new = jnp.exp(l_new - m)
    den = jnp.sum(p, axis=1, keepdims=True) + p_new
    p = p / den
    p_new = p_new / den
    o_ref[0] = _dot_nt(p.astype(BF16), vals_t) + p_new * vn_ref[0].astype(F32)


def _dsa_sample_attend(page_table, q_heads, k_new, v_new, mask, take_new, cache_k, cache_v):
    db, n_pages = page_table.shape
    past = n_pages * PAGE_SIZE
    per = lambda b, pt: (b, 0, 0)
    grid_spec = pltpu.PrefetchScalarGridSpec(
        num_scalar_prefetch=1,
        grid=(db,),
        in_specs=[pl.BlockSpec((1, ATTN_HEADS, HEAD_DIM), per),
                  pl.BlockSpec((1, 1, HEAD_DIM), per),
                  pl.BlockSpec((1, 1, HEAD_DIM), per),
                  pl.BlockSpec((1, 1, past), per),
                  pl.BlockSpec((1, 1, 1), per),
                  pl.BlockSpec(memory_space=pl.ANY),
                  pl.BlockSpec(memory_space=pl.ANY)],
        out_specs=pl.BlockSpec((1, ATTN_HEADS, HEAD_DIM), per),
        scratch_shapes=[pltpu.VMEM((2, HEAD_DIM, past), F32),
                        pltpu.VMEM((2, HEAD_DIM, past), F32),
                        pltpu.SemaphoreType.DMA((2,)),
                        pltpu.SemaphoreType.DMA((2,))],
    )
    return pl.pallas_call(
        _dsa_sample_attend_kernel,
        grid_spec=grid_spec,
        out_shape=jax.ShapeDtypeStruct((db, ATTN_HEADS, HEAD_DIM), F32),
        compiler_params=_params("arbitrary"),
        name="dsa_sample_attend",
    )(page_table, q_heads, k_new, v_new, mask, take_new, cache_k, cache_v)


def _mid_sample_a_kernel(x_ref, attn_ref, u_ref, st_ref, pw_ref, ps_ref, wo_ref, gm_ref, wmq_ref,
                         h1_ref, qm_ref, *, past):
    u = u_ref[...]
    rows = u.shape[0]
    sums, cnts = [], []
    for g, w in enumerate(POOL_WINDOWS):
        sl = slice(g * POOL_CG, (g + 1) * POOL_CG)
        s = u[:, sl]
        for j in range(1, w):
            s = s + st_ref[POOL_MAX - 1 - j][:, sl]
        sums.append(s)
        cnts.append(jnp.full((rows, 1), float(min(w, past + 1)), F32))
    pool = _pool_project(sums, u, cnts, pw_ref, ps_ref)
    h1 = x_ref[...] + _dot(attn_ref[...], wo_ref[0:ATTN_WIDTH, :]) \
        + _dot(pool.astype(BF16), wo_ref[ATTN_WIDTH:, :])
    h1_ref[...] = h1
    qm_ref[...] = _dot(_rms(h1, gm_ref[...]).astype(BF16), wmq_ref[...]).astype(BF16)


def _mid_sample_a(x, attn, u, state_t, pool_w, pool_scale, w_o, g_mem, w_mq, past):
    db = x.shape[0]
    full = lambda a: pl.BlockSpec(a.shape, lambda i, nd=a.ndim: (0,) * nd)
    args = (x, attn, u, state_t, pool_w, pool_scale, w_o, g_mem, w_mq)
    return pl.pallas_call(
        functools.partial(_mid_sample_a_kernel, past=past),
        grid=(1,),
        in_specs=[full(a) for a in args],
        out_specs=[pl.BlockSpec((db, D_MODEL), lambda i: (0, 0)),
                   pl.BlockSpec((db, MEM_WIDTH), lambda i: (0, 0))],
        out_shape=[jax.ShapeDtypeStruct((db, D_MODEL), F32), jax.ShapeDtypeStruct((db, MEM_WIDTH), BF16)],
        compiler_params=_params("arbitrary"),
        name="mid_sample_a",
    )(*args)


def _mem_sample_kernel(q_ref, mk_ref, mv_ref, o_ref):
    rows = 8
    for h in range(MEM_HEADS):
        sl = slice(h * MEM_HEAD_DIM, (h + 1) * MEM_HEAD_DIM)
        q = jnp.broadcast_to(q_ref[0, :, sl], (rows, MEM_HEAD_DIM))
        logits = _dot_nt(q, mk_ref[0, 0, :, h, :].astype(BF16)) * (MEM_HEAD_DIM ** -0.5)
        m = jnp.max(logits, axis=1, keepdims=True)
        p = jnp.exp(logits - m)
        p = p / jnp.sum(p, axis=1, keepdims=True)
        o = _dot(p.astype(BF16), mv_ref[0, 0, :, h, :].astype(BF16))
        o_ref[0, :, sl] = o[0:1, :]


def _mem_sample(qm, mk, mv):
    db = qm.shape[0]
    per = lambda b: (b, 0, 0)
    cache = lambda b: (0, b, 0, 0, 0)
    return pl.pallas_call(
        _mem_sample_kernel,
        grid=(db,),
        in_specs=[pl.BlockSpec((1, 1, MEM_WIDTH), per),
                  pl.BlockSpec((1, 1, N_MEM, MEM_HEADS, MEM_HEAD_DIM), cache),
                  pl.BlockSpec((1, 1, N_MEM, MEM_HEADS, MEM_HEAD_DIM), cache)],
        out_specs=pl.BlockSpec((1, 1, MEM_WIDTH), per),
        out_shape=jax.ShapeDtypeStruct((db, 1, MEM_WIDTH), F32),
        compiler_params=_params("parallel"),
        name="mem_sample",
    )(qm.reshape(db, 1, MEM_WIDTH), mk, mv)


def _sample_join_kernel(h1_ref, o_ref, wmo_ref, gf_ref, h2_any, tok_any, h2_ref, tok_ref):
    del h2_any, tok_any
    db = h1_ref.shape[0]
    h2_ref[...] = jnp.zeros_like(h2_ref)
    tok_ref[...] = jnp.zeros_like(tok_ref)

    @pl.when(pl.program_id(0) == 0)
    def _():
        h2 = h1_ref[...] + _dot(o_ref[...].astype(BF16), wmo_ref[...])
        h2_ref[0:db, :] = h2
        tok_ref[0:db, :] = _rms(h2, gf_ref[...]).astype(BF16)


def _sample_join(h1, o, w_mo, g_ffn, h2_all, tok_all, n_prompt):
    db = h1.shape[0]
    n_rows = h2_all.shape[0]
    assert db <= SAMPLE_ROWS and n_prompt % SAMPLE_ROWS == 0 and n_rows % SAMPLE_ROWS == 0
    first = n_prompt // SAMPLE_ROWS
    f2 = lambda j: (0, 0)
    tail = lambda j: (first + j, 0)
    return pl.pallas_call(
        _sample_join_kernel,
        grid=((n_rows - n_prompt) // SAMPLE_ROWS,),
        in_specs=[pl.BlockSpec((db, D_MODEL), f2),
                  pl.BlockSpec((db, MEM_WIDTH), f2),
                  pl.BlockSpec((MEM_WIDTH, D_MODEL), f2),
                  pl.BlockSpec((1, D_MODEL), f2),
                  pl.BlockSpec(memory_space=pl.ANY),
                  pl.BlockSpec(memory_space=pl.ANY)],
        out_specs=[pl.BlockSpec((SAMPLE_ROWS, D_MODEL), tail), pl.BlockSpec((SAMPLE_ROWS, D_MODEL), tail)],
        out_shape=[jax.ShapeDtypeStruct(h2_all.shape, F32), jax.ShapeDtypeStruct(tok_all.shape, BF16)],
        input_output_aliases={4: 0, 5: 1},
        compiler_params=_params("arbitrary"),
        name="sample_join",
    )(h1, o, w_mo, g_ffn, h2_all, tok_all)


def _rope_tables(pos):
    half = HEAD_DIM // 2
    inv = ROPE_THETA ** (-jnp.arange(half, dtype=F32) / half)
    ang = pos.astype(F32)[:, None] * inv[None, :]
    cos = jnp.cos(ang)
    sin = jnp.sin(ang)
    return jnp.tile(cos, (1, 4)), jnp.concatenate([-sin, sin, -sin, sin], axis=1)


def _pick_tile(n, pref):
    t = min(n, pref)
    while n % t:
        t //= 2
    return t


def kernel(x_prompt, x_sample, mem_prompt, cache_k, cache_v, cache_idx_k, state_pool, cache_mem_k, cache_mem_v, page_table, norm_mix, w_in, pool_w, pool_scale, w_o, norm_mem, w_mq, w_mk, w_mv, w_mo, norm_ffn, router_w, router_b, w_up, b_up, w_down, b_down, norm_final):
    B, T = x_prompt.shape[:2]
    DB, DS = x_sample.shape[:2]
    assert DS == 1 and w_in.shape[0] == 1
    n_pages = page_table.shape[1]
    past = n_pages * PAGE_SIZE
    NP = B * T

    wi_ = w_in[0]
    wq, wk, wv, wu_, wqi, wki, wwi = jnp.split(wi_, [512, 576, 640, 1152, 1664, 1728], axis=1)
    w_all = jnp.concatenate(
        [wq, wqi, wu_, wk, wk, wki, wki, wv, wv, wwi, jnp.zeros((D_MODEL, LANES - IDX_HEADS), F32)],
        axis=1).astype(BF16)
    g_mix = norm_mix[0][None, :]
    g_mem = norm_mem[0][None, :]
    g_ffn = norm_ffn[0][None, :]
    g_fin = norm_final[None, :]
    pw = pool_w[0].astype(BF16)
    ps = pool_scale[0][None, :]
    wo = w_o[0].astype(BF16)
    wmq = w_mq[0].reshape(D_MODEL, MEM_WIDTH).astype(BF16)
    wkv = jnp.concatenate([w_mk[0].reshape(D_MODEL, MEM_WIDTH), w_mv[0].reshape(D_MODEL, MEM_WIDTH)],
                          axis=1).astype(BF16)
    wmo = w_mo[0].reshape(MEM_WIDTH, D_MODEL).astype(BF16)
    rw = router_w[0].astype(BF16)
    rb = router_b[0]
    wup = w_up[0].astype(BF16)
    wdn = w_down[0].astype(BF16)
    bup = b_up[0][:, None, :]
    bdn = b_down[0][:, None, :]

    xp = x_prompt.reshape(NP, D_MODEL)
    tm_in = _pick_tile(T, 512)
    cos_p, sin_p = _rope_tables(jnp.arange(T, dtype=jnp.int32))
    q, qi, u, kk, kiki, vv, k_p, ki_p, v_p, wi_p = _in_project(xp, g_mix, w_all, cos_p, sin_p, tm_in, T // tm_in)
    attn = _dsa_prompt(q, qi, wi_p, kk, kiki, vv, B, T)
    mk_p, mv_p = _mem_kv(mem_prompt.reshape(B * N_MEM, D_MODEL), wkv)
    tm_moe = min(MOE_TILE, NP)
    n_tiles = -(-(NP + SAMPLE_ROWS) // tm_moe)
    assert (n_tiles - 1) * tm_moe <= NP and NP + SAMPLE_ROWS <= n_tiles * tm_moe
    h2_all, tok_all = _mid_prompt(xp, attn, u, mk_p.reshape(B, N_MEM, MEM_WIDTH), mv_p.reshape(B, N_MEM, MEM_WIDTH),
                                  pw, ps, wo, g_mem, wmq, wmo, g_ffn, B, T, _pick_tile(T, 512), n_tiles * tm_moe)

    xs = x_sample.reshape(DB, D_MODEL)
    cos_s, sin_s = _rope_tables(jnp.full((DB,), past, jnp.int32))
    q_s, qi_s, u_s, kk_s, kiki_s, vv_s, k_s, ki_s, v_s, wi_s = _in_project(xs, g_mix, w_all, cos_s, sin_s, DB, 1)
    mask, take_new = _dsa_sample_mask(
        page_table, qi_s.reshape(DB, IDX_HEADS, IDX_DIM), wi_s[:, :IDX_HEADS, None],
        kiki_s[:, None, :IDX_DIM], jnp.swapaxes(cache_idx_k, 2, 3))
    attn_s = _dsa_sample_attend(
        page_table, q_s.reshape(DB, ATTN_HEADS, HEAD_DIM), kk_s[:, None, :HEAD_DIM], vv_s[:, None, :HEAD_DIM],
        mask, take_new, jnp.swapaxes(cache_k, 2, 3), jnp.swapaxes(cache_v, 2, 3))
    attn_s = attn_s.reshape(DB, ATTN_WIDTH).astype(BF16)
    h1_s, qm_s = _mid_sample_a(xs, attn_s, u_s, jnp.transpose(state_pool[0], (1, 0, 2)), pw, ps, wo, g_mem, wmq, past)
    o_s = _mem_sample(qm_s, cache_mem_k, cache_mem_v)
    h2_all, tok_all = _sample_join(h1_s, o_s.reshape(DB, MEM_WIDTH), wmo, g_ffn, h2_all, tok_all, NP)

    rank, gate, cnt = _router(tok_all, rw.T, rb[:, None], tm_moe, NP + DB)
    y_p, y_s = _moe(tok_all, h2_all, rank, gate, cnt.reshape(-1), wup, bup, wdn, bdn, g_fin, tm_moe, NP, 256, 128)
    y_prompt = y_p.reshape(B, T, D_MODEL)
    y_sample = y_s[:DB].reshape(DB, 1, D_MODEL)

    new_pool_p = u.reshape(B, T, POOL_WIDTH)[:, T - (POOL_MAX - 1):][None]
    new_pool_s = jnp.concatenate([state_pool[0][:, 1:], u_s[:, None, :]], axis=1)[None]
    return (y_prompt, y_sample,
            k_p.reshape(1, B, T, HEAD_DIM), v_p.reshape(1, B, T, HEAD_DIM), ki_p.reshape(1, B, T, IDX_DIM),
            new_pool_p,
            mk_p.reshape(1, B, N_MEM, MEM_HEADS, MEM_HEAD_DIM), mv_p.reshape(1, B, N_MEM, MEM_HEADS, MEM_HEAD_DIM),
            k_s.reshape(1, DB, 1, HEAD_DIM), v_s.reshape(1, DB, 1, HEAD_DIM), ki_s.reshape(1, DB, 1, IDX_DIM),
            new_pool_s)
```

```python
import functools

import jax
import jax.numpy as jnp
from jax import lax
from jax.experimental import pallas as pl
from jax.experimental.pallas import tpu as pltpu

D_MODEL = 1024
HEAD_DIM = 64
ATTN_WIDTH = 512
ATTN_HEADS = 8
IDX_HEADS = 8
IDX_DIM = 64
TOPK_MAX = 256
Q_BLOCK = 128
PAGE_SIZE = 128
POOL_WIDTH = 512
POOL_WINDOWS = (2, 4, 8, 16)
POOL_CG = 128
POOL_MAX = 16
N_MEM = 256
MEM_HEADS = 4
MEM_HEAD_DIM = 128
MEM_WIDTH = MEM_HEADS * MEM_HEAD_DIM
N_EXPERTS = 32
TOP_K = 4
D_FF = 1024
SWIGLU_LIMIT = 7.0
SWIGLU_ALPHA = 1.702
ROPE_THETA = 10000.0
EPS = 1e-6
NEG = -1e30

LANES = 128
MOE_TILE = 13 * LANES
MOE_FIRST_SIZES = (208, 224)
SAMPLE_ROWS = 128
DSA_Q = 256
DSA_KCHUNK = 256
IN_COLS = 3 * 512 + 4 * LANES
VMEM_LIMIT = 56 * 1024 * 1024

BF16 = jnp.bfloat16
F32 = jnp.float32


def _dot(a, b):
    return jnp.dot(a, b, preferred_element_type=F32)


def _dot_nt(a, b):
    return lax.dot_general(a, b, (((1,), (1,)), ((), ())), preferred_element_type=F32)


def _dot_tn(a, b):
    return lax.dot_general(a, b, (((0,), (0,)), ((), ())), preferred_element_type=F32)


def _rms(x, g):
    return x * lax.rsqrt(jnp.mean(x * x, axis=-1, keepdims=True) + EPS) * g


def _params(*sem):
    return pltpu.CompilerParams(dimension_semantics=sem, vmem_limit_bytes=VMEM_LIMIT)


def _in_kernel(x_ref, g_ref, w_ref, cos_ref, sin_ref,
               q_ref, qi_ref, u_ref, kk_ref, kiki_ref, vv_ref, k_ref, ki_ref, v_ref, wi_ref):
    xn = _rms(x_ref[...], g_ref[...]).astype(BF16)
    z = _dot(xn, w_ref[...])
    cos = cos_ref[...]
    sin = sin_ref[...]
    lane = lax.broadcasted_iota(jnp.int32, cos.shape, 1)
    first_half = (lane % HEAD_DIM) < (HEAD_DIM // 2)

    def rope(zc):
        swapped = jnp.where(first_half,
                            pltpu.roll(zc, LANES - HEAD_DIM // 2, 1),
                            pltpu.roll(zc, HEAD_DIM // 2, 1))
        return zc * cos + swapped * sin

    for c in range(4):
        q_c = rope(z[:, c * LANES:(c + 1) * LANES]) * (HEAD_DIM ** -0.5)
        qi_c = rope(z[:, 512 + c * LANES:512 + (c + 1) * LANES]) * (IDX_DIM ** -0.5)
        q_ref[:, c * LANES:(c + 1) * LANES] = q_c.astype(BF16)
        qi_ref[:, c * LANES:(c + 1) * LANES] = qi_c.astype(BF16)
    u_ref[...] = z[:, 1024:1536]
    kk = rope(z[:, 1536:1664])
    kiki = rope(z[:, 1664:1792])
    vv = z[:, 1792:1920]
    kk_ref[...] = kk.astype(BF16)
    kiki_ref[...] = kiki.astype(BF16)
    vv_ref[...] = vv.astype(BF16)
    k_ref[...] = kk[:, :HEAD_DIM]
    ki_ref[...] = kiki[:, :IDX_DIM]
    v_ref[...] = vv[:, :HEAD_DIM]
    wi_ref[...] = z[:, 1920:2048]


def _in_project(x, g, w_all, cos, sin, tm, table_blocks):
    n = x.shape[0]
    row = lambda i: (i, 0)
    tab = lambda i: (i % table_blocks, 0)
    fixed = lambda i: (0, 0)
    outs = [
        jax.ShapeDtypeStruct((n, 512), BF16),
        jax.ShapeDtypeStruct((n, 512), BF16),
        jax.ShapeDtypeStruct((n, 512), F32),
        jax.ShapeDtypeStruct((n, LANES), BF16),
        jax.ShapeDtypeStruct((n, LANES), BF16),
        jax.ShapeDtypeStruct((n, LANES), BF16),
        jax.ShapeDtypeStruct((n, HEAD_DIM), F32),
        jax.ShapeDtypeStruct((n, IDX_DIM), F32),
        jax.ShapeDtypeStruct((n, HEAD_DIM), F32),
        jax.ShapeDtypeStruct((n, LANES), F32),
    ]
    out_specs = [pl.BlockSpec((tm, s.shape[1]), row) for s in outs]
    return pl.pallas_call(
        _in_kernel,
        grid=(n // tm,),
        in_specs=[pl.BlockSpec((tm, D_MODEL), row),
                  pl.BlockSpec((1, D_MODEL), fixed),
                  pl.BlockSpec((D_MODEL, IN_COLS), fixed),
                  pl.BlockSpec((tm, LANES), tab),
                  pl.BlockSpec((tm, LANES), tab)],
        out_specs=out_specs,
        out_shape=outs,
        compiler_params=_params("parallel"),
        name="in_project",
    )(x, g, w_all, cos, sin)


def _f32_to_key(x):
    bits = lax.bitcast_convert_type(x, jnp.int32)
    return jnp.where(bits < 0, bits ^ jnp.int32(0x7FFFFFFF), bits)


def _key_to_f32(key):
    bits = jnp.where(key < 0, key ^ jnp.int32(0x7FFFFFFF), key)
    return lax.bitcast_convert_type(bits, F32)


def _kth_largest(count_ge, k, shape):
    def body(it, tau):
        cand = tau + jnp.left_shift(jnp.int32(1), jnp.int32(31) - it)
        return jnp.where(count_ge(_key_to_f32(cand)) >= k, cand, tau)

    return _key_to_f32(lax.fori_loop(0, 32, body, jnp.full(shape, -2 ** 31, jnp.int32)))


def _strict_triangle(n, lower):
    r = lax.broadcasted_iota(jnp.int32, (n, n), 0)
    c = lax.broadcasted_iota(jnp.int32, (n, n), 1)
    return jnp.where((r > c) if lower else (r < c), 1.0, 0.0).astype(BF16)


def _take_first_ties(eq, need_f, tied, axis):
    n = eq.shape[axis]
    cshape = (1, eq.shape[1]) if axis == 0 else (eq.shape[0], 1)

    def ranked():
        tri = _strict_triangle(LANES, lower=(axis == 0))
        carry = jnp.zeros(cshape, F32)
        parts = []
        for c in range(n // LANES):
            sl = slice(c * LANES, (c + 1) * LANES)
            eq_f = jnp.where(eq[sl, :] if axis == 0 else eq[:, sl], 1.0, 0.0)
            eq_b = eq_f.astype(BF16)
            rank = (_dot(tri, eq_b) if axis == 0 else _dot(eq_b, tri)) + carry
            parts.append(jnp.where(rank < need_f, eq_f, 0.0))
            carry = carry + jnp.sum(eq_f, axis=axis, keepdims=True)
        return jnp.concatenate(parts, axis=axis), carry

    def plain():
        return jnp.where(eq, 1.0, 0.0), jnp.zeros(cshape, F32)

    return lax.cond(jnp.max(jnp.where(tied, 1, 0)) > 0, ranked, plain)


def _dsa_prompt_kernel(q_ref, qi_ref, wi_ref, kk_ref, kiki_ref, vvt_ref, o_ref, sm_ref, bias_ref, *,
                       topk, first_block, select):
    i = first_block + pl.program_id(1)
    qb = q_ref.shape[0]
    lk = kk_ref.shape[1]
    ck = min(DSA_KCHUNK, lk)
    chunks = [slice(c * ck, (c + 1) * ck) for c in range(lk // ck)]
    upper_rows = lax.broadcasted_iota(jnp.int32, (LANES, qb), 0) < HEAD_DIM

    def head_operand(ref, h):
        pair_t = jnp.transpose(ref[:, (h // 2) * LANES:(h // 2 + 1) * LANES].astype(F32))
        keep = upper_rows if h % 2 == 0 else jnp.logical_not(upper_rows)
        return jnp.where(keep, pair_t, 0.0).astype(BF16)

    pos_q = i * qb + lax.broadcasted_iota(jnp.int32, (1, qb), 1)

    def causal(sl):
        return sl.start + lax.broadcasted_iota(jnp.int32, (ck, 1), 0) <= pos_q

    if select:
        w_t = jnp.transpose(wi_ref[...])[0:IDX_HEADS, :] * (IDX_HEADS ** -0.5)
        qi_ops = [head_operand(qi_ref, h) for h in range(IDX_HEADS)]
        for sl in chunks:
            keys = kiki_ref[0, sl, :]
            score = jnp.zeros((ck, qb), F32)
            for h in range(IDX_HEADS):
                score = score + jnp.maximum(_dot(keys, qi_ops[h]), 0.0) * w_t[h:h + 1, :]
            sm_ref[sl, :] = jnp.where(causal(sl), score, -jnp.inf)

        def count_ge(cand):
            return jnp.sum(jnp.where(sm_ref[...] >= cand, 1, 0), axis=0, keepdims=True)

        tau = _kth_largest(count_ge, topk, (1, qb))
        sm = sm_ref[...]
        gt = sm > tau
        eq = sm == tau
        n_gt = jnp.sum(jnp.where(gt, 1, 0), axis=0, keepdims=True)
        n_eq = jnp.sum(jnp.where(eq, 1, 0), axis=0, keepdims=True)
        tied = n_gt + n_eq > topk
        need_f = jnp.where(tied, topk - n_gt, lk + 1).astype(F32)
        take_eq, _ = _take_first_ties(eq, need_f, tied, axis=0)
        bias_ref[...] = jnp.where(jnp.logical_or(gt, take_eq > 0.5), 0.0, NEG)

    def chunk_bias(sl):
        if select:
            return bias_ref[sl, :]
        return jnp.where(causal(sl), 0.0, NEG)

    def fold8(x, op):
        return op(x.reshape(ck // 8, 8, qb), axis=0)

    for j in range(ATTN_HEADS // 2):
        outs = []
        for h in (2 * j, 2 * j + 1):
            q_op = head_operand(q_ref, h)
            mx8 = jnp.full((8, qb), -jnp.inf, F32)
            for sl in chunks:
                mx8 = jnp.maximum(mx8, fold8(_dot(kk_ref[0, sl, :], q_op) + chunk_bias(sl), jnp.max))
            m = jnp.max(mx8, axis=0, keepdims=True)
            den8 = jnp.zeros((8, qb), F32)
            acc = jnp.zeros((LANES, qb), F32)
            for sl in chunks:
                p = jnp.exp(_dot(kk_ref[0, sl, :], q_op) + chunk_bias(sl) - m)
                den8 = den8 + fold8(p, jnp.sum)
                acc = acc + _dot(vvt_ref[0, :, sl], p.astype(BF16))
            outs.append(acc * (1.0 / jnp.sum(den8, axis=0, keepdims=True)))
        pair_t = jnp.where(upper_rows, outs[0], outs[1])
        o_ref[0, :, j * LANES:(j + 1) * LANES] = jnp.transpose(pair_t).astype(BF16)


def _dsa_prompt_group(q, qi, wi, kk, kiki, vvt, batch, t_len, first_block, n_blocks, topk):
    nb = t_len // DSA_Q
    lk = (first_block + n_blocks) * DSA_Q
    blk = lambda b, j: (b * nb + first_block + j, 0)
    return pl.pallas_call(
        functools.partial(_dsa_prompt_kernel, topk=topk, first_block=first_block, select=lk > topk),
        grid=(batch, n_blocks),
        in_specs=[pl.BlockSpec((DSA_Q, 512), blk),
                  pl.BlockSpec((DSA_Q, 512), blk),
                  pl.BlockSpec((DSA_Q, LANES), blk),
                  pl.BlockSpec((1, lk, LANES), lambda b, j: (b, 0, 0)),
                  pl.BlockSpec((1, lk, LANES), lambda b, j: (b, 0, 0)),
                  pl.BlockSpec((1, LANES, lk), lambda b, j: (b, 0, 0))],
        out_specs=pl.BlockSpec((1, DSA_Q, 512), lambda b, j: (b, j, 0)),
        out_shape=jax.ShapeDtypeStruct((batch, n_blocks * DSA_Q, 512), BF16),
        scratch_shapes=[pltpu.VMEM((lk, DSA_Q), F32), pltpu.VMEM((lk, DSA_Q), F32)],
        compiler_params=_params("parallel", "parallel"),
        name="dsa_prompt_%d" % lk,
    )(q, qi, wi, kk, kiki, vvt)


def _dsa_prompt(q, qi, wi, kk, kiki, vv, batch, t_len):
    nb = t_len // DSA_Q
    topk = min(TOPK_MAX, t_len // 4)
    assert topk % DSA_Q == 0 and t_len % DSA_Q == 0
    kk, kiki, vv = (a.reshape(batch, t_len, LANES) for a in (kk, kiki, vv))
    vvt = jnp.swapaxes(vv, 1, 2)
    edges = [0, topk // DSA_Q]
    while edges[-1] < nb:
        edges.append(min(edges[-1] + (1 if edges[-1] < 2 else 2), nb))
    parts = [_dsa_prompt_group(q, qi, wi, kk, kiki, vvt, batch, t_len, a, b - a, topk)
             for a, b in zip(edges[:-1], edges[1:]) if b > a]
    return jnp.concatenate(parts, axis=1).reshape(batch * t_len, 512)


def _mem_kv_kernel(m_ref, w_ref, k_ref, v_ref):
    z = _dot(m_ref[...].astype(BF16), w_ref[...])
    k_ref[...] = z[:, :MEM_WIDTH]
    v_ref[...] = z[:, MEM_WIDTH:]


def _mem_kv(mem, w_kv):
    n = mem.shape[0]
    tm = min(n, 512)
    row = lambda i: (i, 0)
    return pl.pallas_call(
        _mem_kv_kernel,
        grid=(n // tm,),
        in_specs=[pl.BlockSpec((tm, D_MODEL), row),
                  pl.BlockSpec((D_MODEL, 2 * MEM_WIDTH), lambda i: (0, 0))],
        out_specs=[pl.BlockSpec((tm, MEM_WIDTH), row), pl.BlockSpec((tm, MEM_WIDTH), row)],
        out_shape=[jax.ShapeDtypeStruct((n, MEM_WIDTH), F32)] * 2,
        compiler_params=_params("parallel"),
        name="mem_kv",
    )(mem, w_kv)


def _pool_project(window_sums, u, cnts, pw_ref, ps_ref):
    ys = []
    for g in range(len(POOL_WINDOWS)):
        sl = slice(g * POOL_CG, (g + 1) * POOL_CG)
        pooled = window_sums[g] / cnts[g] - u[:, sl]
        ys.append(_dot(pooled.astype(BF16), pw_ref[g]) * ps_ref[:, sl])
    return jnp.concatenate(ys, axis=1)


def _mid_prompt_kernel(x_ref, attn_ref, u_ref, halo_ref, mk_ref, mv_ref, pw_ref, ps_ref, wo_ref,
                       gm_ref, wmq_ref, wmo_ref, gf_ref, h2_ref, tok_ref, e_ref, *, nt, n_real):
    step = pl.program_id(0)

    @pl.when(step >= n_real)
    def _():
        h2_ref[...] = jnp.zeros_like(h2_ref)
        tok_ref[...] = jnp.zeros_like(tok_ref)

    @pl.when(step < n_real)
    def _():
        _mid_prompt_rows(x_ref, attn_ref, u_ref, halo_ref, mk_ref, mv_ref, pw_ref, ps_ref, wo_ref,
                         gm_ref, wmq_ref, wmo_ref, gf_ref, h2_ref, tok_ref, e_ref, step % nt)


def _mid_prompt_rows(x_ref, attn_ref, u_ref, halo_ref, mk_ref, mv_ref, pw_ref, ps_ref, wo_ref,
                     gm_ref, wmq_ref, wmo_ref, gf_ref, h2_ref, tok_ref, e_ref, i):
    tm = x_ref.shape[0]
    u = u_ref[...]
    halo = halo_ref[...]
    e_ref[0:POOL_MAX, :] = jnp.where(i == 0, jnp.zeros_like(halo), halo)
    e_ref[POOL_MAX:POOL_MAX + tm, :] = u
    pos = i * tm + lax.broadcasted_iota(jnp.int32, (tm, 1), 0)
    sums, cnts = [], []
    for g, w in enumerate(POOL_WINDOWS):
        sl = slice(g * POOL_CG, (g + 1) * POOL_CG)
        s = u[:, sl]
        for j in range(1, w):
            s = s + e_ref[POOL_MAX - j:POOL_MAX - j + tm, sl]
        sums.append(s)
        cnts.append(jnp.minimum(w, pos + 1).astype(F32))
    pool = _pool_project(sums, u, cnts, pw_ref, ps_ref)
    h1 = x_ref[...] + _dot(attn_ref[...], wo_ref[0:ATTN_WIDTH, :]) \
        + _dot(pool.astype(BF16), wo_ref[ATTN_WIDTH:, :])

    qm = _dot(_rms(h1, gm_ref[...]).astype(BF16), wmq_ref[...]).astype(BF16)
    mk = mk_ref[0].astype(BF16)
    mv = mv_ref[0].astype(BF16)
    outs = []
    for h in range(MEM_HEADS):
        sl = slice(h * MEM_HEAD_DIM, (h + 1) * MEM_HEAD_DIM)
        logits = _dot_nt(qm[:, sl], mk[:, sl]) * (MEM_HEAD_DIM ** -0.5)
        m = jnp.max(logits, axis=1, keepdims=True)
        p = jnp.exp(logits - m)
        p = p / jnp.sum(p, axis=1, keepdims=True)
        outs.append(_dot(p.astype(BF16), mv[:, sl]))
    o = jnp.concatenate(outs, axis=1)
    h2 = h1 + _dot(o.astype(BF16), wmo_ref[...])
    h2_ref[...] = h2
    tok_ref[...] = _rms(h2, gf_ref[...]).astype(BF16)


def _mid_prompt(x, attn, u, mk, mv, pool_w, pool_scale, w_o, g_mem, w_mq, w_mo, g_ffn, batch, t_len, tm, n_rows):
    nt = t_len // tm
    hb = tm // POOL_MAX
    n_real = batch * nt
    n_steps = -(-n_rows // tm)
    last = n_real - 1
    row = lambda s: (jnp.minimum(s, last), 0)
    halo = lambda s: (jnp.maximum(jnp.minimum(s, last) * hb - 1, 0), 0)
    fixed2 = lambda s: (0, 0)
    fixed3 = lambda s: (0, 0, 0)
    perb = lambda s: (jnp.minimum(s, last) // nt, 0, 0)
    out_row = lambda s: (s, 0)
    n = n_rows
    return pl.pallas_call(
        functools.partial(_mid_prompt_kernel, nt=nt, n_real=n_real),
        grid=(n_steps,),
        in_specs=[pl.BlockSpec((tm, D_MODEL), row),
                  pl.BlockSpec((tm, ATTN_WIDTH), row),
                  pl.BlockSpec((tm, POOL_WIDTH), row),
                  pl.BlockSpec((POOL_MAX, POOL_WIDTH), halo),
                  pl.BlockSpec((1, N_MEM, MEM_WIDTH), perb),
                  pl.BlockSpec((1, N_MEM, MEM_WIDTH), perb),
                  pl.BlockSpec((len(POOL_WINDOWS), POOL_CG, POOL_CG), fixed3),
                  pl.BlockSpec((1, POOL_WIDTH), fixed2),
                  pl.BlockSpec((D_MODEL, D_MODEL), fixed2),
                  pl.BlockSpec((1, D_MODEL), fixed2),
                  pl.BlockSpec((D_MODEL, MEM_WIDTH), fixed2),
                  pl.BlockSpec((MEM_WIDTH, D_MODEL), fixed2),
                  pl.BlockSpec((1, D_MODEL), fixed2)],
        out_specs=[pl.BlockSpec((tm, D_MODEL), out_row), pl.BlockSpec((tm, D_MODEL), out_row)],
        out_shape=[jax.ShapeDtypeStruct((n, D_MODEL), F32), jax.ShapeDtypeStruct((n, D_MODEL), BF16)],
        scratch_shapes=[pltpu.VMEM((POOL_MAX + tm, POOL_WIDTH), F32)],
        compiler_params=_params("arbitrary"),
        name="mid_prompt",
    )(x, attn, u, u, mk, mv, pool_w, pool_scale, w_o, g_mem, w_mq, w_mo, g_ffn)


def _router_kernel(tok_ref, rw_ref, rb_ref, rank_ref, gate_ref, cnt_ref, *, n_valid):
    tm = tok_ref.shape[0]
    token = pl.program_id(0) * tm + lax.broadcasted_iota(jnp.int32, (1, tm), 1)
    valid = token < n_valid
    logits = _dot_nt(rw_ref[...], tok_ref[...]) + rb_ref[...]
    eid = lax.broadcasted_iota(jnp.int32, (N_EXPERTS, tm), 0)
    vals, hots = [], []
    cur = logits
    for _ in range(TOP_K):
        m = jnp.max(cur, axis=0, keepdims=True)
        idx = jnp.min(jnp.where(cur == m, eid, N_EXPERTS), axis=0, keepdims=True)
        hot = eid == idx
        vals.append(m)
        hots.append(hot)
        cur = jnp.where(hot, -jnp.inf, cur)
    exps = [jnp.exp(v - vals[0]) for v in vals]
    den = exps[0] + exps[1] + exps[2] + exps[3]
    gate = jnp.zeros((N_EXPERTS, tm), F32)
    sel = jnp.zeros((N_EXPERTS, tm), jnp.bool_)
    for hot, ex in zip(hots, exps):
        gate = jnp.where(hot, ex / den, gate)
        sel = jnp.logical_or(sel, hot)
    sel = jnp.logical_and(sel, valid)
    gate_ref[0] = gate

    upper = _strict_triangle(LANES, lower=False)
    carry = jnp.zeros((N_EXPERTS, 1), F32)
    for c in range(tm // LANES):
        sl = slice(c * LANES, (c + 1) * LANES)
        sel_c = sel[:, sl]
        sel_b = jnp.where(sel_c, 1.0, 0.0).astype(BF16)
        rank = _dot(sel_b, upper) + carry
        rank_ref[0, :, sl] = jnp.where(sel_c, rank, -1.0).astype(jnp.int32)
        carry = carry + jnp.sum(sel_b.astype(F32), axis=1, keepdims=True)
    cnt_ref[0] = carry.astype(jnp.int32)


def _router(tok, rw_t, rb, tm, n_valid):
    n = tok.shape[0]
    nt = n // tm
    t3 = lambda i: (i, 0, 0)
    return pl.pallas_call(
        functools.partial(_router_kernel, n_valid=n_valid),
        grid=(nt,),
        in_specs=[pl.BlockSpec((tm, D_MODEL), lambda i: (i, 0)),
                  pl.BlockSpec((N_EXPERTS, D_MODEL), lambda i: (0, 0)),
                  pl.BlockSpec((N_EXPERTS, 1), lambda i: (0, 0))],
        out_specs=[pl.BlockSpec((1, N_EXPERTS, tm), t3),
                   pl.BlockSpec((1, N_EXPERTS, tm), t3),
                   pl.BlockSpec((1, N_EXPERTS, 1), t3)],
        out_shape=[jax.ShapeDtypeStruct((nt, N_EXPERTS, tm), jnp.int32),
                   jax.ShapeDtypeStruct((nt, N_EXPERTS, tm), F32),
                   jax.ShapeDtypeStruct((nt, N_EXPERTS, 1), jnp.int32)],
        compiler_params=_params("parallel"),
        name="router",
    )(tok, rw_t, rb)


def _swiglu(h):
    gate = jnp.minimum(h[:, :D_FF], SWIGLU_LIMIT)
    up = jnp.clip(h[:, D_FF:], -SWIGLU_LIMIT, SWIGLU_LIMIT)
    return gate * jax.nn.sigmoid(SWIGLU_ALPHA * gate) * (up + 1.0)


def _moe_kernel(cnt_ref, x_ref, h2_ref, rank_ref, gate_ref, wu_ref, bu_ref, wd_ref, bd_ref, gfin_ref,
                o_ref, os_ref, *, first_chunk, chunk, sample_row):
    i = pl.program_id(0)
    e = pl.program_id(1)
    tm = x_ref.shape[0]

    @pl.when(e == 0)
    def _():
        o_ref[...] = jnp.zeros_like(o_ref)

    n_routed = cnt_ref[i * N_EXPERTS + e]
    rank = rank_ref[0, pl.ds(e, 1), :]
    gate = gate_ref[0, pl.ds(e, 1), :]

    def run_slots(first, size):
        slot = first + lax.broadcasted_iota(jnp.int32, (size, tm), 0)
        hit = rank == slot
        onehot = jnp.where(hit, 1.0, 0.0).astype(BF16)
        xg = _dot(onehot, x_ref[...]).astype(BF16)
        act = _swiglu(_dot(xg, wu_ref[0]) + bu_ref[0])
        out = _dot(act.astype(BF16), wd_ref[0]) + bd_ref[0]
        g_slot = jnp.sum(jnp.where(hit, gate, 0.0), axis=1, keepdims=True)
        o_ref[...] += _dot_tn(onehot, (out * g_slot).astype(BF16))

    sizes = [s for s in MOE_FIRST_SIZES if s < first_chunk] + [first_chunk]
    for lo, size in zip([0] + sizes[:-1], sizes):
        fits = n_routed <= size if size < first_chunk else n_routed > lo
        @pl.when(jnp.logical_and(n_routed > lo, fits))
        def _():
            run_slots(0, size)

    def body(c, carry):
        run_slots(first_chunk + c * chunk, chunk)
        return carry

    lax.fori_loop(0, (jnp.maximum(n_routed - first_chunk, 0) + chunk - 1) // chunk, body, 0)

    @pl.when(e == pl.num_programs(1) - 1)
    def _():
        o_ref[...] = _rms(h2_ref[...] + o_ref[...], gfin_ref[...])

        @pl.when(i == pl.num_programs(0) - 1)
        def _():
            os_ref[...] = o_ref[sample_row:sample_row + os_ref.shape[0], :]


def _moe(tok, h2, rank, gate, cnt, wu, bu, wd, bd, g_final, tm, n_prompt, first_chunk, chunk):
    nt = tok.shape[0] // tm
    tile = lambda i, e, c: (i, 0)
    tile3 = lambda i, e, c: (i, 0, 0)
    expert = lambda i, e, c: (e, 0, 0)
    grid_spec = pltpu.PrefetchScalarGridSpec(
        num_scalar_prefetch=1,
        grid=(nt, N_EXPERTS),
        in_specs=[pl.BlockSpec((tm, D_MODEL), tile),
                  pl.BlockSpec((tm, D_MODEL), tile),
                  pl.BlockSpec((1, N_EXPERTS, tm), tile3),
                  pl.BlockSpec((1, N_EXPERTS, tm), tile3),
                  pl.BlockSpec((1, D_MODEL, 2 * D_FF), expert),
                  pl.BlockSpec((1, 1, 2 * D_FF), expert),
                  pl.BlockSpec((1, D_FF, D_MODEL), expert),
                  pl.BlockSpec((1, 1, D_MODEL), expert),
                  pl.BlockSpec((1, D_MODEL), lambda i, e, c: (0, 0))],
        out_specs=[pl.BlockSpec((tm, D_MODEL), tile),
                   pl.BlockSpec((SAMPLE_ROWS, D_MODEL), lambda i, e, c: (0, 0))],
    )
    return pl.pallas_call(
        functools.partial(_moe_kernel, first_chunk=first_chunk, chunk=chunk,
                          sample_row=n_prompt - (nt - 1) * tm),
        grid_spec=grid_spec,
        out_shape=[jax.ShapeDtypeStruct((n_prompt, D_MODEL), F32),
                   jax.ShapeDtypeStruct((SAMPLE_ROWS, D_MODEL), F32)],
        compiler_params=_params("arbitrary", "arbitrary"),
        name="moe",
    )(cnt, tok, h2, rank, gate, wu, bu, wd, bd, g_final)


def _page_copy(cache_ref, buf_ref, sem_ref, pt_ref, b, p, slot):
    dst = buf_ref.at[slot, :, pl.ds(pl.multiple_of(p * PAGE_SIZE, PAGE_SIZE), PAGE_SIZE)]
    return pltpu.make_async_copy(cache_ref.at[0, pt_ref[b, p]], dst, sem_ref.at[slot])


def _fetch_pages(cache_ref, buf_ref, sem_ref, pt_ref, b, slot, n_pages):
    def body(p, c):
        _page_copy(cache_ref, buf_ref, sem_ref, pt_ref, b, p, slot).start()
        return c
    lax.fori_loop(0, n_pages, body, 0, unroll=8)


def _wait_pages(cache_ref, buf_ref, sem_ref, pt_ref, b, slot, n_pages):
    def body(p, c):
        _page_copy(cache_ref, buf_ref, sem_ref, pt_ref, b, p, slot).wait()
        return c
    lax.fori_loop(0, n_pages, body, 0, unroll=8)


def _dsa_sample_score_kernel(pt_ref, qi_ref, wi_ref, kin_ref, cache_ref, score_ref, new_ref, buf_ref, sem_ref):
    b = pl.program_id(0)
    nb = pl.num_programs(0)
    past = buf_ref.shape[2]
    n_pages = past // PAGE_SIZE
    slot = b % 2

    @pl.when(b == 0)
    def _():
        _fetch_pages(cache_ref, buf_ref, sem_ref, pt_ref, b, slot, n_pages)

    @pl.when(b + 1 < nb)
    def _():
        _fetch_pages(cache_ref, buf_ref, sem_ref, pt_ref, b + 1, 1 - slot, n_pages)

    _wait_pages(cache_ref, buf_ref, sem_ref, pt_ref, b, slot, n_pages)

    qi = qi_ref[0]
    w = wi_ref[0] * (IDX_HEADS ** -0.5)
    keys_t = buf_ref[slot].astype(BF16)
    score_ref[0] = jnp.sum(jnp.maximum(_dot(qi, keys_t), 0.0) * w, axis=0, keepdims=True)
    s_new = jnp.sum(qi.astype(F32) * kin_ref[0].astype(F32), axis=1, keepdims=True)
    new_ref[0] = jnp.sum(jnp.maximum(s_new, 0.0) * w, axis=0, keepdims=True)


def _dsa_sample_select_kernel(score_ref, new_ref, mask_ref, take_new_ref, *, topk):
    score = score_ref[...]
    score_new = new_ref[...]
    past = score.shape[1]

    def count_ge(cand):
        return (jnp.sum(jnp.where(score >= cand, 1, 0), axis=1, keepdims=True)
                + jnp.where(score_new >= cand, 1, 0))

    tau = _kth_largest(count_ge, topk, score_new.shape)
    gt = score > tau
    eq = score == tau
    n_gt = jnp.sum(jnp.where(gt, 1, 0), axis=1, keepdims=True) + jnp.where(score_new > tau, 1, 0)
    n_eq_all = jnp.sum(jnp.where(eq, 1, 0), axis=1, keepdims=True) + jnp.where(score_new == tau, 1, 0)
    tied = n_gt + n_eq_all > topk
    need_f = jnp.where(tied, topk - n_gt, past + 2).astype(F32)
    take_eq, n_eq = _take_first_ties(eq, need_f, tied, axis=1)
    mask_ref[...] = jnp.where(jnp.logical_or(gt, take_eq > 0.5), 0.0, NEG)
    take_new = jnp.logical_or(score_new > tau, jnp.logical_and(score_new == tau, n_eq < need_f))
    take_new_ref[...] = jnp.where(take_new, 0.0, NEG)


def _dsa_sample_select(score, score_new, topk):
    db, past = score.shape
    full = lambda shape: pl.BlockSpec(shape, lambda i: (0, 0))
    return pl.pallas_call(
        functools.partial(_dsa_sample_select_kernel, topk=topk),
        grid=(1,),
        in_specs=[full((db, past)), full((db, 1))],
        out_specs=[full((db, past)), full((db, 1))],
        out_shape=[jax.ShapeDtypeStruct((db, past), F32), jax.ShapeDtypeStruct((db, 1), F32)],
        compiler_params=_params("arbitrary"),
        name="dsa_sample_select",
    )(score, score_new)


def _dsa_sample_mask(page_table, qi_heads, wi_col, ki_new, cache_idx_k):
    db, n_pages = page_table.shape
    past = n_pages * PAGE_SIZE
    topk = min(TOPK_MAX, (past + 1) // 4)
    per = lambda b, pt: (b, 0, 0)
    grid_spec = pltpu.PrefetchScalarGridSpec(
        num_scalar_prefetch=1,
        grid=(db,),
        in_specs=[pl.BlockSpec((1, IDX_HEADS, IDX_DIM), per),
                  pl.BlockSpec((1, IDX_HEADS, 1), per),
                  pl.BlockSpec((1, 1, IDX_DIM), per),
                  pl.BlockSpec(memory_space=pl.ANY)],
        out_specs=[pl.BlockSpec((1, 1, past), per), pl.BlockSpec((1, 1, 1), per)],
        scratch_shapes=[pltpu.VMEM((2, IDX_DIM, past), F32),
                        pltpu.SemaphoreType.DMA((2,))],
    )
    score, score_new = pl.pallas_call(
        _dsa_sample_score_kernel,
        grid_spec=grid_spec,
        out_shape=[jax.ShapeDtypeStruct((db, 1, past), F32), jax.ShapeDtypeStruct((db, 1, 1), F32)],
        compiler_params=_params("arbitrary"),
        name="dsa_sample_score",
    )(page_table, qi_heads, wi_col, ki_new, cache_idx_k)
    mask, take_new = _dsa_sample_select(score.reshape(db, past), score_new.reshape(db, 1), topk)
    return mask.reshape(db, 1, past), take_new.reshape(db, 1, 1)


def _dsa_sample_attend_kernel(pt_ref, q_ref, kn_ref, vn_ref, mask_ref, new_ref, ck_ref, cv_ref, o_ref,
                              kbuf_ref, vbuf_ref, ksem_ref, vsem_ref):
    b = pl.program_id(0)
    nb = pl.num_programs(0)
    past = kbuf_ref.shape[2]
    n_pages = past // PAGE_SIZE
    slot = b % 2

    def fetch(bb, sl):
        _fetch_pages(ck_ref, kbuf_ref, ksem_ref, pt_ref, bb, sl, n_pages)
        _fetch_pages(cv_ref, vbuf_ref, vsem_ref, pt_ref, bb, sl, n_pages)

    @pl.when(b == 0)
    def _():
        fetch(b, slot)

    @pl.when(b + 1 < nb)
    def _():
        fetch(b + 1, 1 - slot)

    _wait_pages(ck_ref, kbuf_ref, ksem_ref, pt_ref, b, slot, n_pages)
    _wait_pages(cv_ref, vbuf_ref, vsem_ref, pt_ref, b, slot, n_pages)

    q = q_ref[0]
    keys_t = kbuf_ref[slot].astype(BF16)
    vals_t = vbuf_ref[slot].astype(BF16)
    logits = _dot(q, keys_t) + mask_ref[0]
    l_new = jnp.sum(q.astype(F32) * kn_ref[0].astype(F32), axis=1, keepdims=True) + new_ref[0]
    m = jnp.maximum(jnp.max(logits, axis=1, keepdims=True), l_new)
    p = jnp.exp(logits - m)
    p_new = jnp.exp(l_new - m)
    den = jnp.sum(p, axis=1, keepdims=True) + p_new
    p = p / den
    p_new = p_new / den
    o_ref[0] = _dot_nt(p.astype(BF16), vals_t) + p_new * vn_ref[0].astype(F32)


def _dsa_sample_attend(page_table, q_heads, k_new, v_new, mask, take_new, cache_k, cache_v):
    db, n_pages = page_table.shape
    past = n_pages * PAGE_SIZE
    per = lambda b, pt: (b, 0, 0)
    grid_spec = pltpu.PrefetchScalarGridSpec(
        num_scalar_prefetch=1,
        grid=(db,),
        in_specs=[pl.BlockSpec((1, ATTN_HEADS, HEAD_DIM), per),
                  pl.BlockSpec((1, 1, HEAD_DIM), per),
                  pl.BlockSpec((1, 1, HEAD_DIM), per),
                  pl.BlockSpec((1, 1, past), per),
                  pl.BlockSpec((1, 1, 1), per),
                  pl.BlockSpec(memory_space=pl.ANY),
                  pl.BlockSpec(memory_space=pl.ANY)],
        out_specs=pl.BlockSpec((1, ATTN_HEADS, HEAD_DIM), per),
        scratch_shapes=[pltpu.VMEM((2, HEAD_DIM, past), F32),
                        pltpu.VMEM((2, HEAD_DIM, past), F32),
                        pltpu.SemaphoreType.DMA((2,)),
                        pltpu.SemaphoreType.DMA((2,))],
    )
    return pl.pallas_call(
        _dsa_sample_attend_kernel,
        grid_spec=grid_spec,
        out_shape=jax.ShapeDtypeStruct((db, ATTN_HEADS, HEAD_DIM), F32),
        compiler_params=_params("arbitrary"),
        name="dsa_sample_attend",
    )(page_table, q_heads, k_new, v_new, mask, take_new, cache_k, cache_v)


def _mid_sample_a_kernel(x_ref, attn_ref, u_ref, st_ref, pw_ref, ps_ref, wo_ref, gm_ref, wmq_ref,
                         h1_ref, qm_ref, *, past):
    u = u_ref[...]
    rows = u.shape[0]
    sums, cnts = [], []
    for g, w in enumerate(POOL_WINDOWS):
        sl = slice(g * POOL_CG, (g + 1) * POOL_CG)
        s = u[:, sl]
        for j in range(1, w):
            s = s + st_ref[POOL_MAX - 1 - j][:, sl]
        sums.append(s)
        cnts.append(jnp.full((rows, 1), float(min(w, past + 1)), F32))
    pool = _pool_project(sums, u, cnts, pw_ref, ps_ref)
    h1 = x_ref[...] + _dot(attn_ref[...], wo_ref[0:ATTN_WIDTH, :]) \
        + _dot(pool.astype(BF16), wo_ref[ATTN_WIDTH:, :])
    h1_ref[...] = h1
    qm_ref[...] = _dot(_rms(h1, gm_ref[...]).astype(BF16), wmq_ref[...]).astype(BF16)


def _mid_sample_a(x, attn, u, state_t, pool_w, pool_scale, w_o, g_mem, w_mq, past):
    db = x.shape[0]
    full = lambda a: pl.BlockSpec(a.shape, lambda i, nd=a.ndim: (0,) * nd)
    args = (x, attn, u, state_t, pool_w, pool_scale, w_o, g_mem, w_mq)
    return pl.pallas_call(
        functools.partial(_mid_sample_a_kernel, past=past),
        grid=(1,),
        in_specs=[full(a) for a in args],
        out_specs=[pl.BlockSpec((db, D_MODEL), lambda i: (0, 0)),
                   pl.BlockSpec((db, MEM_WIDTH), lambda i: (0, 0))],
        out_shape=[jax.ShapeDtypeStruct((db, D_MODEL), F32), jax.ShapeDtypeStruct((db, MEM_WIDTH), BF16)],
        compiler_params=_params("arbitrary"),
        name="mid_sample_a",
    )(*args)


def _mem_sample_kernel(q_ref, mk_ref, mv_ref, o_ref):
    rows = 2 * MEM_HEADS
    n = mk_ref.shape[1]
    q_all = q_ref[0].astype(F32)
    q = jnp.concatenate([q_all[:, (r % MEM_HEADS) * MEM_HEAD_DIM:(r % MEM_HEADS + 1) * MEM_HEAD_DIM]
                         for r in range(rows)], axis=0).astype(BF16)
    logits = _dot_nt(q, mk_ref[0].astype(BF16)) * (MEM_HEAD_DIM ** -0.5)
    own = (lax.broadcasted_iota(jnp.int32, (rows, n), 1) % MEM_HEADS
           == lax.broadcasted_iota(jnp.int32, (rows, n), 0) % MEM_HEADS)
    logits = jnp.where(own, logits, -jnp.inf)
    m = jnp.max(logits, axis=1, keepdims=True)
    p = jnp.exp(logits - m)
    p = p / jnp.sum(p, axis=1, keepdims=True)
    o = _dot(p.astype(BF16), mv_ref[0].astype(BF16))
    for h in range(MEM_HEADS):
        o_ref[0, :, h * MEM_HEAD_DIM:(h + 1) * MEM_HEAD_DIM] = o[h:h + 1, :]


def _mem_sample(qm, mk, mv):
    db = qm.shape[0]
    per = lambda b: (b, 0, 0)
    return pl.pallas_call(
        _mem_sample_kernel,
        grid=(db,),
        in_specs=[pl.BlockSpec((1, 1, MEM_WIDTH), per),
                  pl.BlockSpec((1, N_MEM * MEM_HEADS, MEM_HEAD_DIM), per),
                  pl.BlockSpec((1, N_MEM * MEM_HEADS, MEM_HEAD_DIM), per)],
        out_specs=pl.BlockSpec((1, 1, MEM_WIDTH), per),
        out_shape=jax.ShapeDtypeStruct((db, 1, MEM_WIDTH), F32),
        compiler_params=_params("parallel"),
        name="mem_sample",
    )(qm.reshape(db, 1, MEM_WIDTH), mk, mv)


def _sample_join_kernel(h1_ref, o_ref, wmo_ref, gf_ref, h2_any, tok_any, h2_ref, tok_ref):
    del h2_any, tok_any
    db = h1_ref.shape[0]
    h2_ref[...] = jnp.zeros_like(h2_ref)
    tok_ref[...] = jnp.zeros_like(tok_ref)

    @pl.when(pl.program_id(0) == 0)
    def _():
        h2 = h1_ref[...] + _dot(o_ref[...].astype(BF16), wmo_ref[...])
        h2_ref[0:db, :] = h2
        tok_ref[0:db, :] = _rms(h2, gf_ref[...]).astype(BF16)


def _sample_join(h1, o, w_mo, g_ffn, h2_all, tok_all, n_prompt):
    db = h1.shape[0]
    n_rows = h2_all.shape[0]
    assert db <= SAMPLE_ROWS and n_prompt % SAMPLE_ROWS == 0 and n_rows % SAMPLE_ROWS == 0
    first = n_prompt // SAMPLE_ROWS
    f2 = lambda j: (0, 0)
    tail = lambda j: (first + j, 0)
    return pl.pallas_call(
        _sample_join_kernel,
        grid=((n_rows - n_prompt) // SAMPLE_ROWS,),
        in_specs=[pl.BlockSpec((db, D_MODEL), f2),
                  pl.BlockSpec((db, MEM_WIDTH), f2),
                  pl.BlockSpec((MEM_WIDTH, D_MODEL), f2),
                  pl.BlockSpec((1, D_MODEL), f2),
                  pl.BlockSpec(memory_space=pl.ANY),
                  pl.BlockSpec(memory_space=pl.ANY)],
        out_specs=[pl.BlockSpec((SAMPLE_ROWS, D_MODEL), tail), pl.BlockSpec((SAMPLE_ROWS, D_MODEL), tail)],
        out_shape=[jax.ShapeDtypeStruct(h2_all.shape, F32), jax.ShapeDtypeStruct(tok_all.shape, BF16)],
        input_output_aliases={4: 0, 5: 1},
        compiler_params=_params("arbitrary"),
        name="sample_join",
    )(h1, o, w_mo, g_ffn, h2_all, tok_all)


def _rope_tables(pos):
    half = HEAD_DIM // 2
    inv = ROPE_THETA ** (-jnp.arange(half, dtype=F32) / half)
    ang = pos.astype(F32)[:, None] * inv[None, :]
    cos = jnp.cos(ang)
    sin = jnp.sin(ang)
    return jnp.tile(cos, (1, 4)), jnp.concatenate([-sin, sin, -sin, sin], axis=1)


def _pick_tile(n, pref):
    t = min(n, pref)
    while n % t:
        t //= 2
    return t


def kernel(x_prompt, x_sample, mem_prompt, cache_k, cache_v, cache_idx_k, state_pool, cache_mem_k, cache_mem_v, page_table, norm_mix, w_in, pool_w, pool_scale, w_o, norm_mem, w_mq, w_mk, w_mv, w_mo, norm_ffn, router_w, router_b, w_up, b_up, w_down, b_down, norm_final):
    B, T = x_prompt.shape[:2]
    DB, DS = x_sample.shape[:2]
    assert DS == 1 and w_in.shape[0] == 1
    n_pages = page_table.shape[1]
    past = n_pages * PAGE_SIZE
    NP = B * T

    wi_ = w_in[0]
    wq, wk, wv, wu_, wqi, wki, wwi = jnp.split(wi_, [512, 576, 640, 1152, 1664, 1728], axis=1)
    w_all = jnp.concatenate(
        [wq, wqi, wu_, wk, wk, wki, wki, wv, wv, wwi, jnp.zeros((D_MODEL, LANES - IDX_HEADS), F32)],
        axis=1).astype(BF16)
    g_mix = norm_mix[0][None, :]
    g_mem = norm_mem[0][None, :]
    g_ffn = norm_ffn[0][None, :]
    g_fin = norm_final[None, :]
    pw = pool_w[0].astype(BF16)
    ps = pool_scale[0][None, :]
    wo = w_o[0].astype(BF16)
    wmq = w_mq[0].reshape(D_MODEL, MEM_WIDTH).astype(BF16)
    wkv = jnp.concatenate([w_mk[0].reshape(D_MODEL, MEM_WIDTH), w_mv[0].reshape(D_MODEL, MEM_WIDTH)],
                          axis=1).astype(BF16)
    wmo = w_mo[0].reshape(MEM_WIDTH, D_MODEL).astype(BF16)
    rw = router_w[0].astype(BF16)
    rb = router_b[0]
    wup = w_up[0].astype(BF16)
    wdn = w_down[0].astype(BF16)
    bup = b_up[0][:, None, :]
    bdn = b_down[0][:, None, :]

    xp = x_prompt.reshape(NP, D_MODEL)
    tm_in = _pick_tile(T, 512)
    cos_p, sin_p = _rope_tables(jnp.arange(T, dtype=jnp.int32))
    q, qi, u, kk, kiki, vv, k_p, ki_p, v_p, wi_p = _in_project(xp, g_mix, w_all, cos_p, sin_p, tm_in, T // tm_in)
    attn = _dsa_prompt(q, qi, wi_p, kk, kiki, vv, B, T)
    mk_p, mv_p = _mem_kv(mem_prompt.reshape(B * N_MEM, D_MODEL), wkv)
    tm_moe = min(MOE_TILE, NP)
    n_tiles = -(-(NP + SAMPLE_ROWS) // tm_moe)
    assert (n_tiles - 1) * tm_moe < NP and NP + SAMPLE_ROWS <= n_tiles * tm_moe
    h2_all, tok_all = _mid_prompt(xp, attn, u, mk_p.reshape(B, N_MEM, MEM_WIDTH), mv_p.reshape(B, N_MEM, MEM_WIDTH),
                                  pw, ps, wo, g_mem, wmq, wmo, g_ffn, B, T, _pick_tile(T, 1024), n_tiles * tm_moe)

    xs = x_sample.reshape(DB, D_MODEL)
    cos_s, sin_s = _rope_tables(jnp.full((DB,), past, jnp.int32))
    q_s, qi_s, u_s, kk_s, kiki_s, vv_s, k_s, ki_s, v_s, wi_s = _in_project(xs, g_mix, w_all, cos_s, sin_s, DB, 1)
    mask, take_new = _dsa_sample_mask(
        page_table, qi_s.reshape(DB, IDX_HEADS, IDX_DIM), wi_s[:, :IDX_HEADS, None],
        kiki_s[:, None, :IDX_DIM], jnp.swapaxes(cache_idx_k, 2, 3))
    attn_s = _dsa_sample_attend(
        page_table, q_s.reshape(DB, ATTN_HEADS, HEAD_DIM), kk_s[:, None, :HEAD_DIM], vv_s[:, None, :HEAD_DIM],
        mask, take_new, jnp.swapaxes(cache_k, 2, 3), jnp.swapaxes(cache_v, 2, 3))
    attn_s = attn_s.reshape(DB, ATTN_WIDTH).astype(BF16)
    h1_s, qm_s = _mid_sample_a(xs, attn_s, u_s, jnp.transpose(state_pool[0], (1, 0, 2)), pw, ps, wo, g_mem, wmq, past)
    o_s = _mem_sample(qm_s, cache_mem_k.reshape(DB, N_MEM * MEM_HEADS, MEM_HEAD_DIM),
                      cache_mem_v.reshape(DB, N_MEM * MEM_HEADS, MEM_HEAD_DIM))
    h2_all, tok_all = _sample_join(h1_s, o_s.reshape(DB, MEM_WIDTH), wmo, g_ffn, h2_all, tok_all, NP)

    rank, gate, cnt = _router(tok_all, rw.T, rb[:, None], tm_moe, NP + DB)
    y_p, y_s = _moe(tok_all, h2_all, rank, gate, cnt.reshape(-1), wup, bup, wdn, bdn, g_fin, tm_moe, NP, 256, 128)
    y_prompt = y_p.reshape(B, T, D_MODEL)
    y_sample = y_s[:DB].reshape(DB, 1, D_MODEL)

    new_pool_p = u.reshape(B, T, POOL_WIDTH)[:, T - (POOL_MAX - 1):][None]
    new_pool_s = jnp.concatenate([state_pool[0][:, 1:], u_s[:, None, :]], axis=1)[None]
    return (y_prompt, y_sample,
            k_p.reshape(1, B, T, HEAD_DIM), v_p.reshape(1, B, T, HEAD_DIM), ki_p.reshape(1, B, T, IDX_DIM),
            new_pool_p,
            mk_p.reshape(1, B, N_MEM, MEM_HEADS, MEM_HEAD_DIM), mv_p.reshape(1, B, N_MEM, MEM_HEADS, MEM_HEAD_DIM),
            k_s.reshape(1, DB, 1, HEAD_DIM), v_s.reshape(1, DB, 1, HEAD_DIM), ki_s.reshape(1, DB, 1, IDX_DIM),
            new_pool_s)
```

```python
import functools

import jax
import jax.numpy as jnp
from jax import lax
from jax.experimental import pallas as pl
from jax.experimental.pallas import tpu as pltpu

D_MODEL = 1024
HEAD_DIM = 64
ATTN_WIDTH = 512
ATTN_HEADS = 8
IDX_HEADS = 8
IDX_DIM = 64
TOPK_MAX = 256
Q_BLOCK = 128
PAGE_SIZE = 128
POOL_WIDTH = 512
POOL_WINDOWS = (2, 4, 8, 16)
POOL_CG = 128
POOL_MAX = 16
N_MEM = 256
MEM_HEADS = 4
MEM_HEAD_DIM = 128
MEM_WIDTH = MEM_HEADS * MEM_HEAD_DIM
N_EXPERTS = 32
TOP_K = 4
D_FF = 1024
SWIGLU_LIMIT = 7.0
SWIGLU_ALPHA = 1.702
ROPE_THETA = 10000.0
EPS = 1e-6
NEG = -1e30

LANES = 128
MOE_TILE = 13 * LANES
MOE_FIRST_SIZES = (208, 224)
SAMPLE_ROWS = 128
DSA_Q = 256
DSA_KCHUNK = 256
IN_COLS = 3 * 512 + 4 * LANES
VMEM_LIMIT = 56 * 1024 * 1024

BF16 = jnp.bfloat16
F32 = jnp.float32


def _dot(a, b):
    return jnp.dot(a, b, preferred_element_type=F32)


def _dot_nt(a, b):
    return lax.dot_general(a, b, (((1,), (1,)), ((), ())), preferred_element_type=F32)


def _dot_tn(a, b):
    return lax.dot_general(a, b, (((0,), (0,)), ((), ())), preferred_element_type=F32)


def _rms(x, g):
    return x * lax.rsqrt(jnp.mean(x * x, axis=-1, keepdims=True) + EPS) * g


def _params(*sem):
    return pltpu.CompilerParams(dimension_semantics=sem, vmem_limit_bytes=VMEM_LIMIT)


def _in_kernel(x_ref, g_ref, w_ref, cos_ref, sin_ref,
               q_ref, qi_ref, u_ref, kk_ref, kiki_ref, vv_ref, k_ref, ki_ref, v_ref, wi_ref, *, feature_major):
    xn = _rms(x_ref[...], g_ref[...]).astype(BF16)
    z = _dot(xn, w_ref[...])
    cos = cos_ref[...]
    sin = sin_ref[...]
    lane = lax.broadcasted_iota(jnp.int32, cos.shape, 1)
    first_half = (lane % HEAD_DIM) < (HEAD_DIM // 2)

    def rope(zc):
        swapped = jnp.where(first_half,
                            pltpu.roll(zc, LANES - HEAD_DIM // 2, 1),
                            pltpu.roll(zc, HEAD_DIM // 2, 1))
        return zc * cos + swapped * sin

    for c in range(4):
        q_c = rope(z[:, c * LANES:(c + 1) * LANES]) * (HEAD_DIM ** -0.5)
        qi_c = rope(z[:, 512 + c * LANES:512 + (c + 1) * LANES]) * (IDX_DIM ** -0.5)
        q_ref[:, c * LANES:(c + 1) * LANES] = q_c.astype(BF16)
        qi_ref[:, c * LANES:(c + 1) * LANES] = qi_c.astype(BF16)
    u_ref[...] = z[:, 1024:1536]
    kk = rope(z[:, 1536:1664])
    kiki = rope(z[:, 1664:1792])
    vv = z[:, 1792:1920]
    kk_ref[...] = kk.astype(BF16)
    kiki_ref[...] = kiki.astype(BF16)
    wi_ref[...] = z[:, 1920:2048]
    if feature_major:
        kk_t = jnp.transpose(kk)
        kiki_t = jnp.transpose(kiki)
        vv_t = jnp.transpose(vv)
        vv_ref[0] = vv_t.astype(BF16)
        k_ref[0] = kk_t[0:HEAD_DIM, :]
        ki_ref[0] = kiki_t[0:IDX_DIM, :]
        v_ref[0] = vv_t[0:HEAD_DIM, :]
    else:
        vv_ref[...] = vv.astype(BF16)
        k_ref[...] = kk[:, :HEAD_DIM]
        ki_ref[...] = kiki[:, :IDX_DIM]
        v_ref[...] = vv[:, :HEAD_DIM]


def _in_project(x, g, w_all, cos, sin, tm, table_blocks, batch=None):
    n = x.shape[0]
    row = lambda i: (i, 0)
    tab = lambda i: (i % table_blocks, 0)
    fixed = lambda i: (0, 0)
    feature_major = batch is not None
    if feature_major:
        t_len = n // batch
        assert t_len == table_blocks * tm
        fm = lambda feat, dt: jax.ShapeDtypeStruct((batch, feat, t_len), dt)
        fm_spec = lambda feat: pl.BlockSpec((1, feat, tm), lambda i: (i // table_blocks, 0, i % table_blocks))
    else:
        fm = lambda feat, dt: jax.ShapeDtypeStruct((n, feat), dt)
        fm_spec = lambda feat: pl.BlockSpec((tm, feat), row)
    outs = [
        jax.ShapeDtypeStruct((n, 512), BF16),
        jax.ShapeDtypeStruct((n, 512), BF16),
        jax.ShapeDtypeStruct((n, 512), F32),
        jax.ShapeDtypeStruct((n, LANES), BF16),
        jax.ShapeDtypeStruct((n, LANES), BF16),
        fm(LANES, BF16),
        fm(HEAD_DIM, F32),
        fm(IDX_DIM, F32),
        fm(HEAD_DIM, F32),
        jax.ShapeDtypeStruct((n, LANES), F32),
    ]
    out_specs = ([pl.BlockSpec((tm, s.shape[1]), row) for s in outs[:5]]
                 + [fm_spec(LANES), fm_spec(HEAD_DIM), fm_spec(IDX_DIM), fm_spec(HEAD_DIM)]
                 + [pl.BlockSpec((tm, LANES), row)])
    return pl.pallas_call(
        functools.partial(_in_kernel, feature_major=feature_major),
        grid=(n // tm,),
        in_specs=[pl.BlockSpec((tm, D_MODEL), row),
                  pl.BlockSpec((1, D_MODEL), fixed),
                  pl.BlockSpec((D_MODEL, IN_COLS), fixed),
                  pl.BlockSpec((tm, LANES), tab),
                  pl.BlockSpec((tm, LANES), tab)],
        out_specs=out_specs,
        out_shape=outs,
        compiler_params=_params("parallel"),
        name="in_project",
    )(x, g, w_all, cos, sin)


def _f32_to_key(x):
    bits = lax.bitcast_convert_type(x, jnp.int32)
    return jnp.where(bits < 0, bits ^ jnp.int32(0x7FFFFFFF), bits)


def _key_to_f32(key):
    bits = jnp.where(key < 0, key ^ jnp.int32(0x7FFFFFFF), key)
    return lax.bitcast_convert_type(bits, F32)


def _kth_largest(count_ge, k, shape):
    def body(it, tau):
        cand = tau + jnp.left_shift(jnp.int32(1), jnp.int32(31) - it)
        return jnp.where(count_ge(_key_to_f32(cand)) >= k, cand, tau)

    return _key_to_f32(lax.fori_loop(0, 32, body, jnp.full(shape, -2 ** 31, jnp.int32)))


def _kth_largest_between(count_ge, k, lo, hi):
    key_lo = _f32_to_key(lo)
    span = _f32_to_key(hi) - key_lo
    n_bits = jnp.max(32 - lax.clz(span))
    flip = jnp.int32(-2 ** 31)

    def body(it, off):
        cand = off | jnp.left_shift(jnp.int32(1), n_bits - 1 - it)
        ok = jnp.logical_and(count_ge(_key_to_f32(key_lo + cand)) >= k, (cand ^ flip) <= (span ^ flip))
        return jnp.where(ok, cand, off)

    return _key_to_f32(key_lo + lax.fori_loop(0, n_bits, body, jnp.zeros_like(key_lo)))


def _strict_triangle(n, lower):
    r = lax.broadcasted_iota(jnp.int32, (n, n), 0)
    c = lax.broadcasted_iota(jnp.int32, (n, n), 1)
    return jnp.where((r > c) if lower else (r < c), 1.0, 0.0).astype(BF16)


def _take_first_ties(eq, need_f, tied, axis):
    n = eq.shape[axis]
    cshape = (1, eq.shape[1]) if axis == 0 else (eq.shape[0], 1)

    def ranked():
        tri = _strict_triangle(LANES, lower=(axis == 0))
        carry = jnp.zeros(cshape, F32)
        parts = []
        for c in range(n // LANES):
            sl = slice(c * LANES, (c + 1) * LANES)
            eq_f = jnp.where(eq[sl, :] if axis == 0 else eq[:, sl], 1.0, 0.0)
            eq_b = eq_f.astype(BF16)
            rank = (_dot(tri, eq_b) if axis == 0 else _dot(eq_b, tri)) + carry
            parts.append(jnp.where(rank < need_f, eq_f, 0.0))
            carry = carry + jnp.sum(eq_f, axis=axis, keepdims=True)
        return jnp.concatenate(parts, axis=axis), carry

    def plain():
        return jnp.where(eq, 1.0, 0.0), jnp.zeros(cshape, F32)

    return lax.cond(jnp.max(jnp.where(tied, 1, 0)) > 0, ranked, plain)


def _dsa_prompt_kernel(q_ref, qi_ref, wi_ref, kk_ref, kiki_ref, vvt_ref, o_ref, sm_ref, bias_ref, *,
                       topk, first_block, select):
    i = first_block + pl.program_id(1)
    qb = q_ref.shape[0]
    lk = kk_ref.shape[1]
    ck = min(DSA_KCHUNK, lk)
    chunks = [slice(c * ck, (c + 1) * ck) for c in range(lk // ck)]
    upper_rows = lax.broadcasted_iota(jnp.int32, (LANES, qb), 0) < HEAD_DIM

    def head_operand(ref, h):
        pair_t = jnp.transpose(ref[:, (h // 2) * LANES:(h // 2 + 1) * LANES].astype(F32))
        keep = upper_rows if h % 2 == 0 else jnp.logical_not(upper_rows)
        return jnp.where(keep, pair_t, 0.0).astype(BF16)

    pos_q = i * qb + lax.broadcasted_iota(jnp.int32, (1, qb), 1)

    def causal(sl):
        return sl.start + lax.broadcasted_iota(jnp.int32, (ck, 1), 0) <= pos_q

    if select:
        w_t = jnp.transpose(wi_ref[...])[0:IDX_HEADS, :] * (IDX_HEADS ** -0.5)
        qi_ops = [head_operand(qi_ref, h) for h in range(IDX_HEADS)]
        for sl in chunks:
            keys = kiki_ref[0, sl, :]
            score = jnp.zeros((ck, qb), F32)
            for h in range(IDX_HEADS):
                score = score + jnp.maximum(_dot(keys, qi_ops[h]), 0.0) * w_t[h:h + 1, :]
            sm_ref[sl, :] = jnp.where(causal(sl), score, -jnp.inf)

        def count_ge(cand):
            return jnp.sum(jnp.where(sm_ref[...] >= cand, 1, 0), axis=0, keepdims=True)

        if ck == topk:
            class_max = sm_ref[chunks[0], :]
            for sl in chunks[1:]:
                class_max = jnp.maximum(class_max, sm_ref[sl, :])
            tau = _kth_largest_between(count_ge, topk, jnp.min(class_max, axis=0, keepdims=True),
                                       jnp.max(class_max, axis=0, keepdims=True))
        else:
            tau = _kth_largest(count_ge, topk, (1, qb))
        sm = sm_ref[...]
        gt = sm > tau
        eq = sm == tau
        n_gt = jnp.sum(jnp.where(gt, 1, 0), axis=0, keepdims=True)
        n_eq = jnp.sum(jnp.where(eq, 1, 0), axis=0, keepdims=True)
        tied = n_gt + n_eq > topk
        need_f = jnp.where(tied, topk - n_gt, lk + 1).astype(F32)
        take_eq, _ = _take_first_ties(eq, need_f, tied, axis=0)
        bias_ref[...] = jnp.where(jnp.logical_or(gt, take_eq > 0.5), 0.0, NEG)

    def chunk_bias(sl):
        if select:
            return bias_ref[sl, :]
        return jnp.where(causal(sl), 0.0, NEG)

    def fold8(x, op):
        return op(x.reshape(ck // 8, 8, qb), axis=0)

    for j in range(ATTN_HEADS // 2):
        outs = []
        for h in (2 * j, 2 * j + 1):
            q_op = head_operand(q_ref, h)
            mx8 = jnp.full((8, qb), -jnp.inf, F32)
            for sl in chunks:
                mx8 = jnp.maximum(mx8, fold8(_dot(kk_ref[0, sl, :], q_op) + chunk_bias(sl), jnp.max))
            m = jnp.max(mx8, axis=0, keepdims=True)
            den8 = jnp.zeros((8, qb), F32)
            acc = jnp.zeros((LANES, qb), F32)
            for sl in chunks:
                p = jnp.exp(_dot(kk_ref[0, sl, :], q_op) + chunk_bias(sl) - m)
                den8 = den8 + fold8(p, jnp.sum)
                acc = acc + _dot(vvt_ref[0, :, sl], p.astype(BF16))
            outs.append(acc * (1.0 / jnp.sum(den8, axis=0, keepdims=True)))
        pair_t = jnp.where(upper_rows, outs[0], outs[1])
        o_ref[0, :, j * LANES:(j + 1) * LANES] = jnp.transpose(pair_t).astype(BF16)


def _dsa_prompt_group(q, qi, wi, kk, kiki, vvt, batch, t_len, first_block, n_blocks, topk):
    nb = t_len // DSA_Q
    lk = (first_block + n_blocks) * DSA_Q
    blk = lambda b, j: (b * nb + first_block + j, 0)
    return pl.pallas_call(
        functools.partial(_dsa_prompt_kernel, topk=topk, first_block=first_block, select=lk > topk),
        grid=(batch, n_blocks),
        in_specs=[pl.BlockSpec((DSA_Q, 512), blk),
                  pl.BlockSpec((DSA_Q, 512), blk),
                  pl.BlockSpec((DSA_Q, LANES), blk),
                  pl.BlockSpec((1, lk, LANES), lambda b, j: (b, 0, 0)),
                  pl.BlockSpec((1, lk, LANES), lambda b, j: (b, 0, 0)),
                  pl.BlockSpec((1, LANES, lk), lambda b, j: (b, 0, 0))],
        out_specs=pl.BlockSpec((1, DSA_Q, 512), lambda b, j: (b, j, 0)),
        out_shape=jax.ShapeDtypeStruct((batch, n_blocks * DSA_Q, 512), BF16),
        scratch_shapes=[pltpu.VMEM((lk, DSA_Q), F32), pltpu.VMEM((lk, DSA_Q), F32)],
        compiler_params=_params("parallel", "parallel"),
        name="dsa_prompt_%d" % lk,
    )(q, qi, wi, kk, kiki, vvt)


def _dsa_prompt(q, qi, wi, kk, kiki, vvt, batch, t_len):
    nb = t_len // DSA_Q
    topk = min(TOPK_MAX, t_len // 4)
    assert topk % DSA_Q == 0 and t_len % DSA_Q == 0
    kk, kiki = (a.reshape(batch, t_len, LANES) for a in (kk, kiki))
    edges = [0] + list(range(topk // DSA_Q, nb + 1))
    parts = [_dsa_prompt_group(q, qi, wi, kk, kiki, vvt, batch, t_len, a, b - a, topk)
             for a, b in zip(edges[:-1], edges[1:]) if b > a]
    return jnp.concatenate(parts, axis=1).reshape(batch * t_len, 512)


def _mem_kv_kernel(m_ref, w_ref, k_ref, v_ref):
    z = _dot(m_ref[...].astype(BF16), w_ref[...])
    k_ref[...] = z[:, :MEM_WIDTH]
    v_ref[...] = z[:, MEM_WIDTH:]


def _mem_kv(mem, w_kv):
    n = mem.shape[0]
    tm = min(n, 512)
    row = lambda i: (i, 0)
    return pl.pallas_call(
        _mem_kv_kernel,
        grid=(n // tm,),
        in_specs=[pl.BlockSpec((tm, D_MODEL), row),
                  pl.BlockSpec((D_MODEL, 2 * MEM_WIDTH), lambda i: (0, 0))],
        out_specs=[pl.BlockSpec((tm, MEM_WIDTH), row), pl.BlockSpec((tm, MEM_WIDTH), row)],
        out_shape=[jax.ShapeDtypeStruct((n, MEM_WIDTH), F32)] * 2,
        compiler_params=_params("parallel"),
        name="mem_kv",
    )(mem, w_kv)


def _pool_project(window_sums, u, cnts, pw_ref, ps_ref):
    ys = []
    for g in range(len(POOL_WINDOWS)):
        sl = slice(g * POOL_CG, (g + 1) * POOL_CG)
        pooled = window_sums[g] / cnts[g] - u[:, sl]
        ys.append(_dot(pooled.astype(BF16), pw_ref[g]) * ps_ref[:, sl])
    return jnp.concatenate(ys, axis=1)


def _mid_prompt_kernel(x_ref, attn_ref, u_ref, halo_ref, mk_ref, mv_ref, pw_ref, ps_ref, wo_ref,
                       gm_ref, wmq_ref, wmo_ref, gf_ref, h2_ref, tok_ref, e_ref, *, nt, n_real):
    step = pl.program_id(0)

    @pl.when(step >= n_real)
    def _():
        h2_ref[...] = jnp.zeros_like(h2_ref)
        tok_ref[...] = jnp.zeros_like(tok_ref)

    @pl.when(step < n_real)
    def _():
        _mid_prompt_rows(x_ref, attn_ref, u_ref, halo_ref, mk_ref, mv_ref, pw_ref, ps_ref, wo_ref,
                         gm_ref, wmq_ref, wmo_ref, gf_ref, h2_ref, tok_ref, e_ref, step % nt)


def _mid_prompt_rows(x_ref, attn_ref, u_ref, halo_ref, mk_ref, mv_ref, pw_ref, ps_ref, wo_ref,
                     gm_ref, wmq_ref, wmo_ref, gf_ref, h2_ref, tok_ref, e_ref, i):
    tm = x_ref.shape[0]
    u = u_ref[...]
    halo = halo_ref[...]
    e_ref[0:POOL_MAX, :] = jnp.where(i == 0, jnp.zeros_like(halo), halo)
    e_ref[POOL_MAX:POOL_MAX + tm, :] = u
    pos = i * tm + lax.broadcasted_iota(jnp.int32, (tm, 1), 0)
    sums, cnts = [], []
    for g, w in enumerate(POOL_WINDOWS):
        sl = slice(g * POOL_CG, (g + 1) * POOL_CG)
        s = u[:, sl]
        for j in range(1, w):
            s = s + e_ref[POOL_MAX - j:POOL_MAX - j + tm, sl]
        sums.append(s)
        cnts.append(jnp.minimum(w, pos + 1).astype(F32))
    pool = _pool_project(sums, u, cnts, pw_ref, ps_ref)
    h1 = x_ref[...] + _dot(attn_ref[...], wo_ref[0:ATTN_WIDTH, :]) \
        + _dot(pool.astype(BF16), wo_ref[ATTN_WIDTH:, :])

    qm = _dot(_rms(h1, gm_ref[...]).astype(BF16), wmq_ref[...]).astype(BF16)
    mk = mk_ref[0].astype(BF16)
    mv = mv_ref[0].astype(BF16)
    outs = []
    for h in range(MEM_HEADS):
        sl = slice(h * MEM_HEAD_DIM, (h + 1) * MEM_HEAD_DIM)
        logits = _dot_nt(qm[:, sl], mk[:, sl]) * (MEM_HEAD_DIM ** -0.5)
        m = jnp.max(logits, axis=1, keepdims=True)
        p = jnp.exp(logits - m)
        p = p / jnp.sum(p, axis=1, keepdims=True)
        outs.append(_dot(p.astype(BF16), mv[:, sl]))
    o = jnp.concatenate(outs, axis=1)
    h2 = h1 + _dot(o.astype(BF16), wmo_ref[...])
    h2_ref[...] = h2
    tok_ref[...] = _rms(h2, gf_ref[...]).astype(BF16)


def _mid_prompt(x, attn, u, mk, mv, pool_w, pool_scale, w_o, g_mem, w_mq, w_mo, g_ffn, batch, t_len, tm, n_rows):
    nt = t_len // tm
    hb = tm // POOL_MAX
    n_real = batch * nt
    n_steps = -(-n_rows // tm)
    last = n_real - 1
    row = lambda s: (jnp.minimum(s, last), 0)
    halo = lambda s: (jnp.maximum(jnp.minimum(s, last) * hb - 1, 0), 0)
    fixed2 = lambda s: (0, 0)
    fixed3 = lambda s: (0, 0, 0)
    perb = lambda s: (jnp.minimum(s, last) // nt, 0, 0)
    out_row = lambda s: (s, 0)
    n = n_rows
    return pl.pallas_call(
        functools.partial(_mid_prompt_kernel, nt=nt, n_real=n_real),
        grid=(n_steps,),
        in_specs=[pl.BlockSpec((tm, D_MODEL), row),
                  pl.BlockSpec((tm, ATTN_WIDTH), row),
                  pl.BlockSpec((tm, POOL_WIDTH), row),
                  pl.BlockSpec((POOL_MAX, POOL_WIDTH), halo),
                  pl.BlockSpec((1, N_MEM, MEM_WIDTH), perb),
                  pl.BlockSpec((1, N_MEM, MEM_WIDTH), perb),
                  pl.BlockSpec((len(POOL_WINDOWS), POOL_CG, POOL_CG), fixed3),
                  pl.BlockSpec((1, POOL_WIDTH), fixed2),
                  pl.BlockSpec((D_MODEL, D_MODEL), fixed2),
                  pl.BlockSpec((1, D_MODEL), fixed2),
                  pl.BlockSpec((D_MODEL, MEM_WIDTH), fixed2),
                  pl.BlockSpec((MEM_WIDTH, D_MODEL), fixed2),
                  pl.BlockSpec((1, D_MODEL), fixed2)],
        out_specs=[pl.BlockSpec((tm, D_MODEL), out_row), pl.BlockSpec((tm, D_MODEL), out_row)],
        out_shape=[jax.ShapeDtypeStruct((n, D_MODEL), F32), jax.ShapeDtypeStruct((n, D_MODEL), BF16)],
        scratch_shapes=[pltpu.VMEM((POOL_MAX + tm, POOL_WIDTH), F32)],
        compiler_params=_params("arbitrary"),
        name="mid_prompt",
    )(x, attn, u, u, mk, mv, pool_w, pool_scale, w_o, g_mem, w_mq, w_mo, g_ffn)


def _router_kernel(tok_ref, rw_ref, rb_ref, rank_ref, gate_ref, cnt_ref, *, n_valid):
    tm = tok_ref.shape[0]
    token = pl.program_id(0) * tm + lax.broadcasted_iota(jnp.int32, (1, tm), 1)
    valid = token < n_valid
    logits = _dot_nt(rw_ref[...], tok_ref[...]) + rb_ref[...]
    eid = lax.broadcasted_iota(jnp.int32, (N_EXPERTS, tm), 0)
    vals, hots = [], []
    cur = logits
    for _ in range(TOP_K):
        m = jnp.max(cur, axis=0, keepdims=True)
        idx = jnp.min(jnp.where(cur == m, eid, N_EXPERTS), axis=0, keepdims=True)
        hot = eid == idx
        vals.append(m)
        hots.append(hot)
        cur = jnp.where(hot, -jnp.inf, cur)
    exps = [jnp.exp(v - vals[0]) for v in vals]
    den = exps[0] + exps[1] + exps[2] + exps[3]
    gate = jnp.zeros((N_EXPERTS, tm), F32)
    sel = jnp.zeros((N_EXPERTS, tm), jnp.bool_)
    for hot, ex in zip(hots, exps):
        gate = jnp.where(hot, ex / den, gate)
        sel = jnp.logical_or(sel, hot)
    sel = jnp.logical_and(sel, valid)
    gate_ref[0] = gate

    upper = _strict_triangle(LANES, lower=False)
    carry = jnp.zeros((N_EXPERTS, 1), F32)
    for c in range(tm // LANES):
        sl = slice(c * LANES, (c + 1) * LANES)
        sel_c = sel[:, sl]
        sel_b = jnp.where(sel_c, 1.0, 0.0).astype(BF16)
        rank = _dot(sel_b, upper) + carry
        rank_ref[0, :, sl] = jnp.where(sel_c, rank, -1.0).astype(jnp.int32)
        carry = carry + jnp.sum(sel_b.astype(F32), axis=1, keepdims=True)
    cnt_ref[0] = carry.astype(jnp.int32)


def _router(tok, rw_t, rb, tm, n_valid):
    n = tok.shape[0]
    nt = n // tm
    t3 = lambda i: (i, 0, 0)
    return pl.pallas_call(
        functools.partial(_router_kernel, n_valid=n_valid),
        grid=(nt,),
        in_specs=[pl.BlockSpec((tm, D_MODEL), lambda i: (i, 0)),
                  pl.BlockSpec((N_EXPERTS, D_MODEL), lambda i: (0, 0)),
                  pl.BlockSpec((N_EXPERTS, 1), lambda i: (0, 0))],
        out_specs=[pl.BlockSpec((1, N_EXPERTS, tm), t3),
                   pl.BlockSpec((1, N_EXPERTS, tm), t3),
                   pl.BlockSpec((1, N_EXPERTS, 1), t3)],
        out_shape=[jax.ShapeDtypeStruct((nt, N_EXPERTS, tm), jnp.int32),
                   jax.ShapeDtypeStruct((nt, N_EXPERTS, tm), F32),
                   jax.ShapeDtypeStruct((nt, N_EXPERTS, 1), jnp.int32)],
        compiler_params=_params("parallel"),
        name="router",
    )(tok, rw_t, rb)


def _swiglu(h):
    gate = jnp.minimum(h[:, :D_FF], SWIGLU_LIMIT)
    up = jnp.clip(h[:, D_FF:], -SWIGLU_LIMIT, SWIGLU_LIMIT)
    return gate * jax.nn.sigmoid(SWIGLU_ALPHA * gate) * (up + 1.0)


def _moe_kernel(cnt_ref, x_ref, h2_ref, rank_ref, gate_ref, wu_ref, bu_ref, wd_ref, bd_ref, gfin_ref,
                o_ref, os_ref, *, first_chunk, chunk, sample_row):
    i = pl.program_id(0)
    e = pl.program_id(1)
    tm = x_ref.shape[0]

    @pl.when(e == 0)
    def _():
        o_ref[...] = jnp.zeros_like(o_ref)

    n_routed = cnt_ref[i * N_EXPERTS + e]
    rank = rank_ref[0, pl.ds(e, 1), :]
    gate = gate_ref[0, pl.ds(e, 1), :]

    def run_slots(first, size):
        slot = first + lax.broadcasted_iota(jnp.int32, (size, tm), 0)
        hit = rank == slot
        onehot = jnp.where(hit, 1.0, 0.0).astype(BF16)
        xg = _dot(onehot, x_ref[...]).astype(BF16)
        act = _swiglu(_dot(xg, wu_ref[0]) + bu_ref[0])
        out = _dot(act.astype(BF16), wd_ref[0]) + bd_ref[0]
        g_slot = jnp.sum(jnp.where(hit, gate, 0.0), axis=1, keepdims=True)
        o_ref[...] += _dot_tn(onehot, (out * g_slot).astype(BF16))

    sizes = [s for s in MOE_FIRST_SIZES if s < first_chunk] + [first_chunk]
    for lo, size in zip([0] + sizes[:-1], sizes):
        fits = n_routed <= size if size < first_chunk else n_routed > lo
        @pl.when(jnp.logical_and(n_routed > lo, fits))
        def _():
            run_slots(0, size)

    def body(c, carry):
        run_slots(first_chunk + c * chunk, chunk)
        return carry

    lax.fori_loop(0, (jnp.maximum(n_routed - first_chunk, 0) + chunk - 1) // chunk, body, 0)

    @pl.when(e == pl.num_programs(1) - 1)
    def _():
        o_ref[...] = _rms(h2_ref[...] + o_ref[...], gfin_ref[...])

        @pl.when(i == pl.num_programs(0) - 1)
        def _():
            os_ref[...] = o_ref[sample_row:sample_row + os_ref.shape[0], :]


def _moe(tok, h2, rank, gate, cnt, wu, bu, wd, bd, g_final, tm, n_prompt, first_chunk, chunk):
    nt = tok.shape[0] // tm
    tile = lambda i, e, c: (i, 0)
    tile3 = lambda i, e, c: (i, 0, 0)
    expert = lambda i, e, c: (e, 0, 0)
    grid_spec = pltpu.PrefetchScalarGridSpec(
        num_scalar_prefetch=1,
        grid=(nt, N_EXPERTS),
        in_specs=[pl.BlockSpec((tm, D_MODEL), tile),
                  pl.BlockSpec((tm, D_MODEL), tile),
                  pl.BlockSpec((1, N_EXPERTS, tm), tile3),
                  pl.BlockSpec((1, N_EXPERTS, tm), tile3),
                  pl.BlockSpec((1, D_MODEL, 2 * D_FF), expert),
                  pl.BlockSpec((1, 1, 2 * D_FF), expert),
                  pl.BlockSpec((1, D_FF, D_MODEL), expert),
                  pl.BlockSpec((1, 1, D_MODEL), expert),
                  pl.BlockSpec((1, D_MODEL), lambda i, e, c: (0, 0))],
        out_specs=[pl.BlockSpec((tm, D_MODEL), tile),
                   pl.BlockSpec((SAMPLE_ROWS, D_MODEL), lambda i, e, c: (0, 0))],
    )
    return pl.pallas_call(
        functools.partial(_moe_kernel, first_chunk=first_chunk, chunk=chunk,
                          sample_row=n_prompt - (nt - 1) * tm),
        grid_spec=grid_spec,
        out_shape=[jax.ShapeDtypeStruct((n_prompt, D_MODEL), F32),
                   jax.ShapeDtypeStruct((SAMPLE_ROWS, D_MODEL), F32)],
        compiler_params=_params("arbitrary", "arbitrary"),
        name="moe",
    )(cnt, tok, h2, rank, gate, wu, bu, wd, bd, g_final)


def _page_copy(cache_ref, buf_ref, sem_ref, pt_ref, b, p, slot):
    dst = buf_ref.at[slot, :, pl.ds(pl.multiple_of(p * PAGE_SIZE, PAGE_SIZE), PAGE_SIZE)]
    return pltpu.make_async_copy(cache_ref.at[0, pt_ref[b, p]], dst, sem_ref.at[slot])


def _fetch_pages(cache_ref, buf_ref, sem_ref, pt_ref, b, slot, n_pages):
    def body(p, c):
        _page_copy(cache_ref, buf_ref, sem_ref, pt_ref, b, p, slot).start()
        return c
    lax.fori_loop(0, n_pages, body, 0, unroll=8)


def _wait_pages(cache_ref, buf_ref, sem_ref, pt_ref, b, slot, n_pages):
    def body(p, c):
        _page_copy(cache_ref, buf_ref, sem_ref, pt_ref, b, p, slot).wait()
        return c
    lax.fori_loop(0, n_pages, body, 0, unroll=8)


def _dsa_sample_score_kernel(pt_ref, qi_ref, wi_ref, kin_ref, cache_ref, score_ref, new_ref, buf_ref, sem_ref):
    b = pl.program_id(0)
    nb = pl.num_programs(0)
    past = buf_ref.shape[2]
    n_pages = past // PAGE_SIZE
    slot = b % 2

    @pl.when(b == 0)
    def _():
        _fetch_pages(cache_ref, buf_ref, sem_ref, pt_ref, b, slot, n_pages)

    @pl.when(b + 1 < nb)
    def _():
        _fetch_pages(cache_ref, buf_ref, sem_ref, pt_ref, b + 1, 1 - slot, n_pages)

    _wait_pages(cache_ref, buf_ref, sem_ref, pt_ref, b, slot, n_pages)

    qi = qi_ref[0]
    w = wi_ref[0] * (IDX_HEADS ** -0.5)
    keys_t = buf_ref[slot].astype(BF16)
    score_ref[0] = jnp.sum(jnp.maximum(_dot(qi, keys_t), 0.0) * w, axis=0, keepdims=True)
    s_new = jnp.sum(qi.astype(F32) * kin_ref[0].astype(F32), axis=1, keepdims=True)
    new_ref[0] = jnp.sum(jnp.maximum(s_new, 0.0) * w, axis=0, keepdims=True)


def _dsa_sample_select_kernel(score_ref, new_ref, mask_ref, take_new_ref, *, topk):
    score = score_ref[...]
    score_new = new_ref[...]
    past = score.shape[1]

    def count_ge(cand):
        return (jnp.sum(jnp.where(score >= cand, 1, 0), axis=1, keepdims=True)
                + jnp.where(score_new >= cand, 1, 0))

    tau = _kth_largest(count_ge, topk, score_new.shape)
    gt = score > tau
    eq = score == tau
    n_gt = jnp.sum(jnp.where(gt, 1, 0), axis=1, keepdims=True) + jnp.where(score_new > tau, 1, 0)
    n_eq_all = jnp.sum(jnp.where(eq, 1, 0), axis=1, keepdims=True) + jnp.where(score_new == tau, 1, 0)
    tied = n_gt + n_eq_all > topk
    need_f = jnp.where(tied, topk - n_gt, past + 2).astype(F32)
    take_eq, n_eq = _take_first_ties(eq, need_f, tied, axis=1)
    mask_ref[...] = jnp.where(jnp.logical_or(gt, take_eq > 0.5), 0.0, NEG)
    take_new = jnp.logical_or(score_new > tau, jnp.logical_and(score_new == tau, n_eq < need_f))
    take_new_ref[...] = jnp.where(take_new, 0.0, NEG)


def _dsa_sample_select(score, score_new, topk):
    db, past = score.shape
    full = lambda shape: pl.BlockSpec(shape, lambda i: (0, 0))
    return pl.pallas_call(
        functools.partial(_dsa_sample_select_kernel, topk=topk),
        grid=(1,),
        in_specs=[full((db, past)), full((db, 1))],
        out_specs=[full((db, past)), full((db, 1))],
        out_shape=[jax.ShapeDtypeStruct((db, past), F32), jax.ShapeDtypeStruct((db, 1), F32)],
        compiler_params=_params("arbitrary"),
        name="dsa_sample_select",
    )(score, score_new)


def _dsa_sample_mask(page_table, qi_heads, wi_col, ki_new, cache_idx_k):
    db, n_pages = page_table.shape
    past = n_pages * PAGE_SIZE
    topk = min(TOPK_MAX, (past + 1) // 4)
    per = lambda b, pt: (b, 0, 0)
    grid_spec = pltpu.PrefetchScalarGridSpec(
        num_scalar_prefetch=1,
        grid=(db,),
        in_specs=[pl.BlockSpec((1, IDX_HEADS, IDX_DIM), per),
                  pl.BlockSpec((1, IDX_HEADS, 1), per),
                  pl.BlockSpec((1, 1, IDX_DIM), per),
                  pl.BlockSpec(memory_space=pl.ANY)],
        out_specs=[pl.BlockSpec((1, 1, past), per), pl.BlockSpec((1, 1, 1), per)],
        scratch_shapes=[pltpu.VMEM((2, IDX_DIM, past), F32),
                        pltpu.SemaphoreType.DMA((2,))],
    )
    score, score_new = pl.pallas_call(
        _dsa_sample_score_kernel,
        grid_spec=grid_spec,
        out_shape=[jax.ShapeDtypeStruct((db, 1, past), F32), jax.ShapeDtypeStruct((db, 1, 1), F32)],
        compiler_params=_params("arbitrary"),
        name="dsa_sample_score",
    )(page_table, qi_heads, wi_col, ki_new, cache_idx_k)
    mask, take_new = _dsa_sample_select(score.reshape(db, past), score_new.reshape(db, 1), topk)
    return mask.reshape(db, 1, past), take_new.reshape(db, 1, 1)


def _dsa_sample_attend_kernel(pt_ref, q_ref, kn_ref, vn_ref, mask_ref, new_ref, ck_ref, cv_ref, o_ref,
                              kbuf_ref, vbuf_ref, ksem_ref, vsem_ref):
    b = pl.program_id(0)
    nb = pl.num_programs(0)
    past = kbuf_ref.shape[2]
    n_pages = past // PAGE_SIZE
    slot = b % 2

    def fetch(bb, sl):
        _fetch_pages(ck_ref, kbuf_ref, ksem_ref, pt_ref, bb, sl, n_pages)
        _fetch_pages(cv_ref, vbuf_ref, vsem_ref, pt_ref, bb, sl, n_pages)

    @pl.when(b == 0)
    def _():
        fetch(b, slot)

    @pl.when(b + 1 < nb)
    def _():
        fetch(b + 1, 1 - slot)

    _wait_pages(ck_ref, kbuf_ref, ksem_ref, pt_ref, b, slot, n_pages)
    _wait_pages(cv_ref, vbuf_ref, vsem_ref, pt_ref, b, slot, n_pages)

    q = q_ref[0]
    keys_t = kbuf_ref[slot].astype(BF16)
    vals_t = vbuf_ref[slot].astype(BF16)
    logits = _dot(q, keys_t) + mask_ref[0]
    l_new = jnp.sum(q.astype(F32) * kn_ref[0].astype(F32), axis=1, keepdims=True) + new_ref[0]
    m = jnp.maximum(jnp.max(logits, axis=1, keepdims=True), l_new)
    p = jnp.exp(logits - m)
    p_new = jnp.exp(l_new - m)
    den = jnp.sum(p, axis=1, keepdims=True) + p_new
    p = p / den
    p_new = p_new / den
    o_ref[0] = _dot_nt(p.astype(BF16), vals_t) + p_new * vn_ref[0].astype(F32)


def _dsa_sample_attend(page_table, q_heads, k_new, v_new, mask, take_new, cache_k, cache_v):
    db, n_pages = page_table.shape
    past = n_pages * PAGE_SIZE
    per = lambda b, pt: (b, 0, 0)
    grid_spec = pltpu.PrefetchScalarGridSpec(
        num_scalar_prefetch=1,
        grid=(db,),
        in_specs=[pl.BlockSpec((1, ATTN_HEADS, HEAD_DIM), per),
                  pl.BlockSpec((1, 1, HEAD_DIM), per),
                  pl.BlockSpec((1, 1, HEAD_DIM), per),
                  pl.BlockSpec((1, 1, past), per),
                  pl.BlockSpec((1, 1, 1), per),
                  pl.BlockSpec(memory_space=pl.ANY),
                  pl.BlockSpec(memory_space=pl.ANY)],
        out_specs=pl.BlockSpec((1, ATTN_HEADS, HEAD_DIM), per),
        scratch_shapes=[pltpu.VMEM((2, HEAD_DIM, past), F32),
                        pltpu.VMEM((2, HEAD_DIM, past), F32),
                        pltpu.SemaphoreType.DMA((2,)),
                        pltpu.SemaphoreType.DMA((2,))],
    )
    return pl.pallas_call(
        _dsa_sample_attend_kernel,
        grid_spec=grid_spec,
        out_shape=jax.ShapeDtypeStruct((db, ATTN_HEADS, HEAD_DIM), F32),
        compiler_params=_params("arbitrary"),
        name="dsa_sample_attend",
    )(page_table, q_heads, k_new, v_new, mask, take_new, cache_k, cache_v)


def _mid_sample_a_kernel(x_ref, attn_ref, u_ref, st_ref, pw_ref, ps_ref, wo_ref, gm_ref, wmq_ref,
                         h1_ref, qm_ref, *, past):
    u = u_ref[...]
    rows = u.shape[0]
    sums, cnts = [], []
    for g, w in enumerate(POOL_WINDOWS):
        sl = slice(g * POOL_CG, (g + 1) * POOL_CG)
        s = u[:, sl]
        for j in range(1, w):
            s = s + st_ref[POOL_MAX - 1 - j][:, sl]
        sums.append(s)
        cnts.append(jnp.full((rows, 1), float(min(w, past + 1)), F32))
    pool = _pool_project(sums, u, cnts, pw_ref, ps_ref)
    h1 = x_ref[...] + _dot(attn_ref[...], wo_ref[0:ATTN_WIDTH, :]) \
        + _dot(pool.astype(BF16), wo_ref[ATTN_WIDTH:, :])
    h1_ref[...] = h1
    qm_ref[...] = _dot(_rms(h1, gm_ref[...]).astype(BF16), wmq_ref[...]).astype(BF16)


def _mid_sample_a(x, attn, u, state_t, pool_w, pool_scale, w_o, g_mem, w_mq, past):
    db = x.shape[0]
    full = lambda a: pl.BlockSpec(a.shape, lambda i, nd=a.ndim: (0,) * nd)
    args = (x, attn, u, state_t, pool_w, pool_scale, w_o, g_mem, w_mq)
    return pl.pallas_call(
        functools.partial(_mid_sample_a_kernel, past=past),
        grid=(1,),
        in_specs=[full(a) for a in args],
        out_specs=[pl.BlockSpec((db, D_MODEL), lambda i: (0, 0)),
                   pl.BlockSpec((db, MEM_WIDTH), lambda i: (0, 0))],
        out_shape=[jax.ShapeDtypeStruct((db, D_MODEL), F32), jax.ShapeDtypeStruct((db, MEM_WIDTH), BF16)],
        compiler_params=_params("arbitrary"),
        name="mid_sample_a",
    )(*args)


def _mem_sample_kernel(q_ref, mk_ref, mv_ref, o_ref):
    rows = 2 * MEM_HEADS
    n = mk_ref.shape[1]
    q_all = q_ref[0].astype(F32)
    q = jnp.concatenate([q_all[:, (r % MEM_HEADS) * MEM_HEAD_DIM:(r % MEM_HEADS + 1) * MEM_HEAD_DIM]
                         for r in range(rows)], axis=0).astype(BF16)
    logits = _dot_nt(q, mk_ref[0].astype(BF16)) * (MEM_HEAD_DIM ** -0.5)
    own = (lax.broadcasted_iota(jnp.int32, (rows, n), 1) % MEM_HEADS
           == lax.broadcasted_iota(jnp.int32, (rows, n), 0) % MEM_HEADS)
    logits = jnp.where(own, logits, -jnp.inf)
    m = jnp.max(logits, axis=1, keepdims=True)
    p = jnp.exp(logits - m)
    p = p / jnp.sum(p, axis=1, keepdims=True)
    o = _dot(p.astype(BF16), mv_ref[0].astype(BF16))
    for h in range(MEM_HEADS):
        o_ref[0, :, h * MEM_HEAD_DIM:(h + 1) * MEM_HEAD_DIM] = o[h:h + 1, :]


def _mem_sample(qm, mk, mv):
    db = qm.shape[0]
    per = lambda b: (b, 0, 0)
    return pl.pallas_call(
        _mem_sample_kernel,
        grid=(db,),
        in_specs=[pl.BlockSpec((1, 1, MEM_WIDTH), per),
                  pl.BlockSpec((1, N_MEM * MEM_HEADS, MEM_HEAD_DIM), per),
                  pl.BlockSpec((1, N_MEM * MEM_HEADS, MEM_HEAD_DIM), per)],
        out_specs=pl.BlockSpec((1, 1, MEM_WIDTH), per),
        out_shape=jax.ShapeDtypeStruct((db, 1, MEM_WIDTH), F32),
        compiler_params=_params("parallel"),
        name="mem_sample",
    )(qm.reshape(db, 1, MEM_WIDTH), mk, mv)


def _sample_join_kernel(h1_ref, o_ref, wmo_ref, gf_ref, h2_any, tok_any, h2_ref, tok_ref):
    del h2_any, tok_any
    db = h1_ref.shape[0]
    h2_ref[...] = jnp.zeros_like(h2_ref)
    tok_ref[...] = jnp.zeros_like(tok_ref)

    @pl.when(pl.program_id(0) == 0)
    def _():
        h2 = h1_ref[...] + _dot(o_ref[...].astype(BF16), wmo_ref[...])
        h2_ref[0:db, :] = h2
        tok_ref[0:db, :] = _rms(h2, gf_ref[...]).astype(BF16)


def _sample_join(h1, o, w_mo, g_ffn, h2_all, tok_all, n_prompt):
    db = h1.shape[0]
    n_rows = h2_all.shape[0]
    assert db <= SAMPLE_ROWS and n_prompt % SAMPLE_ROWS == 0 and n_rows % SAMPLE_ROWS == 0
    first = n_prompt // SAMPLE_ROWS
    f2 = lambda j: (0, 0)
    tail = lambda j: (first + j, 0)
    return pl.pallas_call(
        _sample_join_kernel,
        grid=((n_rows - n_prompt) // SAMPLE_ROWS,),
        in_specs=[pl.BlockSpec((db, D_MODEL), f2),
                  pl.BlockSpec((db, MEM_WIDTH), f2),
                  pl.BlockSpec((MEM_WIDTH, D_MODEL), f2),
                  pl.BlockSpec((1, D_MODEL), f2),
                  pl.BlockSpec(memory_space=pl.ANY),
                  pl.BlockSpec(memory_space=pl.ANY)],
        out_specs=[pl.BlockSpec((SAMPLE_ROWS, D_MODEL), tail), pl.BlockSpec((SAMPLE_ROWS, D_MODEL), tail)],
        out_shape=[jax.ShapeDtypeStruct(h2_all.shape, F32), jax.ShapeDtypeStruct(tok_all.shape, BF16)],
        input_output_aliases={4: 0, 5: 1},
        compiler_params=_params("arbitrary"),
        name="sample_join",
    )(h1, o, w_mo, g_ffn, h2_all, tok_all)


def _rope_tables(pos):
    half = HEAD_DIM // 2
    inv = ROPE_THETA ** (-jnp.arange(half, dtype=F32) / half)
    ang = pos.astype(F32)[:, None] * inv[None, :]
    cos = jnp.cos(ang)
    sin = jnp.sin(ang)
    return jnp.tile(cos, (1, 4)), jnp.concatenate([-sin, sin, -sin, sin], axis=1)


def _pick_tile(n, pref):
    t = min(n, pref)
    while n % t:
        t //= 2
    return t


def kernel(x_prompt, x_sample, mem_prompt, cache_k, cache_v, cache_idx_k, state_pool, cache_mem_k, cache_mem_v, page_table, norm_mix, w_in, pool_w, pool_scale, w_o, norm_mem, w_mq, w_mk, w_mv, w_mo, norm_ffn, router_w, router_b, w_up, b_up, w_down, b_down, norm_final):
    B, T = x_prompt.shape[:2]
    DB, DS = x_sample.shape[:2]
    assert DS == 1 and w_in.shape[0] == 1
    n_pages = page_table.shape[1]
    past = n_pages * PAGE_SIZE
    NP = B * T

    wi_ = w_in[0]
    wq, wk, wv, wu_, wqi, wki, wwi = jnp.split(wi_, [512, 576, 640, 1152, 1664, 1728], axis=1)
    w_all = jnp.concatenate(
        [wq, wqi, wu_, wk, wk, wki, wki, wv, wv, wwi, jnp.zeros((D_MODEL, LANES - IDX_HEADS), F32)],
        axis=1).astype(BF16)
    g_mix = norm_mix[0][None, :]
    g_mem = norm_mem[0][None, :]
    g_ffn = norm_ffn[0][None, :]
    g_fin = norm_final[None, :]
    pw = pool_w[0].astype(BF16)
    ps = pool_scale[0][None, :]
    wo = w_o[0].astype(BF16)
    wmq = w_mq[0].reshape(D_MODEL, MEM_WIDTH).astype(BF16)
    wkv = jnp.concatenate([w_mk[0].reshape(D_MODEL, MEM_WIDTH), w_mv[0].reshape(D_MODEL, MEM_WIDTH)],
                          axis=1).astype(BF16)
    wmo = w_mo[0].reshape(MEM_WIDTH, D_MODEL).astype(BF16)
    rw = router_w[0].astype(BF16)
    rb = router_b[0]
    wup = w_up[0].astype(BF16)
    wdn = w_down[0].astype(BF16)
    bup = b_up[0][:, None, :]
    bdn = b_down[0][:, None, :]

    xp = x_prompt.reshape(NP, D_MODEL)
    tm_in = _pick_tile(T, 512)
    cos_p, sin_p = _rope_tables(jnp.arange(T, dtype=jnp.int32))
    q, qi, u, kk, kiki, vvt, k_pt, ki_pt, v_pt, wi_p = _in_project(xp, g_mix, w_all, cos_p, sin_p, tm_in,
                                                                   T // tm_in, batch=B)
    attn = _dsa_prompt(q, qi, wi_p, kk, kiki, vvt, B, T)
    mk_p, mv_p = _mem_kv(mem_prompt.reshape(B * N_MEM, D_MODEL), wkv)
    tm_moe = min(MOE_TILE, NP)
    n_tiles = -(-(NP + SAMPLE_ROWS) // tm_moe)
    assert (n_tiles - 1) * tm_moe < NP and NP + SAMPLE_ROWS <= n_tiles * tm_moe
    h2_all, tok_all = _mid_prompt(xp, attn, u, mk_p.reshape(B, N_MEM, MEM_WIDTH), mv_p.reshape(B, N_MEM, MEM_WIDTH),
                                  pw, ps, wo, g_mem, wmq, wmo, g_ffn, B, T, _pick_tile(T, 1024), n_tiles * tm_moe)

    xs = x_sample.reshape(DB, D_MODEL)
    cos_s, sin_s = _rope_tables(jnp.full((DB,), past, jnp.int32))
    q_s, qi_s, u_s, kk_s, kiki_s, vv_s, k_s, ki_s, v_s, wi_s = _in_project(xs, g_mix, w_all, cos_s, sin_s, DB, 1)
    mask, take_new = _dsa_sample_mask(
        page_table, qi_s.reshape(DB, IDX_HEADS, IDX_DIM), wi_s[:, :IDX_HEADS, None],
        kiki_s[:, None, :IDX_DIM], jnp.swapaxes(cache_idx_k, 2, 3))
    attn_s = _dsa_sample_attend(
        page_table, q_s.reshape(DB, ATTN_HEADS, HEAD_DIM), kk_s[:, None, :HEAD_DIM], vv_s[:, None, :HEAD_DIM],
        mask, take_new, jnp.swapaxes(cache_k, 2, 3), jnp.swapaxes(cache_v, 2, 3))
    attn_s = attn_s.reshape(DB, ATTN_WIDTH).astype(BF16)
    h1_s, qm_s = _mid_sample_a(xs, attn_s, u_s, jnp.transpose(state_pool[0], (1, 0, 2)), pw, ps, wo, g_mem, wmq, past)
    o_s = _mem_sample(qm_s, cache_mem_k.reshape(DB, N_MEM * MEM_HEADS, MEM_HEAD_DIM),
                      cache_mem_v.reshape(DB, N_MEM * MEM_HEADS, MEM_HEAD_DIM))
    h2_all, tok_all = _sample_join(h1_s, o_s.reshape(DB, MEM_WIDTH), wmo, g_ffn, h2_all, tok_all, NP)

    rank, gate, cnt = _router(tok_all, rw.T, rb[:, None], tm_moe, NP + DB)
    y_p, y_s = _moe(tok_all, h2_all, rank, gate, cnt.reshape(-1), wup, bup, wdn, bdn, g_fin, tm_moe, NP, 256, 128)
    y_prompt = y_p.reshape(B, T, D_MODEL)
    y_sample = y_s[:DB].reshape(DB, 1, D_MODEL)

    new_pool_p = u.reshape(B, T, POOL_WIDTH)[:, T - (POOL_MAX - 1):][None]
    new_pool_s = jnp.concatenate([state_pool[0][:, 1:], u_s[:, None, :]], axis=1)[None]
    return (y_prompt, y_sample,
            jnp.swapaxes(k_pt, 1, 2)[None], jnp.swapaxes(v_pt, 1, 2)[None], jnp.swapaxes(ki_pt, 1, 2)[None],
            new_pool_p,
            mk_p.reshape(1, B, N_MEM, MEM_HEADS, MEM_HEAD_DIM), mv_p.reshape(1, B, N_MEM, MEM_HEADS, MEM_HEAD_DIM),
            k_s.reshape(1, DB, 1, HEAD_DIM), v_s.reshape(1, DB, 1, HEAD_DIM), ki_s.reshape(1, DB, 1, IDX_DIM),
            new_pool_s)
```

```python
import functools

import jax
import jax.numpy as jnp
from jax import lax
from jax.experimental import pallas as pl
from jax.experimental.pallas import tpu as pltpu

D_MODEL = 1024
HEAD_DIM = 64
ATTN_WIDTH = 512
ATTN_HEADS = 8
IDX_HEADS = 8
IDX_DIM = 64
TOPK_MAX = 256
Q_BLOCK = 128
PAGE_SIZE = 128
POOL_WIDTH = 512
POOL_WINDOWS = (2, 4, 8, 16)
POOL_CG = 128
POOL_MAX = 16
N_MEM = 256
MEM_HEADS = 4
MEM_HEAD_DIM = 128
MEM_WIDTH = MEM_HEADS * MEM_HEAD_DIM
N_EXPERTS = 32
TOP_K = 4
D_FF = 1024
SWIGLU_LIMIT = 7.0
SWIGLU_ALPHA = 1.702
ROPE_THETA = 10000.0
EPS = 1e-6
NEG = -1e30

LANES = 128
MOE_TILE = 13 * LANES
MOE_FIRST_ROWS = (208, 224, 256)
MOE_TAIL_ROWS = 128
SAMPLE_ROWS = 128
DSA_Q = 256
DSA_KCHUNK = 256
IN_COLS = 3 * 512 + 4 * LANES
VMEM_LIMIT = 56 * 1024 * 1024

BF16 = jnp.bfloat16
F32 = jnp.float32


def _dot(a, b):
    return jnp.dot(a, b, preferred_element_type=F32)


def _dot_nt(a, b):
    return lax.dot_general(a, b, (((1,), (1,)), ((), ())), preferred_element_type=F32)


def _dot_tn(a, b):
    return lax.dot_general(a, b, (((0,), (0,)), ((), ())), preferred_element_type=F32)


def _rms(x, g):
    return x * lax.rsqrt(jnp.mean(x * x, axis=-1, keepdims=True) + EPS) * g


def _params(*sem):
    return pltpu.CompilerParams(dimension_semantics=sem, vmem_limit_bytes=VMEM_LIMIT)


def _in_kernel(x_ref, g_ref, w_ref, cos_ref, sin_ref,
               q_ref, qi_ref, u_ref, kk_ref, kiki_ref, vv_ref, k_ref, ki_ref, v_ref, wi_ref, *, feature_major):
    xn = _rms(x_ref[...], g_ref[...]).astype(BF16)
    z = _dot(xn, w_ref[...])
    cos = cos_ref[...]
    sin = sin_ref[...]
    lane = lax.broadcasted_iota(jnp.int32, cos.shape, 1)
    first_half = (lane % HEAD_DIM) < (HEAD_DIM // 2)

    def rope(zc):
        swapped = jnp.where(first_half,
                            pltpu.roll(zc, LANES - HEAD_DIM // 2, 1),
                            pltpu.roll(zc, HEAD_DIM // 2, 1))
        return zc * cos + swapped * sin

    for c in range(4):
        q_c = rope(z[:, c * LANES:(c + 1) * LANES]) * (HEAD_DIM ** -0.5)
        qi_c = rope(z[:, 512 + c * LANES:512 + (c + 1) * LANES]) * (IDX_DIM ** -0.5)
        q_ref[:, c * LANES:(c + 1) * LANES] = q_c.astype(BF16)
        qi_ref[:, c * LANES:(c + 1) * LANES] = qi_c.astype(BF16)
    u_ref[...] = z[:, 1024:1536]
    kk = rope(z[:, 1536:1664])
    kiki = rope(z[:, 1664:1792])
    vv = z[:, 1792:1920]
    kk_ref[...] = kk.astype(BF16)
    kiki_ref[...] = kiki.astype(BF16)
    wi_ref[...] = z[:, 1920:2048]
    if feature_major:
        kk_t = jnp.transpose(kk)
        kiki_t = jnp.transpose(kiki)
        vv_t = jnp.transpose(vv)
        vv_ref[0] = vv_t.astype(BF16)
        k_ref[0] = kk_t[0:HEAD_DIM, :]
        ki_ref[0] = kiki_t[0:IDX_DIM, :]
        v_ref[0] = vv_t[0:HEAD_DIM, :]
    else:
        vv_ref[...] = vv.astype(BF16)
        k_ref[...] = kk[:, :HEAD_DIM]
        ki_ref[...] = kiki[:, :IDX_DIM]
        v_ref[...] = vv[:, :HEAD_DIM]


def _in_project(x, g, w_all, cos, sin, tm, table_blocks, batch=None):
    n = x.shape[0]
    row = lambda i: (i, 0)
    tab = lambda i: (i % table_blocks, 0)
    fixed = lambda i: (0, 0)
    feature_major = batch is not None
    if feature_major:
        t_len = n // batch
        assert t_len == table_blocks * tm
        fm = lambda feat, dt: jax.ShapeDtypeStruct((batch, feat, t_len), dt)
        fm_spec = lambda feat: pl.BlockSpec((1, feat, tm), lambda i: (i // table_blocks, 0, i % table_blocks))
    else:
        fm = lambda feat, dt: jax.ShapeDtypeStruct((n, feat), dt)
        fm_spec = lambda feat: pl.BlockSpec((tm, feat), row)
    outs = [
        jax.ShapeDtypeStruct((n, 512), BF16),
        jax.ShapeDtypeStruct((n, 512), BF16),
        jax.ShapeDtypeStruct((n, 512), F32),
        jax.ShapeDtypeStruct((n, LANES), BF16),
        jax.ShapeDtypeStruct((n, LANES), BF16),
        fm(LANES, BF16),
        fm(HEAD_DIM, F32),
        fm(IDX_DIM, F32),
        fm(HEAD_DIM, F32),
        jax.ShapeDtypeStruct((n, LANES), F32),
    ]
    out_specs = ([pl.BlockSpec((tm, s.shape[1]), row) for s in outs[:5]]
                 + [fm_spec(LANES), fm_spec(HEAD_DIM), fm_spec(IDX_DIM), fm_spec(HEAD_DIM)]
                 + [pl.BlockSpec((tm, LANES), row)])
    return pl.pallas_call(
        functools.partial(_in_kernel, feature_major=feature_major),
        grid=(n // tm,),
        in_specs=[pl.BlockSpec((tm, D_MODEL), row),
                  pl.BlockSpec((1, D_MODEL), fixed),
                  pl.BlockSpec((D_MODEL, IN_COLS), fixed),
                  pl.BlockSpec((tm, LANES), tab),
                  pl.BlockSpec((tm, LANES), tab)],
        out_specs=out_specs,
        out_shape=outs,
        compiler_params=_params("parallel"),
        name="in_project",
    )(x, g, w_all, cos, sin)


def _f32_to_key(x):
    bits = lax.bitcast_convert_type(x, jnp.int32)
    return jnp.where(bits < 0, bits ^ jnp.int32(0x7FFFFFFF), bits)


def _key_to_f32(key):
    bits = jnp.where(key < 0, key ^ jnp.int32(0x7FFFFFFF), key)
    return lax.bitcast_convert_type(bits, F32)


def _kth_largest(count_ge, k, shape):
    def body(it, tau):
        cand = tau + jnp.left_shift(jnp.int32(1), jnp.int32(31) - it)
        return jnp.where(count_ge(_key_to_f32(cand)) >= k, cand, tau)

    return _key_to_f32(lax.fori_loop(0, 32, body, jnp.full(shape, -2 ** 31, jnp.int32)))


def _kth_largest_between(count_ge, k, lo, hi):
    key_lo = _f32_to_key(lo)
    span = _f32_to_key(hi) - key_lo
    n_bits = jnp.max(32 - lax.clz(span))
    flip = jnp.int32(-2 ** 31)

    def body(it, off):
        cand = off | jnp.left_shift(jnp.int32(1), n_bits - 1 - it)
        ok = jnp.logical_and(count_ge(_key_to_f32(key_lo + cand)) >= k, (cand ^ flip) <= (span ^ flip))
        return jnp.where(ok, cand, off)

    return _key_to_f32(key_lo + lax.fori_loop(0, n_bits, body, jnp.zeros_like(key_lo)))


def _strict_triangle(n, lower):
    r = lax.broadcasted_iota(jnp.int32, (n, n), 0)
    c = lax.broadcasted_iota(jnp.int32, (n, n), 1)
    return jnp.where((r > c) if lower else (r < c), 1.0, 0.0).astype(BF16)


def _take_first_ties(eq, need_f, tied, axis):
    n = eq.shape[axis]
    cshape = (1, eq.shape[1]) if axis == 0 else (eq.shape[0], 1)

    def ranked():
        tri = _strict_triangle(LANES, lower=(axis == 0))
        carry = jnp.zeros(cshape, F32)
        parts = []
        for c in range(n // LANES):
            sl = slice(c * LANES, (c + 1) * LANES)
            eq_f = jnp.where(eq[sl, :] if axis == 0 else eq[:, sl], 1.0, 0.0)
            eq_b = eq_f.astype(BF16)
            rank = (_dot(tri, eq_b) if axis == 0 else _dot(eq_b, tri)) + carry
            parts.append(jnp.where(rank < need_f, eq_f, 0.0))
            carry = carry + jnp.sum(eq_f, axis=axis, keepdims=True)
        return jnp.concatenate(parts, axis=axis), carry

    def plain():
        return jnp.where(eq, 1.0, 0.0), jnp.zeros(cshape, F32)

    return lax.cond(jnp.max(jnp.where(tied, 1, 0)) > 0, ranked, plain)


def _dsa_prompt_kernel(q_ref, qi_ref, wi_ref, kk_ref, kiki_ref, vvt_ref, o_ref, sm_ref, bias_ref, *,
                       topk, first_block, select):
    i = first_block + pl.program_id(1)
    qb = q_ref.shape[0]
    lk = kk_ref.shape[1]
    ck = min(DSA_KCHUNK, lk)
    chunks = [slice(c * ck, (c + 1) * ck) for c in range(lk // ck)]
    upper_rows = lax.broadcasted_iota(jnp.int32, (LANES, qb), 0) < HEAD_DIM

    def head_operand(ref, h):
        pair_t = jnp.transpose(ref[:, (h // 2) * LANES:(h // 2 + 1) * LANES].astype(F32))
        keep = upper_rows if h % 2 == 0 else jnp.logical_not(upper_rows)
        return jnp.where(keep, pair_t, 0.0).astype(BF16)

    pos_q = i * qb + lax.broadcasted_iota(jnp.int32, (1, qb), 1)

    def causal(sl):
        return sl.start + lax.broadcasted_iota(jnp.int32, (ck, 1), 0) <= pos_q

    if select:
        w_t = jnp.transpose(wi_ref[...])[0:IDX_HEADS, :] * (IDX_HEADS ** -0.5)
        qi_ops = [head_operand(qi_ref, h) for h in range(IDX_HEADS)]
        for sl in chunks:
            keys = kiki_ref[0, sl, :]
            score = jnp.zeros((ck, qb), F32)
            for h in range(IDX_HEADS):
                score = score + jnp.maximum(_dot(keys, qi_ops[h]), 0.0) * w_t[h:h + 1, :]
            sm_ref[sl, :] = jnp.where(causal(sl), score, -jnp.inf)

        def count_ge(cand):
            return jnp.sum(jnp.where(sm_ref[...] >= cand, 1, 0), axis=0, keepdims=True)

        if ck == topk:
            class_max = sm_ref[chunks[0], :]
            for sl in chunks[1:]:
                class_max = jnp.maximum(class_max, sm_ref[sl, :])
            tau = _kth_largest_between(count_ge, topk, jnp.min(class_max, axis=0, keepdims=True),
                                       jnp.max(class_max, axis=0, keepdims=True))
        else:
            tau = _kth_largest(count_ge, topk, (1, qb))
        sm = sm_ref[...]
        gt = sm > tau
        eq = sm == tau
        n_gt = jnp.sum(jnp.where(gt, 1, 0), axis=0, keepdims=True)
        n_eq = jnp.sum(jnp.where(eq, 1, 0), axis=0, keepdims=True)
        tied = n_gt + n_eq > topk
        need_f = jnp.where(tied, topk - n_gt, lk + 1).astype(F32)
        take_eq, _ = _take_first_ties(eq, need_f, tied, axis=0)
        bias_ref[...] = jnp.where(jnp.logical_or(gt, take_eq > 0.5), 0.0, NEG)

    def chunk_bias(sl):
        if select:
            return bias_ref[sl, :]
        return jnp.where(causal(sl), 0.0, NEG)

    def fold8(x, op):
        return op(x.reshape(ck // 8, 8, qb), axis=0)

    for j in range(ATTN_HEADS // 2):
        outs = []
        for h in (2 * j, 2 * j + 1):
            q_op = head_operand(q_ref, h)
            mx8 = jnp.full((8, qb), -jnp.inf, F32)
            for sl in chunks:
                logits = _dot(kk_ref[0, sl, :], q_op) + chunk_bias(sl)
                sm_ref[sl, :] = logits
                mx8 = jnp.maximum(mx8, fold8(logits, jnp.max))
            m = jnp.max(mx8, axis=0, keepdims=True)
            den8 = jnp.zeros((8, qb), F32)
            acc = jnp.zeros((LANES, qb), F32)
            for sl in chunks:
                p = jnp.exp(sm_ref[sl, :] - m)
                den8 = den8 + fold8(p, jnp.sum)
                acc = acc + _dot(vvt_ref[0, :, sl], p.astype(BF16))
            outs.append(acc * (1.0 / jnp.sum(den8, axis=0, keepdims=True)))
        pair_t = jnp.where(upper_rows, outs[0], outs[1])
        o_ref[0, :, j * LANES:(j + 1) * LANES] = jnp.transpose(pair_t).astype(BF16)


def _dsa_prompt_group(q, qi, wi, kk, kiki, vvt, batch, t_len, first_block, n_blocks, topk):
    nb = t_len // DSA_Q
    lk = (first_block + n_blocks) * DSA_Q
    blk = lambda b, j: (b * nb + first_block + j, 0)
    return pl.pallas_call(
        functools.partial(_dsa_prompt_kernel, topk=topk, first_block=first_block, select=lk > topk),
        grid=(batch, n_blocks),
        in_specs=[pl.BlockSpec((DSA_Q, 512), blk),
                  pl.BlockSpec((DSA_Q, 512), blk),
                  pl.BlockSpec((DSA_Q, LANES), blk),
                  pl.BlockSpec((1, lk, LANES), lambda b, j: (b, 0, 0)),
                  pl.BlockSpec((1, lk, LANES), lambda b, j: (b, 0, 0)),
                  pl.BlockSpec((1, LANES, lk), lambda b, j: (b, 0, 0))],
        out_specs=pl.BlockSpec((1, DSA_Q, 512), lambda b, j: (b, j, 0)),
        out_shape=jax.ShapeDtypeStruct((batch, n_blocks * DSA_Q, 512), BF16),
        scratch_shapes=[pltpu.VMEM((lk, DSA_Q), F32), pltpu.VMEM((lk, DSA_Q), F32)],
        compiler_params=_params("parallel", "parallel"),
        name="dsa_prompt_%d" % lk,
    )(q, qi, wi, kk, kiki, vvt)


def _dsa_prompt(q, qi, wi, kk, kiki, vvt, batch, t_len):
    nb = t_len // DSA_Q
    topk = min(TOPK_MAX, t_len // 4)
    assert topk % DSA_Q == 0 and t_len % DSA_Q == 0
    kk, kiki = (a.reshape(batch, t_len, LANES) for a in (kk, kiki))
    edges = [0] + list(range(topk // DSA_Q, nb + 1))
    parts = [_dsa_prompt_group(q, qi, wi, kk, kiki, vvt, batch, t_len, a, b - a, topk)
             for a, b in zip(edges[:-1], edges[1:]) if b > a]
    return jnp.concatenate(parts, axis=1).reshape(batch * t_len, 512)


def _mem_kv_kernel(m_ref, w_ref, k_ref, v_ref):
    z = _dot(m_ref[...].astype(BF16), w_ref[...])
    k_ref[...] = z[:, :MEM_WIDTH]
    v_ref[...] = z[:, MEM_WIDTH:]


def _mem_kv(mem, w_kv):
    n = mem.shape[0]
    tm = min(n, 512)
    row = lambda i: (i, 0)
    return pl.pallas_call(
        _mem_kv_kernel,
        grid=(n // tm,),
        in_specs=[pl.BlockSpec((tm, D_MODEL), row),
                  pl.BlockSpec((D_MODEL, 2 * MEM_WIDTH), lambda i: (0, 0))],
        out_specs=[pl.BlockSpec((tm, MEM_WIDTH), row), pl.BlockSpec((tm, MEM_WIDTH), row)],
        out_shape=[jax.ShapeDtypeStruct((n, MEM_WIDTH), F32)] * 2,
        compiler_params=_params("parallel"),
        name="mem_kv",
    )(mem, w_kv)


def _pool_project(window_sums, u, cnts, pw_ref, ps_ref):
    ys = []
    for g in range(len(POOL_WINDOWS)):
        sl = slice(g * POOL_CG, (g + 1) * POOL_CG)
        pooled = window_sums[g] / cnts[g] - u[:, sl]
        ys.append(_dot(pooled.astype(BF16), pw_ref[g]) * ps_ref[:, sl])
    return jnp.concatenate(ys, axis=1)


def _mid_prompt_kernel(x_ref, attn_ref, u_ref, halo_ref, mk_ref, mv_ref, pw_ref, ps_ref, wo_ref,
                       gm_ref, wmq_ref, wmo_ref, gf_ref, h2_ref, tok_ref, e_ref, *, nt, n_real):
    step = pl.program_id(0)

    @pl.when(step >= n_real)
    def _():
        h2_ref[...] = jnp.zeros_like(h2_ref)
        tok_ref[...] = jnp.zeros_like(tok_ref)

    @pl.when(step < n_real)
    def _():
        _mid_prompt_rows(x_ref, attn_ref, u_ref, halo_ref, mk_ref, mv_ref, pw_ref, ps_ref, wo_ref,
                         gm_ref, wmq_ref, wmo_ref, gf_ref, h2_ref, tok_ref, e_ref, step % nt)


def _mid_prompt_rows(x_ref, attn_ref, u_ref, halo_ref, mk_ref, mv_ref, pw_ref, ps_ref, wo_ref,
                     gm_ref, wmq_ref, wmo_ref, gf_ref, h2_ref, tok_ref, e_ref, i):
    tm = x_ref.shape[0]
    u = u_ref[...]
    halo = halo_ref[...]
    e_ref[0:POOL_MAX, :] = jnp.where(i == 0, jnp.zeros_like(halo), halo)
    e_ref[POOL_MAX:POOL_MAX + tm, :] = u
    pos = i * tm + lax.broadcasted_iota(jnp.int32, (tm, 1), 0)
    sums, cnts = [], []
    for g, w in enumerate(POOL_WINDOWS):
        sl = slice(g * POOL_CG, (g + 1) * POOL_CG)
        s = u[:, sl]
        for j in range(1, w):
            s = s + e_ref[POOL_MAX - j:POOL_MAX - j + tm, sl]
        sums.append(s)
        cnts.append(jnp.minimum(w, pos + 1).astype(F32))
    pool = _pool_project(sums, u, cnts, pw_ref, ps_ref)
    h1 = x_ref[...] + _dot(attn_ref[...], wo_ref[0:ATTN_WIDTH, :]) \
        + _dot(pool.astype(BF16), wo_ref[ATTN_WIDTH:, :])

    qm = _dot(_rms(h1, gm_ref[...]).astype(BF16), wmq_ref[...]).astype(BF16)
    mk = mk_ref[0].astype(BF16)
    mv = mv_ref[0].astype(BF16)
    outs = []
    for h in range(MEM_HEADS):
        sl = slice(h * MEM_HEAD_DIM, (h + 1) * MEM_HEAD_DIM)
        logits = _dot_nt(qm[:, sl], mk[:, sl]) * (MEM_HEAD_DIM ** -0.5)
        m = jnp.max(logits, axis=1, keepdims=True)
        p = jnp.exp(logits - m)
        p = p / jnp.sum(p, axis=1, keepdims=True)
        outs.append(_dot(p.astype(BF16), mv[:, sl]))
    o = jnp.concatenate(outs, axis=1)
    h2 = h1 + _dot(o.astype(BF16), wmo_ref[...])
    h2_ref[...] = h2
    tok_ref[...] = _rms(h2, gf_ref[...]).astype(BF16)


def _mid_prompt(x, attn, u, mk, mv, pool_w, pool_scale, w_o, g_mem, w_mq, w_mo, g_ffn, batch, t_len, tm, n_rows):
    nt = t_len // tm
    hb = tm // POOL_MAX
    n_real = batch * nt
    n_steps = -(-n_rows // tm)
    last = n_real - 1
    row = lambda s: (jnp.minimum(s, last), 0)
    halo = lambda s: (jnp.maximum(jnp.minimum(s, last) * hb - 1, 0), 0)
    fixed2 = lambda s: (0, 0)
    fixed3 = lambda s: (0, 0, 0)
    perb = lambda s: (jnp.minimum(s, last) // nt, 0, 0)
    out_row = lambda s: (s, 0)
    n = n_rows
    return pl.pallas_call(
        functools.partial(_mid_prompt_kernel, nt=nt, n_real=n_real),
        grid=(n_steps,),
        in_specs=[pl.BlockSpec((tm, D_MODEL), row),
                  pl.BlockSpec((tm, ATTN_WIDTH), row),
                  pl.BlockSpec((tm, POOL_WIDTH), row),
                  pl.BlockSpec((POOL_MAX, POOL_WIDTH), halo),
                  pl.BlockSpec((1, N_MEM, MEM_WIDTH), perb),
                  pl.BlockSpec((1, N_MEM, MEM_WIDTH), perb),
                  pl.BlockSpec((len(POOL_WINDOWS), POOL_CG, POOL_CG), fixed3),
                  pl.BlockSpec((1, POOL_WIDTH), fixed2),
                  pl.BlockSpec((D_MODEL, D_MODEL), fixed2),
                  pl.BlockSpec((1, D_MODEL), fixed2),
                  pl.BlockSpec((D_MODEL, MEM_WIDTH), fixed2),
                  pl.BlockSpec((MEM_WIDTH, D_MODEL), fixed2),
                  pl.BlockSpec((1, D_MODEL), fixed2)],
        out_specs=[pl.BlockSpec((tm, D_MODEL), out_row), pl.BlockSpec((tm, D_MODEL), out_row)],
        out_shape=[jax.ShapeDtypeStruct((n, D_MODEL), F32), jax.ShapeDtypeStruct((n, D_MODEL), BF16)],
        scratch_shapes=[pltpu.VMEM((POOL_MAX + tm, POOL_WIDTH), F32)],
        compiler_params=_params("arbitrary"),
        name="mid_prompt",
    )(x, attn, u, u, mk, mv, pool_w, pool_scale, w_o, g_mem, w_mq, w_mo, g_ffn)


def _router_kernel(tok_ref, rw_ref, rb_ref, rank_ref, gate_ref, cnt_ref, *, n_valid):
    tm = tok_ref.shape[0]
    token = pl.program_id(0) * tm + lax.broadcasted_iota(jnp.int32, (1, tm), 1)
    valid = token < n_valid
    logits = _dot_nt(rw_ref[...], tok_ref[...]) + rb_ref[...]
    eid = lax.broadcasted_iota(jnp.int32, (N_EXPERTS, tm), 0)
    vals, hots = [], []
    cur = logits
    for _ in range(TOP_K):
        m = jnp.max(cur, axis=0, keepdims=True)
        idx = jnp.min(jnp.where(cur == m, eid, N_EXPERTS), axis=0, keepdims=True)
        hot = eid == idx
        vals.append(m)
        hots.append(hot)
        cur = jnp.where(hot, -jnp.inf, cur)
    exps = [jnp.exp(v - vals[0]) for v in vals]
    den = exps[0] + exps[1] + exps[2] + exps[3]
    gate = jnp.zeros((N_EXPERTS, tm), F32)
    sel = jnp.zeros((N_EXPERTS, tm), jnp.bool_)
    for hot, ex in zip(hots, exps):
        gate = jnp.where(hot, ex / den, gate)
        sel = jnp.logical_or(sel, hot)
    sel = jnp.logical_and(sel, valid)
    gate_ref[0] = gate

    upper = _strict_triangle(LANES, lower=False)
    carry = jnp.zeros((N_EXPERTS, 1), F32)
    for c in range(tm // LANES):
        sl = slice(c * LANES, (c + 1) * LANES)
        sel_c = sel[:, sl]
        sel_b = jnp.where(sel_c, 1.0, 0.0).astype(BF16)
        rank = _dot(sel_b, upper) + carry
        rank_ref[0, :, sl] = jnp.where(sel_c, rank, -1.0).astype(jnp.int32)
        carry = carry + jnp.sum(sel_b.astype(F32), axis=1, keepdims=True)
    cnt_ref[0] = carry.astype(jnp.int32)


def _router(tok, rw_t, rb, tm, n_valid):
    n = tok.shape[0]
    nt = n // tm
    t3 = lambda i: (i, 0, 0)
    return pl.pallas_call(
        functools.partial(_router_kernel, n_valid=n_valid),
        grid=(nt,),
        in_specs=[pl.BlockSpec((tm, D_MODEL), lambda i: (i, 0)),
                  pl.BlockSpec((N_EXPERTS, D_MODEL), lambda i: (0, 0)),
                  pl.BlockSpec((N_EXPERTS, 1), lambda i: (0, 0))],
        out_specs=[pl.BlockSpec((1, N_EXPERTS, tm), t3),
                   pl.BlockSpec((1, N_EXPERTS, tm), t3),
                   pl.BlockSpec((1, N_EXPERTS, 1), t3)],
        out_shape=[jax.ShapeDtypeStruct((nt, N_EXPERTS, tm), jnp.int32),
                   jax.ShapeDtypeStruct((nt, N_EXPERTS, tm), F32),
                   jax.ShapeDtypeStruct((nt, N_EXPERTS, 1), jnp.int32)],
        compiler_params=_params("parallel"),
        name="router",
    )(tok, rw_t, rb)


def _swiglu(h):
    gate = jnp.minimum(h[:, :D_FF], SWIGLU_LIMIT)
    up = jnp.clip(h[:, D_FF:], -SWIGLU_LIMIT, SWIGLU_LIMIT)
    return gate * jax.nn.sigmoid(SWIGLU_ALPHA * gate) * (up + 1.0)


def _moe_kernel(cnt_ref, x_ref, h2_ref, rank_ref, gate_ref, wu_ref, bu_ref, wd_ref, bd_ref, gfin_ref,
                o_ref, os_ref, *, sample_row):
    i = pl.program_id(0)
    e = pl.program_id(1)
    tm = x_ref.shape[0]

    @pl.when(e == 0)
    def _():
        o_ref[...] = jnp.zeros_like(o_ref)

    n_routed = cnt_ref[i * N_EXPERTS + e]
    rank = rank_ref[0, pl.ds(e, 1), :]
    gate = gate_ref[0, pl.ds(e, 1), :]

    def run_slots(first, size):
        slot = first + lax.broadcasted_iota(jnp.int32, (size, tm), 0)
        hit = rank == slot
        onehot = jnp.where(hit, 1.0, 0.0).astype(BF16)
        xg = _dot(onehot, x_ref[...]).astype(BF16)
        act = _swiglu(_dot(xg, wu_ref[0]) + bu_ref[0])
        out = _dot(act.astype(BF16), wd_ref[0]) + bd_ref[0]
        g_slot = jnp.sum(jnp.where(hit, gate, 0.0), axis=1, keepdims=True)
        o_ref[...] += _dot_tn(onehot, (out * g_slot).astype(BF16))

    largest = MOE_FIRST_ROWS[-1]
    for lo, size in zip((0,) + MOE_FIRST_ROWS[:-1], MOE_FIRST_ROWS):
        fits = n_routed <= size if size < largest else n_routed > lo
        @pl.when(jnp.logical_and(n_routed > lo, fits))
        def _():
            run_slots(0, size)

    def body(c, carry):
        run_slots(largest + c * MOE_TAIL_ROWS, MOE_TAIL_ROWS)
        return carry

    lax.fori_loop(0, (jnp.maximum(n_routed - largest, 0) + MOE_TAIL_ROWS - 1) // MOE_TAIL_ROWS, body, 0)

    @pl.when(e == pl.num_programs(1) - 1)
    def _():
        o_ref[...] = _rms(h2_ref[...] + o_ref[...], gfin_ref[...])

        @pl.when(i == pl.num_programs(0) - 1)
        def _():
            os_ref[...] = o_ref[sample_row:sample_row + os_ref.shape[0], :]


def _moe(tok, h2, rank, gate, cnt, wu, bu, wd, bd, g_final, tm, n_prompt):
    nt = tok.shape[0] // tm
    tile = lambda i, e, c: (i, 0)
    tile3 = lambda i, e, c: (i, 0, 0)
    expert = lambda i, e, c: (e, 0, 0)
    grid_spec = pltpu.PrefetchScalarGridSpec(
        num_scalar_prefetch=1,
        grid=(nt, N_EXPERTS),
        in_specs=[pl.BlockSpec((tm, D_MODEL), tile),
                  pl.BlockSpec((tm, D_MODEL), tile),
                  pl.BlockSpec((1, N_EXPERTS, tm), tile3),
                  pl.BlockSpec((1, N_EXPERTS, tm), tile3),
                  pl.BlockSpec((1, D_MODEL, 2 * D_FF), expert),
                  pl.BlockSpec((1, 1, 2 * D_FF), expert),
                  pl.BlockSpec((1, D_FF, D_MODEL), expert),
                  pl.BlockSpec((1, 1, D_MODEL), expert),
                  pl.BlockSpec((1, D_MODEL), lambda i, e, c: (0, 0))],
        out_specs=[pl.BlockSpec((tm, D_MODEL), tile),
                   pl.BlockSpec((SAMPLE_ROWS, D_MODEL), lambda i, e, c: (0, 0))],
    )
    return pl.pallas_call(
        functools.partial(_moe_kernel, sample_row=n_prompt - (nt - 1) * tm),
        grid_spec=grid_spec,
        out_shape=[jax.ShapeDtypeStruct((n_prompt, D_MODEL), F32),
                   jax.ShapeDtypeStruct((SAMPLE_ROWS, D_MODEL), F32)],
        compiler_params=_params("arbitrary", "arbitrary"),
        name="moe",
    )(cnt, tok, h2, rank, gate, wu, bu, wd, bd, g_final)


def _page_copy(cache_ref, buf_ref, sem_ref, pt_ref, b, p, slot):
    dst = buf_ref.at[slot, :, pl.ds(pl.multiple_of(p * PAGE_SIZE, PAGE_SIZE), PAGE_SIZE)]
    return pltpu.make_async_copy(cache_ref.at[0, pt_ref[b, p]], dst, sem_ref.at[slot])


def _fetch_pages(cache_ref, buf_ref, sem_ref, pt_ref, b, slot, n_pages):
    def body(p, c):
        _page_copy(cache_ref, buf_ref, sem_ref, pt_ref, b, p, slot).start()
        return c
    lax.fori_loop(0, n_pages, body, 0, unroll=8)


def _wait_pages(cache_ref, buf_ref, sem_ref, pt_ref, b, slot, n_pages):
    def body(p, c):
        _page_copy(cache_ref, buf_ref, sem_ref, pt_ref, b, p, slot).wait()
        return c
    lax.fori_loop(0, n_pages, body, 0, unroll=8)


def _dsa_sample_score_kernel(pt_ref, qi_ref, wi_ref, kin_ref, cache_ref, score_ref, new_ref, buf_ref, sem_ref):
    b = pl.program_id(0)
    nb = pl.num_programs(0)
    past = buf_ref.shape[2]
    n_pages = past // PAGE_SIZE
    slot = b % 2

    @pl.when(b == 0)
    def _():
        _fetch_pages(cache_ref, buf_ref, sem_ref, pt_ref, b, slot, n_pages)

    @pl.when(b + 1 < nb)
    def _():
        _fetch_pages(cache_ref, buf_ref, sem_ref, pt_ref, b + 1, 1 - slot, n_pages)

    _wait_pages(cache_ref, buf_ref, sem_ref, pt_ref, b, slot, n_pages)

    qi = qi_ref[0]
    w = wi_ref[0] * (IDX_HEADS ** -0.5)
    keys_t = buf_ref[slot].astype(BF16)
    score_ref[0] = jnp.sum(jnp.maximum(_dot(qi, keys_t), 0.0) * w, axis=0, keepdims=True)
    s_new = jnp.sum(qi.astype(F32) * kin_ref[0].astype(F32), axis=1, keepdims=True)
    new_ref[0] = jnp.sum(jnp.maximum(s_new, 0.0) * w, axis=0, keepdims=True)


def _dsa_sample_select_kernel(score_ref, new_ref, mask_ref, take_new_ref, *, topk):
    score = score_ref[...]
    score_new = new_ref[...]
    past = score.shape[1]

    def count_ge(cand):
        return (jnp.sum(jnp.where(score >= cand, 1, 0), axis=1, keepdims=True)
                + jnp.where(score_new >= cand, 1, 0))

    tau = _kth_largest(count_ge, topk, score_new.shape)
    gt = score > tau
    eq = score == tau
    n_gt = jnp.sum(jnp.where(gt, 1, 0), axis=1, keepdims=True) + jnp.where(score_new > tau, 1, 0)
    n_eq_all = jnp.sum(jnp.where(eq, 1, 0), axis=1, keepdims=True) + jnp.where(score_new == tau, 1, 0)
    tied = n_gt + n_eq_all > topk
    need_f = jnp.where(tied, topk - n_gt, past + 2).astype(F32)
    take_eq, n_eq = _take_first_ties(eq, need_f, tied, axis=1)
    mask_ref[...] = jnp.where(jnp.logical_or(gt, take_eq > 0.5), 0.0, NEG)
    take_new = jnp.logical_or(score_new > tau, jnp.logical_and(score_new == tau, n_eq < need_f))
    take_new_ref[...] = jnp.where(take_new, 0.0, NEG)


def _dsa_sample_select(score, score_new, topk):
    db, past = score.shape
    full = lambda shape: pl.BlockSpec(shape, lambda i: (0, 0))
    return pl.pallas_call(
        functools.partial(_dsa_sample_select_kernel, topk=topk),
        grid=(1,),
        in_specs=[full((db, past)), full((db, 1))],
        out_specs=[full((db, past)), full((db, 1))],
        out_shape=[jax.ShapeDtypeStruct((db, past), F32), jax.ShapeDtypeStruct((db, 1), F32)],
        compiler_params=_params("arbitrary"),
        name="dsa_sample_select",
    )(score, score_new)


def _dsa_sample_mask(page_table, qi_heads, wi_col, ki_new, cache_idx_k):
    db, n_pages = page_table.shape
    past = n_pages * PAGE_SIZE
    topk = min(TOPK_MAX, (past + 1) // 4)
    per = lambda b, pt: (b, 0, 0)
    grid_spec = pltpu.PrefetchScalarGridSpec(
        num_scalar_prefetch=1,
        grid=(db,),
        in_specs=[pl.BlockSpec((1, IDX_HEADS, IDX_DIM), per),
                  pl.BlockSpec((1, IDX_HEADS, 1), per),
                  pl.BlockSpec((1, 1, IDX_DIM), per),
                  pl.BlockSpec(memory_space=pl.ANY)],
        out_specs=[pl.BlockSpec((1, 1, past), per), pl.BlockSpec((1, 1, 1), per)],
        scratch_shapes=[pltpu.VMEM((2, IDX_DIM, past), F32),
                        pltpu.SemaphoreType.DMA((2,))],
    )
    score, score_new = pl.pallas_call(
        _dsa_sample_score_kernel,
        grid_spec=grid_spec,
        out_shape=[jax.ShapeDtypeStruct((db, 1, past), F32), jax.ShapeDtypeStruct((db, 1, 1), F32)],
        compiler_params=_params("arbitrary"),
        name="dsa_sample_score",
    )(page_table, qi_heads, wi_col, ki_new, cache_idx_k)
    mask, take_new = _dsa_sample_select(score.reshape(db, past), score_new.reshape(db, 1), topk)
    return mask.reshape(db, 1, past), take_new.reshape(db, 1, 1)


def _dsa_sample_attend_kernel(pt_ref, q_ref, kn_ref, vn_ref, mask_ref, new_ref, ck_ref, cv_ref, o_ref,
                              kbuf_ref, vbuf_ref, ksem_ref, vsem_ref):
    b = pl.program_id(0)
    nb = pl.num_programs(0)
    past = kbuf_ref.shape[2]
    n_pages = past // PAGE_SIZE
    slot = b % 2

    def fetch(bb, sl):
        _fetch_pages(ck_ref, kbuf_ref, ksem_ref, pt_ref, bb, sl, n_pages)
        _fetch_pages(cv_ref, vbuf_ref, vsem_ref, pt_ref, bb, sl, n_pages)

    @pl.when(b == 0)
    def _():
        fetch(b, slot)

    @pl.when(b + 1 < nb)
    def _():
        fetch(b + 1, 1 - slot)

    _wait_pages(ck_ref, kbuf_ref, ksem_ref, pt_ref, b, slot, n_pages)
    _wait_pages(cv_ref, vbuf_ref, vsem_ref, pt_ref, b, slot, n_pages)

    q = q_ref[0]
    keys_t = kbuf_ref[slot].astype(BF16)
    vals_t = vbuf_ref[slot].astype(BF16)
    logits = _dot(q, keys_t) + mask_ref[0]
    l_new = jnp.sum(q.astype(F32) * kn_ref[0].astype(F32), axis=1, keepdims=True) + new_ref[0]
    m = jnp.maximum(jnp.max(logits, axis=1, keepdims=True), l_new)
    p = jnp.exp(logits - m)
    p_new = jnp.exp(l_new - m)
    den = jnp.sum(p, axis=1, keepdims=True) + p_new
    p = p / den
    p_new = p_new / den
    o_ref[0] = _dot_nt(p.astype(BF16), vals_t) + p_new * vn_ref[0].astype(F32)


def _dsa_sample_attend(page_table, q_heads, k_new, v_new, mask, take_new, cache_k, cache_v):
    db, n_pages = page_table.shape
    past = n_pages * PAGE_SIZE
    per = lambda b, pt: (b, 0, 0)
    grid_spec = pltpu.PrefetchScalarGridSpec(
        num_scalar_prefetch=1,
        grid=(db,),
        in_specs=[pl.BlockSpec((1, ATTN_HEADS, HEAD_DIM), per),
                  pl.BlockSpec((1, 1, HEAD_DIM), per),
                  pl.BlockSpec((1, 1, HEAD_DIM), per),
                  pl.BlockSpec((1, 1, past), per),
                  pl.BlockSpec((1, 1, 1), per),
                  pl.BlockSpec(memory_space=pl.ANY),
                  pl.BlockSpec(memory_space=pl.ANY)],
        out_specs=pl.BlockSpec((1, ATTN_HEADS, HEAD_DIM), per),
        scratch_shapes=[pltpu.VMEM((2, HEAD_DIM, past), F32),
                        pltpu.VMEM((2, HEAD_DIM, past), F32),
                        pltpu.SemaphoreType.DMA((2,)),
                        pltpu.SemaphoreType.DMA((2,))],
    )
    return pl.pallas_call(
        _dsa_sample_attend_kernel,
        grid_spec=grid_spec,
        out_shape=jax.ShapeDtypeStruct((db, ATTN_HEADS, HEAD_DIM), F32),
        compiler_params=_params("arbitrary"),
        name="dsa_sample_attend",
    )(page_table, q_heads, k_new, v_new, mask, take_new, cache_k, cache_v)


def _mid_sample_a_kernel(x_ref, attn_ref, u_ref, st_ref, pw_ref, ps_ref, wo_ref, gm_ref, wmq_ref,
                         h1_ref, qm_ref, *, past):
    u = u_ref[...]
    rows = u.shape[0]
    sums, cnts = [], []
    for g, w in enumerate(POOL_WINDOWS):
        sl = slice(g * POOL_CG, (g + 1) * POOL_CG)
        s = u[:, sl]
        for j in range(1, w):
            s = s + st_ref[POOL_MAX - 1 - j][:, sl]
        sums.append(s)
        cnts.append(jnp.full((rows, 1), float(min(w, past + 1)), F32))
    pool = _pool_project(sums, u, cnts, pw_ref, ps_ref)
    h1 = x_ref[...] + _dot(attn_ref[...], wo_ref[0:ATTN_WIDTH, :]) \
        + _dot(pool.astype(BF16), wo_ref[ATTN_WIDTH:, :])
    h1_ref[...] = h1
    qm_ref[...] = _dot(_rms(h1, gm_ref[...]).astype(BF16), wmq_ref[...]).astype(BF16)


def _mid_sample_a(x, attn, u, state_t, pool_w, pool_scale, w_o, g_mem, w_mq, past):
    db = x.shape[0]
    full = lambda a: pl.BlockSpec(a.shape, lambda i, nd=a.ndim: (0,) * nd)
    args = (x, attn, u, state_t, pool_w, pool_scale, w_o, g_mem, w_mq)
    return pl.pallas_call(
        functools.partial(_mid_sample_a_kernel, past=past),
        grid=(1,),
        in_specs=[full(a) for a in args],
        out_specs=[pl.BlockSpec((db, D_MODEL), lambda i: (0, 0)),
                   pl.BlockSpec((db, MEM_WIDTH), lambda i: (0, 0))],
        out_shape=[jax.ShapeDtypeStruct((db, D_MODEL), F32), jax.ShapeDtypeStruct((db, MEM_WIDTH), BF16)],
        compiler_params=_params("arbitrary"),
        name="mid_sample_a",
    )(*args)


def _mem_sample_kernel(q_ref, mk_ref, mv_ref, o_ref):
    rows = 2 * MEM_HEADS
    n = mk_ref.shape[1]
    q_all = q_ref[0].astype(F32)
    q = jnp.concatenate([q_all[:, (r % MEM_HEADS) * MEM_HEAD_DIM:(r % MEM_HEADS + 1) * MEM_HEAD_DIM]
                         for r in range(rows)], axis=0).astype(BF16)
    logits = _dot_nt(q, mk_ref[0].astype(BF16)) * (MEM_HEAD_DIM ** -0.5)
    own = (lax.broadcasted_iota(jnp.int32, (rows, n), 1) % MEM_HEADS
           == lax.broadcasted_iota(jnp.int32, (rows, n), 0) % MEM_HEADS)
    logits = jnp.where(own, logits, -jnp.inf)
    m = jnp.max(logits, axis=1, keepdims=True)
    p = jnp.exp(logits - m)
    p = p / jnp.sum(p, axis=1, keepdims=True)
    o = _dot(p.astype(BF16), mv_ref[0].astype(BF16))
    for h in range(MEM_HEADS):
        o_ref[0, :, h * MEM_HEAD_DIM:(h + 1) * MEM_HEAD_DIM] = o[h:h + 1, :]


def _mem_sample(qm, mk, mv):
    db = qm.shape[0]
    per = lambda b: (b, 0, 0)
    return pl.pallas_call(
        _mem_sample_kernel,
        grid=(db,),
        in_specs=[pl.BlockSpec((1, 1, MEM_WIDTH), per),
                  pl.BlockSpec((1, N_MEM * MEM_HEADS, MEM_HEAD_DIM), per),
                  pl.BlockSpec((1, N_MEM * MEM_HEADS, MEM_HEAD_DIM), per)],
        out_specs=pl.BlockSpec((1, 1, MEM_WIDTH), per),
        out_shape=jax.ShapeDtypeStruct((db, 1, MEM_WIDTH), F32),
        compiler_params=_params("parallel"),
        name="mem_sample",
    )(qm.reshape(db, 1, MEM_WIDTH), mk, mv)


def _sample_join_kernel(h1_ref, o_ref, wmo_ref, gf_ref, h2_any, tok_any, h2_ref, tok_ref):
    del h2_any, tok_any
    db = h1_ref.shape[0]
    h2_ref[...] = jnp.zeros_like(h2_ref)
    tok_ref[...] = jnp.zeros_like(tok_ref)

    @pl.when(pl.program_id(0) == 0)
    def _():
        h2 = h1_ref[...] + _dot(o_ref[...].astype(BF16), wmo_ref[...])
        h2_ref[0:db, :] = h2
        tok_ref[0:db, :] = _rms(h2, gf_ref[...]).astype(BF16)


def _sample_join(h1, o, w_mo, g_ffn, h2_all, tok_all, n_prompt):
    db = h1.shape[0]
    n_rows = h2_all.shape[0]
    assert db <= SAMPLE_ROWS and n_prompt % SAMPLE_ROWS == 0 and n_rows % SAMPLE_ROWS == 0
    first = n_prompt // SAMPLE_ROWS
    f2 = lambda j: (0, 0)
    tail = lambda j: (first + j, 0)
    return pl.pallas_call(
        _sample_join_kernel,
        grid=((n_rows - n_prompt) // SAMPLE_ROWS,),
        in_specs=[pl.BlockSpec((db, D_MODEL), f2),
                  pl.BlockSpec((db, MEM_WIDTH), f2),
                  pl.BlockSpec((MEM_WIDTH, D_MODEL), f2),
                  pl.BlockSpec((1, D_MODEL), f2),
                  pl.BlockSpec(memory_space=pl.ANY),
                  pl.BlockSpec(memory_space=pl.ANY)],
        out_specs=[pl.BlockSpec((SAMPLE_ROWS, D_MODEL), tail), pl.BlockSpec((SAMPLE_ROWS, D_MODEL), tail)],
        out_shape=[jax.ShapeDtypeStruct(h2_all.shape, F32), jax.ShapeDtypeStruct(tok_all.shape, BF16)],
        input_output_aliases={4: 0, 5: 1},
        compiler_params=_params("arbitrary"),
        name="sample_join",
    )(h1, o, w_mo, g_ffn, h2_all, tok_all)


def _rope_tables(pos):
    half = HEAD_DIM // 2
    inv = ROPE_THETA ** (-jnp.arange(half, dtype=F32) / half)
    ang = pos.astype(F32)[:, None] * inv[None, :]
    cos = jnp.cos(ang)
    sin = jnp.sin(ang)
    return jnp.tile(cos, (1, 4)), jnp.concatenate([-sin, sin, -sin, sin], axis=1)


def _pick_tile(n, pref):
    t = min(n, pref)
    while n % t:
        t //= 2
    return t


def kernel(x_prompt, x_sample, mem_prompt, cache_k, cache_v, cache_idx_k, state_pool, cache_mem_k, cache_mem_v, page_table, norm_mix, w_in, pool_w, pool_scale, w_o, norm_mem, w_mq, w_mk, w_mv, w_mo, norm_ffn, router_w, router_b, w_up, b_up, w_down, b_down, norm_final):
    B, T = x_prompt.shape[:2]
    DB, DS = x_sample.shape[:2]
    assert DS == 1 and w_in.shape[0] == 1
    n_pages = page_table.shape[1]
    past = n_pages * PAGE_SIZE
    NP = B * T

    wi_ = w_in[0]
    wq, wk, wv, wu_, wqi, wki, wwi = jnp.split(wi_, [512, 576, 640, 1152, 1664, 1728], axis=1)
    w_all = jnp.concatenate(
        [wq, wqi, wu_, wk, wk, wki, wki, wv, wv, wwi, jnp.zeros((D_MODEL, LANES - IDX_HEADS), F32)],
        axis=1).astype(BF16)
    g_mix = norm_mix[0][None, :]
    g_mem = norm_mem[0][None, :]
    g_ffn = norm_ffn[0][None, :]
    g_fin = norm_final[None, :]
    pw = pool_w[0].astype(BF16)
    ps = pool_scale[0][None, :]
    wo = w_o[0].astype(BF16)
    wmq = w_mq[0].reshape(D_MODEL, MEM_WIDTH).astype(BF16)
    wkv = jnp.concatenate([w_mk[0].reshape(D_MODEL, MEM_WIDTH), w_mv[0].reshape(D_MODEL, MEM_WIDTH)],
                          axis=1).astype(BF16)
    wmo = w_mo[0].reshape(MEM_WIDTH, D_MODEL).astype(BF16)
    rw = router_w[0].astype(BF16)
    rb = router_b[0]
    wup = w_up[0].astype(BF16)
    wdn = w_down[0].astype(BF16)
    bup = b_up[0][:, None, :]
    bdn = b_down[0][:, None, :]

    xp = x_prompt.reshape(NP, D_MODEL)
    tm_in = _pick_tile(T, 512)
    cos_p, sin_p = _rope_tables(jnp.arange(T, dtype=jnp.int32))
    q, qi, u, kk, kiki, vvt, k_pt, ki_pt, v_pt, wi_p = _in_project(xp, g_mix, w_all, cos_p, sin_p, tm_in,
                                                                   T // tm_in, batch=B)
    attn = _dsa_prompt(q, qi, wi_p, kk, kiki, vvt, B, T)
    mk_p, mv_p = _mem_kv(mem_prompt.reshape(B * N_MEM, D_MODEL), wkv)
    tm_moe = min(MOE_TILE, NP)
    n_tiles = -(-(NP + SAMPLE_ROWS) // tm_moe)
    assert (n_tiles - 1) * tm_moe < NP and NP + SAMPLE_ROWS <= n_tiles * tm_moe
    h2_all, tok_all = _mid_prompt(xp, attn, u, mk_p.reshape(B, N_MEM, MEM_WIDTH), mv_p.reshape(B, N_MEM, MEM_WIDTH),
                                  pw, ps, wo, g_mem, wmq, wmo, g_ffn, B, T, _pick_tile(T, 1024), n_tiles * tm_moe)

    xs = x_sample.reshape(DB, D_MODEL)
    cos_s, sin_s = _rope_tables(jnp.full((DB,), past, jnp.int32))
    q_s, qi_s, u_s, kk_s, kiki_s, vv_s, k_s, ki_s, v_s, wi_s = _in_project(xs, g_mix, w_all, cos_s, sin_s, DB, 1)
    mask, take_new = _dsa_sample_mask(
        page_table, qi_s.reshape(DB, IDX_HEADS, IDX_DIM), wi_s[:, :IDX_HEADS, None],
        kiki_s[:, None, :IDX_DIM], jnp.swapaxes(cache_idx_k, 2, 3))
    attn_s = _dsa_sample_attend(
        page_table, q_s.reshape(DB, ATTN_HEADS, HEAD_DIM), kk_s[:, None, :HEAD_DIM], vv_s[:, None, :HEAD_DIM],
        mask, take_new, jnp.swapaxes(cache_k, 2, 3), jnp.swapaxes(cache_v, 2, 3))
    attn_s = attn_s.reshape(DB, ATTN_WIDTH).astype(BF16)
    h1_s, qm_s = _mid_sample_a(xs, attn_s, u_s, jnp.transpose(state_pool[0], (1, 0, 2)), pw, ps, wo, g_mem, wmq, past)
    o_s = _mem_sample(qm_s, cache_mem_k.reshape(DB, N_MEM * MEM_HEADS, MEM_HEAD_DIM),
                      cache_mem_v.reshape(DB, N_MEM * MEM_HEADS, MEM_HEAD_DIM))
    h2_all, tok_all = _sample_join(h1_s, o_s.reshape(DB, MEM_WIDTH), wmo, g_ffn, h2_all, tok_all, NP)

    rank, gate, cnt = _router(tok_all, rw.T, rb[:, None], tm_moe, NP + DB)
    y_p, y_s = _moe(tok_all, h2_all, rank, gate, cnt.reshape(-1), wup, bup, wdn, bdn, g_fin, tm_moe, NP)
    y_prompt = y_p.reshape(B, T, D_MODEL)
    y_sample = y_s[:DB].reshape(DB, 1, D_MODEL)

    new_pool_p = u.reshape(B, T, POOL_WIDTH)[:, T - (POOL_MAX - 1):][None]
    new_pool_s = jnp.concatenate([state_pool[0][:, 1:], u_s[:, None, :]], axis=1)[None]
    return (y_prompt, y_sample,
            jnp.swapaxes(k_pt, 1, 2)[None], jnp.swapaxes(v_pt, 1, 2)[None], jnp.swapaxes(ki_pt, 1, 2)[None],
            new_pool_p,
            mk_p.reshape(1, B, N_MEM, MEM_HEADS, MEM_HEAD_DIM), mv_p.reshape(1, B, N_MEM, MEM_HEADS, MEM_HEAD_DIM),
            k_s.reshape(1, DB, 1, HEAD_DIM), v_s.reshape(1, DB, 1, HEAD_DIM), ki_s.reshape(1, DB, 1, IDX_DIM),
            new_pool_s)
```

```python
import functools

import jax
import jax.numpy as jnp
from jax import lax
from jax.experimental import pallas as pl
from jax.experimental.pallas import tpu as pltpu

D_MODEL = 1024
HEAD_DIM = 64
ATTN_WIDTH = 512
ATTN_HEADS = 8
IDX_HEADS = 8
IDX_DIM = 64
TOPK_MAX = 256
Q_BLOCK = 128
PAGE_SIZE = 128
POOL_WIDTH = 512
POOL_WINDOWS = (2, 4, 8, 16)
POOL_CG = 128
POOL_MAX = 16
N_MEM = 256
MEM_HEADS = 4
MEM_HEAD_DIM = 128
MEM_WIDTH = MEM_HEADS * MEM_HEAD_DIM
N_EXPERTS = 32
TOP_K = 4
D_FF = 1024
SWIGLU_LIMIT = 7.0
SWIGLU_ALPHA = 1.702
ROPE_THETA = 10000.0
EPS = 1e-6
NEG = -1e30

LANES = 128
MOE_TILE = 13 * LANES
MOE_FIRST_ROWS = (208, 224, 256)
MOE_TAIL_ROWS = 128
SAMPLE_ROWS = 128
DSA_Q = 256
DSA_CAST_BLOCK_BYTES = 4 * 1024 * 1024
DSA_KCHUNK = 256
IN_COLS = 3 * 512 + 4 * LANES
VMEM_LIMIT = 56 * 1024 * 1024

BF16 = jnp.bfloat16
F32 = jnp.float32


def _dot(a, b):
    return jnp.dot(a, b, preferred_element_type=F32)


def _dot_nt(a, b):
    return lax.dot_general(a, b, (((1,), (1,)), ((), ())), preferred_element_type=F32)


def _dot_tn(a, b):
    return lax.dot_general(a, b, (((0,), (0,)), ((), ())), preferred_element_type=F32)


def _rms(x, g):
    return x * lax.rsqrt(jnp.mean(x * x, axis=-1, keepdims=True) + EPS) * g


def _params(*sem):
    return pltpu.CompilerParams(dimension_semantics=sem, vmem_limit_bytes=VMEM_LIMIT)


def _in_kernel(x_ref, g_ref, w_ref, cos_ref, sin_ref,
               q_ref, qi_ref, u_ref, kk_ref, kiki_ref, vv_ref, k_ref, ki_ref, v_ref, wi_ref, *, feature_major):
    xn = _rms(x_ref[...], g_ref[...]).astype(BF16)
    z = _dot(xn, w_ref[...])
    cos = cos_ref[...]
    sin = sin_ref[...]
    lane = lax.broadcasted_iota(jnp.int32, cos.shape, 1)
    first_half = (lane % HEAD_DIM) < (HEAD_DIM // 2)

    def rope(zc):
        swapped = jnp.where(first_half,
                            pltpu.roll(zc, LANES - HEAD_DIM // 2, 1),
                            pltpu.roll(zc, HEAD_DIM // 2, 1))
        return zc * cos + swapped * sin

    for c in range(4):
        q_c = rope(z[:, c * LANES:(c + 1) * LANES]) * (HEAD_DIM ** -0.5)
        qi_c = rope(z[:, 512 + c * LANES:512 + (c + 1) * LANES]) * (IDX_DIM ** -0.5)
        q_ref[:, c * LANES:(c + 1) * LANES] = q_c.astype(BF16)
        qi_ref[:, c * LANES:(c + 1) * LANES] = qi_c.astype(BF16)
    u_ref[...] = z[:, 1024:1536]
    kk = rope(z[:, 1536:1664])
    kiki = rope(z[:, 1664:1792])
    vv = z[:, 1792:1920]
    kk_ref[...] = kk.astype(BF16)
    kiki_ref[...] = kiki.astype(BF16)
    wi_ref[...] = z[:, 1920:2048]
    if feature_major:
        kk_t = jnp.transpose(kk)
        kiki_t = jnp.transpose(kiki)
        vv_t = jnp.transpose(vv)
        vv_ref[0] = vv_t.astype(BF16)
        k_ref[0] = kk_t[0:HEAD_DIM, :]
        ki_ref[0] = kiki_t[0:IDX_DIM, :]
        v_ref[0] = vv_t[0:HEAD_DIM, :]
    else:
        vv_ref[...] = vv.astype(BF16)
        k_ref[...] = kk[:, :HEAD_DIM]
        ki_ref[...] = kiki[:, :IDX_DIM]
        v_ref[...] = vv[:, :HEAD_DIM]


def _in_project(x, g, w_all, cos, sin, tm, table_blocks, batch=None):
    n = x.shape[0]
    row = lambda i: (i, 0)
    tab = lambda i: (i % table_blocks, 0)
    fixed = lambda i: (0, 0)
    feature_major = batch is not None
    if feature_major:
        t_len = n // batch
        assert t_len == table_blocks * tm
        fm = lambda feat, dt: jax.ShapeDtypeStruct((batch, feat, t_len), dt)
        fm_spec = lambda feat: pl.BlockSpec((1, feat, tm), lambda i: (i // table_blocks, 0, i % table_blocks))
    else:
        fm = lambda feat, dt: jax.ShapeDtypeStruct((n, feat), dt)
        fm_spec = lambda feat: pl.BlockSpec((tm, feat), row)
    outs = [
        jax.ShapeDtypeStruct((n, 512), BF16),
        jax.ShapeDtypeStruct((n, 512), BF16),
        jax.ShapeDtypeStruct((n, 512), F32),
        jax.ShapeDtypeStruct((n, LANES), BF16),
        jax.ShapeDtypeStruct((n, LANES), BF16),
        fm(LANES, BF16),
        fm(HEAD_DIM, F32),
        fm(IDX_DIM, F32),
        fm(HEAD_DIM, F32),
        jax.ShapeDtypeStruct((n, LANES), F32),
    ]
    out_specs = ([pl.BlockSpec((tm, s.shape[1]), row) for s in outs[:5]]
                 + [fm_spec(LANES), fm_spec(HEAD_DIM), fm_spec(IDX_DIM), fm_spec(HEAD_DIM)]
                 + [pl.BlockSpec((tm, LANES), row)])
    return pl.pallas_call(
        functools.partial(_in_kernel, feature_major=feature_major),
        grid=(n // tm,),
        in_specs=[pl.BlockSpec((tm, D_MODEL), row),
                  pl.BlockSpec((1, D_MODEL), fixed),
                  pl.BlockSpec((D_MODEL, IN_COLS), fixed),
                  pl.BlockSpec((tm, LANES), tab),
                  pl.BlockSpec((tm, LANES), tab)],
        out_specs=out_specs,
        out_shape=outs,
        compiler_params=_params("parallel"),
        name="in_project",
    )(x, g, w_all, cos, sin)


def _f32_to_key(x):
    bits = lax.bitcast_convert_type(x, jnp.int32)
    return jnp.where(bits < 0, bits ^ jnp.int32(0x7FFFFFFF), bits)


def _key_to_f32(key):
    bits = jnp.where(key < 0, key ^ jnp.int32(0x7FFFFFFF), key)
    return lax.bitcast_convert_type(bits, F32)


def _kth_largest(count_ge, k, shape):
    def body(it, tau):
        cand = tau + jnp.left_shift(jnp.int32(1), jnp.int32(31) - it)
        return jnp.where(count_ge(_key_to_f32(cand)) >= k, cand, tau)

    return _key_to_f32(lax.fori_loop(0, 32, body, jnp.full(shape, -2 ** 31, jnp.int32)))


def _kth_largest_between(count_ge, k, lo, hi):
    key_lo = _f32_to_key(lo)
    span = _f32_to_key(hi) - key_lo
    n_bits = jnp.max(32 - lax.clz(span))
    flip = jnp.int32(-2 ** 31)

    def body(it, off):
        cand = off | jnp.left_shift(jnp.int32(1), n_bits - 1 - it)
        ok = jnp.logical_and(count_ge(_key_to_f32(key_lo + cand)) >= k, (cand ^ flip) <= (span ^ flip))
        return jnp.where(ok, cand, off)

    return _key_to_f32(key_lo + lax.fori_loop(0, n_bits, body, jnp.zeros_like(key_lo)))


def _strict_triangle(n, lower):
    r = lax.broadcasted_iota(jnp.int32, (n, n), 0)
    c = lax.broadcasted_iota(jnp.int32, (n, n), 1)
    return jnp.where((r > c) if lower else (r < c), 1.0, 0.0).astype(BF16)


def _take_first_ties(eq, need_f, tied, axis):
    n = eq.shape[axis]
    cshape = (1, eq.shape[1]) if axis == 0 else (eq.shape[0], 1)

    def ranked():
        tri = _strict_triangle(LANES, lower=(axis == 0))
        carry = jnp.zeros(cshape, F32)
        parts = []
        for c in range(n // LANES):
            sl = slice(c * LANES, (c + 1) * LANES)
            eq_f = jnp.where(eq[sl, :] if axis == 0 else eq[:, sl], 1.0, 0.0)
            eq_b = eq_f.astype(BF16)
            rank = (_dot(tri, eq_b) if axis == 0 else _dot(eq_b, tri)) + carry
            parts.append(jnp.where(rank < need_f, eq_f, 0.0))
            carry = carry + jnp.sum(eq_f, axis=axis, keepdims=True)
        return jnp.concatenate(parts, axis=axis), carry

    def plain():
        return jnp.where(eq, 1.0, 0.0), jnp.zeros(cshape, F32)

    return lax.cond(jnp.max(jnp.where(tied, 1, 0)) > 0, ranked, plain)


def _dsa_prompt_kernel(*refs, topk, n_blocks, n_cast):
    cast_in, rest = refs[6:6 + n_cast], refs[6 + n_cast:]
    o_ref, cast_out, (sm_ref, bias_ref) = rest[0], rest[1:1 + n_cast], rest[1 + n_cast:]
    for src, dst in zip(cast_in, cast_out):
        dst[...] = src[...].astype(BF16)
    for block in range(n_blocks):
        @pl.when(pl.program_id(1) == block)
        def _():
            _dsa_prompt_block(*refs[:6], o_ref, sm_ref, bias_ref, topk=topk, block=block)


def _dsa_prompt_block(q_ref, qi_ref, wi_ref, kk_ref, kiki_ref, vvt_ref, o_ref, sm_ref, bias_ref, *, topk, block):
    i = block
    qb = q_ref.shape[0]
    lk = (block + 1) * qb
    select = lk > topk
    ck = min(DSA_KCHUNK, lk)
    chunks = [slice(c * ck, (c + 1) * ck) for c in range(lk // ck)]
    upper_rows = lax.broadcasted_iota(jnp.int32, (LANES, qb), 0) < HEAD_DIM

    def head_operand(ref, h):
        pair_t = jnp.transpose(ref[:, (h // 2) * LANES:(h // 2 + 1) * LANES].astype(F32))
        keep = upper_rows if h % 2 == 0 else jnp.logical_not(upper_rows)
        return jnp.where(keep, pair_t, 0.0).astype(BF16)

    pos_q = i * qb + lax.broadcasted_iota(jnp.int32, (1, qb), 1)

    def causal(sl):
        return sl.start + lax.broadcasted_iota(jnp.int32, (ck, 1), 0) <= pos_q

    if select:
        w_t = jnp.transpose(wi_ref[...])[0:IDX_HEADS, :] * (IDX_HEADS ** -0.5)
        qi_ops = [head_operand(qi_ref, h) for h in range(IDX_HEADS)]
        for sl in chunks:
            keys = kiki_ref[0, sl, :]
            score = jnp.zeros((ck, qb), F32)
            for h in range(IDX_HEADS):
                score = score + jnp.maximum(_dot(keys, qi_ops[h]), 0.0) * w_t[h:h + 1, :]
            sm_ref[sl, :] = jnp.where(causal(sl), score, -jnp.inf)

        def count_ge(cand):
            return jnp.sum(jnp.where(sm_ref[0:lk, :] >= cand, 1, 0), axis=0, keepdims=True)

        if ck == topk:
            class_max = sm_ref[chunks[0], :]
            for sl in chunks[1:]:
                class_max = jnp.maximum(class_max, sm_ref[sl, :])
            tau = _kth_largest_between(count_ge, topk, jnp.min(class_max, axis=0, keepdims=True),
                                       jnp.max(class_max, axis=0, keepdims=True))
        else:
            tau = _kth_largest(count_ge, topk, (1, qb))
        sm = sm_ref[0:lk, :]
        gt = sm > tau
        eq = sm == tau
        n_gt = jnp.sum(jnp.where(gt, 1, 0), axis=0, keepdims=True)
        n_eq = jnp.sum(jnp.where(eq, 1, 0), axis=0, keepdims=True)
        tied = n_gt + n_eq > topk
        need_f = jnp.where(tied, topk - n_gt, lk + 1).astype(F32)
        take_eq, _ = _take_first_ties(eq, need_f, tied, axis=0)
        bias_ref[0:lk, :] = jnp.where(jnp.logical_or(gt, take_eq > 0.5), 0.0, NEG)

    def chunk_bias(sl):
        if select:
            return bias_ref[sl, :]
        return jnp.where(causal(sl), 0.0, NEG)

    def fold8(x, op):
        return op(x.reshape(ck // 8, 8, qb), axis=0)

    for j in range(ATTN_HEADS // 2):
        outs = []
        for h in (2 * j, 2 * j + 1):
            q_op = head_operand(q_ref, h)
            mx8 = jnp.full((8, qb), -jnp.inf, F32)
            for sl in chunks:
                logits = _dot(kk_ref[0, sl, :], q_op) + chunk_bias(sl)
                sm_ref[sl, :] = logits
                mx8 = jnp.maximum(mx8, fold8(logits, jnp.max))
            m = jnp.max(mx8, axis=0, keepdims=True)
            den8 = jnp.zeros((8, qb), F32)
            acc = jnp.zeros((LANES, qb), F32)
            for sl in chunks:
                p = jnp.exp(sm_ref[sl, :] - m)
                den8 = den8 + fold8(p, jnp.sum)
                acc = acc + _dot(vvt_ref[0, :, sl], p.astype(BF16))
            outs.append(acc * (1.0 / jnp.sum(den8, axis=0, keepdims=True)))
        pair_t = jnp.where(upper_rows, outs[0], outs[1])
        o_ref[0, :, j * LANES:(j + 1) * LANES] = jnp.transpose(pair_t).astype(BF16)


def _cast_rows_per_step(arrays, n_steps):
    rows = []
    for a in arrays:
        r, rem = divmod(a.shape[0], n_steps)
        if rem or r % 16 or r * a.shape[1] * 4 > DSA_CAST_BLOCK_BYTES:
            return None
        rows.append(r)
    return rows


def _dsa_prompt(q, qi, wi, kk, kiki, vvt, batch, t_len, to_bf16=()):
    nb = t_len // DSA_Q
    topk = min(TOPK_MAX, t_len // 4)
    assert topk % DSA_Q == 0 and t_len % DSA_Q == 0
    kk, kiki = (a.reshape(batch, t_len, LANES) for a in (kk, kiki))
    cast_rows = _cast_rows_per_step(to_bf16, batch * nb)
    cast = tuple(to_bf16) if cast_rows else ()
    blk = lambda b, j: (b * nb + j, 0)
    seq = lambda b, j: (b, 0, 0)
    outs = pl.pallas_call(
        functools.partial(_dsa_prompt_kernel, topk=topk, n_blocks=nb, n_cast=len(cast)),
        grid=(batch, nb),
        in_specs=[pl.BlockSpec((DSA_Q, 512), blk),
                  pl.BlockSpec((DSA_Q, 512), blk),
                  pl.BlockSpec((DSA_Q, LANES), blk),
                  pl.BlockSpec((1, t_len, LANES), seq),
                  pl.BlockSpec((1, t_len, LANES), seq),
                  pl.BlockSpec((1, LANES, t_len), seq)]
                 + [pl.BlockSpec((r, a.shape[1]), blk) for r, a in zip(cast_rows or (), cast)],
        out_specs=[pl.BlockSpec((1, DSA_Q, 512), lambda b, j: (b, j, 0))]
                  + [pl.BlockSpec((r, a.shape[1]), blk) for r, a in zip(cast_rows or (), cast)],
        out_shape=[jax.ShapeDtypeStruct((batch, t_len, 512), BF16)]
                  + [jax.ShapeDtypeStruct(a.shape, BF16) for a in cast],
        scratch_shapes=[pltpu.VMEM((t_len, DSA_Q), F32), pltpu.VMEM((t_len, DSA_Q), F32)],
        compiler_params=_params("arbitrary", "arbitrary"),
        name="dsa_prompt",
    )(q, qi, wi, kk, kiki, vvt, *cast)
    done = list(outs[1:]) if cast else [a.astype(BF16) for a in to_bf16]
    return outs[0].reshape(batch * t_len, 512), done


def _mem_kv_kernel(m_ref, w_ref, k_ref, v_ref):
    z = _dot(m_ref[...].astype(BF16), w_ref[...])
    k_ref[...] = z[:, :MEM_WIDTH]
    v_ref[...] = z[:, MEM_WIDTH:]


def _mem_kv(mem, w_kv):
    n = mem.shape[0]
    tm = min(n, 512)
    row = lambda i: (i, 0)
    return pl.pallas_call(
        _mem_kv_kernel,
        grid=(n // tm,),
        in_specs=[pl.BlockSpec((tm, D_MODEL), row),
                  pl.BlockSpec((D_MODEL, 2 * MEM_WIDTH), lambda i: (0, 0))],
        out_specs=[pl.BlockSpec((tm, MEM_WIDTH), row), pl.BlockSpec((tm, MEM_WIDTH), row)],
        out_shape=[jax.ShapeDtypeStruct((n, MEM_WIDTH), F32)] * 2,
        compiler_params=_params("parallel"),
        name="mem_kv",
    )(mem, w_kv)


def _pool_project(window_sums, u, cnts, pw_ref, ps_ref):
    ys = []
    for g in range(len(POOL_WINDOWS)):
        sl = slice(g * POOL_CG, (g + 1) * POOL_CG)
        pooled = window_sums[g] / cnts[g] - u[:, sl]
        ys.append(_dot(pooled.astype(BF16), pw_ref[g]) * ps_ref[:, sl])
    return jnp.concatenate(ys, axis=1)


def _mid_prompt_kernel(x_ref, attn_ref, u_ref, halo_ref, mk_ref, mv_ref, pw_ref, ps_ref, wo_ref,
                       gm_ref, wmq_ref, wmo_ref, gf_ref, h2_ref, tok_ref, e_ref, *, nt, n_real):
    step = pl.program_id(0)

    @pl.when(step >= n_real)
    def _():
        h2_ref[...] = jnp.zeros_like(h2_ref)
        tok_ref[...] = jnp.zeros_like(tok_ref)

    @pl.when(step < n_real)
    def _():
        _mid_prompt_rows(x_ref, attn_ref, u_ref, halo_ref, mk_ref, mv_ref, pw_ref, ps_ref, wo_ref,
                         gm_ref, wmq_ref, wmo_ref, gf_ref, h2_ref, tok_ref, e_ref, step % nt)


def _mid_prompt_rows(x_ref, attn_ref, u_ref, halo_ref, mk_ref, mv_ref, pw_ref, ps_ref, wo_ref,
                     gm_ref, wmq_ref, wmo_ref, gf_ref, h2_ref, tok_ref, e_ref, i):
    tm = x_ref.shape[0]
    u = u_ref[...]
    halo = halo_ref[...]
    e_ref[0:POOL_MAX, :] = jnp.where(i == 0, jnp.zeros_like(halo), halo)
    e_ref[POOL_MAX:POOL_MAX + tm, :] = u
    pos = i * tm + lax.broadcasted_iota(jnp.int32, (tm, 1), 0)
    sums, cnts = [], []
    for g, w in enumerate(POOL_WINDOWS):
        sl = slice(g * POOL_CG, (g + 1) * POOL_CG)
        s = u[:, sl]
        for j in range(1, w):
            s = s + e_ref[POOL_MAX - j:POOL_MAX - j + tm, sl]
        sums.append(s)
        cnts.append(jnp.minimum(w, pos + 1).astype(F32))
    pool = _pool_project(sums, u, cnts, pw_ref, ps_ref)
    h1 = x_ref[...] + _dot(attn_ref[...], wo_ref[0:ATTN_WIDTH, :]) \
        + _dot(pool.astype(BF16), wo_ref[ATTN_WIDTH:, :])

    qm = _dot(_rms(h1, gm_ref[...]).astype(BF16), wmq_ref[...]).astype(BF16)
    mk = mk_ref[0].astype(BF16)
    mv = mv_ref[0].astype(BF16)
    outs = []
    for h in range(MEM_HEADS):
        sl = slice(h * MEM_HEAD_DIM, (h + 1) * MEM_HEAD_DIM)
        logits = _dot_nt(qm[:, sl], mk[:, sl]) * (MEM_HEAD_DIM ** -0.5)
        m = jnp.max(logits, axis=1, keepdims=True)
        p = jnp.exp(logits - m)
        p = p / jnp.sum(p, axis=1, keepdims=True)
        outs.append(_dot(p.astype(BF16), mv[:, sl]))
    o = jnp.concatenate(outs, axis=1)
    h2 = h1 + _dot(o.astype(BF16), wmo_ref[...])
    h2_ref[...] = h2
    tok_ref[...] = _rms(h2, gf_ref[...]).astype(BF16)


def _mid_prompt(x, attn, u, mk, mv, pool_w, pool_scale, w_o, g_mem, w_mq, w_mo, g_ffn, batch, t_len, tm, n_rows):
    nt = t_len // tm
    hb = tm // POOL_MAX
    n_real = batch * nt
    n_steps = -(-n_rows // tm)
    last = n_real - 1
    row = lambda s: (jnp.minimum(s, last), 0)
    halo = lambda s: (jnp.maximum(jnp.minimum(s, last) * hb - 1, 0), 0)
    fixed2 = lambda s: (0, 0)
    fixed3 = lambda s: (0, 0, 0)
    perb = lambda s: (jnp.minimum(s, last) // nt, 0, 0)
    out_row = lambda s: (s, 0)
    n = n_rows
    return pl.pallas_call(
        functools.partial(_mid_prompt_kernel, nt=nt, n_real=n_real),
        grid=(n_steps,),
        in_specs=[pl.BlockSpec((tm, D_MODEL), row),
                  pl.BlockSpec((tm, ATTN_WIDTH), row),
                  pl.BlockSpec((tm, POOL_WIDTH), row),
                  pl.BlockSpec((POOL_MAX, POOL_WIDTH), halo),
                  pl.BlockSpec((1, N_MEM, MEM_WIDTH), perb),
                  pl.BlockSpec((1, N_MEM, MEM_WIDTH), perb),
                  pl.BlockSpec((len(POOL_WINDOWS), POOL_CG, POOL_CG), fixed3),
                  pl.BlockSpec((1, POOL_WIDTH), fixed2),
                  pl.BlockSpec((D_MODEL, D_MODEL), fixed2),
                  pl.BlockSpec((1, D_MODEL), fixed2),
                  pl.BlockSpec((D_MODEL, MEM_WIDTH), fixed2),
                  pl.BlockSpec((MEM_WIDTH, D_MODEL), fixed2),
                  pl.BlockSpec((1, D_MODEL), fixed2)],
        out_specs=[pl.BlockSpec((tm, D_MODEL), out_row), pl.BlockSpec((tm, D_MODEL), out_row)],
        out_shape=[jax.ShapeDtypeStruct((n, D_MODEL), F32), jax.ShapeDtypeStruct((n, D_MODEL), BF16)],
        scratch_shapes=[pltpu.VMEM((POOL_MAX + tm, POOL_WIDTH), F32)],
        compiler_params=_params("arbitrary"),
        name="mid_prompt",
    )(x, attn, u, u, mk, mv, pool_w, pool_scale, w_o, g_mem, w_mq, w_mo, g_ffn)


def _router_kernel(tok_ref, rw_ref, rb_ref, rank_ref, gate_ref, cnt_ref, *, n_valid):
    tm = tok_ref.shape[0]
    token = pl.program_id(0) * tm + lax.broadcasted_iota(jnp.int32, (1, tm), 1)
    valid = token < n_valid
    logits = _dot_nt(rw_ref[...], tok_ref[...]) + rb_ref[...]
    eid = lax.broadcasted_iota(jnp.int32, (N_EXPERTS, tm), 0)
    vals, hots = [], []
    cur = logits
    for _ in range(TOP_K):
        m = jnp.max(cur, axis=0, keepdims=True)
        idx = jnp.min(jnp.where(cur == m, eid, N_EXPERTS), axis=0, keepdims=True)
        hot = eid == idx
        vals.append(m)
        hots.append(hot)
        cur = jnp.where(hot, -jnp.inf, cur)
    exps = [jnp.exp(v - vals[0]) for v in vals]
    den = exps[0] + exps[1] + exps[2] + exps[3]
    gate = jnp.zeros((N_EXPERTS, tm), F32)
    sel = jnp.zeros((N_EXPERTS, tm), jnp.bool_)
    for hot, ex in zip(hots, exps):
        gate = jnp.where(hot, ex / den, gate)
        sel = jnp.logical_or(sel, hot)
    sel = jnp.logical_and(sel, valid)
    gate_ref[0] = gate

    upper = _strict_triangle(LANES, lower=False)
    carry = jnp.zeros((N_EXPERTS, 1), F32)
    for c in range(tm // LANES):
        sl = slice(c * LANES, (c + 1) * LANES)
        sel_c = sel[:, sl]
        sel_b = jnp.where(sel_c, 1.0, 0.0).astype(BF16)
        rank = _dot(sel_b, upper) + carry
        rank_ref[0, :, sl] = jnp.where(sel_c, rank, -1.0).astype(jnp.int32)
        carry = carry + jnp.sum(sel_b.astype(F32), axis=1, keepdims=True)
    cnt_ref[0] = carry.astype(jnp.int32)


def _router(tok, rw_t, rb, tm, n_valid):
    n = tok.shape[0]
    nt = n // tm
    t3 = lambda i: (i, 0, 0)
    return pl.pallas_call(
        functools.partial(_router_kernel, n_valid=n_valid),
        grid=(nt,),
        in_specs=[pl.BlockSpec((tm, D_MODEL), lambda i: (i, 0)),
                  pl.BlockSpec((N_EXPERTS, D_MODEL), lambda i: (0, 0)),
                  pl.BlockSpec((N_EXPERTS, 1), lambda i: (0, 0))],
        out_specs=[pl.BlockSpec((1, N_EXPERTS, tm), t3),
                   pl.BlockSpec((1, N_EXPERTS, tm), t3),
                   pl.BlockSpec((1, N_EXPERTS, 1), t3)],
        out_shape=[jax.ShapeDtypeStruct((nt, N_EXPERTS, tm), jnp.int32),
                   jax.ShapeDtypeStruct((nt, N_EXPERTS, tm), F32),
                   jax.ShapeDtypeStruct((nt, N_EXPERTS, 1), jnp.int32)],
        compiler_params=_params("parallel"),
        name="router",
    )(tok, rw_t, rb)


def _swiglu(h):
    gate = jnp.minimum(h[:, :D_FF], SWIGLU_LIMIT)
    up = jnp.clip(h[:, D_FF:], -SWIGLU_LIMIT, SWIGLU_LIMIT)
    return gate * jax.nn.sigmoid(SWIGLU_ALPHA * gate) * (up + 1.0)


def _moe_kernel(cnt_ref, x_ref, h2_ref, rank_ref, gate_ref, wu_ref, bu_ref, wd_ref, bd_ref, gfin_ref,
                o_ref, os_ref, *, sample_row):
    i = pl.program_id(0)
    e = pl.program_id(1)
    tm = x_ref.shape[0]

    @pl.when(e == 0)
    def _():
        o_ref[...] = jnp.zeros_like(o_ref)

    n_routed = cnt_ref[i * N_EXPERTS + e]
    rank = rank_ref[0, pl.ds(e, 1), :]
    gate = gate_ref[0, pl.ds(e, 1), :]

    def run_slots(first, size):
        slot = first + lax.broadcasted_iota(jnp.int32, (size, tm), 0)
        hit = rank == slot
        onehot = jnp.where(hit, 1.0, 0.0).astype(BF16)
        xg = _dot(onehot, x_ref[...]).astype(BF16)
        act = _swiglu(_dot(xg, wu_ref[0]) + bu_ref[0])
        out = _dot(act.astype(BF16), wd_ref[0]) + bd_ref[0]
        g_slot = jnp.sum(jnp.where(hit, gate, 0.0), axis=1, keepdims=True)
        o_ref[...] += _dot_tn(onehot, (out * g_slot).astype(BF16))

    largest = MOE_FIRST_ROWS[-1]
    for lo, size in zip((0,) + MOE_FIRST_ROWS[:-1], MOE_FIRST_ROWS):
        fits = n_routed <= size if size < largest else n_routed > lo
        @pl.when(jnp.logical_and(n_routed > lo, fits))
        def _():
            run_slots(0, size)

    def body(c, carry):
        run_slots(largest + c * MOE_TAIL_ROWS, MOE_TAIL_ROWS)
        return carry

    lax.fori_loop(0, (jnp.maximum(n_routed - largest, 0) + MOE_TAIL_ROWS - 1) // MOE_TAIL_ROWS, body, 0)

    @pl.when(e == pl.num_programs(1) - 1)
    def _():
        o_ref[...] = _rms(h2_ref[...] + o_ref[...], gfin_ref[...])

        @pl.when(i == pl.num_programs(0) - 1)
        def _():
            os_ref[...] = o_ref[sample_row:sample_row + os_ref.shape[0], :]


def _moe(tok, h2, rank, gate, cnt, wu, bu, wd, bd, g_final, tm, n_prompt):
    nt = tok.shape[0] // tm
    tile = lambda i, e, c: (i, 0)
    tile3 = lambda i, e, c: (i, 0, 0)
    expert = lambda i, e, c: (e, 0, 0)
    grid_spec = pltpu.PrefetchScalarGridSpec(
        num_scalar_prefetch=1,
        grid=(nt, N_EXPERTS),
        in_specs=[pl.BlockSpec((tm, D_MODEL), tile),
                  pl.BlockSpec((tm, D_MODEL), tile),
                  pl.BlockSpec((1, N_EXPERTS, tm), tile3),
                  pl.BlockSpec((1, N_EXPERTS, tm), tile3),
                  pl.BlockSpec((1, D_MODEL, 2 * D_FF), expert),
                  pl.BlockSpec((1, 1, 2 * D_FF), expert),
                  pl.BlockSpec((1, D_FF, D_MODEL), expert),
                  pl.BlockSpec((1, 1, D_MODEL), expert),
                  pl.BlockSpec((1, D_MODEL), lambda i, e, c: (0, 0))],
        out_specs=[pl.BlockSpec((tm, D_MODEL), tile),
                   pl.BlockSpec((SAMPLE_ROWS, D_MODEL), lambda i, e, c: (0, 0))],
    )
    return pl.pallas_call(
        functools.partial(_moe_kernel, sample_row=n_prompt - (nt - 1) * tm),
        grid_spec=grid_spec,
        out_shape=[jax.ShapeDtypeStruct((n_prompt, D_MODEL), F32),
                   jax.ShapeDtypeStruct((SAMPLE_ROWS, D_MODEL), F32)],
        compiler_params=_params("arbitrary", "arbitrary"),
        name="moe",
    )(cnt, tok, h2, rank, gate, wu, bu, wd, bd, g_final)


def _page_copy(cache_ref, buf_ref, sem_ref, pt_ref, b, p, slot):
    dst = buf_ref.at[slot, :, pl.ds(pl.multiple_of(p * PAGE_SIZE, PAGE_SIZE), PAGE_SIZE)]
    return pltpu.make_async_copy(cache_ref.at[0, pt_ref[b, p]], dst, sem_ref.at[slot])


def _fetch_pages(cache_ref, buf_ref, sem_ref, pt_ref, b, slot, n_pages):
    def body(p, c):
        _page_copy(cache_ref, buf_ref, sem_ref, pt_ref, b, p, slot).start()
        return c
    lax.fori_loop(0, n_pages, body, 0, unroll=8)


def _wait_pages(cache_ref, buf_ref, sem_ref, pt_ref, b, slot, n_pages):
    def body(p, c):
        _page_copy(cache_ref, buf_ref, sem_ref, pt_ref, b, p, slot).wait()
        return c
    lax.fori_loop(0, n_pages, body, 0, unroll=8)


def _dsa_sample_score_kernel(pt_ref, qi_ref, wi_ref, kin_ref, cache_ref, score_ref, new_ref, buf_ref, sem_ref):
    b = pl.program_id(0)
    nb = pl.num_programs(0)
    past = buf_ref.shape[2]
    n_pages = past // PAGE_SIZE
    slot = b % 2

    @pl.when(b == 0)
    def _():
        _fetch_pages(cache_ref, buf_ref, sem_ref, pt_ref, b, slot, n_pages)

    @pl.when(b + 1 < nb)
    def _():
        _fetch_pages(cache_ref, buf_ref, sem_ref, pt_ref, b + 1, 1 - slot, n_pages)

    _wait_pages(cache_ref, buf_ref, sem_ref, pt_ref, b, slot, n_pages)

    qi = qi_ref[0]
    w = wi_ref[0] * (IDX_HEADS ** -0.5)
    keys_t = buf_ref[slot].astype(BF16)
    score_ref[0] = jnp.sum(jnp.maximum(_dot(qi, keys_t), 0.0) * w, axis=0, keepdims=True)
    s_new = jnp.sum(qi.astype(F32) * kin_ref[0].astype(F32), axis=1, keepdims=True)
    new_ref[0] = jnp.sum(jnp.maximum(s_new, 0.0) * w, axis=0, keepdims=True)


def _dsa_sample_select_kernel(score_ref, new_ref, mask_ref, take_new_ref, *, topk):
    score = score_ref[...]
    score_new = new_ref[...]
    past = score.shape[1]

    def count_ge(cand):
        return (jnp.sum(jnp.where(score >= cand, 1, 0), axis=1, keepdims=True)
                + jnp.where(score_new >= cand, 1, 0))

    tau = _kth_largest(count_ge, topk, score_new.shape)
    gt = score > tau
    eq = score == tau
    n_gt = jnp.sum(jnp.where(gt, 1, 0), axis=1, keepdims=True) + jnp.where(score_new > tau, 1, 0)
    n_eq_all = jnp.sum(jnp.where(eq, 1, 0), axis=1, keepdims=True) + jnp.where(score_new == tau, 1, 0)
    tied = n_gt + n_eq_all > topk
    need_f = jnp.where(tied, topk - n_gt, past + 2).astype(F32)
    take_eq, n_eq = _take_first_ties(eq, need_f, tied, axis=1)
    mask_ref[...] = jnp.where(jnp.logical_or(gt, take_eq > 0.5), 0.0, NEG)
    take_new = jnp.logical_or(score_new > tau, jnp.logical_and(score_new == tau, n_eq < need_f))
    take_new_ref[...] = jnp.where(take_new, 0.0, NEG)


def _dsa_sample_select(score, score_new, topk):
    db, past = score.shape
    full = lambda shape: pl.BlockSpec(shape, lambda i: (0, 0))
    return pl.pallas_call(
        functools.partial(_dsa_sample_select_kernel, topk=topk),
        grid=(1,),
        in_specs=[full((db, past)), full((db, 1))],
        out_specs=[full((db, past)), full((db, 1))],
        out_shape=[jax.ShapeDtypeStruct((db, past), F32), jax.ShapeDtypeStruct((db, 1), F32)],
        compiler_params=_params("arbitrary"),
        name="dsa_sample_select",
    )(score, score_new)


def _dsa_sample_mask(page_table, qi_heads, wi_col, ki_new, cache_idx_k):
    db, n_pages = page_table.shape
    past = n_pages * PAGE_SIZE
    topk = min(TOPK_MAX, (past + 1) // 4)
    per = lambda b, pt: (b, 0, 0)
    grid_spec = pltpu.PrefetchScalarGridSpec(
        num_scalar_prefetch=1,
        grid=(db,),
        in_specs=[pl.BlockSpec((1, IDX_HEADS, IDX_DIM), per),
                  pl.BlockSpec((1, IDX_HEADS, 1), per),
                  pl.BlockSpec((1, 1, IDX_DIM), per),
                  pl.BlockSpec(memory_space=pl.ANY)],
        out_specs=[pl.BlockSpec((1, 1, past), per), pl.BlockSpec((1, 1, 1), per)],
        scratch_shapes=[pltpu.VMEM((2, IDX_DIM, past), F32),
                        pltpu.SemaphoreType.DMA((2,))],
    )
    score, score_new = pl.pallas_call(
        _dsa_sample_score_kernel,
        grid_spec=grid_spec,
        out_shape=[jax.ShapeDtypeStruct((db, 1, past), F32), jax.ShapeDtypeStruct((db, 1, 1), F32)],
        compiler_params=_params("arbitrary"),
        name="dsa_sample_score",
    )(page_table, qi_heads, wi_col, ki_new, cache_idx_k)
    mask, take_new = _dsa_sample_select(score.reshape(db, past), score_new.reshape(db, 1), topk)
    return mask.reshape(db, 1, past), take_new.reshape(db, 1, 1)


def _dsa_sample_attend_kernel(pt_ref, q_ref, kn_ref, vn_ref, mask_ref, new_ref, ck_ref, cv_ref, o_ref,
                              kbuf_ref, vbuf_ref, ksem_ref, vsem_ref):
    b = pl.program_id(0)
    nb = pl.num_programs(0)
    past = kbuf_ref.shape[2]
    n_pages = past // PAGE_SIZE
    slot = b % 2

    def fetch(bb, sl):
        _fetch_pages(ck_ref, kbuf_ref, ksem_ref, pt_ref, bb, sl, n_pages)
        _fetch_pages(cv_ref, vbuf_ref, vsem_ref, pt_ref, bb, sl, n_pages)

    @pl.when(b == 0)
    def _():
        fetch(b, slot)

    @pl.when(b + 1 < nb)
    def _():
        fetch(b + 1, 1 - slot)

    _wait_pages(ck_ref, kbuf_ref, ksem_ref, pt_ref, b, slot, n_pages)
    _wait_pages(cv_ref, vbuf_ref, vsem_ref, pt_ref, b, slot, n_pages)

    q = q_ref[0]
    keys_t = kbuf_ref[slot].astype(BF16)
    vals_t = vbuf_ref[slot].astype(BF16)
    logits = _dot(q, keys_t) + mask_ref[0]
    l_new = jnp.sum(q.astype(F32) * kn_ref[0].astype(F32), axis=1, keepdims=True) + new_ref[0]
    m = jnp.maximum(jnp.max(logits, axis=1, keepdims=True), l_new)
    p = jnp.exp(logits - m)
    p_new = jnp.exp(l_new - m)
    den = jnp.sum(p, axis=1, keepdims=True) + p_new
    p = p / den
    p_new = p_new / den
    o_ref[0] = _dot_nt(p.astype(BF16), vals_t) + p_new * vn_ref[0].astype(F32)


def _dsa_sample_attend(page_table, q_heads, k_new, v_new, mask, take_new, cache_k, cache_v):
    db, n_pages = page_table.shape
    past = n_pages * PAGE_SIZE
    per = lambda b, pt: (b, 0, 0)
    grid_spec = pltpu.PrefetchScalarGridSpec(
        num_scalar_prefetch=1,
        grid=(db,),
        in_specs=[pl.BlockSpec((1, ATTN_HEADS, HEAD_DIM), per),
                  pl.BlockSpec((1, 1, HEAD_DIM), per),
                  pl.BlockSpec((1, 1, HEAD_DIM), per),
                  pl.BlockSpec((1, 1, past), per),
                  pl.BlockSpec((1, 1, 1), per),
                  pl.BlockSpec(memory_space=pl.ANY),
                  pl.BlockSpec(memory_space=pl.ANY)],
        out_specs=pl.BlockSpec((1, ATTN_HEADS, HEAD_DIM), per),
        scratch_shapes=[pltpu.VMEM((2, HEAD_DIM, past), F32),
                        pltpu.VMEM((2, HEAD_DIM, past), F32),
                        pltpu.SemaphoreType.DMA((2,)),
                        pltpu.SemaphoreType.DMA((2,))],
    )
    return pl.pallas_call(
        _dsa_sample_attend_kernel,
        grid_spec=grid_spec,
        out_shape=jax.ShapeDtypeStruct((db, ATTN_HEADS, HEAD_DIM), F32),
        compiler_params=_params("arbitrary"),
        name="dsa_sample_attend",
    )(page_table, q_heads, k_new, v_new, mask, take_new, cache_k, cache_v)


def _mid_sample_a_kernel(x_ref, attn_ref, u_ref, st_ref, pw_ref, ps_ref, wo_ref, gm_ref, wmq_ref,
                         h1_ref, qm_ref, *, past):
    u = u_ref[...]
    rows = u.shape[0]
    sums, cnts = [], []
    for g, w in enumerate(POOL_WINDOWS):
        sl = slice(g * POOL_CG, (g + 1) * POOL_CG)
        s = u[:, sl]
        for j in range(1, w):
            s = s + st_ref[POOL_MAX - 1 - j][:, sl]
        sums.append(s)
        cnts.append(jnp.full((rows, 1), float(min(w, past + 1)), F32))
    pool = _pool_project(sums, u, cnts, pw_ref, ps_ref)
    h1 = x_ref[...] + _dot(attn_ref[...], wo_ref[0:ATTN_WIDTH, :]) \
        + _dot(pool.astype(BF16), wo_ref[ATTN_WIDTH:, :])
    h1_ref[...] = h1
    qm_ref[...] = _dot(_rms(h1, gm_ref[...]).astype(BF16), wmq_ref[...]).astype(BF16)


def _mid_sample_a(x, attn, u, state_t, pool_w, pool_scale, w_o, g_mem, w_mq, past):
    db = x.shape[0]
    full = lambda a: pl.BlockSpec(a.shape, lambda i, nd=a.ndim: (0,) * nd)
    args = (x, attn, u, state_t, pool_w, pool_scale, w_o, g_mem, w_mq)
    return pl.pallas_call(
        functools.partial(_mid_sample_a_kernel, past=past),
        grid=(1,),
        in_specs=[full(a) for a in args],
        out_specs=[pl.BlockSpec((db, D_MODEL), lambda i: (0, 0)),
                   pl.BlockSpec((db, MEM_WIDTH), lambda i: (0, 0))],
        out_shape=[jax.ShapeDtypeStruct((db, D_MODEL), F32), jax.ShapeDtypeStruct((db, MEM_WIDTH), BF16)],
        compiler_params=_params("arbitrary"),
        name="mid_sample_a",
    )(*args)


def _mem_sample_kernel(q_ref, mk_ref, mv_ref, o_ref):
    rows = 2 * MEM_HEADS
    n = mk_ref.shape[1]
    q_all = q_ref[0].astype(F32)
    q = jnp.concatenate([q_all[:, (r % MEM_HEADS) * MEM_HEAD_DIM:(r % MEM_HEADS + 1) * MEM_HEAD_DIM]
                         for r in range(rows)], axis=0).astype(BF16)
    logits = _dot_nt(q, mk_ref[0].astype(BF16)) * (MEM_HEAD_DIM ** -0.5)
    own = (lax.broadcasted_iota(jnp.int32, (rows, n), 1) % MEM_HEADS
           == lax.broadcasted_iota(jnp.int32, (rows, n), 0) % MEM_HEADS)
    logits = jnp.where(own, logits, -jnp.inf)
    m = jnp.max(logits, axis=1, keepdims=True)
    p = jnp.exp(logits - m)
    p = p / jnp.sum(p, axis=1, keepdims=True)
    o = _dot(p.astype(BF16), mv_ref[0].astype(BF16))
    for h in range(MEM_HEADS):
        o_ref[0, :, h * MEM_HEAD_DIM:(h + 1) * MEM_HEAD_DIM] = o[h:h + 1, :]


def _mem_sample(qm, mk, mv):
    db = qm.shape[0]
    per = lambda b: (b, 0, 0)
    return pl.pallas_call(
        _mem_sample_kernel,
        grid=(db,),
        in_specs=[pl.BlockSpec((1, 1, MEM_WIDTH), per),
                  pl.BlockSpec((1, N_MEM * MEM_HEADS, MEM_HEAD_DIM), per),
                  pl.BlockSpec((1, N_MEM * MEM_HEADS, MEM_HEAD_DIM), per)],
        out_specs=pl.BlockSpec((1, 1, MEM_WIDTH), per),
        out_shape=jax.ShapeDtypeStruct((db, 1, MEM_WIDTH), F32),
        compiler_params=_params("parallel"),
        name="mem_sample",
    )(qm.reshape(db, 1, MEM_WIDTH), mk, mv)


def _sample_join_kernel(h1_ref, o_ref, wmo_ref, gf_ref, h2_any, tok_any, h2_ref, tok_ref):
    del h2_any, tok_any
    db = h1_ref.shape[0]
    h2_ref[...] = jnp.zeros_like(h2_ref)
    tok_ref[...] = jnp.zeros_like(tok_ref)

    @pl.when(pl.program_id(0) == 0)
    def _():
        h2 = h1_ref[...] + _dot(o_ref[...].astype(BF16), wmo_ref[...])
        h2_ref[0:db, :] = h2
        tok_ref[0:db, :] = _rms(h2, gf_ref[...]).astype(BF16)


def _sample_join(h1, o, w_mo, g_ffn, h2_all, tok_all, n_prompt):
    db = h1.shape[0]
    n_rows = h2_all.shape[0]
    assert db <= SAMPLE_ROWS and n_prompt % SAMPLE_ROWS == 0 and n_rows % SAMPLE_ROWS == 0
    first = n_prompt // SAMPLE_ROWS
    f2 = lambda j: (0, 0)
    tail = lambda j: (first + j, 0)
    return pl.pallas_call(
        _sample_join_kernel,
        grid=((n_rows - n_prompt) // SAMPLE_ROWS,),
        in_specs=[pl.BlockSpec((db, D_MODEL), f2),
                  pl.BlockSpec((db, MEM_WIDTH), f2),
                  pl.BlockSpec((MEM_WIDTH, D_MODEL), f2),
                  pl.BlockSpec((1, D_MODEL), f2),
                  pl.BlockSpec(memory_space=pl.ANY),
                  pl.BlockSpec(memory_space=pl.ANY)],
        out_specs=[pl.BlockSpec((SAMPLE_ROWS, D_MODEL), tail), pl.BlockSpec((SAMPLE_ROWS, D_MODEL), tail)],
        out_shape=[jax.ShapeDtypeStruct(h2_all.shape, F32), jax.ShapeDtypeStruct(tok_all.shape, BF16)],
        input_output_aliases={4: 0, 5: 1},
        compiler_params=_params("arbitrary"),
        name="sample_join",
    )(h1, o, w_mo, g_ffn, h2_all, tok_all)


def _rope_tables(pos):
    half = HEAD_DIM // 2
    inv = ROPE_THETA ** (-jnp.arange(half, dtype=F32) / half)
    ang = pos.astype(F32)[:, None] * inv[None, :]
    cos = jnp.cos(ang)
    sin = jnp.sin(ang)
    return jnp.tile(cos, (1, 4)), jnp.concatenate([-sin, sin, -sin, sin], axis=1)


def _pick_tile(n, pref):
    t = min(n, pref)
    while n % t:
        t //= 2
    return t


def kernel(x_prompt, x_sample, mem_prompt, cache_k, cache_v, cache_idx_k, state_pool, cache_mem_k, cache_mem_v, page_table, norm_mix, w_in, pool_w, pool_scale, w_o, norm_mem, w_mq, w_mk, w_mv, w_mo, norm_ffn, router_w, router_b, w_up, b_up, w_down, b_down, norm_final):
    B, T = x_prompt.shape[:2]
    DB, DS = x_sample.shape[:2]
    assert DS == 1 and w_in.shape[0] == 1
    n_pages = page_table.shape[1]
    past = n_pages * PAGE_SIZE
    NP = B * T

    wi_ = w_in[0]
    wq, wk, wv, wu_, wqi, wki, wwi = jnp.split(wi_, [512, 576, 640, 1152, 1664, 1728], axis=1)
    w_all = jnp.concatenate(
        [wq, wqi, wu_, wk, wk, wki, wki, wv, wv, wwi, jnp.zeros((D_MODEL, LANES - IDX_HEADS), F32)],
        axis=1).astype(BF16)
    g_mix = norm_mix[0][None, :]
    g_mem = norm_mem[0][None, :]
    g_ffn = norm_ffn[0][None, :]
    g_fin = norm_final[None, :]
    pw = pool_w[0].astype(BF16)
    ps = pool_scale[0][None, :]
    wo = w_o[0].astype(BF16)
    wmq = w_mq[0].reshape(D_MODEL, MEM_WIDTH).astype(BF16)
    wkv = jnp.concatenate([w_mk[0].reshape(D_MODEL, MEM_WIDTH), w_mv[0].reshape(D_MODEL, MEM_WIDTH)],
                          axis=1).astype(BF16)
    wmo = w_mo[0].reshape(MEM_WIDTH, D_MODEL).astype(BF16)
    rw = router_w[0].astype(BF16)
    rb = router_b[0]
    bup = b_up[0][:, None, :]
    bdn = b_down[0][:, None, :]

    xp = x_prompt.reshape(NP, D_MODEL)
    tm_in = _pick_tile(T, 512)
    cos_p, sin_p = _rope_tables(jnp.arange(T, dtype=jnp.int32))
    q, qi, u, kk, kiki, vvt, k_pt, ki_pt, v_pt, wi_p = _in_project(xp, g_mix, w_all, cos_p, sin_p, tm_in,
                                                                   T // tm_in, batch=B)
    attn, (wup, wdn) = _dsa_prompt(q, qi, wi_p, kk, kiki, vvt, B, T,
                                   to_bf16=(w_up.reshape(N_EXPERTS * D_MODEL, 2 * D_FF),
                                            w_down.reshape(N_EXPERTS * D_FF, D_MODEL)))
    wup = wup.reshape(N_EXPERTS, D_MODEL, 2 * D_FF)
    wdn = wdn.reshape(N_EXPERTS, D_FF, D_MODEL)
    mk_p, mv_p = _mem_kv(mem_prompt.reshape(B * N_MEM, D_MODEL), wkv)
    tm_moe = min(MOE_TILE, NP)
    n_tiles = -(-(NP + SAMPLE_ROWS) // tm_moe)
    assert (n_tiles - 1) * tm_moe < NP and NP + SAMPLE_ROWS <= n_tiles * tm_moe
    h2_all, tok_all = _mid_prompt(xp, attn, u, mk_p.reshape(B, N_MEM, MEM_WIDTH), mv_p.reshape(B, N_MEM, MEM_WIDTH),
                                  pw, ps, wo, g_mem, wmq, wmo, g_ffn, B, T, _pick_tile(T, 1024), n_tiles * tm_moe)

    xs = x_sample.reshape(DB, D_MODEL)
    cos_s, sin_s = _rope_tables(jnp.full((DB,), past, jnp.int32))
    q_s, qi_s, u_s, kk_s, kiki_s, vv_s, k_s, ki_s, v_s, wi_s = _in_project(xs, g_mix, w_all, cos_s, sin_s, DB, 1)
    mask, take_new = _dsa_sample_mask(
        page_table, qi_s.reshape(DB, IDX_HEADS, IDX_DIM), wi_s[:, :IDX_HEADS, None],
        kiki_s[:, None, :IDX_DIM], jnp.swapaxes(cache_idx_k, 2, 3))
    attn_s = _dsa_sample_attend(
        page_table, q_s.reshape(DB, ATTN_HEADS, HEAD_DIM), kk_s[:, None, :HEAD_DIM], vv_s[:, None, :HEAD_DIM],
        mask, take_new, jnp.swapaxes(cache_k, 2, 3), jnp.swapaxes(cache_v, 2, 3))
    attn_s = attn_s.reshape(DB, ATTN_WIDTH).astype(BF16)
    h1_s, qm_s = _mid_sample_a(xs, attn_s, u_s, jnp.transpose(state_pool[0], (1, 0, 2)), pw, ps, wo, g_mem, wmq, past)
    o_s = _mem_sample(qm_s, cache_mem_k.reshape(DB, N_MEM * MEM_HEADS, MEM_HEAD_DIM),
                      cache_mem_v.reshape(DB, N_MEM * MEM_HEADS, MEM_HEAD_DIM))
    h2_all, tok_all = _sample_join(h1_s, o_s.reshape(DB, MEM_WIDTH), wmo, g_ffn, h2_all, tok_all, NP)

    rank, gate, cnt = _router(tok_all, rw.T, rb[:, None], tm_moe, NP + DB)
    y_p, y_s = _moe(tok_all, h2_all, rank, gate, cnt.reshape(-1), wup, bup, wdn, bdn, g_fin, tm_moe, NP)
    y_prompt = y_p.reshape(B, T, D_MODEL)
    y_sample = y_s[:DB].reshape(DB, 1, D_MODEL)

    new_pool_p = u.reshape(B, T, POOL_WIDTH)[:, T - (POOL_MAX - 1):][None]
    new_pool_s = jnp.concatenate([state_pool[0][:, 1:], u_s[:, None, :]], axis=1)[None]
    return (y_prompt, y_sample,
            jnp.swapaxes(k_pt, 1, 2)[None], jnp.swapaxes(v_pt, 1, 2)[None], jnp.swapaxes(ki_pt, 1, 2)[None],
            new_pool_p,
            mk_p.reshape(1, B, N_MEM, MEM_HEADS, MEM_HEAD_DIM), mv_p.reshape(1, B, N_MEM, MEM_HEADS, MEM_HEAD_DIM),
            k_s.reshape(1, DB, 1, HEAD_DIM), v_s.reshape(1, DB, 1, HEAD_DIM), ki_s.reshape(1, DB, 1, IDX_DIM),
            new_pool_s)
```

```python
import functools

import jax
import jax.numpy as jnp
from jax import lax
from jax.experimental import pallas as pl
from jax.experimental.pallas import tpu as pltpu

D_MODEL = 1024
HEAD_DIM = 64
ATTN_WIDTH = 512
ATTN_HEADS = 8
IDX_HEADS = 8
IDX_DIM = 64
TOPK_MAX = 256
Q_BLOCK = 128
PAGE_SIZE = 128
POOL_WIDTH = 512
POOL_WINDOWS = (2, 4, 8, 16)
POOL_CG = 128
POOL_MAX = 16
N_MEM = 256
MEM_HEADS = 4
MEM_HEAD_DIM = 128
MEM_WIDTH = MEM_HEADS * MEM_HEAD_DIM
N_EXPERTS = 32
TOP_K = 4
D_FF = 1024
SWIGLU_LIMIT = 7.0
SWIGLU_ALPHA = 1.702
ROPE_THETA = 10000.0
EPS = 1e-6
NEG = -1e30

LANES = 128
MOE_TILE = 13 * LANES
MOE_FIRST_ROWS = (208, 224, 256)
MOE_TAIL_ROWS = 128
SAMPLE_ROWS = 128
DSA_Q = 256
DSA_KCHUNK = 256
IN_COLS = 3 * 512 + 4 * LANES
VMEM_LIMIT = 56 * 1024 * 1024

BF16 = jnp.bfloat16
F32 = jnp.float32


def _dot(a, b):
    return jnp.dot(a, b, preferred_element_type=F32)


def _dot_nt(a, b):
    return lax.dot_general(a, b, (((1,), (1,)), ((), ())), preferred_element_type=F32)


def _dot_tn(a, b):
    return lax.dot_general(a, b, (((0,), (0,)), ((), ())), preferred_element_type=F32)


def _rms(x, g):
    return x * lax.rsqrt(jnp.mean(x * x, axis=-1, keepdims=True) + EPS) * g


def _params(*sem):
    return pltpu.CompilerParams(dimension_semantics=sem, vmem_limit_bytes=VMEM_LIMIT)


def _in_kernel(x_ref, g_ref, w_ref, cos_ref, sin_ref,
               q_ref, qi_ref, u_ref, kk_ref, kiki_ref, vv_ref, k_ref, ki_ref, v_ref, wi_ref, *, feature_major):
    xn = _rms(x_ref[...], g_ref[...]).astype(BF16)
    z = _dot(xn, w_ref[...])
    cos = cos_ref[...]
    sin = sin_ref[...]
    lane = lax.broadcasted_iota(jnp.int32, cos.shape, 1)
    first_half = (lane % HEAD_DIM) < (HEAD_DIM // 2)

    def rope(zc):
        swapped = jnp.where(first_half,
                            pltpu.roll(zc, LANES - HEAD_DIM // 2, 1),
                            pltpu.roll(zc, HEAD_DIM // 2, 1))
        return zc * cos + swapped * sin

    for c in range(4):
        q_c = rope(z[:, c * LANES:(c + 1) * LANES]) * (HEAD_DIM ** -0.5)
        qi_c = rope(z[:, 512 + c * LANES:512 + (c + 1) * LANES]) * (IDX_DIM ** -0.5)
        q_ref[:, c * LANES:(c + 1) * LANES] = q_c.astype(BF16)
        qi_ref[:, c * LANES:(c + 1) * LANES] = qi_c.astype(BF16)
    u_ref[...] = z[:, 1024:1536]
    kk = rope(z[:, 1536:1664])
    kiki = rope(z[:, 1664:1792])
    vv = z[:, 1792:1920]
    kk_ref[...] = kk.astype(BF16)
    kiki_ref[...] = kiki.astype(BF16)
    wi_ref[...] = z[:, 1920:2048]
    if feature_major:
        kk_t = jnp.transpose(kk)
        kiki_t = jnp.transpose(kiki)
        vv_t = jnp.transpose(vv)
        vv_ref[0] = vv_t.astype(BF16)
        k_ref[0] = kk_t[0:HEAD_DIM, :]
        ki_ref[0] = kiki_t[0:IDX_DIM, :]
        v_ref[0] = vv_t[0:HEAD_DIM, :]
    else:
        vv_ref[...] = vv.astype(BF16)
        k_ref[...] = kk[:, :HEAD_DIM]
        ki_ref[...] = kiki[:, :IDX_DIM]
        v_ref[...] = vv[:, :HEAD_DIM]


def _in_project(x, g, w_all, cos, sin, tm, table_blocks, batch=None):
    n = x.shape[0]
    row = lambda i: (i, 0)
    tab = lambda i: (i % table_blocks, 0)
    fixed = lambda i: (0, 0)
    feature_major = batch is not None
    if feature_major:
        t_len = n // batch
        assert t_len == table_blocks * tm
        fm = lambda feat, dt: jax.ShapeDtypeStruct((batch, feat, t_len), dt)
        fm_spec = lambda feat: pl.BlockSpec((1, feat, tm), lambda i: (i // table_blocks, 0, i % table_blocks))
    else:
        fm = lambda feat, dt: jax.ShapeDtypeStruct((n, feat), dt)
        fm_spec = lambda feat: pl.BlockSpec((tm, feat), row)
    outs = [
        jax.ShapeDtypeStruct((n, 512), BF16),
        jax.ShapeDtypeStruct((n, 512), BF16),
        jax.ShapeDtypeStruct((n, 512), F32),
        jax.ShapeDtypeStruct((n, LANES), BF16),
        jax.ShapeDtypeStruct((n, LANES), BF16),
        fm(LANES, BF16),
        fm(HEAD_DIM, F32),
        fm(IDX_DIM, F32),
        fm(HEAD_DIM, F32),
        jax.ShapeDtypeStruct((n, LANES), F32),
    ]
    out_specs = ([pl.BlockSpec((tm, s.shape[1]), row) for s in outs[:5]]
                 + [fm_spec(LANES), fm_spec(HEAD_DIM), fm_spec(IDX_DIM), fm_spec(HEAD_DIM)]
                 + [pl.BlockSpec((tm, LANES), row)])
    return pl.pallas_call(
        functools.partial(_in_kernel, feature_major=feature_major),
        grid=(n // tm,),
        in_specs=[pl.BlockSpec((tm, D_MODEL), row),
                  pl.BlockSpec((1, D_MODEL), fixed),
                  pl.BlockSpec((D_MODEL, IN_COLS), fixed),
                  pl.BlockSpec((tm, LANES), tab),
                  pl.BlockSpec((tm, LANES), tab)],
        out_specs=out_specs,
        out_shape=outs,
        compiler_params=_params("parallel"),
        name="in_project",
    )(x, g, w_all, cos, sin)


def _f32_to_key(x):
    bits = lax.bitcast_convert_type(x, jnp.int32)
    return jnp.where(bits < 0, bits ^ jnp.int32(0x7FFFFFFF), bits)


def _key_to_f32(key):
    bits = jnp.where(key < 0, key ^ jnp.int32(0x7FFFFFFF), key)
    return lax.bitcast_convert_type(bits, F32)


def _kth_largest(count_ge, k, shape):
    def body(it, tau):
        cand = tau + jnp.left_shift(jnp.int32(1), jnp.int32(31) - it)
        return jnp.where(count_ge(_key_to_f32(cand)) >= k, cand, tau)

    return _key_to_f32(lax.fori_loop(0, 32, body, jnp.full(shape, -2 ** 31, jnp.int32)))


def _kth_largest_between(count_ge, k, lo, hi):
    key_lo = _f32_to_key(lo)
    span = _f32_to_key(hi) - key_lo
    n_bits = jnp.max(32 - lax.clz(span))
    flip = jnp.int32(-2 ** 31)

    def body(it, off):
        cand = off | jnp.left_shift(jnp.int32(1), n_bits - 1 - it)
        ok = jnp.logical_and(count_ge(_key_to_f32(key_lo + cand)) >= k, (cand ^ flip) <= (span ^ flip))
        return jnp.where(ok, cand, off)

    return _key_to_f32(key_lo + lax.fori_loop(0, n_bits, body, jnp.zeros_like(key_lo)))


def _strict_triangle(n, lower):
    r = lax.broadcasted_iota(jnp.int32, (n, n), 0)
    c = lax.broadcasted_iota(jnp.int32, (n, n), 1)
    return jnp.where((r > c) if lower else (r < c), 1.0, 0.0).astype(BF16)


def _take_first_ties(eq, need_f, tied, axis):
    n = eq.shape[axis]
    cshape = (1, eq.shape[1]) if axis == 0 else (eq.shape[0], 1)

    def ranked():
        tri = _strict_triangle(LANES, lower=(axis == 0))
        carry = jnp.zeros(cshape, F32)
        parts = []
        for c in range(n // LANES):
            sl = slice(c * LANES, (c + 1) * LANES)
            eq_f = jnp.where(eq[sl, :] if axis == 0 else eq[:, sl], 1.0, 0.0)
            eq_b = eq_f.astype(BF16)
            rank = (_dot(tri, eq_b) if axis == 0 else _dot(eq_b, tri)) + carry
            parts.append(jnp.where(rank < need_f, eq_f, 0.0))
            carry = carry + jnp.sum(eq_f, axis=axis, keepdims=True)
        return jnp.concatenate(parts, axis=axis), carry

    def plain():
        return jnp.where(eq, 1.0, 0.0), jnp.zeros(cshape, F32)

    return lax.cond(jnp.max(jnp.where(tied, 1, 0)) > 0, ranked, plain)


def _dsa_prompt_kernel(q_ref, qi_ref, wi_ref, kk_ref, kiki_ref, vvt_ref, o_ref, sm_ref, bias_ref, *, topk, block):
    i = block
    qb = q_ref.shape[0]
    lk = kk_ref.shape[1]
    select = lk > topk
    ck = min(DSA_KCHUNK, lk)
    chunks = [slice(c * ck, (c + 1) * ck) for c in range(lk // ck)]
    upper_rows = lax.broadcasted_iota(jnp.int32, (LANES, qb), 0) < HEAD_DIM

    def head_operand(ref, h):
        pair_t = jnp.transpose(ref[:, (h // 2) * LANES:(h // 2 + 1) * LANES].astype(F32))
        keep = upper_rows if h % 2 == 0 else jnp.logical_not(upper_rows)
        return jnp.where(keep, pair_t, 0.0).astype(BF16)

    pos_q = i * qb + lax.broadcasted_iota(jnp.int32, (1, qb), 1)

    def causal(sl):
        return sl.start + lax.broadcasted_iota(jnp.int32, (ck, 1), 0) <= pos_q

    if select:
        w_t = jnp.transpose(wi_ref[...])[0:IDX_HEADS, :] * (IDX_HEADS ** -0.5)
        qi_ops = [head_operand(qi_ref, h) for h in range(IDX_HEADS)]
        for sl in chunks:
            keys = kiki_ref[0, sl, :]
            score = jnp.zeros((ck, qb), F32)
            for h in range(IDX_HEADS):
                score = score + jnp.maximum(_dot(keys, qi_ops[h]), 0.0) * w_t[h:h + 1, :]
            sm_ref[sl, :] = jnp.where(causal(sl), score, -jnp.inf)

        def count_ge(cand):
            return jnp.sum(jnp.where(sm_ref[0:lk, :] >= cand, 1, 0), axis=0, keepdims=True)

        if ck == topk:
            class_max = sm_ref[chunks[0], :]
            for sl in chunks[1:]:
                class_max = jnp.maximum(class_max, sm_ref[sl, :])
            tau = _kth_largest_between(count_ge, topk, jnp.min(class_max, axis=0, keepdims=True),
                                       jnp.max(class_max, axis=0, keepdims=True))
        else:
            tau = _kth_largest(count_ge, topk, (1, qb))
        sm = sm_ref[0:lk, :]
        gt = sm > tau
        eq = sm == tau
        n_gt = jnp.sum(jnp.where(gt, 1, 0), axis=0, keepdims=True)
        n_eq = jnp.sum(jnp.where(eq, 1, 0), axis=0, keepdims=True)
        tied = n_gt + n_eq > topk
        need_f = jnp.where(tied, topk - n_gt, lk + 1).astype(F32)
        take_eq, _ = _take_first_ties(eq, need_f, tied, axis=0)
        bias_ref[0:lk, :] = jnp.where(jnp.logical_or(gt, take_eq > 0.5), 0.0, NEG)

    def chunk_bias(sl):
        if select:
            return bias_ref[sl, :]
        return jnp.where(causal(sl), 0.0, NEG)

    def fold8(x, op):
        return op(x.reshape(ck // 8, 8, qb), axis=0)

    for j in range(ATTN_HEADS // 2):
        outs = []
        for h in (2 * j, 2 * j + 1):
            q_op = head_operand(q_ref, h)
            mx8 = jnp.full((8, qb), -jnp.inf, F32)
            for sl in chunks:
                logits = _dot(kk_ref[0, sl, :], q_op) + chunk_bias(sl)
                sm_ref[sl, :] = logits
                mx8 = jnp.maximum(mx8, fold8(logits, jnp.max))
            m = jnp.max(mx8, axis=0, keepdims=True)
            den8 = jnp.zeros((8, qb), F32)
            acc = jnp.zeros((LANES, qb), F32)
            for sl in chunks:
                p = jnp.exp(sm_ref[sl, :] - m)
                den8 = den8 + fold8(p, jnp.sum)
                acc = acc + _dot(vvt_ref[0, :, sl], p.astype(BF16))
            outs.append(acc * (1.0 / jnp.sum(den8, axis=0, keepdims=True)))
        pair_t = jnp.where(upper_rows, outs[0], outs[1])
        o_ref[0, :, j * LANES:(j + 1) * LANES] = jnp.transpose(pair_t).astype(BF16)


def _dsa_prompt_block(q, qi, wi, kk, kiki, vvt, batch, t_len, block, topk):
    nb = t_len // DSA_Q
    lk = (block + 1) * DSA_Q
    blk = lambda b: (b * nb + block, 0)
    seq = lambda b: (b, 0, 0)
    return pl.pallas_call(
        functools.partial(_dsa_prompt_kernel, topk=topk, block=block),
        grid=(batch,),
        in_specs=[pl.BlockSpec((DSA_Q, 512), blk),
                  pl.BlockSpec((DSA_Q, 512), blk),
                  pl.BlockSpec((DSA_Q, LANES), blk),
                  pl.BlockSpec((1, lk, LANES), seq),
                  pl.BlockSpec((1, lk, LANES), seq),
                  pl.BlockSpec((1, LANES, lk), seq)],
        out_specs=pl.BlockSpec((1, DSA_Q, 512), seq),
        out_shape=jax.ShapeDtypeStruct((batch, DSA_Q, 512), BF16),
        scratch_shapes=[pltpu.VMEM((lk, DSA_Q), F32), pltpu.VMEM((lk, DSA_Q), F32)],
        compiler_params=_params("parallel"),
        name="dsa_prompt_%d" % lk,
    )(q, qi, wi, kk, kiki, vvt)


def _dsa_prompt(q, qi, wi, kk, kiki, vvt, batch, t_len):
    nb = t_len // DSA_Q
    topk = min(TOPK_MAX, t_len // 4)
    assert topk % DSA_Q == 0 and t_len % DSA_Q == 0
    kk, kiki = (a.reshape(batch, t_len, LANES) for a in (kk, kiki))
    parts = [_dsa_prompt_block(q, qi, wi, kk, kiki, vvt, batch, t_len, block, topk) for block in range(nb)]
    return jnp.concatenate(parts, axis=1).reshape(batch * t_len, 512)


def _mem_kv_kernel(m_ref, w_ref, k_ref, v_ref):
    z = _dot(m_ref[...].astype(BF16), w_ref[...])
    k_ref[...] = z[:, :MEM_WIDTH]
    v_ref[...] = z[:, MEM_WIDTH:]


def _mem_kv(mem, w_kv):
    n = mem.shape[0]
    tm = min(n, 512)
    row = lambda i: (i, 0)
    return pl.pallas_call(
        _mem_kv_kernel,
        grid=(n // tm,),
        in_specs=[pl.BlockSpec((tm, D_MODEL), row),
                  pl.BlockSpec((D_MODEL, 2 * MEM_WIDTH), lambda i: (0, 0))],
        out_specs=[pl.BlockSpec((tm, MEM_WIDTH), row), pl.BlockSpec((tm, MEM_WIDTH), row)],
        out_shape=[jax.ShapeDtypeStruct((n, MEM_WIDTH), F32)] * 2,
        compiler_params=_params("parallel"),
        name="mem_kv",
    )(mem, w_kv)


def _pool_project(window_sums, u, cnts, pw_ref, ps_ref):
    ys = []
    for g in range(len(POOL_WINDOWS)):
        sl = slice(g * POOL_CG, (g + 1) * POOL_CG)
        pooled = window_sums[g] / cnts[g] - u[:, sl]
        ys.append(_dot(pooled.astype(BF16), pw_ref[g]) * ps_ref[:, sl])
    return jnp.concatenate(ys, axis=1)


def _mid_prompt_kernel(x_ref, attn_ref, u_ref, halo_ref, mk_ref, mv_ref, pw_ref, ps_ref, wo_ref,
                       gm_ref, wmq_ref, wmo_ref, gf_ref, h2_ref, tok_ref, e_ref, *, nt, n_real):
    step = pl.program_id(0)

    @pl.when(step >= n_real)
    def _():
        h2_ref[...] = jnp.zeros_like(h2_ref)
        tok_ref[...] = jnp.zeros_like(tok_ref)

    @pl.when(step < n_real)
    def _():
        _mid_prompt_rows(x_ref, attn_ref, u_ref, halo_ref, mk_ref, mv_ref, pw_ref, ps_ref, wo_ref,
                         gm_ref, wmq_ref, wmo_ref, gf_ref, h2_ref, tok_ref, e_ref, step % nt)


def _mid_prompt_rows(x_ref, attn_ref, u_ref, halo_ref, mk_ref, mv_ref, pw_ref, ps_ref, wo_ref,
                     gm_ref, wmq_ref, wmo_ref, gf_ref, h2_ref, tok_ref, e_ref, i):
    tm = x_ref.shape[0]
    u = u_ref[...]
    halo = halo_ref[...]
    e_ref[0:POOL_MAX, :] = jnp.where(i == 0, jnp.zeros_like(halo), halo)
    e_ref[POOL_MAX:POOL_MAX + tm, :] = u
    pos = i * tm + lax.broadcasted_iota(jnp.int32, (tm, 1), 0)
    sums, cnts = [], []
    for g, w in enumerate(POOL_WINDOWS):
        sl = slice(g * POOL_CG, (g + 1) * POOL_CG)
        s = u[:, sl]
        for j in range(1, w):
            s = s + e_ref[POOL_MAX - j:POOL_MAX - j + tm, sl]
        sums.append(s)
        cnts.append(jnp.minimum(w, pos + 1).astype(F32))
    pool = _pool_project(sums, u, cnts, pw_ref, ps_ref)
    h1 = x_ref[...] + _dot(attn_ref[...], wo_ref[0:ATTN_WIDTH, :]) \
        + _dot(pool.astype(BF16), wo_ref[ATTN_WIDTH:, :])

    qm = _dot(_rms(h1, gm_ref[...]).astype(BF16), wmq_ref[...]).astype(BF16)
    mk = mk_ref[0].astype(BF16)
    mv = mv_ref[0].astype(BF16)
    outs = []
    for h in range(MEM_HEADS):
        sl = slice(h * MEM_HEAD_DIM, (h + 1) * MEM_HEAD_DIM)
        logits = _dot_nt(qm[:, sl], mk[:, sl]) * (MEM_HEAD_DIM ** -0.5)
        m = jnp.max(logits, axis=1, keepdims=True)
        p = jnp.exp(logits - m)
        p = p / jnp.sum(p, axis=1, keepdims=True)
        outs.append(_dot(p.astype(BF16), mv[:, sl]))
    o = jnp.concatenate(outs, axis=1)
    h2 = h1 + _dot(o.astype(BF16), wmo_ref[...])
    h2_ref[...] = h2
    tok_ref[...] = _rms(h2, gf_ref[...]).astype(BF16)


def _mid_prompt(x, attn, u, mk, mv, pool_w, pool_scale, w_o, g_mem, w_mq, w_mo, g_ffn, batch, t_len, tm, n_rows):
    nt = t_len // tm
    hb = tm // POOL_MAX
    n_real = batch * nt
    n_steps = -(-n_rows // tm)
    last = n_real - 1
    row = lambda s: (jnp.minimum(s, last), 0)
    halo = lambda s: (jnp.maximum(jnp.minimum(s, last) * hb - 1, 0), 0)
    fixed2 = lambda s: (0, 0)
    fixed3 = lambda s: (0, 0, 0)
    perb = lambda s: (jnp.minimum(s, last) // nt, 0, 0)
    out_row = lambda s: (s, 0)
    n = n_rows
    return pl.pallas_call(
        functools.partial(_mid_prompt_kernel, nt=nt, n_real=n_real),
        grid=(n_steps,),
        in_specs=[pl.BlockSpec((tm, D_MODEL), row),
                  pl.BlockSpec((tm, ATTN_WIDTH), row),
                  pl.BlockSpec((tm, POOL_WIDTH), row),
                  pl.BlockSpec((POOL_MAX, POOL_WIDTH), halo),
                  pl.BlockSpec((1, N_MEM, MEM_WIDTH), perb),
                  pl.BlockSpec((1, N_MEM, MEM_WIDTH), perb),
                  pl.BlockSpec((len(POOL_WINDOWS), POOL_CG, POOL_CG), fixed3),
                  pl.BlockSpec((1, POOL_WIDTH), fixed2),
                  pl.BlockSpec((D_MODEL, D_MODEL), fixed2),
                  pl.BlockSpec((1, D_MODEL), fixed2),
                  pl.BlockSpec((D_MODEL, MEM_WIDTH), fixed2),
                  pl.BlockSpec((MEM_WIDTH, D_MODEL), fixed2),
                  pl.BlockSpec((1, D_MODEL), fixed2)],
        out_specs=[pl.BlockSpec((tm, D_MODEL), out_row), pl.BlockSpec((tm, D_MODEL), out_row)],
        out_shape=[jax.ShapeDtypeStruct((n, D_MODEL), F32), jax.ShapeDtypeStruct((n, D_MODEL), BF16)],
        scratch_shapes=[pltpu.VMEM((POOL_MAX + tm, POOL_WIDTH), F32)],
        compiler_params=_params("arbitrary"),
        name="mid_prompt",
    )(x, attn, u, u, mk, mv, pool_w, pool_scale, w_o, g_mem, w_mq, w_mo, g_ffn)


def _router_kernel(tok_ref, rw_ref, rb_ref, rank_ref, gate_ref, cnt_ref, *, n_valid):
    tm = tok_ref.shape[0]
    token = pl.program_id(0) * tm + lax.broadcasted_iota(jnp.int32, (1, tm), 1)
    valid = token < n_valid
    logits = _dot_nt(rw_ref[...], tok_ref[...]) + rb_ref[...]
    eid = lax.broadcasted_iota(jnp.int32, (N_EXPERTS, tm), 0)
    vals, hots = [], []
    cur = logits
    for _ in range(TOP_K):
        m = jnp.max(cur, axis=0, keepdims=True)
        idx = jnp.min(jnp.where(cur == m, eid, N_EXPERTS), axis=0, keepdims=True)
        hot = eid == idx
        vals.append(m)
        hots.append(hot)
        cur = jnp.where(hot, -jnp.inf, cur)
    exps = [jnp.exp(v - vals[0]) for v in vals]
    den = exps[0] + exps[1] + exps[2] + exps[3]
    gate = jnp.zeros((N_EXPERTS, tm), F32)
    sel = jnp.zeros((N_EXPERTS, tm), jnp.bool_)
    for hot, ex in zip(hots, exps):
        gate = jnp.where(hot, ex / den, gate)
        sel = jnp.logical_or(sel, hot)
    sel = jnp.logical_and(sel, valid)
    gate_ref[0] = gate

    upper = _strict_triangle(LANES, lower=False)
    carry = jnp.zeros((N_EXPERTS, 1), F32)
    for c in range(tm // LANES):
        sl = slice(c * LANES, (c + 1) * LANES)
        sel_c = sel[:, sl]
        sel_b = jnp.where(sel_c, 1.0, 0.0).astype(BF16)
        rank = _dot(sel_b, upper) + carry
        rank_ref[0, :, sl] = jnp.where(sel_c, rank, -1.0).astype(jnp.int32)
        carry = carry + jnp.sum(sel_b.astype(F32), axis=1, keepdims=True)
    cnt_ref[0] = carry.astype(jnp.int32)


def _router(tok, rw_t, rb, tm, n_valid):
    n = tok.shape[0]
    nt = n // tm
    t3 = lambda i: (i, 0, 0)
    return pl.pallas_call(
        functools.partial(_router_kernel, n_valid=n_valid),
        grid=(nt,),
        in_specs=[pl.BlockSpec((tm, D_MODEL), lambda i: (i, 0)),
                  pl.BlockSpec((N_EXPERTS, D_MODEL), lambda i: (0, 0)),
                  pl.BlockSpec((N_EXPERTS, 1), lambda i: (0, 0))],
        out_specs=[pl.BlockSpec((1, N_EXPERTS, tm), t3),
                   pl.BlockSpec((1, N_EXPERTS, tm), t3),
                   pl.BlockSpec((1, N_EXPERTS, 1), t3)],
        out_shape=[jax.ShapeDtypeStruct((nt, N_EXPERTS, tm), jnp.int32),
                   jax.ShapeDtypeStruct((nt, N_EXPERTS, tm), F32),
                   jax.ShapeDtypeStruct((nt, N_EXPERTS, 1), jnp.int32)],
        compiler_params=_params("parallel"),
        name="router",
    )(tok, rw_t, rb)


def _swiglu(h):
    gate = jnp.minimum(h[:, :D_FF], SWIGLU_LIMIT)
    up = jnp.clip(h[:, D_FF:], -SWIGLU_LIMIT, SWIGLU_LIMIT)
    return gate * jax.nn.sigmoid(SWIGLU_ALPHA * gate) * (up + 1.0)


def _moe_kernel(cnt_ref, x_ref, h2_ref, rank_ref, gate_ref, wu_ref, bu_ref, wd_ref, bd_ref, gfin_ref,
                o_ref, os_ref, *, sample_row):
    i = pl.program_id(0)
    e = pl.program_id(1)
    tm = x_ref.shape[0]

    @pl.when(e == 0)
    def _():
        o_ref[...] = jnp.zeros_like(o_ref)

    n_routed = cnt_ref[i * N_EXPERTS + e]
    rank = rank_ref[0, pl.ds(e, 1), :]
    gate = gate_ref[0, pl.ds(e, 1), :]

    def run_slots(first, size):
        slot = first + lax.broadcasted_iota(jnp.int32, (size, tm), 0)
        hit = rank == slot
        onehot = jnp.where(hit, 1.0, 0.0).astype(BF16)
        xg = _dot(onehot, x_ref[...]).astype(BF16)
        act = _swiglu(_dot(xg, wu_ref[0]) + bu_ref[0])
        out = _dot(act.astype(BF16), wd_ref[0]) + bd_ref[0]
        g_slot = jnp.sum(jnp.where(hit, gate, 0.0), axis=1, keepdims=True)
        o_ref[...] += _dot_tn(onehot, (out * g_slot).astype(BF16))

    largest = MOE_FIRST_ROWS[-1]
    for lo, size in zip((0,) + MOE_FIRST_ROWS[:-1], MOE_FIRST_ROWS):
        fits = n_routed <= size if size < largest else n_routed > lo
        @pl.when(jnp.logical_and(n_routed > lo, fits))
        def _():
            run_slots(0, size)

    def body(c, carry):
        run_slots(largest + c * MOE_TAIL_ROWS, MOE_TAIL_ROWS)
        return carry

    lax.fori_loop(0, (jnp.maximum(n_routed - largest, 0) + MOE_TAIL_ROWS - 1) // MOE_TAIL_ROWS, body, 0)

    @pl.when(e == pl.num_programs(1) - 1)
    def _():
        o_ref[...] = _rms(h2_ref[...] + o_ref[...], gfin_ref[...])

        @pl.when(i == pl.num_programs(0) - 1)
        def _():
            os_ref[...] = o_ref[sample_row:sample_row + os_ref.shape[0], :]


def _moe(tok, h2, rank, gate, cnt, wu, bu, wd, bd, g_final, tm, n_prompt):
    nt = tok.shape[0] // tm
    tile = lambda i, e, c: (i, 0)
    tile3 = lambda i, e, c: (i, 0, 0)
    expert = lambda i, e, c: (e, 0, 0)
    grid_spec = pltpu.PrefetchScalarGridSpec(
        num_scalar_prefetch=1,
        grid=(nt, N_EXPERTS),
        in_specs=[pl.BlockSpec((tm, D_MODEL), tile),
                  pl.BlockSpec((tm, D_MODEL), tile),
                  pl.BlockSpec((1, N_EXPERTS, tm), tile3),
                  pl.BlockSpec((1, N_EXPERTS, tm), tile3),
                  pl.BlockSpec((1, D_MODEL, 2 * D_FF), expert),
                  pl.BlockSpec((1, 1, 2 * D_FF), expert),
                  pl.BlockSpec((1, D_FF, D_MODEL), expert),
                  pl.BlockSpec((1, 1, D_MODEL), expert),
                  pl.BlockSpec((1, D_MODEL), lambda i, e, c: (0, 0))],
        out_specs=[pl.BlockSpec((tm, D_MODEL), tile),
                   pl.BlockSpec((SAMPLE_ROWS, D_MODEL), lambda i, e, c: (0, 0))],
    )
    return pl.pallas_call(
        functools.partial(_moe_kernel, sample_row=n_prompt - (nt - 1) * tm),
        grid_spec=grid_spec,
        out_shape=[jax.ShapeDtypeStruct((n_prompt, D_MODEL), F32),
                   jax.ShapeDtypeStruct((SAMPLE_ROWS, D_MODEL), F32)],
        compiler_params=_params("arbitrary", "arbitrary"),
        name="moe",
    )(cnt, tok, h2, rank, gate, wu, bu, wd, bd, g_final)


def _page_copy(cache_ref, buf_ref, sem_ref, pt_ref, b, p, slot):
    dst = buf_ref.at[slot, :, pl.ds(pl.multiple_of(p * PAGE_SIZE, PAGE_SIZE), PAGE_SIZE)]
    return pltpu.make_async_copy(cache_ref.at[0, pt_ref[b, p]], dst, sem_ref.at[slot])


def _fetch_pages(cache_ref, buf_ref, sem_ref, pt_ref, b, slot, n_pages):
    def body(p, c):
        _page_copy(cache_ref, buf_ref, sem_ref, pt_ref, b, p, slot).start()
        return c
    lax.fori_loop(0, n_pages, body, 0, unroll=8)


def _wait_pages(cache_ref, buf_ref, sem_ref, pt_ref, b, slot, n_pages):
    def body(p, c):
        _page_copy(cache_ref, buf_ref, sem_ref, pt_ref, b, p, slot).wait()
        return c
    lax.fori_loop(0, n_pages, body, 0, unroll=8)


def _dsa_sample_score_kernel(pt_ref, qi_ref, wi_ref, kin_ref, cache_ref, score_ref, new_ref, buf_ref, sem_ref):
    b = pl.program_id(0)
    nb = pl.num_programs(0)
    past = buf_ref.shape[2]
    n_pages = past // PAGE_SIZE
    slot = b % 2

    @pl.when(b == 0)
    def _():
        _fetch_pages(cache_ref, buf_ref, sem_ref, pt_ref, b, slot, n_pages)

    @pl.when(b + 1 < nb)
    def _():
        _fetch_pages(cache_ref, buf_ref, sem_ref, pt_ref, b + 1, 1 - slot, n_pages)

    _wait_pages(cache_ref, buf_ref, sem_ref, pt_ref, b, slot, n_pages)

    qi = qi_ref[0]
    w = wi_ref[0] * (IDX_HEADS ** -0.5)
    keys_t = buf_ref[slot].astype(BF16)
    score_ref[0] = jnp.sum(jnp.maximum(_dot(qi, keys_t), 0.0) * w, axis=0, keepdims=True)
    s_new = jnp.sum(qi.astype(F32) * kin_ref[0].astype(F32), axis=1, keepdims=True)
    new_ref[0] = jnp.sum(jnp.maximum(s_new, 0.0) * w, axis=0, keepdims=True)


def _dsa_sample_select_kernel(score_ref, new_ref, mask_ref, take_new_ref, *, topk):
    score = score_ref[...]
    score_new = new_ref[...]
    past = score.shape[1]

    def count_ge(cand):
        return (jnp.sum(jnp.where(score >= cand, 1, 0), axis=1, keepdims=True)
                + jnp.where(score_new >= cand, 1, 0))

    tau = _kth_largest(count_ge, topk, score_new.shape)
    gt = score > tau
    eq = score == tau
    n_gt = jnp.sum(jnp.where(gt, 1, 0), axis=1, keepdims=True) + jnp.where(score_new > tau, 1, 0)
    n_eq_all = jnp.sum(jnp.where(eq, 1, 0), axis=1, keepdims=True) + jnp.where(score_new == tau, 1, 0)
    tied = n_gt + n_eq_all > topk
    need_f = jnp.where(tied, topk - n_gt, past + 2).astype(F32)
    take_eq, n_eq = _take_first_ties(eq, need_f, tied, axis=1)
    mask_ref[...] = jnp.where(jnp.logical_or(gt, take_eq > 0.5), 0.0, NEG)
    take_new = jnp.logical_or(score_new > tau, jnp.logical_and(score_new == tau, n_eq < need_f))
    take_new_ref[...] = jnp.where(take_new, 0.0, NEG)


def _dsa_sample_select(score, score_new, topk):
    db, past = score.shape
    full = lambda shape: pl.BlockSpec(shape, lambda i: (0, 0))
    return pl.pallas_call(
        functools.partial(_dsa_sample_select_kernel, topk=topk),
        grid=(1,),
        in_specs=[full((db, past)), full((db, 1))],
        out_specs=[full((db, past)), full((db, 1))],
        out_shape=[jax.ShapeDtypeStruct((db, past), F32), jax.ShapeDtypeStruct((db, 1), F32)],
        compiler_params=_params("arbitrary"),
        name="dsa_sample_select",
    )(score, score_new)


def _dsa_sample_mask(page_table, qi_heads, wi_col, ki_new, cache_idx_k):
    db, n_pages = page_table.shape
    past = n_pages * PAGE_SIZE
    topk = min(TOPK_MAX, (past + 1) // 4)
    per = lambda b, pt: (b, 0, 0)
    grid_spec = pltpu.PrefetchScalarGridSpec(
        num_scalar_prefetch=1,
        grid=(db,),
        in_specs=[pl.BlockSpec((1, IDX_HEADS, IDX_DIM), per),
                  pl.BlockSpec((1, IDX_HEADS, 1), per),
                  pl.BlockSpec((1, 1, IDX_DIM), per),
                  pl.BlockSpec(memory_space=pl.ANY)],
        out_specs=[pl.BlockSpec((1, 1, past), per), pl.BlockSpec((1, 1, 1), per)],
        scratch_shapes=[pltpu.VMEM((2, IDX_DIM, past), F32),
                        pltpu.SemaphoreType.DMA((2,))],
    )
    score, score_new = pl.pallas_call(
        _dsa_sample_score_kernel,
        grid_spec=grid_spec,
        out_shape=[jax.ShapeDtypeStruct((db, 1, past), F32), jax.ShapeDtypeStruct((db, 1, 1), F32)],
        compiler_params=_params("arbitrary"),
        name="dsa_sample_score",
    )(page_table, qi_heads, wi_col, ki_new, cache_idx_k)
    mask, take_new = _dsa_sample_select(score.reshape(db, past), score_new.reshape(db, 1), topk)
    return mask.reshape(db, 1, past), take_new.reshape(db, 1, 1)


def _dsa_sample_attend_kernel(pt_ref, q_ref, kn_ref, vn_ref, mask_ref, new_ref, ck_ref, cv_ref, o_ref,
                              kbuf_ref, vbuf_ref, ksem_ref, vsem_ref):
    b = pl.program_id(0)
    nb = pl.num_programs(0)
    past = kbuf_ref.shape[2]
    n_pages = past // PAGE_SIZE
    slot = b % 2

    def fetch(bb, sl):
        _fetch_pages(ck_ref, kbuf_ref, ksem_ref, pt_ref, bb, sl, n_pages)
        _fetch_pages(cv_ref, vbuf_ref, vsem_ref, pt_ref, bb, sl, n_pages)

    @pl.when(b == 0)
    def _():
        fetch(b, slot)

    @pl.when(b + 1 < nb)
    def _():
        fetch(b + 1, 1 - slot)

    _wait_pages(ck_ref, kbuf_ref, ksem_ref, pt_ref, b, slot, n_pages)
    _wait_pages(cv_ref, vbuf_ref, vsem_ref, pt_ref, b, slot, n_pages)

    q = q_ref[0]
    keys_t = kbuf_ref[slot].astype(BF16)
    vals_t = vbuf_ref[slot].astype(BF16)
    logits = _dot(q, keys_t) + mask_ref[0]
    l_new = jnp.sum(q.astype(F32) * kn_ref[0].astype(F32), axis=1, keepdims=True) + new_ref[0]
    m = jnp.maximum(jnp.max(logits, axis=1, keepdims=True), l_new)
    p = jnp.exp(logits - m)
    p_new = jnp.exp(l_new - m)
    den = jnp.sum(p, axis=1, keepdims=True) + p_new
    p = p / den
    p_new = p_new / den
    o_ref[0] = _dot_nt(p.astype(BF16), vals_t) + p_new * vn_ref[0].astype(F32)


def _dsa_sample_attend(page_table, q_heads, k_new, v_new, mask, take_new, cache_k, cache_v):
    db, n_pages = page_table.shape
    past = n_pages * PAGE_SIZE
    per = lambda b, pt: (b, 0, 0)
    grid_spec = pltpu.PrefetchScalarGridSpec(
        num_scalar_prefetch=1,
        grid=(db,),
        in_specs=[pl.BlockSpec((1, ATTN_HEADS, HEAD_DIM), per),
                  pl.BlockSpec((1, 1, HEAD_DIM), per),
                  pl.BlockSpec((1, 1, HEAD_DIM), per),
                  pl.BlockSpec((1, 1, past), per),
                  pl.BlockSpec((1, 1, 1), per),
                  pl.BlockSpec(memory_space=pl.ANY),
                  pl.BlockSpec(memory_space=pl.ANY)],
        out_specs=pl.BlockSpec((1, ATTN_HEADS, HEAD_DIM), per),
        scratch_shapes=[pltpu.VMEM((2, HEAD_DIM, past), F32),
                        pltpu.VMEM((2, HEAD_DIM, past), F32),
                        pltpu.SemaphoreType.DMA((2,)),
                        pltpu.SemaphoreType.DMA((2,))],
    )
    return pl.pallas_call(
        _dsa_sample_attend_kernel,
        grid_spec=grid_spec,
        out_shape=jax.ShapeDtypeStruct((db, ATTN_HEADS, HEAD_DIM), F32),
        compiler_params=_params("arbitrary"),
        name="dsa_sample_attend",
    )(page_table, q_heads, k_new, v_new, mask, take_new, cache_k, cache_v)


def _mid_sample_a_kernel(x_ref, attn_ref, u_ref, st_ref, pw_ref, ps_ref, wo_ref, gm_ref, wmq_ref,
                         h1_ref, qm_ref, *, past):
    u = u_ref[...]
    rows = u.shape[0]
    sums, cnts = [], []
    for g, w in enumerate(POOL_WINDOWS):
        sl = slice(g * POOL_CG, (g + 1) * POOL_CG)
        s = u[:, sl]
        for j in range(1, w):
            s = s + st_ref[POOL_MAX - 1 - j][:, sl]
        sums.append(s)
        cnts.append(jnp.full((rows, 1), float(min(w, past + 1)), F32))
    pool = _pool_project(sums, u, cnts, pw_ref, ps_ref)
    h1 = x_ref[...] + _dot(attn_ref[...], wo_ref[0:ATTN_WIDTH, :]) \
        + _dot(pool.astype(BF16), wo_ref[ATTN_WIDTH:, :])
    h1_ref[...] = h1
    qm_ref[...] = _dot(_rms(h1, gm_ref[...]).astype(BF16), wmq_ref[...]).astype(BF16)


def _mid_sample_a(x, attn, u, state_t, pool_w, pool_scale, w_o, g_mem, w_mq, past):
    db = x.shape[0]
    full = lambda a: pl.BlockSpec(a.shape, lambda i, nd=a.ndim: (0,) * nd)
    args = (x, attn, u, state_t, pool_w, pool_scale, w_o, g_mem, w_mq)
    return pl.pallas_call(
        functools.partial(_mid_sample_a_kernel, past=past),
        grid=(1,),
        in_specs=[full(a) for a in args],
        out_specs=[pl.BlockSpec((db, D_MODEL), lambda i: (0, 0)),
                   pl.BlockSpec((db, MEM_WIDTH), lambda i: (0, 0))],
        out_shape=[jax.ShapeDtypeStruct((db, D_MODEL), F32), jax.ShapeDtypeStruct((db, MEM_WIDTH), BF16)],
        compiler_params=_params("arbitrary"),
        name="mid_sample_a",
    )(*args)


def _mem_sample_kernel(q_ref, mk_ref, mv_ref, o_ref):
    rows = 2 * MEM_HEADS
    n = mk_ref.shape[1]
    q_all = q_ref[0].astype(F32)
    q = jnp.concatenate([q_all[:, (r % MEM_HEADS) * MEM_HEAD_DIM:(r % MEM_HEADS + 1) * MEM_HEAD_DIM]
                         for r in range(rows)], axis=0).astype(BF16)
    logits = _dot_nt(q, mk_ref[0].astype(BF16)) * (MEM_HEAD_DIM ** -0.5)
    own = (lax.broadcasted_iota(jnp.int32, (rows, n), 1) % MEM_HEADS
           == lax.broadcasted_iota(jnp.int32, (rows, n), 0) % MEM_HEADS)
    logits = jnp.where(own, logits, -jnp.inf)
    m = jnp.max(logits, axis=1, keepdims=True)
    p = jnp.exp(logits - m)
    p = p / jnp.sum(p, axis=1, keepdims=True)
    o = _dot(p.astype(BF16), mv_ref[0].astype(BF16))
    for h in range(MEM_HEADS):
        o_ref[0, :, h * MEM_HEAD_DIM:(h + 1) * MEM_HEAD_DIM] = o[h:h + 1, :]


def _mem_sample(qm, mk, mv):
    db = qm.shape[0]
    per = lambda b: (b, 0, 0)
    return pl.pallas_call(
        _mem_sample_kernel,
        grid=(db,),
        in_specs=[pl.BlockSpec((1, 1, MEM_WIDTH), per),
                  pl.BlockSpec((1, N_MEM * MEM_HEADS, MEM_HEAD_DIM), per),
                  pl.BlockSpec((1, N_MEM * MEM_HEADS, MEM_HEAD_DIM), per)],
        out_specs=pl.BlockSpec((1, 1, MEM_WIDTH), per),
        out_shape=jax.ShapeDtypeStruct((db, 1, MEM_WIDTH), F32),
        compiler_params=_params("parallel"),
        name="mem_sample",
    )(qm.reshape(db, 1, MEM_WIDTH), mk, mv)


def _sample_join_kernel(h1_ref, o_ref, wmo_ref, gf_ref, h2_any, tok_any, h2_ref, tok_ref):
    del h2_any, tok_any
    db = h1_ref.shape[0]
    h2_ref[...] = jnp.zeros_like(h2_ref)
    tok_ref[...] = jnp.zeros_like(tok_ref)

    @pl.when(pl.program_id(0) == 0)
    def _():
        h2 = h1_ref[...] + _dot(o_ref[...].astype(BF16), wmo_ref[...])
        h2_ref[0:db, :] = h2
        tok_ref[0:db, :] = _rms(h2, gf_ref[...]).astype(BF16)


def _sample_join(h1, o, w_mo, g_ffn, h2_all, tok_all, n_prompt):
    db = h1.shape[0]
    n_rows = h2_all.shape[0]
    assert db <= SAMPLE_ROWS and n_prompt % SAMPLE_ROWS == 0 and n_rows % SAMPLE_ROWS == 0
    first = n_prompt // SAMPLE_ROWS
    f2 = lambda j: (0, 0)
    tail = lambda j: (first + j, 0)
    return pl.pallas_call(
        _sample_join_kernel,
        grid=((n_rows - n_prompt) // SAMPLE_ROWS,),
        in_specs=[pl.BlockSpec((db, D_MODEL), f2),
                  pl.BlockSpec((db, MEM_WIDTH), f2),
                  pl.BlockSpec((MEM_WIDTH, D_MODEL), f2),
                  pl.BlockSpec((1, D_MODEL), f2),
                  pl.BlockSpec(memory_space=pl.ANY),
                  pl.BlockSpec(memory_space=pl.ANY)],
        out_specs=[pl.BlockSpec((SAMPLE_ROWS, D_MODEL), tail), pl.BlockSpec((SAMPLE_ROWS, D_MODEL), tail)],
        out_shape=[jax.ShapeDtypeStruct(h2_all.shape, F32), jax.ShapeDtypeStruct(tok_all.shape, BF16)],
        input_output_aliases={4: 0, 5: 1},
        compiler_params=_params("arbitrary"),
        name="sample_join",
    )(h1, o, w_mo, g_ffn, h2_all, tok_all)


def _rope_tables(pos):
    half = HEAD_DIM // 2
    inv = ROPE_THETA ** (-jnp.arange(half, dtype=F32) / half)
    ang = pos.astype(F32)[:, None] * inv[None, :]
    cos = jnp.cos(ang)
    sin = jnp.sin(ang)
    return jnp.tile(cos, (1, 4)), jnp.concatenate([-sin, sin, -sin, sin], axis=1)


def _pick_tile(n, pref):
    t = min(n, pref)
    while n % t:
        t //= 2
    return t


def kernel(x_prompt, x_sample, mem_prompt, cache_k, cache_v, cache_idx_k, state_pool, cache_mem_k, cache_mem_v, page_table, norm_mix, w_in, pool_w, pool_scale, w_o, norm_mem, w_mq, w_mk, w_mv, w_mo, norm_ffn, router_w, router_b, w_up, b_up, w_down, b_down, norm_final):
    B, T = x_prompt.shape[:2]
    DB, DS = x_sample.shape[:2]
    assert DS == 1 and w_in.shape[0] == 1
    n_pages = page_table.shape[1]
    past = n_pages * PAGE_SIZE
    NP = B * T

    wi_ = w_in[0]
    wq, wk, wv, wu_, wqi, wki, wwi = jnp.split(wi_, [512, 576, 640, 1152, 1664, 1728], axis=1)
    w_all = jnp.concatenate(
        [wq, wqi, wu_, wk, wk, wki, wki, wv, wv, wwi, jnp.zeros((D_MODEL, LANES - IDX_HEADS), F32)],
        axis=1).astype(BF16)
    g_mix = norm_mix[0][None, :]
    g_mem = norm_mem[0][None, :]
    g_ffn = norm_ffn[0][None, :]
    g_fin = norm_final[None, :]
    pw = pool_w[0].astype(BF16)
    ps = pool_scale[0][None, :]
    wo = w_o[0].astype(BF16)
    wmq = w_mq[0].reshape(D_MODEL, MEM_WIDTH).astype(BF16)
    wkv = jnp.concatenate([w_mk[0].reshape(D_MODEL, MEM_WIDTH), w_mv[0].reshape(D_MODEL, MEM_WIDTH)],
                          axis=1).astype(BF16)
    wmo = w_mo[0].reshape(MEM_WIDTH, D_MODEL).astype(BF16)
    rw = router_w[0].astype(BF16)
    rb = router_b[0]
    wup = w_up[0].astype(BF16)
    wdn = w_down[0].astype(BF16)
    bup = b_up[0][:, None, :]
    bdn = b_down[0][:, None, :]

    xp = x_prompt.reshape(NP, D_MODEL)
    tm_in = _pick_tile(T, 1024)
    cos_p, sin_p = _rope_tables(jnp.arange(T, dtype=jnp.int32))
    q, qi, u, kk, kiki, vvt, k_pt, ki_pt, v_pt, wi_p = _in_project(xp, g_mix, w_all, cos_p, sin_p, tm_in,
                                                                   T // tm_in, batch=B)
    attn = _dsa_prompt(q, qi, wi_p, kk, kiki, vvt, B, T)
    mk_p, mv_p = _mem_kv(mem_prompt.reshape(B * N_MEM, D_MODEL), wkv)
    tm_moe = min(MOE_TILE, NP)
    n_tiles = -(-(NP + SAMPLE_ROWS) // tm_moe)
    assert (n_tiles - 1) * tm_moe < NP and NP + SAMPLE_ROWS <= n_tiles * tm_moe
    h2_all, tok_all = _mid_prompt(xp, attn, u, mk_p.reshape(B, N_MEM, MEM_WIDTH), mv_p.reshape(B, N_MEM, MEM_WIDTH),
                                  pw, ps, wo, g_mem, wmq, wmo, g_ffn, B, T, _pick_tile(T, 1024), n_tiles * tm_moe)

    xs = x_sample.reshape(DB, D_MODEL)
    cos_s, sin_s = _rope_tables(jnp.full((DB,), past, jnp.int32))
    q_s, qi_s, u_s, kk_s, kiki_s, vv_s, k_s, ki_s, v_s, wi_s = _in_project(xs, g_mix, w_all, cos_s, sin_s, DB, 1)
    mask, take_new = _dsa_sample_mask(
        page_table, qi_s.reshape(DB, IDX_HEADS, IDX_DIM), wi_s[:, :IDX_HEADS, None],
        kiki_s[:, None, :IDX_DIM], jnp.swapaxes(cache_idx_k, 2, 3))
    attn_s = _dsa_sample_attend(
        page_table, q_s.reshape(DB, ATTN_HEADS, HEAD_DIM), kk_s[:, None, :HEAD_DIM], vv_s[:, None, :HEAD_DIM],
        mask, take_new, jnp.swapaxes(cache_k, 2, 3), jnp.swapaxes(cache_v, 2, 3))
    attn_s = attn_s.reshape(DB, ATTN_WIDTH).astype(BF16)
    h1_s, qm_s = _mid_sample_a(xs, attn_s, u_s, jnp.transpose(state_pool[0], (1, 0, 2)), pw, ps, wo, g_mem, wmq, past)
    o_s = _mem_sample(qm_s, cache_mem_k.reshape(DB, N_MEM * MEM_HEADS, MEM_HEAD_DIM),
                      cache_mem_v.reshape(DB, N_MEM * MEM_HEADS, MEM_HEAD_DIM))
    h2_all, tok_all = _sample_join(h1_s, o_s.reshape(DB, MEM_WIDTH), wmo, g_ffn, h2_all, tok_all, NP)

    rank, gate, cnt = _router(tok_all, rw.T, rb[:, None], tm_moe, NP + DB)
    y_p, y_s = _moe(tok_all, h2_all, rank, gate, cnt.reshape(-1), wup, bup, wdn, bdn, g_fin, tm_moe, NP)
    y_prompt = y_p.reshape(B, T, D_MODEL)
    y_sample = y_s[:DB].reshape(DB, 1, D_MODEL)

    new_pool_p = u.reshape(B, T, POOL_WIDTH)[:, T - (POOL_MAX - 1):][None]
    new_pool_s = jnp.concatenate([state_pool[0][:, 1:], u_s[:, None, :]], axis=1)[None]
    return (y_prompt, y_sample,
            jnp.swapaxes(k_pt, 1, 2)[None], jnp.swapaxes(v_pt, 1, 2)[None], jnp.swapaxes(ki_pt, 1, 2)[None],
            new_pool_p,
            mk_p.reshape(1, B, N_MEM, MEM_HEADS, MEM_HEAD_DIM), mv_p.reshape(1, B, N_MEM, MEM_HEADS, MEM_HEAD_DIM),
            k_s.reshape(1, DB, 1, HEAD_DIM), v_s.reshape(1, DB, 1, HEAD_DIM), ki_s.reshape(1, DB, 1, IDX_DIM),
            new_pool_s)
```

```python
import functools

import jax
import jax.numpy as jnp
from jax import lax
from jax.experimental import pallas as pl
from jax.experimental.pallas import tpu as pltpu

D_MODEL = 1024
HEAD_DIM = 64
ATTN_WIDTH = 512
ATTN_HEADS = 8
IDX_HEADS = 8
IDX_DIM = 64
TOPK_MAX = 256
Q_BLOCK = 128
PAGE_SIZE = 128
POOL_WIDTH = 512
POOL_WINDOWS = (2, 4, 8, 16)
POOL_CG = 128
POOL_MAX = 16
N_MEM = 256
MEM_HEADS = 4
MEM_HEAD_DIM = 128
MEM_WIDTH = MEM_HEADS * MEM_HEAD_DIM
N_EXPERTS = 32
TOP_K = 4
D_FF = 1024
SWIGLU_LIMIT = 7.0
SWIGLU_ALPHA = 1.702
ROPE_THETA = 10000.0
EPS = 1e-6
NEG = -1e30

LANES = 128
MOE_TILE = 13 * LANES
MOE_FIRST_ROWS = (208, 224, 256)
MOE_EXPERTS_PER_STEP = 2
MOE_TAIL_ROWS = 128
SAMPLE_ROWS = 128
DSA_Q = 256
DSA_KCHUNK = 256
IN_COLS = 3 * 512 + 4 * LANES
VMEM_LIMIT = 56 * 1024 * 1024

BF16 = jnp.bfloat16
F32 = jnp.float32


def _dot(a, b):
    return jnp.dot(a, b, preferred_element_type=F32)


def _dot_nt(a, b):
    return lax.dot_general(a, b, (((1,), (1,)), ((), ())), preferred_element_type=F32)


def _dot_tn(a, b):
    return lax.dot_general(a, b, (((0,), (0,)), ((), ())), preferred_element_type=F32)


def _rms(x, g):
    return x * lax.rsqrt(jnp.mean(x * x, axis=-1, keepdims=True) + EPS) * g


def _params(*sem):
    return pltpu.CompilerParams(dimension_semantics=sem, vmem_limit_bytes=VMEM_LIMIT)


def _in_kernel(x_ref, g_ref, w_ref, cos_ref, sin_ref,
               q_ref, qi_ref, u_ref, kk_ref, kiki_ref, vv_ref, k_ref, ki_ref, v_ref, wi_ref, *, feature_major):
    xn = _rms(x_ref[...], g_ref[...]).astype(BF16)
    z = _dot(xn, w_ref[...])
    cos = cos_ref[...]
    sin = sin_ref[...]
    lane = lax.broadcasted_iota(jnp.int32, cos.shape, 1)
    first_half = (lane % HEAD_DIM) < (HEAD_DIM // 2)

    def rope(zc):
        swapped = jnp.where(first_half,
                            pltpu.roll(zc, LANES - HEAD_DIM // 2, 1),
                            pltpu.roll(zc, HEAD_DIM // 2, 1))
        return zc * cos + swapped * sin

    for c in range(4):
        q_c = rope(z[:, c * LANES:(c + 1) * LANES]) * (HEAD_DIM ** -0.5)
        qi_c = rope(z[:, 512 + c * LANES:512 + (c + 1) * LANES]) * (IDX_DIM ** -0.5)
        q_ref[:, c * LANES:(c + 1) * LANES] = q_c.astype(BF16)
        qi_ref[:, c * LANES:(c + 1) * LANES] = qi_c.astype(BF16)
    u_ref[...] = z[:, 1024:1536]
    kk = rope(z[:, 1536:1664])
    kiki = rope(z[:, 1664:1792])
    vv = z[:, 1792:1920]
    kk_ref[...] = kk.astype(BF16)
    kiki_ref[...] = kiki.astype(BF16)
    wi_ref[...] = z[:, 1920:2048]
    if feature_major:
        kk_t = jnp.transpose(kk)
        kiki_t = jnp.transpose(kiki)
        vv_t = jnp.transpose(vv)
        vv_ref[0] = vv_t.astype(BF16)
        k_ref[0] = kk_t[0:HEAD_DIM, :]
        ki_ref[0] = kiki_t[0:IDX_DIM, :]
        v_ref[0] = vv_t[0:HEAD_DIM, :]
    else:
        vv_ref[...] = vv.astype(BF16)
        k_ref[...] = kk[:, :HEAD_DIM]
        ki_ref[...] = kiki[:, :IDX_DIM]
        v_ref[...] = vv[:, :HEAD_DIM]


def _in_project(x, g, w_all, cos, sin, tm, table_blocks, batch=None):
    n = x.shape[0]
    row = lambda i: (i, 0)
    tab = lambda i: (i % table_blocks, 0)
    fixed = lambda i: (0, 0)
    feature_major = batch is not None
    if feature_major:
        t_len = n // batch
        assert t_len == table_blocks * tm
        fm = lambda feat, dt: jax.ShapeDtypeStruct((batch, feat, t_len), dt)
        fm_spec = lambda feat: pl.BlockSpec((1, feat, tm), lambda i: (i // table_blocks, 0, i % table_blocks))
    else:
        fm = lambda feat, dt: jax.ShapeDtypeStruct((n, feat), dt)
        fm_spec = lambda feat: pl.BlockSpec((tm, feat), row)
    outs = [
        jax.ShapeDtypeStruct((n, 512), BF16),
        jax.ShapeDtypeStruct((n, 512), BF16),
        jax.ShapeDtypeStruct((n, 512), F32),
        jax.ShapeDtypeStruct((n, LANES), BF16),
        jax.ShapeDtypeStruct((n, LANES), BF16),
        fm(LANES, BF16),
        fm(HEAD_DIM, F32),
        fm(IDX_DIM, F32),
        fm(HEAD_DIM, F32),
        jax.ShapeDtypeStruct((n, LANES), F32),
    ]
    out_specs = ([pl.BlockSpec((tm, s.shape[1]), row) for s in outs[:5]]
                 + [fm_spec(LANES), fm_spec(HEAD_DIM), fm_spec(IDX_DIM), fm_spec(HEAD_DIM)]
                 + [pl.BlockSpec((tm, LANES), row)])
    return pl.pallas_call(
        functools.partial(_in_kernel, feature_major=feature_major),
        grid=(n // tm,),
        in_specs=[pl.BlockSpec((tm, D_MODEL), row),
                  pl.BlockSpec((1, D_MODEL), fixed),
                  pl.BlockSpec((D_MODEL, IN_COLS), fixed),
                  pl.BlockSpec((tm, LANES), tab),
                  pl.BlockSpec((tm, LANES), tab)],
        out_specs=out_specs,
        out_shape=outs,
        compiler_params=_params("parallel"),
        name="in_project",
    )(x, g, w_all, cos, sin)


def _f32_to_key(x):
    bits = lax.bitcast_convert_type(x, jnp.int32)
    return jnp.where(bits < 0, bits ^ jnp.int32(0x7FFFFFFF), bits)


def _key_to_f32(key):
    bits = jnp.where(key < 0, key ^ jnp.int32(0x7FFFFFFF), key)
    return lax.bitcast_convert_type(bits, F32)


def _kth_largest(count_ge, k, shape):
    def body(it, tau):
        cand = tau + jnp.left_shift(jnp.int32(1), jnp.int32(31) - it)
        return jnp.where(count_ge(_key_to_f32(cand)) >= k, cand, tau)

    return _key_to_f32(lax.fori_loop(0, 32, body, jnp.full(shape, -2 ** 31, jnp.int32)))


def _kth_largest_between(count_ge, k, lo, hi):
    key_lo = _f32_to_key(lo)
    span = _f32_to_key(hi) - key_lo
    n_bits = jnp.max(32 - lax.clz(span))
    flip = jnp.int32(-2 ** 31)

    def body(it, off):
        cand = off | jnp.left_shift(jnp.int32(1), n_bits - 1 - it)
        ok = jnp.logical_and(count_ge(_key_to_f32(key_lo + cand)) >= k, (cand ^ flip) <= (span ^ flip))
        return jnp.where(ok, cand, off)

    return _key_to_f32(key_lo + lax.fori_loop(0, n_bits, body, jnp.zeros_like(key_lo)))


def _strict_triangle(n, lower):
    r = lax.broadcasted_iota(jnp.int32, (n, n), 0)
    c = lax.broadcasted_iota(jnp.int32, (n, n), 1)
    return jnp.where((r > c) if lower else (r < c), 1.0, 0.0).astype(BF16)


def _take_first_ties(eq, need_f, tied, axis):
    n = eq.shape[axis]
    cshape = (1, eq.shape[1]) if axis == 0 else (eq.shape[0], 1)

    def ranked():
        tri = _strict_triangle(LANES, lower=(axis == 0))
        carry = jnp.zeros(cshape, F32)
        parts = []
        for c in range(n // LANES):
            sl = slice(c * LANES, (c + 1) * LANES)
            eq_f = jnp.where(eq[sl, :] if axis == 0 else eq[:, sl], 1.0, 0.0)
            eq_b = eq_f.astype(BF16)
            rank = (_dot(tri, eq_b) if axis == 0 else _dot(eq_b, tri)) + carry
            parts.append(jnp.where(rank < need_f, eq_f, 0.0))
            carry = carry + jnp.sum(eq_f, axis=axis, keepdims=True)
        return jnp.concatenate(parts, axis=axis), carry

    def plain():
        return jnp.where(eq, 1.0, 0.0), jnp.zeros(cshape, F32)

    return lax.cond(jnp.max(jnp.where(tied, 1, 0)) > 0, ranked, plain)


def _dsa_prompt_kernel(q_ref, qi_ref, wi_ref, kk_ref, kiki_ref, vvt_ref, o_ref, sm_ref, bias_ref, *, topk, block):
    i = block
    qb = q_ref.shape[0]
    lk = kk_ref.shape[1]
    select = lk > topk
    ck = min(DSA_KCHUNK, lk)
    chunks = [slice(c * ck, (c + 1) * ck) for c in range(lk // ck)]
    upper_rows = lax.broadcasted_iota(jnp.int32, (LANES, qb), 0) < HEAD_DIM

    def head_operand(ref, h):
        pair_t = jnp.transpose(ref[:, (h // 2) * LANES:(h // 2 + 1) * LANES].astype(F32))
        keep = upper_rows if h % 2 == 0 else jnp.logical_not(upper_rows)
        return jnp.where(keep, pair_t, 0.0).astype(BF16)

    pos_q = i * qb + lax.broadcasted_iota(jnp.int32, (1, qb), 1)

    def causal(sl):
        return sl.start + lax.broadcasted_iota(jnp.int32, (ck, 1), 0) <= pos_q

    if select:
        w_t = jnp.transpose(wi_ref[...])[0:IDX_HEADS, :] * (IDX_HEADS ** -0.5)
        qi_ops = [head_operand(qi_ref, h) for h in range(IDX_HEADS)]
        for sl in chunks:
            keys = kiki_ref[0, sl, :]
            score = jnp.zeros((ck, qb), F32)
            for h in range(IDX_HEADS):
                score = score + jnp.maximum(_dot(keys, qi_ops[h]), 0.0) * w_t[h:h + 1, :]
            sm_ref[sl, :] = jnp.where(causal(sl), score, -jnp.inf)

        def count_ge(cand):
            return jnp.sum(jnp.where(sm_ref[0:lk, :] >= cand, 1, 0), axis=0, keepdims=True)

        if ck == topk:
            class_max = sm_ref[chunks[0], :]
            for sl in chunks[1:]:
                class_max = jnp.maximum(class_max, sm_ref[sl, :])
            tau = _kth_largest_between(count_ge, topk, jnp.min(class_max, axis=0, keepdims=True),
                                       jnp.max(class_max, axis=0, keepdims=True))
        else:
            tau = _kth_largest(count_ge, topk, (1, qb))
        sm = sm_ref[0:lk, :]
        gt = sm > tau
        eq = sm == tau
        n_gt = jnp.sum(jnp.where(gt, 1, 0), axis=0, keepdims=True)
        n_eq = jnp.sum(jnp.where(eq, 1, 0), axis=0, keepdims=True)
        tied = n_gt + n_eq > topk
        need_f = jnp.where(tied, topk - n_gt, lk + 1).astype(F32)
        take_eq, _ = _take_first_ties(eq, need_f, tied, axis=0)
        bias_ref[0:lk, :] = jnp.where(jnp.logical_or(gt, take_eq > 0.5), 0.0, NEG)

    def chunk_bias(sl):
        if select:
            return bias_ref[sl, :]
        return jnp.where(causal(sl), 0.0, NEG)

    def fold8(x, op):
        return op(x.reshape(ck // 8, 8, qb), axis=0)

    for j in range(ATTN_HEADS // 2):
        outs = []
        for h in (2 * j, 2 * j + 1):
            q_op = head_operand(q_ref, h)
            mx8 = jnp.full((8, qb), -jnp.inf, F32)
            for sl in chunks:
                logits = _dot(kk_ref[0, sl, :], q_op) + chunk_bias(sl)
                sm_ref[sl, :] = logits
                mx8 = jnp.maximum(mx8, fold8(logits, jnp.max))
            m = jnp.max(mx8, axis=0, keepdims=True)
            den8 = jnp.zeros((8, qb), F32)
            acc = jnp.zeros((LANES, qb), F32)
            for sl in chunks:
                p = jnp.exp(sm_ref[sl, :] - m)
                den8 = den8 + fold8(p, jnp.sum)
                acc = acc + _dot(vvt_ref[0, :, sl], p.astype(BF16))
            outs.append(acc * (1.0 / jnp.sum(den8, axis=0, keepdims=True)))
        pair_t = jnp.where(upper_rows, outs[0], outs[1])
        o_ref[0, :, j * LANES:(j + 1) * LANES] = jnp.transpose(pair_t).astype(BF16)


def _dsa_prompt_block(q, qi, wi, kk, kiki, vvt, batch, t_len, block, topk):
    nb = t_len // DSA_Q
    lk = (block + 1) * DSA_Q
    blk = lambda b: (b * nb + block, 0)
    seq = lambda b: (b, 0, 0)
    return pl.pallas_call(
        functools.partial(_dsa_prompt_kernel, topk=topk, block=block),
        grid=(batch,),
        in_specs=[pl.BlockSpec((DSA_Q, 512), blk),
                  pl.BlockSpec((DSA_Q, 512), blk),
                  pl.BlockSpec((DSA_Q, LANES), blk),
                  pl.BlockSpec((1, lk, LANES), seq),
                  pl.BlockSpec((1, lk, LANES), seq),
                  pl.BlockSpec((1, LANES, lk), seq)],
        out_specs=pl.BlockSpec((1, DSA_Q, 512), seq),
        out_shape=jax.ShapeDtypeStruct((batch, DSA_Q, 512), BF16),
        scratch_shapes=[pltpu.VMEM((lk, DSA_Q), F32), pltpu.VMEM((lk, DSA_Q), F32)],
        compiler_params=_params("parallel"),
        name="dsa_prompt_%d" % lk,
    )(q, qi, wi, kk, kiki, vvt)


def _dsa_prompt(q, qi, wi, kk, kiki, vvt, batch, t_len):
    nb = t_len // DSA_Q
    topk = min(TOPK_MAX, t_len // 4)
    assert topk % DSA_Q == 0 and t_len % DSA_Q == 0
    kk, kiki = (a.reshape(batch, t_len, LANES) for a in (kk, kiki))
    parts = [_dsa_prompt_block(q, qi, wi, kk, kiki, vvt, batch, t_len, block, topk) for block in range(nb)]
    return jnp.concatenate(parts, axis=1).reshape(batch * t_len, 512)


def _mem_kv_kernel(m_ref, w_ref, k_ref, v_ref):
    z = _dot(m_ref[...].astype(BF16), w_ref[...])
    k_ref[...] = z[:, :MEM_WIDTH]
    v_ref[...] = z[:, MEM_WIDTH:]


def _mem_kv(mem, w_kv):
    n = mem.shape[0]
    tm = min(n, 512)
    row = lambda i: (i, 0)
    return pl.pallas_call(
        _mem_kv_kernel,
        grid=(n // tm,),
        in_specs=[pl.BlockSpec((tm, D_MODEL), row),
                  pl.BlockSpec((D_MODEL, 2 * MEM_WIDTH), lambda i: (0, 0))],
        out_specs=[pl.BlockSpec((tm, MEM_WIDTH), row), pl.BlockSpec((tm, MEM_WIDTH), row)],
        out_shape=[jax.ShapeDtypeStruct((n, MEM_WIDTH), F32)] * 2,
        compiler_params=_params("parallel"),
        name="mem_kv",
    )(mem, w_kv)


def _pool_project(window_sums, u, cnts, pw_ref, ps_ref):
    ys = []
    for g in range(len(POOL_WINDOWS)):
        sl = slice(g * POOL_CG, (g + 1) * POOL_CG)
        pooled = window_sums[g] / cnts[g] - u[:, sl]
        ys.append(_dot(pooled.astype(BF16), pw_ref[g]) * ps_ref[:, sl])
    return jnp.concatenate(ys, axis=1)


def _mid_prompt_kernel(x_ref, attn_ref, u_ref, halo_ref, mk_ref, mv_ref, pw_ref, ps_ref, wo_ref,
                       gm_ref, wmq_ref, wmo_ref, gf_ref, h2_ref, tok_ref, e_ref, *, nt, n_real):
    step = pl.program_id(0)

    @pl.when(step >= n_real)
    def _():
        h2_ref[...] = jnp.zeros_like(h2_ref)
        tok_ref[...] = jnp.zeros_like(tok_ref)

    @pl.when(step < n_real)
    def _():
        _mid_prompt_rows(x_ref, attn_ref, u_ref, halo_ref, mk_ref, mv_ref, pw_ref, ps_ref, wo_ref,
                         gm_ref, wmq_ref, wmo_ref, gf_ref, h2_ref, tok_ref, e_ref, step % nt)


def _mid_prompt_rows(x_ref, attn_ref, u_ref, halo_ref, mk_ref, mv_ref, pw_ref, ps_ref, wo_ref,
                     gm_ref, wmq_ref, wmo_ref, gf_ref, h2_ref, tok_ref, e_ref, i):
    tm = x_ref.shape[0]
    u = u_ref[...]
    halo = halo_ref[...]
    e_ref[0:POOL_MAX, :] = jnp.where(i == 0, jnp.zeros_like(halo), halo)
    e_ref[POOL_MAX:POOL_MAX + tm, :] = u
    pos = i * tm + lax.broadcasted_iota(jnp.int32, (tm, 1), 0)
    sums, cnts = [], []
    for g, w in enumerate(POOL_WINDOWS):
        sl = slice(g * POOL_CG, (g + 1) * POOL_CG)
        s = u[:, sl]
        for j in range(1, w):
            s = s + e_ref[POOL_MAX - j:POOL_MAX - j + tm, sl]
        sums.append(s)
        cnts.append(jnp.minimum(w, pos + 1).astype(F32))
    pool = _pool_project(sums, u, cnts, pw_ref, ps_ref)
    h1 = x_ref[...] + _dot(attn_ref[...], wo_ref[0:ATTN_WIDTH, :]) \
        + _dot(pool.astype(BF16), wo_ref[ATTN_WIDTH:, :])

    qm = _dot(_rms(h1, gm_ref[...]).astype(BF16), wmq_ref[...]).astype(BF16)
    mk = mk_ref[0].astype(BF16)
    mv = mv_ref[0].astype(BF16)
    outs = []
    for h in range(MEM_HEADS):
        sl = slice(h * MEM_HEAD_DIM, (h + 1) * MEM_HEAD_DIM)
        logits = _dot_nt(qm[:, sl], mk[:, sl]) * (MEM_HEAD_DIM ** -0.5)
        m = jnp.max(logits, axis=1, keepdims=True)
        p = jnp.exp(logits - m)
        p = p / jnp.sum(p, axis=1, keepdims=True)
        outs.append(_dot(p.astype(BF16), mv[:, sl]))
    o = jnp.concatenate(outs, axis=1)
    h2 = h1 + _dot(o.astype(BF16), wmo_ref[...])
    h2_ref[...] = h2
    tok_ref[...] = _rms(h2, gf_ref[...]).astype(BF16)


def _mid_prompt(x, attn, u, mk, mv, pool_w, pool_scale, w_o, g_mem, w_mq, w_mo, g_ffn, batch, t_len, tm, n_rows):
    nt = t_len // tm
    hb = tm // POOL_MAX
    n_real = batch * nt
    n_steps = -(-n_rows // tm)
    last = n_real - 1
    row = lambda s: (jnp.minimum(s, last), 0)
    halo = lambda s: (jnp.maximum(jnp.minimum(s, last) * hb - 1, 0), 0)
    fixed2 = lambda s: (0, 0)
    fixed3 = lambda s: (0, 0, 0)
    perb = lambda s: (jnp.minimum(s, last) // nt, 0, 0)
    out_row = lambda s: (s, 0)
    n = n_rows
    return pl.pallas_call(
        functools.partial(_mid_prompt_kernel, nt=nt, n_real=n_real),
        grid=(n_steps,),
        in_specs=[pl.BlockSpec((tm, D_MODEL), row),
                  pl.BlockSpec((tm, ATTN_WIDTH), row),
                  pl.BlockSpec((tm, POOL_WIDTH), row),
                  pl.BlockSpec((POOL_MAX, POOL_WIDTH), halo),
                  pl.BlockSpec((1, N_MEM, MEM_WIDTH), perb),
                  pl.BlockSpec((1, N_MEM, MEM_WIDTH), perb),
                  pl.BlockSpec((len(POOL_WINDOWS), POOL_CG, POOL_CG), fixed3),
                  pl.BlockSpec((1, POOL_WIDTH), fixed2),
                  pl.BlockSpec((D_MODEL, D_MODEL), fixed2),
                  pl.BlockSpec((1, D_MODEL), fixed2),
                  pl.BlockSpec((D_MODEL, MEM_WIDTH), fixed2),
                  pl.BlockSpec((MEM_WIDTH, D_MODEL), fixed2),
                  pl.BlockSpec((1, D_MODEL), fixed2)],
        out_specs=[pl.BlockSpec((tm, D_MODEL), out_row), pl.BlockSpec((tm, D_MODEL), out_row)],
        out_shape=[jax.ShapeDtypeStruct((n, D_MODEL), F32), jax.ShapeDtypeStruct((n, D_MODEL), BF16)],
        scratch_shapes=[pltpu.VMEM((POOL_MAX + tm, POOL_WIDTH), F32)],
        compiler_params=_params("arbitrary"),
        name="mid_prompt",
    )(x, attn, u, u, mk, mv, pool_w, pool_scale, w_o, g_mem, w_mq, w_mo, g_ffn)


def _router_kernel(tok_ref, rw_ref, rb_ref, rank_ref, gate_ref, cnt_ref, *, n_valid):
    tm = tok_ref.shape[0]
    token = pl.program_id(0) * tm + lax.broadcasted_iota(jnp.int32, (1, tm), 1)
    valid = token < n_valid
    logits = _dot_nt(rw_ref[...], tok_ref[...]) + rb_ref[...]
    eid = lax.broadcasted_iota(jnp.int32, (N_EXPERTS, tm), 0)
    vals, hots = [], []
    cur = logits
    for _ in range(TOP_K):
        m = jnp.max(cur, axis=0, keepdims=True)
        idx = jnp.min(jnp.where(cur == m, eid, N_EXPERTS), axis=0, keepdims=True)
        hot = eid == idx
        vals.append(m)
        hots.append(hot)
        cur = jnp.where(hot, -jnp.inf, cur)
    exps = [jnp.exp(v - vals[0]) for v in vals]
    den = exps[0] + exps[1] + exps[2] + exps[3]
    gate = jnp.zeros((N_EXPERTS, tm), F32)
    sel = jnp.zeros((N_EXPERTS, tm), jnp.bool_)
    for hot, ex in zip(hots, exps):
        gate = jnp.where(hot, ex / den, gate)
        sel = jnp.logical_or(sel, hot)
    sel = jnp.logical_and(sel, valid)
    gate_ref[0] = gate

    upper = _strict_triangle(LANES, lower=False)
    carry = jnp.zeros((N_EXPERTS, 1), F32)
    for c in range(tm // LANES):
        sl = slice(c * LANES, (c + 1) * LANES)
        sel_c = sel[:, sl]
        sel_b = jnp.where(sel_c, 1.0, 0.0).astype(BF16)
        rank = _dot(sel_b, upper) + carry
        rank_ref[0, :, sl] = jnp.where(sel_c, rank, -1.0).astype(jnp.int32)
        carry = carry + jnp.sum(sel_b.astype(F32), axis=1, keepdims=True)
    cnt_ref[0] = carry.astype(jnp.int32)


def _router(tok, rw_t, rb, tm, n_valid):
    n = tok.shape[0]
    nt = n // tm
    t3 = lambda i: (i, 0, 0)
    return pl.pallas_call(
        functools.partial(_router_kernel, n_valid=n_valid),
        grid=(nt,),
        in_specs=[pl.BlockSpec((tm, D_MODEL), lambda i: (i, 0)),
                  pl.BlockSpec((N_EXPERTS, D_MODEL), lambda i: (0, 0)),
                  pl.BlockSpec((N_EXPERTS, 1), lambda i: (0, 0))],
        out_specs=[pl.BlockSpec((1, N_EXPERTS, tm), t3),
                   pl.BlockSpec((1, N_EXPERTS, tm), t3),
                   pl.BlockSpec((1, N_EXPERTS, 1), t3)],
        out_shape=[jax.ShapeDtypeStruct((nt, N_EXPERTS, tm), jnp.int32),
                   jax.ShapeDtypeStruct((nt, N_EXPERTS, tm), F32),
                   jax.ShapeDtypeStruct((nt, N_EXPERTS, 1), jnp.int32)],
        compiler_params=_params("parallel"),
        name="router",
    )(tok, rw_t, rb)


def _swiglu(h):
    gate = jnp.minimum(h[:, :D_FF], SWIGLU_LIMIT)
    up = jnp.clip(h[:, D_FF:], -SWIGLU_LIMIT, SWIGLU_LIMIT)
    return gate * jax.nn.sigmoid(SWIGLU_ALPHA * gate) * (up + 1.0)


def _moe_kernel(cnt_ref, x_ref, h2_ref, rank_ref, gate_ref, wu_ref, bu_ref, wd_ref, bd_ref, gfin_ref,
                o_ref, os_ref, *, sample_row):
    i = pl.program_id(0)
    step = pl.program_id(1)
    tm = x_ref.shape[0]

    @pl.when(step == 0)
    def _():
        o_ref[...] = jnp.zeros_like(o_ref)

    def one_expert(local, carry):
        e = step * MOE_EXPERTS_PER_STEP + local
        n_routed = cnt_ref[i * N_EXPERTS + e]
        rank = rank_ref[0, pl.ds(e, 1), :]
        gate = gate_ref[0, pl.ds(e, 1), :]

        def run_slots(first, size):
            slot = first + lax.broadcasted_iota(jnp.int32, (size, tm), 0)
            hit = rank == slot
            onehot = jnp.where(hit, 1.0, 0.0).astype(BF16)
            xg = _dot(onehot, x_ref[...]).astype(BF16)
            act = _swiglu(_dot(xg, wu_ref[local]) + bu_ref[local])
            out = _dot(act.astype(BF16), wd_ref[local]) + bd_ref[local]
            g_slot = jnp.sum(jnp.where(hit, gate, 0.0), axis=1, keepdims=True)
            o_ref[...] += _dot_tn(onehot, (out * g_slot).astype(BF16))

        largest = MOE_FIRST_ROWS[-1]
        for lo, size in zip((0,) + MOE_FIRST_ROWS[:-1], MOE_FIRST_ROWS):
            fits = n_routed <= size if size < largest else n_routed > lo
            @pl.when(jnp.logical_and(n_routed > lo, fits))
            def _():
                run_slots(0, size)

        def body(c, inner):
            run_slots(largest + c * MOE_TAIL_ROWS, MOE_TAIL_ROWS)
            return inner

        lax.fori_loop(0, (jnp.maximum(n_routed - largest, 0) + MOE_TAIL_ROWS - 1) // MOE_TAIL_ROWS, body, 0)
        return carry

    lax.fori_loop(0, MOE_EXPERTS_PER_STEP, one_expert, 0)

    @pl.when(step == pl.num_programs(1) - 1)
    def _():
        o_ref[...] = _rms(h2_ref[...] + o_ref[...], gfin_ref[...])

        @pl.when(i == pl.num_programs(0) - 1)
        def _():
            os_ref[...] = o_ref[sample_row:sample_row + os_ref.shape[0], :]


def _moe(tok, h2, rank, gate, cnt, wu, bu, wd, bd, g_final, tm, n_prompt):
    nt = tok.shape[0] // tm
    tile = lambda i, e, c: (i, 0)
    tile3 = lambda i, e, c: (i, 0, 0)
    expert = lambda i, e, c: (e, 0, 0)
    per = MOE_EXPERTS_PER_STEP
    grid_spec = pltpu.PrefetchScalarGridSpec(
        num_scalar_prefetch=1,
        grid=(nt, N_EXPERTS // per),
        in_specs=[pl.BlockSpec((tm, D_MODEL), tile),
                  pl.BlockSpec((tm, D_MODEL), tile, pipeline_mode=pl.Buffered(1)),
                  pl.BlockSpec((1, N_EXPERTS, tm), tile3),
                  pl.BlockSpec((1, N_EXPERTS, tm), tile3),
                  pl.BlockSpec((per, D_MODEL, 2 * D_FF), expert),
                  pl.BlockSpec((per, 1, 2 * D_FF), expert),
                  pl.BlockSpec((per, D_FF, D_MODEL), expert),
                  pl.BlockSpec((per, 1, D_MODEL), expert),
                  pl.BlockSpec((1, D_MODEL), lambda i, e, c: (0, 0))],
        out_specs=[pl.BlockSpec((tm, D_MODEL), tile),
                   pl.BlockSpec((SAMPLE_ROWS, D_MODEL), lambda i, e, c: (0, 0))],
    )
    return pl.pallas_call(
        functools.partial(_moe_kernel, sample_row=n_prompt - (nt - 1) * tm),
        grid_spec=grid_spec,
        out_shape=[jax.ShapeDtypeStruct((n_prompt, D_MODEL), F32),
                   jax.ShapeDtypeStruct((SAMPLE_ROWS, D_MODEL), F32)],
        compiler_params=_params("arbitrary", "arbitrary"),
        name="moe",
    )(cnt, tok, h2, rank, gate, wu, bu, wd, bd, g_final)


def _page_copy(cache_ref, buf_ref, sem_ref, pt_ref, b, p, slot):
    dst = buf_ref.at[slot, :, pl.ds(pl.multiple_of(p * PAGE_SIZE, PAGE_SIZE), PAGE_SIZE)]
    return pltpu.make_async_copy(cache_ref.at[0, pt_ref[b, p]], dst, sem_ref.at[slot])


def _fetch_pages(cache_ref, buf_ref, sem_ref, pt_ref, b, slot, n_pages):
    def body(p, c):
        _page_copy(cache_ref, buf_ref, sem_ref, pt_ref, b, p, slot).start()
        return c
    lax.fori_loop(0, n_pages, body, 0, unroll=8)


def _wait_pages(cache_ref, buf_ref, sem_ref, pt_ref, b, slot, n_pages):
    def body(p, c):
        _page_copy(cache_ref, buf_ref, sem_ref, pt_ref, b, p, slot).wait()
        return c
    lax.fori_loop(0, n_pages, body, 0, unroll=8)


def _dsa_sample_score_kernel(pt_ref, qi_ref, wi_ref, kin_ref, cache_ref, score_ref, new_ref, buf_ref, sem_ref):
    b = pl.program_id(0)
    nb = pl.num_programs(0)
    past = buf_ref.shape[2]
    n_pages = past // PAGE_SIZE
    slot = b % 2

    @pl.when(b == 0)
    def _():
        _fetch_pages(cache_ref, buf_ref, sem_ref, pt_ref, b, slot, n_pages)

    @pl.when(b + 1 < nb)
    def _():
        _fetch_pages(cache_ref, buf_ref, sem_ref, pt_ref, b + 1, 1 - slot, n_pages)

    _wait_pages(cache_ref, buf_ref, sem_ref, pt_ref, b, slot, n_pages)

    qi = qi_ref[0]
    w = wi_ref[0] * (IDX_HEADS ** -0.5)
    keys_t = buf_ref[slot].astype(BF16)
    score_ref[0] = jnp.sum(jnp.maximum(_dot(qi, keys_t), 0.0) * w, axis=0, keepdims=True)
    s_new = jnp.sum(qi.astype(F32) * kin_ref[0].astype(F32), axis=1, keepdims=True)
    new_ref[0] = jnp.sum(jnp.maximum(s_new, 0.0) * w, axis=0, keepdims=True)


def _dsa_sample_select_kernel(score_ref, new_ref, mask_ref, take_new_ref, *, topk):
    score = score_ref[...]
    score_new = new_ref[...]
    past = score.shape[1]

    def count_ge(cand):
        return (jnp.sum(jnp.where(score >= cand, 1, 0), axis=1, keepdims=True)
                + jnp.where(score_new >= cand, 1, 0))

    tau = _kth_largest(count_ge, topk, score_new.shape)
    gt = score > tau
    eq = score == tau
    n_gt = jnp.sum(jnp.where(gt, 1, 0), axis=1, keepdims=True) + jnp.where(score_new > tau, 1, 0)
    n_eq_all = jnp.sum(jnp.where(eq, 1, 0), axis=1, keepdims=True) + jnp.where(score_new == tau, 1, 0)
    tied = n_gt + n_eq_all > topk
    need_f = jnp.where(tied, topk - n_gt, past + 2).astype(F32)
    take_eq, n_eq = _take_first_ties(eq, need_f, tied, axis=1)
    mask_ref[...] = jnp.where(jnp.logical_or(gt, take_eq > 0.5), 0.0, NEG)
    take_new = jnp.logical_or(score_new > tau, jnp.logical_and(score_new == tau, n_eq < need_f))
    take_new_ref[...] = jnp.where(take_new, 0.0, NEG)


def _dsa_sample_select(score, score_new, topk):
    db, past = score.shape
    full = lambda shape: pl.BlockSpec(shape, lambda i: (0, 0))
    return pl.pallas_call(
        functools.partial(_dsa_sample_select_kernel, topk=topk),
        grid=(1,),
        in_specs=[full((db, past)), full((db, 1))],
        out_specs=[full((db, past)), full((db, 1))],
        out_shape=[jax.ShapeDtypeStruct((db, past), F32), jax.ShapeDtypeStruct((db, 1), F32)],
        compiler_params=_params("arbitrary"),
        name="dsa_sample_select",
    )(score, score_new)


def _dsa_sample_mask(page_table, qi_heads, wi_col, ki_new, cache_idx_k):
    db, n_pages = page_table.shape
    past = n_pages * PAGE_SIZE
    topk = min(TOPK_MAX, (past + 1) // 4)
    per = lambda b, pt: (b, 0, 0)
    grid_spec = pltpu.PrefetchScalarGridSpec(
        num_scalar_prefetch=1,
        grid=(db,),
        in_specs=[pl.BlockSpec((1, IDX_HEADS, IDX_DIM), per),
                  pl.BlockSpec((1, IDX_HEADS, 1), per),
                  pl.BlockSpec((1, 1, IDX_DIM), per),
                  pl.BlockSpec(memory_space=pl.ANY)],
        out_specs=[pl.BlockSpec((1, 1, past), per), pl.BlockSpec((1, 1, 1), per)],
        scratch_shapes=[pltpu.VMEM((2, IDX_DIM, past), F32),
                        pltpu.SemaphoreType.DMA((2,))],
    )
    score, score_new = pl.pallas_call(
        _dsa_sample_score_kernel,
        grid_spec=grid_spec,
        out_shape=[jax.ShapeDtypeStruct((db, 1, past), F32), jax.ShapeDtypeStruct((db, 1, 1), F32)],
        compiler_params=_params("arbitrary"),
        name="dsa_sample_score",
    )(page_table, qi_heads, wi_col, ki_new, cache_idx_k)
    mask, take_new = _dsa_sample_select(score.reshape(db, past), score_new.reshape(db, 1), topk)
    return mask.reshape(db, 1, past), take_new.reshape(db, 1, 1)


def _dsa_sample_attend_kernel(pt_ref, q_ref, kn_ref, vn_ref, mask_ref, new_ref, ck_ref, cv_ref, o_ref,
                              kbuf_ref, vbuf_ref, ksem_ref, vsem_ref):
    b = pl.program_id(0)
    nb = pl.num_programs(0)
    past = kbuf_ref.shape[2]
    n_pages = past // PAGE_SIZE
    slot = b % 2

    def fetch(bb, sl):
        _fetch_pages(ck_ref, kbuf_ref, ksem_ref, pt_ref, bb, sl, n_pages)
        _fetch_pages(cv_ref, vbuf_ref, vsem_ref, pt_ref, bb, sl, n_pages)

    @pl.when(b == 0)
    def _():
        fetch(b, slot)

    @pl.when(b + 1 < nb)
    def _():
        fetch(b + 1, 1 - slot)

    _wait_pages(ck_ref, kbuf_ref, ksem_ref, pt_ref, b, slot, n_pages)
    _wait_pages(cv_ref, vbuf_ref, vsem_ref, pt_ref, b, slot, n_pages)

    q = q_ref[0]
    keys_t = kbuf_ref[slot].astype(BF16)
    vals_t = vbuf_ref[slot].astype(BF16)
    logits = _dot(q, keys_t) + mask_ref[0]
    l_new = jnp.sum(q.astype(F32) * kn_ref[0].astype(F32), axis=1, keepdims=True) + new_ref[0]
    m = jnp.maximum(jnp.max(logits, axis=1, keepdims=True), l_new)
    p = jnp.exp(logits - m)
    p_new = jnp.exp(l_new - m)
    den = jnp.sum(p, axis=1, keepdims=True) + p_new
    p = p / den
    p_new = p_new / den
    o_ref[0] = _dot_nt(p.astype(BF16), vals_t) + p_new * vn_ref[0].astype(F32)


def _dsa_sample_attend(page_table, q_heads, k_new, v_new, mask, take_new, cache_k, cache_v):
    db, n_pages = page_table.shape
    past = n_pages * PAGE_SIZE
    per = lambda b, pt: (b, 0, 0)
    grid_spec = pltpu.PrefetchScalarGridSpec(
        num_scalar_prefetch=1,
        grid=(db,),
        in_specs=[pl.BlockSpec((1, ATTN_HEADS, HEAD_DIM), per),
                  pl.BlockSpec((1, 1, HEAD_DIM), per),
                  pl.BlockSpec((1, 1, HEAD_DIM), per),
                  pl.BlockSpec((1, 1, past), per),
                  pl.BlockSpec((1, 1, 1), per),
                  pl.BlockSpec(memory_space=pl.ANY),
                  pl.BlockSpec(memory_space=pl.ANY)],
        out_specs=pl.BlockSpec((1, ATTN_HEADS, HEAD_DIM), per),
        scratch_shapes=[pltpu.VMEM((2, HEAD_DIM, past), F32),
                        pltpu.VMEM((2, HEAD_DIM, past), F32),
                        pltpu.SemaphoreType.DMA((2,)),
                        pltpu.SemaphoreType.DMA((2,))],
    )
    return pl.pallas_call(
        _dsa_sample_attend_kernel,
        grid_spec=grid_spec,
        out_shape=jax.ShapeDtypeStruct((db, ATTN_HEADS, HEAD_DIM), F32),
        compiler_params=_params("arbitrary"),
        name="dsa_sample_attend",
    )(page_table, q_heads, k_new, v_new, mask, take_new, cache_k, cache_v)


def _mid_sample_a_kernel(x_ref, attn_ref, u_ref, st_ref, pw_ref, ps_ref, wo_ref, gm_ref, wmq_ref,
                         h1_ref, qm_ref, *, past):
    u = u_ref[...]
    rows = u.shape[0]
    sums, cnts = [], []
    for g, w in enumerate(POOL_WINDOWS):
        sl = slice(g * POOL_CG, (g + 1) * POOL_CG)
        s = u[:, sl]
        for j in range(1, w):
            s = s + st_ref[POOL_MAX - 1 - j][:, sl]
        sums.append(s)
        cnts.append(jnp.full((rows, 1), float(min(w, past + 1)), F32))
    pool = _pool_project(sums, u, cnts, pw_ref, ps_ref)
    h1 = x_ref[...] + _dot(attn_ref[...], wo_ref[0:ATTN_WIDTH, :]) \
        + _dot(pool.astype(BF16), wo_ref[ATTN_WIDTH:, :])
    h1_ref[...] = h1
    qm_ref[...] = _dot(_rms(h1, gm_ref[...]).astype(BF16), wmq_ref[...]).astype(BF16)


def _mid_sample_a(x, attn, u, state_t, pool_w, pool_scale, w_o, g_mem, w_mq, past):
    db = x.shape[0]
    full = lambda a: pl.BlockSpec(a.shape, lambda i, nd=a.ndim: (0,) * nd)
    args = (x, attn, u, state_t, pool_w, pool_scale, w_o, g_mem, w_mq)
    return pl.pallas_call(
        functools.partial(_mid_sample_a_kernel, past=past),
        grid=(1,),
        in_specs=[full(a) for a in args],
        out_specs=[pl.BlockSpec((db, D_MODEL), lambda i: (0, 0)),
                   pl.BlockSpec((db, MEM_WIDTH), lambda i: (0, 0))],
        out_shape=[jax.ShapeDtypeStruct((db, D_MODEL), F32), jax.ShapeDtypeStruct((db, MEM_WIDTH), BF16)],
        compiler_params=_params("arbitrary"),
        name="mid_sample_a",
    )(*args)


def _mem_sample_kernel(q_ref, mk_ref, mv_ref, o_ref):
    rows = 2 * MEM_HEADS
    n = mk_ref.shape[1]
    q_all = q_ref[0].astype(F32)
    q = jnp.concatenate([q_all[:, (r % MEM_HEADS) * MEM_HEAD_DIM:(r % MEM_HEADS + 1) * MEM_HEAD_DIM]
                         for r in range(rows)], axis=0).astype(BF16)
    logits = _dot_nt(q, mk_ref[0].astype(BF16)) * (MEM_HEAD_DIM ** -0.5)
    own = (lax.broadcasted_iota(jnp.int32, (rows, n), 1) % MEM_HEADS
           == lax.broadcasted_iota(jnp.int32, (rows, n), 0) % MEM_HEADS)
    logits = jnp.where(own, logits, -jnp.inf)
    m = jnp.max(logits, axis=1, keepdims=True)
    p = jnp.exp(logits - m)
    p = p / jnp.sum(p, axis=1, keepdims=True)
    o = _dot(p.astype(BF16), mv_ref[0].astype(BF16))
    for h in range(MEM_HEADS):
        o_ref[0, :, h * MEM_HEAD_DIM:(h + 1) * MEM_HEAD_DIM] = o[h:h + 1, :]


def _mem_sample(qm, mk, mv):
    db = qm.shape[0]
    per = lambda b: (b, 0, 0)
    return pl.pallas_call(
        _mem_sample_kernel,
        grid=(db,),
        in_specs=[pl.BlockSpec((1, 1, MEM_WIDTH), per),
                  pl.BlockSpec((1, N_MEM * MEM_HEADS, MEM_HEAD_DIM), per),
                  pl.BlockSpec((1, N_MEM * MEM_HEADS, MEM_HEAD_DIM), per)],
        out_specs=pl.BlockSpec((1, 1, MEM_WIDTH), per),
        out_shape=jax.ShapeDtypeStruct((db, 1, MEM_WIDTH), F32),
        compiler_params=_params("parallel"),
        name="mem_sample",
    )(qm.reshape(db, 1, MEM_WIDTH), mk, mv)


def _sample_join_kernel(h1_ref, o_ref, wmo_ref, gf_ref, h2_any, tok_any, h2_ref, tok_ref):
    del h2_any, tok_any
    db = h1_ref.shape[0]
    h2_ref[...] = jnp.zeros_like(h2_ref)
    tok_ref[...] = jnp.zeros_like(tok_ref)

    @pl.when(pl.program_id(0) == 0)
    def _():
        h2 = h1_ref[...] + _dot(o_ref[...].astype(BF16), wmo_ref[...])
        h2_ref[0:db, :] = h2
        tok_ref[0:db, :] = _rms(h2, gf_ref[...]).astype(BF16)


def _sample_join(h1, o, w_mo, g_ffn, h2_all, tok_all, n_prompt):
    db = h1.shape[0]
    n_rows = h2_all.shape[0]
    assert db <= SAMPLE_ROWS and n_prompt % SAMPLE_ROWS == 0 and n_rows % SAMPLE_ROWS == 0
    first = n_prompt // SAMPLE_ROWS
    f2 = lambda j: (0, 0)
    tail = lambda j: (first + j, 0)
    return pl.pallas_call(
        _sample_join_kernel,
        grid=((n_rows - n_prompt) // SAMPLE_ROWS,),
        in_specs=[pl.BlockSpec((db, D_MODEL), f2),
                  pl.BlockSpec((db, MEM_WIDTH), f2),
                  pl.BlockSpec((MEM_WIDTH, D_MODEL), f2),
                  pl.BlockSpec((1, D_MODEL), f2),
                  pl.BlockSpec(memory_space=pl.ANY),
                  pl.BlockSpec(memory_space=pl.ANY)],
        out_specs=[pl.BlockSpec((SAMPLE_ROWS, D_MODEL), tail), pl.BlockSpec((SAMPLE_ROWS, D_MODEL), tail)],
        out_shape=[jax.ShapeDtypeStruct(h2_all.shape, F32), jax.ShapeDtypeStruct(tok_all.shape, BF16)],
        input_output_aliases={4: 0, 5: 1},
        compiler_params=_params("arbitrary"),
        name="sample_join",
    )(h1, o, w_mo, g_ffn, h2_all, tok_all)


def _rope_tables(pos):
    half = HEAD_DIM // 2
    inv = ROPE_THETA ** (-jnp.arange(half, dtype=F32) / half)
    ang = pos.astype(F32)[:, None] * inv[None, :]
    cos = jnp.cos(ang)
    sin = jnp.sin(ang)
    return jnp.tile(cos, (1, 4)), jnp.concatenate([-sin, sin, -sin, sin], axis=1)


def _pick_tile(n, pref):
    t = min(n, pref)
    while n % t:
        t //= 2
    return t


def kernel(x_prompt, x_sample, mem_prompt, cache_k, cache_v, cache_idx_k, state_pool, cache_mem_k, cache_mem_v, page_table, norm_mix, w_in, pool_w, pool_scale, w_o, norm_mem, w_mq, w_mk, w_mv, w_mo, norm_ffn, router_w, router_b, w_up, b_up, w_down, b_down, norm_final):
    B, T = x_prompt.shape[:2]
    DB, DS = x_sample.shape[:2]
    assert DS == 1 and w_in.shape[0] == 1
    n_pages = page_table.shape[1]
    past = n_pages * PAGE_SIZE
    NP = B * T

    wi_ = w_in[0]
    wq, wk, wv, wu_, wqi, wki, wwi = jnp.split(wi_, [512, 576, 640, 1152, 1664, 1728], axis=1)
    w_all = jnp.concatenate(
        [wq, wqi, wu_, wk, wk, wki, wki, wv, wv, wwi, jnp.zeros((D_MODEL, LANES - IDX_HEADS), F32)],
        axis=1).astype(BF16)
    g_mix = norm_mix[0][None, :]
    g_mem = norm_mem[0][None, :]
    g_ffn = norm_ffn[0][None, :]
    g_fin = norm_final[None, :]
    pw = pool_w[0].astype(BF16)
    ps = pool_scale[0][None, :]
    wo = w_o[0].astype(BF16)
    wmq = w_mq[0].reshape(D_MODEL, MEM_WIDTH).astype(BF16)
    wkv = jnp.concatenate([w_mk[0].reshape(D_MODEL, MEM_WIDTH), w_mv[0].reshape(D_MODEL, MEM_WIDTH)],
                          axis=1).astype(BF16)
    wmo = w_mo[0].reshape(MEM_WIDTH, D_MODEL).astype(BF16)
    rw = router_w[0].astype(BF16)
    rb = router_b[0]
    wup = w_up[0].astype(BF16)
    wdn = w_down[0].astype(BF16)
    bup = b_up[0][:, None, :]
    bdn = b_down[0][:, None, :]

    xp = x_prompt.reshape(NP, D_MODEL)
    tm_in = _pick_tile(T, 1024)
    cos_p, sin_p = _rope_tables(jnp.arange(T, dtype=jnp.int32))
    q, qi, u, kk, kiki, vvt, k_pt, ki_pt, v_pt, wi_p = _in_project(xp, g_mix, w_all, cos_p, sin_p, tm_in,
                                                                   T // tm_in, batch=B)
    attn = _dsa_prompt(q, qi, wi_p, kk, kiki, vvt, B, T)
    mk_p, mv_p = _mem_kv(mem_prompt.reshape(B * N_MEM, D_MODEL), wkv)
    tm_moe = min(MOE_TILE, NP)
    n_tiles = -(-(NP + SAMPLE_ROWS) // tm_moe)
    assert (n_tiles - 1) * tm_moe < NP and NP + SAMPLE_ROWS <= n_tiles * tm_moe
    h2_all, tok_all = _mid_prompt(xp, attn, u, mk_p.reshape(B, N_MEM, MEM_WIDTH), mv_p.reshape(B, N_MEM, MEM_WIDTH),
                                  pw, ps, wo, g_mem, wmq, wmo, g_ffn, B, T, _pick_tile(T, 1024), n_tiles * tm_moe)

    xs = x_sample.reshape(DB, D_MODEL)
    cos_s, sin_s = _rope_tables(jnp.full((DB,), past, jnp.int32))
    q_s, qi_s, u_s, kk_s, kiki_s, vv_s, k_s, ki_s, v_s, wi_s = _in_project(xs, g_mix, w_all, cos_s, sin_s, DB, 1)
    mask, take_new = _dsa_sample_mask(
        page_table, qi_s.reshape(DB, IDX_HEADS, IDX_DIM), wi_s[:, :IDX_HEADS, None],
        kiki_s[:, None, :IDX_DIM], jnp.swapaxes(cache_idx_k, 2, 3))
    attn_s = _dsa_sample_attend(
        page_table, q_s.reshape(DB, ATTN_HEADS, HEAD_DIM), kk_s[:, None, :HEAD_DIM], vv_s[:, None, :HEAD_DIM],
        mask, take_new, jnp.swapaxes(cache_k, 2, 3), jnp.swapaxes(cache_v, 2, 3))
    attn_s = attn_s.reshape(DB, ATTN_WIDTH).astype(BF16)
    h1_s, qm_s = _mid_sample_a(xs, attn_s, u_s, jnp.transpose(state_pool[0], (1, 0, 2)), pw, ps, wo, g_mem, wmq, past)
    o_s = _mem_sample(qm_s, cache_mem_k.reshape(DB, N_MEM * MEM_HEADS, MEM_HEAD_DIM),
                      cache_mem_v.reshape(DB, N_MEM * MEM_HEADS, MEM_HEAD_DIM))
    h2_all, tok_all = _sample_join(h1_s, o_s.reshape(DB, MEM_WIDTH), wmo, g_ffn, h2_all, tok_all, NP)

    rank, gate, cnt = _router(tok_all, rw.T, rb[:, None], tm_moe, NP + DB)
    y_p, y_s = _moe(tok_all, h2_all, rank, gate, cnt.reshape(-1), wup, bup, wdn, bdn, g_fin, tm_moe, NP)
    y_prompt = y_p.reshape(B, T, D_MODEL)
    y_sample = y_s[:DB].reshape(DB, 1, D_MODEL)

    new_pool_p = u.reshape(B, T, POOL_WIDTH)[:, T - (POOL_MAX - 1):][None]
    new_pool_s = jnp.concatenate([state_pool[0][:, 1:], u_s[:, None, :]], axis=1)[None]
    return (y_prompt, y_sample,
            jnp.swapaxes(k_pt, 1, 2)[None], jnp.swapaxes(v_pt, 1, 2)[None], jnp.swapaxes(ki_pt, 1, 2)[None],
            new_pool_p,
            mk_p.reshape(1, B, N_MEM, MEM_HEADS, MEM_HEAD_DIM), mv_p.reshape(1, B, N_MEM, MEM_HEADS, MEM_HEAD_DIM),
            k_s.reshape(1, DB, 1, HEAD_DIM), v_s.reshape(1, DB, 1, HEAD_DIM), ki_s.reshape(1, DB, 1, IDX_DIM),
            new_pool_s)
```

```python
import functools

import jax
import jax.numpy as jnp
from jax import lax
from jax.experimental import pallas as pl
from jax.experimental.pallas import tpu as pltpu

D_MODEL = 1024
HEAD_DIM = 64
ATTN_WIDTH = 512
ATTN_HEADS = 8
IDX_HEADS = 8
IDX_DIM = 64
TOPK_MAX = 256
Q_BLOCK = 128
PAGE_SIZE = 128
POOL_WIDTH = 512
POOL_WINDOWS = (2, 4, 8, 16)
POOL_CG = 128
POOL_MAX = 16
N_MEM = 256
MEM_HEADS = 4
MEM_HEAD_DIM = 128
MEM_WIDTH = MEM_HEADS * MEM_HEAD_DIM
N_EXPERTS = 32
TOP_K = 4
D_FF = 1024
SWIGLU_LIMIT = 7.0
SWIGLU_ALPHA = 1.702
ROPE_THETA = 10000.0
EPS = 1e-6
NEG = -1e30

LANES = 128
MOE_TILE = 13 * LANES
MOE_FIRST_ROWS = (192, 208, 224, 240, 256)
MOE_TAIL_ROWS = 128
SAMPLE_ROWS = 128
DSA_Q = 256
DSA_KCHUNK = 256
IN_COLS = 3 * 512 + 4 * LANES
VMEM_LIMIT = 56 * 1024 * 1024

BF16 = jnp.bfloat16
F32 = jnp.float32


def _dot(a, b):
    return jnp.dot(a, b, preferred_element_type=F32)


def _dot_nt(a, b):
    return lax.dot_general(a, b, (((1,), (1,)), ((), ())), preferred_element_type=F32)


def _dot_tn(a, b):
    return lax.dot_general(a, b, (((0,), (0,)), ((), ())), preferred_element_type=F32)


def _rms(x, g):
    return x * lax.rsqrt(jnp.mean(x * x, axis=-1, keepdims=True) + EPS) * g


def _params(*sem):
    return pltpu.CompilerParams(dimension_semantics=sem, vmem_limit_bytes=VMEM_LIMIT)


def _in_kernel(x_ref, g_ref, w_ref, cos_ref, sin_ref,
               q_ref, qi_ref, u_ref, kk_ref, kiki_ref, vv_ref, k_ref, ki_ref, v_ref, wi_ref, *, feature_major):
    xn = _rms(x_ref[...], g_ref[...]).astype(BF16)
    z = _dot(xn, w_ref[...])
    cos = cos_ref[...]
    sin = sin_ref[...]
    lane = lax.broadcasted_iota(jnp.int32, cos.shape, 1)
    first_half = (lane % HEAD_DIM) < (HEAD_DIM // 2)

    def rope(zc):
        swapped = jnp.where(first_half,
                            pltpu.roll(zc, LANES - HEAD_DIM // 2, 1),
                            pltpu.roll(zc, HEAD_DIM // 2, 1))
        return zc * cos + swapped * sin

    for c in range(4):
        q_c = rope(z[:, c * LANES:(c + 1) * LANES]) * (HEAD_DIM ** -0.5)
        qi_c = rope(z[:, 512 + c * LANES:512 + (c + 1) * LANES]) * (IDX_DIM ** -0.5)
        q_ref[:, c * LANES:(c + 1) * LANES] = q_c.astype(BF16)
        qi_ref[:, c * LANES:(c + 1) * LANES] = qi_c.astype(BF16)
    u_ref[...] = z[:, 1024:1536]
    kk = rope(z[:, 1536:1664])
    kiki = rope(z[:, 1664:1792])
    vv = z[:, 1792:1920]
    kk_ref[...] = kk.astype(BF16)
    kiki_ref[...] = kiki.astype(BF16)
    wi_ref[...] = z[:, 1920:2048]
    if feature_major:
        kk_t = jnp.transpose(kk)
        kiki_t = jnp.transpose(kiki)
        vv_t = jnp.transpose(vv)
        vv_ref[0] = vv_t.astype(BF16)
        k_ref[0] = kk_t[0:HEAD_DIM, :]
        ki_ref[0] = kiki_t[0:IDX_DIM, :]
        v_ref[0] = vv_t[0:HEAD_DIM, :]
    else:
        vv_ref[...] = vv.astype(BF16)
        k_ref[...] = kk[:, :HEAD_DIM]
        ki_ref[...] = kiki[:, :IDX_DIM]
        v_ref[...] = vv[:, :HEAD_DIM]


def _in_project(x, g, w_all, cos, sin, tm, table_blocks, batch=None):
    n = x.shape[0]
    row = lambda i: (i, 0)
    tab = lambda i: (i % table_blocks, 0)
    fixed = lambda i: (0, 0)
    feature_major = batch is not None
    if feature_major:
        t_len = n // batch
        assert t_len == table_blocks * tm
        fm = lambda feat, dt: jax.ShapeDtypeStruct((batch, feat, t_len), dt)
        fm_spec = lambda feat: pl.BlockSpec((1, feat, tm), lambda i: (i // table_blocks, 0, i % table_blocks))
    else:
        fm = lambda feat, dt: jax.ShapeDtypeStruct((n, feat), dt)
        fm_spec = lambda feat: pl.BlockSpec((tm, feat), row)
    outs = [
        jax.ShapeDtypeStruct((n, 512), BF16),
        jax.ShapeDtypeStruct((n, 512), BF16),
        jax.ShapeDtypeStruct((n, 512), F32),
        jax.ShapeDtypeStruct((n, LANES), BF16),
        jax.ShapeDtypeStruct((n, LANES), BF16),
        fm(LANES, BF16),
        fm(HEAD_DIM, F32),
        fm(IDX_DIM, F32),
        fm(HEAD_DIM, F32),
        jax.ShapeDtypeStruct((n, LANES), F32),
    ]
    out_specs = ([pl.BlockSpec((tm, s.shape[1]), row) for s in outs[:5]]
                 + [fm_spec(LANES), fm_spec(HEAD_DIM), fm_spec(IDX_DIM), fm_spec(HEAD_DIM)]
                 + [pl.BlockSpec((tm, LANES), row)])
    return pl.pallas_call(
        functools.partial(_in_kernel, feature_major=feature_major),
        grid=(n // tm,),
        in_specs=[pl.BlockSpec((tm, D_MODEL), row),
                  pl.BlockSpec((1, D_MODEL), fixed),
                  pl.BlockSpec((D_MODEL, IN_COLS), fixed),
                  pl.BlockSpec((tm, LANES), tab),
                  pl.BlockSpec((tm, LANES), tab)],
        out_specs=out_specs,
        out_shape=outs,
        compiler_params=_params("parallel"),
        name="in_project",
    )(x, g, w_all, cos, sin)


def _f32_to_key(x):
    bits = lax.bitcast_convert_type(x, jnp.int32)
    return jnp.where(bits < 0, bits ^ jnp.int32(0x7FFFFFFF), bits)


def _key_to_f32(key):
    bits = jnp.where(key < 0, key ^ jnp.int32(0x7FFFFFFF), key)
    return lax.bitcast_convert_type(bits, F32)


def _kth_largest(count_ge, k, shape):
    def body(it, tau):
        cand = tau + jnp.left_shift(jnp.int32(1), jnp.int32(31) - it)
        return jnp.where(count_ge(_key_to_f32(cand)) >= k, cand, tau)

    return _key_to_f32(lax.fori_loop(0, 32, body, jnp.full(shape, -2 ** 31, jnp.int32)))


def _kth_largest_between(count_ge, k, lo, hi):
    key_lo = _f32_to_key(lo)
    span = _f32_to_key(hi) - key_lo
    n_bits = jnp.max(32 - lax.clz(span))
    flip = jnp.int32(-2 ** 31)

    def body(it, off):
        cand = off | jnp.left_shift(jnp.int32(1), n_bits - 1 - it)
        ok = jnp.logical_and(count_ge(_key_to_f32(key_lo + cand)) >= k, (cand ^ flip) <= (span ^ flip))
        return jnp.where(ok, cand, off)

    return _key_to_f32(key_lo + lax.fori_loop(0, n_bits, body, jnp.zeros_like(key_lo)))


def _strict_triangle(n, lower):
    r = lax.broadcasted_iota(jnp.int32, (n, n), 0)
    c = lax.broadcasted_iota(jnp.int32, (n, n), 1)
    return jnp.where((r > c) if lower else (r < c), 1.0, 0.0).astype(BF16)


def _take_first_ties(eq, need_f, tied, axis):
    n = eq.shape[axis]
    cshape = (1, eq.shape[1]) if axis == 0 else (eq.shape[0], 1)

    def ranked():
        tri = _strict_triangle(LANES, lower=(axis == 0))
        carry = jnp.zeros(cshape, F32)
        parts = []
        for c in range(n // LANES):
            sl = slice(c * LANES, (c + 1) * LANES)
            eq_f = jnp.where(eq[sl, :] if axis == 0 else eq[:, sl], 1.0, 0.0)
            eq_b = eq_f.astype(BF16)
            rank = (_dot(tri, eq_b) if axis == 0 else _dot(eq_b, tri)) + carry
            parts.append(jnp.where(rank < need_f, eq_f, 0.0))
            carry = carry + jnp.sum(eq_f, axis=axis, keepdims=True)
        return jnp.concatenate(parts, axis=axis), carry

    def plain():
        return jnp.where(eq, 1.0, 0.0), jnp.zeros(cshape, F32)

    return lax.cond(jnp.max(jnp.where(tied, 1, 0)) > 0, ranked, plain)


def _dsa_prompt_kernel(q_ref, qi_ref, wi_ref, kk_ref, kiki_ref, vvt_ref, o_ref, sm_ref, bias_ref, *, topk, block):
    i = block
    qb = q_ref.shape[0]
    lk = kk_ref.shape[1]
    select = lk > topk
    ck = min(DSA_KCHUNK, lk)
    chunks = [slice(c * ck, (c + 1) * ck) for c in range(lk // ck)]
    upper_rows = lax.broadcasted_iota(jnp.int32, (LANES, qb), 0) < HEAD_DIM

    def head_operand(ref, h):
        pair_t = jnp.transpose(ref[:, (h // 2) * LANES:(h // 2 + 1) * LANES].astype(F32))
        keep = upper_rows if h % 2 == 0 else jnp.logical_not(upper_rows)
        return jnp.where(keep, pair_t, 0.0).astype(BF16)

    pos_q = i * qb + lax.broadcasted_iota(jnp.int32, (1, qb), 1)

    def causal(sl):
        return sl.start + lax.broadcasted_iota(jnp.int32, (ck, 1), 0) <= pos_q

    if select:
        w_t = jnp.transpose(wi_ref[...])[0:IDX_HEADS, :] * (IDX_HEADS ** -0.5)
        qi_ops = [head_operand(qi_ref, h) for h in range(IDX_HEADS)]
        for sl in chunks:
            keys = kiki_ref[0, sl, :]
            score = jnp.zeros((ck, qb), F32)
            for h in range(IDX_HEADS):
                score = score + jnp.maximum(_dot(keys, qi_ops[h]), 0.0) * w_t[h:h + 1, :]
            sm_ref[sl, :] = jnp.where(causal(sl), score, -jnp.inf)

        def count_ge(cand):
            return jnp.sum(jnp.where(sm_ref[0:lk, :] >= cand, 1, 0), axis=0, keepdims=True)

        if ck == topk:
            class_max = sm_ref[chunks[0], :]
            for sl in chunks[1:]:
                class_max = jnp.maximum(class_max, sm_ref[sl, :])
            tau = _kth_largest_between(count_ge, topk, jnp.min(class_max, axis=0, keepdims=True),
                                       jnp.max(class_max, axis=0, keepdims=True))
        else:
            tau = _kth_largest(count_ge, topk, (1, qb))
        sm = sm_ref[0:lk, :]
        gt = sm > tau
        eq = sm == tau
        n_gt = jnp.sum(jnp.where(gt, 1, 0), axis=0, keepdims=True)
        n_eq = jnp.sum(jnp.where(eq, 1, 0), axis=0, keepdims=True)
        tied = n_gt + n_eq > topk
        need_f = jnp.where(tied, topk - n_gt, lk + 1).astype(F32)
        take_eq, _ = _take_first_ties(eq, need_f, tied, axis=0)
        bias_ref[0:lk, :] = jnp.where(jnp.logical_or(gt, take_eq > 0.5), 0.0, NEG)

    def chunk_bias(sl):
        if select:
            return bias_ref[sl, :]
        return jnp.where(causal(sl), 0.0, NEG)

    def fold8(x, op):
        return op(x.reshape(ck // 8, 8, qb), axis=0)

    for j in range(ATTN_HEADS // 2):
        outs = []
        for h in (2 * j, 2 * j + 1):
            q_op = head_operand(q_ref, h)
            mx8 = jnp.full((8, qb), -jnp.inf, F32)
            for sl in chunks:
                logits = _dot(kk_ref[0, sl, :], q_op) + chunk_bias(sl)
                sm_ref[sl, :] = logits
                mx8 = jnp.maximum(mx8, fold8(logits, jnp.max))
            m = jnp.max(mx8, axis=0, keepdims=True)
            den8 = jnp.zeros((8, qb), F32)
            acc = jnp.zeros((LANES, qb), F32)
            for sl in chunks:
                p = jnp.exp(sm_ref[sl, :] - m)
                den8 = den8 + fold8(p, jnp.sum)
                acc = acc + _dot(vvt_ref[0, :, sl], p.astype(BF16))
            outs.append(acc * (1.0 / jnp.sum(den8, axis=0, keepdims=True)))
        pair_t = jnp.where(upper_rows, outs[0], outs[1])
        o_ref[0, :, j * LANES:(j + 1) * LANES] = jnp.transpose(pair_t).astype(BF16)


def _dsa_prompt_block(q, qi, wi, kk, kiki, vvt, batch, t_len, block, topk):
    nb = t_len // DSA_Q
    lk = (block + 1) * DSA_Q
    blk = lambda b: (b * nb + block, 0)
    seq = lambda b: (b, 0, 0)
    return pl.pallas_call(
        functools.partial(_dsa_prompt_kernel, topk=topk, block=block),
        grid=(batch,),
        in_specs=[pl.BlockSpec((DSA_Q, 512), blk),
                  pl.BlockSpec((DSA_Q, 512), blk),
                  pl.BlockSpec((DSA_Q, LANES), blk),
                  pl.BlockSpec((1, lk, LANES), seq),
                  pl.BlockSpec((1, lk, LANES), seq),
                  pl.BlockSpec((1, LANES, lk), seq)],
        out_specs=pl.BlockSpec((1, DSA_Q, 512), seq),
        out_shape=jax.ShapeDtypeStruct((batch, DSA_Q, 512), BF16),
        scratch_shapes=[pltpu.VMEM((lk, DSA_Q), F32), pltpu.VMEM((lk, DSA_Q), F32)],
        compiler_params=_params("parallel"),
        name="dsa_prompt_%d" % lk,
    )(q, qi, wi, kk, kiki, vvt)


def _dsa_prompt(q, qi, wi, kk, kiki, vvt, batch, t_len):
    nb = t_len // DSA_Q
    topk = min(TOPK_MAX, t_len // 4)
    assert topk % DSA_Q == 0 and t_len % DSA_Q == 0
    kk, kiki = (a.reshape(batch, t_len, LANES) for a in (kk, kiki))
    parts = [_dsa_prompt_block(q, qi, wi, kk, kiki, vvt, batch, t_len, block, topk) for block in range(nb)]
    return jnp.concatenate(parts, axis=1).reshape(batch * t_len, 512)


def _mem_kv_kernel(m_ref, w_ref, k_ref, v_ref):
    z = _dot(m_ref[...].astype(BF16), w_ref[...])
    k_ref[...] = z[:, :MEM_WIDTH]
    v_ref[...] = z[:, MEM_WIDTH:]


def _mem_kv(mem, w_kv):
    n = mem.shape[0]
    tm = min(n, 512)
    row = lambda i: (i, 0)
    return pl.pallas_call(
        _mem_kv_kernel,
        grid=(n // tm,),
        in_specs=[pl.BlockSpec((tm, D_MODEL), row),
                  pl.BlockSpec((D_MODEL, 2 * MEM_WIDTH), lambda i: (0, 0))],
        out_specs=[pl.BlockSpec((tm, MEM_WIDTH), row), pl.BlockSpec((tm, MEM_WIDTH), row)],
        out_shape=[jax.ShapeDtypeStruct((n, MEM_WIDTH), F32)] * 2,
        compiler_params=_params("parallel"),
        name="mem_kv",
    )(mem, w_kv)


def _pool_project(window_sums, u, cnts, pw_ref, ps_ref):
    ys = []
    for g in range(len(POOL_WINDOWS)):
        sl = slice(g * POOL_CG, (g + 1) * POOL_CG)
        pooled = window_sums[g] / cnts[g] - u[:, sl]
        ys.append(_dot(pooled.astype(BF16), pw_ref[g]) * ps_ref[:, sl])
    return jnp.concatenate(ys, axis=1)


def _mid_prompt_kernel(x_ref, attn_ref, u_ref, halo_ref, mk_ref, mv_ref, pw_ref, ps_ref, wo_ref,
                       gm_ref, wmq_ref, wmo_ref, gf_ref, h2_ref, tok_ref, e_ref, *, nt, n_real):
    step = pl.program_id(0)

    @pl.when(step >= n_real)
    def _():
        h2_ref[...] = jnp.zeros_like(h2_ref)
        tok_ref[...] = jnp.zeros_like(tok_ref)

    @pl.when(step < n_real)
    def _():
        _mid_prompt_rows(x_ref, attn_ref, u_ref, halo_ref, mk_ref, mv_ref, pw_ref, ps_ref, wo_ref,
                         gm_ref, wmq_ref, wmo_ref, gf_ref, h2_ref, tok_ref, e_ref, step % nt)


def _mid_prompt_rows(x_ref, attn_ref, u_ref, halo_ref, mk_ref, mv_ref, pw_ref, ps_ref, wo_ref,
                     gm_ref, wmq_ref, wmo_ref, gf_ref, h2_ref, tok_ref, e_ref, i):
    tm = x_ref.shape[0]
    u = u_ref[...]
    halo = halo_ref[...]
    e_ref[0:POOL_MAX, :] = jnp.where(i == 0, jnp.zeros_like(halo), halo)
    e_ref[POOL_MAX:POOL_MAX + tm, :] = u
    pos = i * tm + lax.broadcasted_iota(jnp.int32, (tm, 1), 0)
    sums, cnts = [], []
    for g, w in enumerate(POOL_WINDOWS):
        sl = slice(g * POOL_CG, (g + 1) * POOL_CG)
        s = u[:, sl]
        for j in range(1, w):
            s = s + e_ref[POOL_MAX - j:POOL_MAX - j + tm, sl]
        sums.append(s)
        cnts.append(jnp.minimum(w, pos + 1).astype(F32))
    pool = _pool_project(sums, u, cnts, pw_ref, ps_ref)
    h1 = x_ref[...] + _dot(attn_ref[...], wo_ref[0:ATTN_WIDTH, :]) \
        + _dot(pool.astype(BF16), wo_ref[ATTN_WIDTH:, :])

    qm = _dot(_rms(h1, gm_ref[...]).astype(BF16), wmq_ref[...]).astype(BF16)
    mk = mk_ref[0].astype(BF16)
    mv = mv_ref[0].astype(BF16)
    outs = []
    for h in range(MEM_HEADS):
        sl = slice(h * MEM_HEAD_DIM, (h + 1) * MEM_HEAD_DIM)
        logits = _dot_nt(qm[:, sl], mk[:, sl]) * (MEM_HEAD_DIM ** -0.5)
        m = jnp.max(logits, axis=1, keepdims=True)
        p = jnp.exp(logits - m)
        p = p / jnp.sum(p, axis=1, keepdims=True)
        outs.append(_dot(p.astype(BF16), mv[:, sl]))
    o = jnp.concatenate(outs, axis=1)
    h2 = h1 + _dot(o.astype(BF16), wmo_ref[...])
    h2_ref[...] = h2
    tok_ref[...] = _rms(h2, gf_ref[...]).astype(BF16)


def _mid_prompt(x, attn, u, mk, mv, pool_w, pool_scale, w_o, g_mem, w_mq, w_mo, g_ffn, batch, t_len, tm, n_rows):
    nt = t_len // tm
    hb = tm // POOL_MAX
    n_real = batch * nt
    n_steps = -(-n_rows // tm)
    last = n_real - 1
    row = lambda s: (jnp.minimum(s, last), 0)
    halo = lambda s: (jnp.maximum(jnp.minimum(s, last) * hb - 1, 0), 0)
    fixed2 = lambda s: (0, 0)
    fixed3 = lambda s: (0, 0, 0)
    perb = lambda s: (jnp.minimum(s, last) // nt, 0, 0)
    out_row = lambda s: (s, 0)
    n = n_rows
    return pl.pallas_call(
        functools.partial(_mid_prompt_kernel, nt=nt, n_real=n_real),
        grid=(n_steps,),
        in_specs=[pl.BlockSpec((tm, D_MODEL), row),
                  pl.BlockSpec((tm, ATTN_WIDTH), row),
                  pl.BlockSpec((tm, POOL_WIDTH), row),
                  pl.BlockSpec((POOL_MAX, POOL_WIDTH), halo),
                  pl.BlockSpec((1, N_MEM, MEM_WIDTH), perb),
                  pl.BlockSpec((1, N_MEM, MEM_WIDTH), perb),
                  pl.BlockSpec((len(POOL_WINDOWS), POOL_CG, POOL_CG), fixed3),
                  pl.BlockSpec((1, POOL_WIDTH), fixed2),
                  pl.BlockSpec((D_MODEL, D_MODEL), fixed2),
                  pl.BlockSpec((1, D_MODEL), fixed2),
                  pl.BlockSpec((D_MODEL, MEM_WIDTH), fixed2),
                  pl.BlockSpec((MEM_WIDTH, D_MODEL), fixed2),
                  pl.BlockSpec((1, D_MODEL), fixed2)],
        out_specs=[pl.BlockSpec((tm, D_MODEL), out_row), pl.BlockSpec((tm, D_MODEL), out_row)],
        out_shape=[jax.ShapeDtypeStruct((n, D_MODEL), F32), jax.ShapeDtypeStruct((n, D_MODEL), BF16)],
        scratch_shapes=[pltpu.VMEM((POOL_MAX + tm, POOL_WIDTH), F32)],
        compiler_params=_params("arbitrary"),
        name="mid_prompt",
    )(x, attn, u, u, mk, mv, pool_w, pool_scale, w_o, g_mem, w_mq, w_mo, g_ffn)


def _router_kernel(tok_ref, rw_ref, rb_ref, rank_ref, gate_ref, cnt_ref, *, n_valid):
    tm = tok_ref.shape[0]
    token = pl.program_id(0) * tm + lax.broadcasted_iota(jnp.int32, (1, tm), 1)
    valid = token < n_valid
    logits = _dot_nt(rw_ref[...], tok_ref[...]) + rb_ref[...]
    eid = lax.broadcasted_iota(jnp.int32, (N_EXPERTS, tm), 0)
    vals, hots = [], []
    cur = logits
    for _ in range(TOP_K):
        m = jnp.max(cur, axis=0, keepdims=True)
        idx = jnp.min(jnp.where(cur == m, eid, N_EXPERTS), axis=0, keepdims=True)
        hot = eid == idx
        vals.append(m)
        hots.append(hot)
        cur = jnp.where(hot, -jnp.inf, cur)
    exps = [jnp.exp(v - vals[0]) for v in vals]
    den = exps[0] + exps[1] + exps[2] + exps[3]
    gate = jnp.zeros((N_EXPERTS, tm), F32)
    sel = jnp.zeros((N_EXPERTS, tm), jnp.bool_)
    for hot, ex in zip(hots, exps):
        gate = jnp.where(hot, ex / den, gate)
        sel = jnp.logical_or(sel, hot)
    sel = jnp.logical_and(sel, valid)
    gate_ref[0] = gate

    upper = _strict_triangle(LANES, lower=False)
    carry = jnp.zeros((N_EXPERTS, 1), F32)
    for c in range(tm // LANES):
        sl = slice(c * LANES, (c + 1) * LANES)
        sel_c = sel[:, sl]
        sel_b = jnp.where(sel_c, 1.0, 0.0).astype(BF16)
        rank = _dot(sel_b, upper) + carry
        rank_ref[0, :, sl] = jnp.where(sel_c, rank, -1.0).astype(jnp.int32)
        carry = carry + jnp.sum(sel_b.astype(F32), axis=1, keepdims=True)
    cnt_ref[0] = carry.astype(jnp.int32)


def _router(tok, rw_t, rb, tm, n_valid):
    n = tok.shape[0]
    nt = n // tm
    t3 = lambda i: (i, 0, 0)
    return pl.pallas_call(
        functools.partial(_router_kernel, n_valid=n_valid),
        grid=(nt,),
        in_specs=[pl.BlockSpec((tm, D_MODEL), lambda i: (i, 0)),
                  pl.BlockSpec((N_EXPERTS, D_MODEL), lambda i: (0, 0)),
                  pl.BlockSpec((N_EXPERTS, 1), lambda i: (0, 0))],
        out_specs=[pl.BlockSpec((1, N_EXPERTS, tm), t3),
                   pl.BlockSpec((1, N_EXPERTS, tm), t3),
                   pl.BlockSpec((1, N_EXPERTS, 1), t3)],
        out_shape=[jax.ShapeDtypeStruct((nt, N_EXPERTS, tm), jnp.int32),
                   jax.ShapeDtypeStruct((nt, N_EXPERTS, tm), F32),
                   jax.ShapeDtypeStruct((nt, N_EXPERTS, 1), jnp.int32)],
        compiler_params=_params("parallel"),
        name="router",
    )(tok, rw_t, rb)


def _swiglu(h):
    gate = jnp.minimum(h[:, :D_FF], SWIGLU_LIMIT)
    up = jnp.clip(h[:, D_FF:], -SWIGLU_LIMIT, SWIGLU_LIMIT)
    return gate * jax.nn.sigmoid(SWIGLU_ALPHA * gate) * (up + 1.0)


def _moe_kernel(cnt_ref, x_ref, h2_ref, rank_ref, gate_ref, wu_ref, bu_ref, wd_ref, bd_ref, gfin_ref,
                o_ref, os_ref, *, sample_row):
    i = pl.program_id(0)
    e = pl.program_id(1)
    tm = x_ref.shape[0]

    @pl.when(e == 0)
    def _():
        o_ref[...] = jnp.zeros_like(o_ref)

    n_routed = cnt_ref[i * N_EXPERTS + e]
    rank = rank_ref[0, pl.ds(e, 1), :]
    gate = gate_ref[0, pl.ds(e, 1), :]

    def run_slots(first, size):
        slot = first + lax.broadcasted_iota(jnp.int32, (size, tm), 0)
        hit = rank == slot
        onehot = jnp.where(hit, 1.0, 0.0).astype(BF16)
        xg = _dot(onehot, x_ref[...]).astype(BF16)
        act = _swiglu(_dot(xg, wu_ref[0]) + bu_ref[0])
        out = _dot(act.astype(BF16), wd_ref[0]) + bd_ref[0]
        g_slot = jnp.sum(jnp.where(hit, gate, 0.0), axis=1, keepdims=True)
        o_ref[...] += _dot_tn(onehot, (out * g_slot).astype(BF16))

    largest = MOE_FIRST_ROWS[-1]
    for lo, size in zip((0,) + MOE_FIRST_ROWS[:-1], MOE_FIRST_ROWS):
        fits = n_routed <= size if size < largest else n_routed > lo
        @pl.when(jnp.logical_and(n_routed > lo, fits))
        def _():
            run_slots(0, size)

    def body(c, carry):
        run_slots(largest + c * MOE_TAIL_ROWS, MOE_TAIL_ROWS)
        return carry

    lax.fori_loop(0, (jnp.maximum(n_routed - largest, 0) + MOE_TAIL_ROWS - 1) // MOE_TAIL_ROWS, body, 0)

    @pl.when(e == pl.num_programs(1) - 1)
    def _():
        o_ref[...] = _rms(h2_ref[...] + o_ref[...], gfin_ref[...])

        @pl.when(i == pl.num_programs(0) - 1)
        def _():
            os_ref[...] = o_ref[sample_row:sample_row + os_ref.shape[0], :]


def _moe(tok, h2, rank, gate, cnt, wu, bu, wd, bd, g_final, tm, n_prompt):
    nt = tok.shape[0] // tm
    tile = lambda i, e, c: (i, 0)
    tile3 = lambda i, e, c: (i, 0, 0)
    expert = lambda i, e, c: (e, 0, 0)
    grid_spec = pltpu.PrefetchScalarGridSpec(
        num_scalar_prefetch=1,
        grid=(nt, N_EXPERTS),
        in_specs=[pl.BlockSpec((tm, D_MODEL), tile),
                  pl.BlockSpec((tm, D_MODEL), tile),
                  pl.BlockSpec((1, N_EXPERTS, tm), tile3),
                  pl.BlockSpec((1, N_EXPERTS, tm), tile3),
                  pl.BlockSpec((1, D_MODEL, 2 * D_FF), expert),
                  pl.BlockSpec((1, 1, 2 * D_FF), expert),
                  pl.BlockSpec((1, D_FF, D_MODEL), expert),
                  pl.BlockSpec((1, 1, D_MODEL), expert),
                  pl.BlockSpec((1, D_MODEL), lambda i, e, c: (0, 0))],
        out_specs=[pl.BlockSpec((tm, D_MODEL), tile),
                   pl.BlockSpec((SAMPLE_ROWS, D_MODEL), lambda i, e, c: (0, 0))],
    )
    return pl.pallas_call(
        functools.partial(_moe_kernel, sample_row=n_prompt - (nt - 1) * tm),
        grid_spec=grid_spec,
        out_shape=[jax.ShapeDtypeStruct((n_prompt, D_MODEL), F32),
                   jax.ShapeDtypeStruct((SAMPLE_ROWS, D_MODEL), F32)],
        compiler_params=_params("arbitrary", "arbitrary"),
        name="moe",
    )(cnt, tok, h2, rank, gate, wu, bu, wd, bd, g_final)


def _page_copy(cache_ref, buf_ref, sem_ref, pt_ref, b, p, slot):
    dst = buf_ref.at[slot, :, pl.ds(pl.multiple_of(p * PAGE_SIZE, PAGE_SIZE), PAGE_SIZE)]
    return pltpu.make_async_copy(cache_ref.at[0, pt_ref[b, p]], dst, sem_ref.at[slot])


def _fetch_pages(cache_ref, buf_ref, sem_ref, pt_ref, b, slot, n_pages):
    def body(p, c):
        _page_copy(cache_ref, buf_ref, sem_ref, pt_ref, b, p, slot).start()
        return c
    lax.fori_loop(0, n_pages, body, 0, unroll=8)


def _wait_pages(cache_ref, buf_ref, sem_ref, pt_ref, b, slot, n_pages):
    def body(p, c):
        _page_copy(cache_ref, buf_ref, sem_ref, pt_ref, b, p, slot).wait()
        return c
    lax.fori_loop(0, n_pages, body, 0, unroll=8)


def _dsa_sample_score_kernel(pt_ref, qi_ref, wi_ref, kin_ref, cache_ref, score_ref, new_ref, buf_ref, sem_ref):
    b = pl.program_id(0)
    nb = pl.num_programs(0)
    past = buf_ref.shape[2]
    n_pages = past // PAGE_SIZE
    slot = b % 2

    @pl.when(b == 0)
    def _():
        _fetch_pages(cache_ref, buf_ref, sem_ref, pt_ref, b, slot, n_pages)

    @pl.when(b + 1 < nb)
    def _():
        _fetch_pages(cache_ref, buf_ref, sem_ref, pt_ref, b + 1, 1 - slot, n_pages)

    _wait_pages(cache_ref, buf_ref, sem_ref, pt_ref, b, slot, n_pages)

    qi = qi_ref[0]
    w = wi_ref[0] * (IDX_HEADS ** -0.5)
    keys_t = buf_ref[slot].astype(BF16)
    score_ref[0] = jnp.sum(jnp.maximum(_dot(qi, keys_t), 0.0) * w, axis=0, keepdims=True)
    s_new = jnp.sum(qi.astype(F32) * kin_ref[0].astype(F32), axis=1, keepdims=True)
    new_ref[0] = jnp.sum(jnp.maximum(s_new, 0.0) * w, axis=0, keepdims=True)


def _dsa_sample_select_kernel(score_ref, new_ref, mask_ref, take_new_ref, *, topk):
    score = score_ref[...]
    score_new = new_ref[...]
    past = score.shape[1]

    def count_ge(cand):
        return (jnp.sum(jnp.where(score >= cand, 1, 0), axis=1, keepdims=True)
                + jnp.where(score_new >= cand, 1, 0))

    tau = _kth_largest(count_ge, topk, score_new.shape)
    gt = score > tau
    eq = score == tau
    n_gt = jnp.sum(jnp.where(gt, 1, 0), axis=1, keepdims=True) + jnp.where(score_new > tau, 1, 0)
    n_eq_all = jnp.sum(jnp.where(eq, 1, 0), axis=1, keepdims=True) + jnp.where(score_new == tau, 1, 0)
    tied = n_gt + n_eq_all > topk
    need_f = jnp.where(tied, topk - n_gt, past + 2).astype(F32)
    take_eq, n_eq = _take_first_ties(eq, need_f, tied, axis=1)
    mask_ref[...] = jnp.where(jnp.logical_or(gt, take_eq > 0.5), 0.0, NEG)
    take_new = jnp.logical_or(score_new > tau, jnp.logical_and(score_new == tau, n_eq < need_f))
    take_new_ref[...] = jnp.where(take_new, 0.0, NEG)


def _dsa_sample_select(score, score_new, topk):
    db, past = score.shape
    full = lambda shape: pl.BlockSpec(shape, lambda i: (0, 0))
    return pl.pallas_call(
        functools.partial(_dsa_sample_select_kernel, topk=topk),
        grid=(1,),
        in_specs=[full((db, past)), full((db, 1))],
        out_specs=[full((db, past)), full((db, 1))],
        out_shape=[jax.ShapeDtypeStruct((db, past), F32), jax.ShapeDtypeStruct((db, 1), F32)],
        compiler_params=_params("arbitrary"),
        name="dsa_sample_select",
    )(score, score_new)


def _dsa_sample_mask(page_table, qi_heads, wi_col, ki_new, cache_idx_k):
    db, n_pages = page_table.shape
    past = n_pages * PAGE_SIZE
    topk = min(TOPK_MAX, (past + 1) // 4)
    per = lambda b, pt: (b, 0, 0)
    grid_spec = pltpu.PrefetchScalarGridSpec(
        num_scalar_prefetch=1,
        grid=(db,),
        in_specs=[pl.BlockSpec((1, IDX_HEADS, IDX_DIM), per),
                  pl.BlockSpec((1, IDX_HEADS, 1), per),
                  pl.BlockSpec((1, 1, IDX_DIM), per),
                  pl.BlockSpec(memory_space=pl.ANY)],
        out_specs=[pl.BlockSpec((1, 1, past), per), pl.BlockSpec((1, 1, 1), per)],
        scratch_shapes=[pltpu.VMEM((2, IDX_DIM, past), F32),
                        pltpu.SemaphoreType.DMA((2,))],
    )
    score, score_new = pl.pallas_call(
        _dsa_sample_score_kernel,
        grid_spec=grid_spec,
        out_shape=[jax.ShapeDtypeStruct((db, 1, past), F32), jax.ShapeDtypeStruct((db, 1, 1), F32)],
        compiler_params=_params("arbitrary"),
        name="dsa_sample_score",
    )(page_table, qi_heads, wi_col, ki_new, cache_idx_k)
    mask, take_new = _dsa_sample_select(score.reshape(db, past), score_new.reshape(db, 1), topk)
    return mask.reshape(db, 1, past), take_new.reshape(db, 1, 1)


def _dsa_sample_attend_kernel(pt_ref, q_ref, kn_ref, vn_ref, mask_ref, new_ref, ck_ref, cv_ref, o_ref,
                              kbuf_ref, vbuf_ref, ksem_ref, vsem_ref):
    b = pl.program_id(0)
    nb = pl.num_programs(0)
    past = kbuf_ref.shape[2]
    n_pages = past // PAGE_SIZE
    slot = b % 2

    def fetch(bb, sl):
        _fetch_pages(ck_ref, kbuf_ref, ksem_ref, pt_ref, bb, sl, n_pages)
        _fetch_pages(cv_ref, vbuf_ref, vsem_ref, pt_ref, bb, sl, n_pages)

    @pl.when(b == 0)
    def _():
        fetch(b, slot)

    @pl.when(b + 1 < nb)
    def _():
        fetch(b + 1, 1 - slot)

    _wait_pages(ck_ref, kbuf_ref, ksem_ref, pt_ref, b, slot, n_pages)
    _wait_pages(cv_ref, vbuf_ref, vsem_ref, pt_ref, b, slot, n_pages)

    q = q_ref[0]
    keys_t = kbuf_ref[slot].astype(BF16)
    vals_t = vbuf_ref[slot].astype(BF16)
    logits = _dot(q, keys_t) + mask_ref[0]
    l_new = jnp.sum(q.astype(F32) * kn_ref[0].astype(F32), axis=1, keepdims=True) + new_ref[0]
    m = jnp.maximum(jnp.max(logits, axis=1, keepdims=True), l_new)
    p = jnp.exp(logits - m)
    p_new = jnp.exp(l_new - m)
    den = jnp.sum(p, axis=1, keepdims=True) + p_new
    p = p / den
    p_new = p_new / den
    o_ref[0] = _dot_nt(p.astype(BF16), vals_t) + p_new * vn_ref[0].astype(F32)


def _dsa_sample_attend(page_table, q_heads, k_new, v_new, mask, take_new, cache_k, cache_v):
    db, n_pages = page_table.shape
    past = n_pages * PAGE_SIZE
    per = lambda b, pt: (b, 0, 0)
    grid_spec = pltpu.PrefetchScalarGridSpec(
        num_scalar_prefetch=1,
        grid=(db,),
        in_specs=[pl.BlockSpec((1, ATTN_HEADS, HEAD_DIM), per),
                  pl.BlockSpec((1, 1, HEAD_DIM), per),
                  pl.BlockSpec((1, 1, HEAD_DIM), per),
                  pl.BlockSpec((1, 1, past), per),
                  pl.BlockSpec((1, 1, 1), per),
                  pl.BlockSpec(memory_space=pl.ANY),
                  pl.BlockSpec(memory_space=pl.ANY)],
        out_specs=pl.BlockSpec((1, ATTN_HEADS, HEAD_DIM), per),
        scratch_shapes=[pltpu.VMEM((2, HEAD_DIM, past), F32),
                        pltpu.VMEM((2, HEAD_DIM, past), F32),
                        pltpu.SemaphoreType.DMA((2,)),
                        pltpu.SemaphoreType.DMA((2,))],
    )
    return pl.pallas_call(
        _dsa_sample_attend_kernel,
        grid_spec=grid_spec,
        out_shape=jax.ShapeDtypeStruct((db, ATTN_HEADS, HEAD_DIM), F32),
        compiler_params=_params("arbitrary"),
        name="dsa_sample_attend",
    )(page_table, q_heads, k_new, v_new, mask, take_new, cache_k, cache_v)


def _mid_sample_a_kernel(x_ref, attn_ref, u_ref, st_ref, pw_ref, ps_ref, wo_ref, gm_ref, wmq_ref,
                         h1_ref, qm_ref, *, past):
    u = u_ref[...]
    rows = u.shape[0]
    sums, cnts = [], []
    for g, w in enumerate(POOL_WINDOWS):
        sl = slice(g * POOL_CG, (g + 1) * POOL_CG)
        s = u[:, sl]
        for j in range(1, w):
            s = s + st_ref[POOL_MAX - 1 - j][:, sl]
        sums.append(s)
        cnts.append(jnp.full((rows, 1), float(min(w, past + 1)), F32))
    pool = _pool_project(sums, u, cnts, pw_ref, ps_ref)
    h1 = x_ref[...] + _dot(attn_ref[...], wo_ref[0:ATTN_WIDTH, :]) \
        + _dot(pool.astype(BF16), wo_ref[ATTN_WIDTH:, :])
    h1_ref[...] = h1
    qm_ref[...] = _dot(_rms(h1, gm_ref[...]).astype(BF16), wmq_ref[...]).astype(BF16)


def _mid_sample_a(x, attn, u, state_t, pool_w, pool_scale, w_o, g_mem, w_mq, past):
    db = x.shape[0]
    full = lambda a: pl.BlockSpec(a.shape, lambda i, nd=a.ndim: (0,) * nd)
    args = (x, attn, u, state_t, pool_w, pool_scale, w_o, g_mem, w_mq)
    return pl.pallas_call(
        functools.partial(_mid_sample_a_kernel, past=past),
        grid=(1,),
        in_specs=[full(a) for a in args],
        out_specs=[pl.BlockSpec((db, D_MODEL), lambda i: (0, 0)),
                   pl.BlockSpec((db, MEM_WIDTH), lambda i: (0, 0))],
        out_shape=[jax.ShapeDtypeStruct((db, D_MODEL), F32), jax.ShapeDtypeStruct((db, MEM_WIDTH), BF16)],
        compiler_params=_params("arbitrary"),
        name="mid_sample_a",
    )(*args)


def _mem_sample_kernel(q_ref, mk_ref, mv_ref, o_ref):
    rows = 2 * MEM_HEADS
    n = mk_ref.shape[1]
    q_all = q_ref[0].astype(F32)
    q = jnp.concatenate([q_all[:, (r % MEM_HEADS) * MEM_HEAD_DIM:(r % MEM_HEADS + 1) * MEM_HEAD_DIM]
                         for r in range(rows)], axis=0).astype(BF16)
    logits = _dot_nt(q, mk_ref[0].astype(BF16)) * (MEM_HEAD_DIM ** -0.5)
    own = (lax.broadcasted_iota(jnp.int32, (rows, n), 1) % MEM_HEADS
           == lax.broadcasted_iota(jnp.int32, (rows, n), 0) % MEM_HEADS)
    logits = jnp.where(own, logits, -jnp.inf)
    m = jnp.max(logits, axis=1, keepdims=True)
    p = jnp.exp(logits - m)
    p = p / jnp.sum(p, axis=1, keepdims=True)
    o = _dot(p.astype(BF16), mv_ref[0].astype(BF16))
    for h in range(MEM_HEADS):
        o_ref[0, :, h * MEM_HEAD_DIM:(h + 1) * MEM_HEAD_DIM] = o[h:h + 1, :]


def _mem_sample(qm, mk, mv):
    db = qm.shape[0]
    per = lambda b: (b, 0, 0)
    return pl.pallas_call(
        _mem_sample_kernel,
        grid=(db,),
        in_specs=[pl.BlockSpec((1, 1, MEM_WIDTH), per),
                  pl.BlockSpec((1, N_MEM * MEM_HEADS, MEM_HEAD_DIM), per),
                  pl.BlockSpec((1, N_MEM * MEM_HEADS, MEM_HEAD_DIM), per)],
        out_specs=pl.BlockSpec((1, 1, MEM_WIDTH), per),
        out_shape=jax.ShapeDtypeStruct((db, 1, MEM_WIDTH), F32),
        compiler_params=_params("parallel"),
        name="mem_sample",
    )(qm.reshape(db, 1, MEM_WIDTH), mk, mv)


def _sample_join_kernel(h1_ref, o_ref, wmo_ref, gf_ref, h2_any, tok_any, h2_ref, tok_ref):
    del h2_any, tok_any
    db = h1_ref.shape[0]
    h2_ref[...] = jnp.zeros_like(h2_ref)
    tok_ref[...] = jnp.zeros_like(tok_ref)

    @pl.when(pl.program_id(0) == 0)
    def _():
        h2 = h1_ref[...] + _dot(o_ref[...].astype(BF16), wmo_ref[...])
        h2_ref[0:db, :] = h2
        tok_ref[0:db, :] = _rms(h2, gf_ref[...]).astype(BF16)


def _sample_join(h1, o, w_mo, g_ffn, h2_all, tok_all, n_prompt):
    db = h1.shape[0]
    n_rows = h2_all.shape[0]
    assert db <= SAMPLE_ROWS and n_prompt % SAMPLE_ROWS == 0 and n_rows % SAMPLE_ROWS == 0
    first = n_prompt // SAMPLE_ROWS
    f2 = lambda j: (0, 0)
    tail = lambda j: (first + j, 0)
    return pl.pallas_call(
        _sample_join_kernel,
        grid=((n_rows - n_prompt) // SAMPLE_ROWS,),
        in_specs=[pl.BlockSpec((db, D_MODEL), f2),
                  pl.BlockSpec((db, MEM_WIDTH), f2),
                  pl.BlockSpec((MEM_WIDTH, D_MODEL), f2),
                  pl.BlockSpec((1, D_MODEL), f2),
                  pl.BlockSpec(memory_space=pl.ANY),
                  pl.BlockSpec(memory_space=pl.ANY)],
        out_specs=[pl.BlockSpec((SAMPLE_ROWS, D_MODEL), tail), pl.BlockSpec((SAMPLE_ROWS, D_MODEL), tail)],
        out_shape=[jax.ShapeDtypeStruct(h2_all.shape, F32), jax.ShapeDtypeStruct(tok_all.shape, BF16)],
        input_output_aliases={4: 0, 5: 1},
        compiler_params=_params("arbitrary"),
        name="sample_join",
    )(h1, o, w_mo, g_ffn, h2_all, tok_all)


def _rope_tables(pos):
    half = HEAD_DIM // 2
    inv = ROPE_THETA ** (-jnp.arange(half, dtype=F32) / half)
    ang = pos.astype(F32)[:, None] * inv[None, :]
    cos = jnp.cos(ang)
    sin = jnp.sin(ang)
    return jnp.tile(cos, (1, 4)), jnp.concatenate([-sin, sin, -sin, sin], axis=1)


def _pick_tile(n, pref):
    t = min(n, pref)
    while n % t:
        t //= 2
    return t


def kernel(x_prompt, x_sample, mem_prompt, cache_k, cache_v, cache_idx_k, state_pool, cache_mem_k, cache_mem_v, page_table, norm_mix, w_in, pool_w, pool_scale, w_o, norm_mem, w_mq, w_mk, w_mv, w_mo, norm_ffn, router_w, router_b, w_up, b_up, w_down, b_down, norm_final):
    B, T = x_prompt.shape[:2]
    DB, DS = x_sample.shape[:2]
    assert DS == 1 and w_in.shape[0] == 1
    n_pages = page_table.shape[1]
    past = n_pages * PAGE_SIZE
    NP = B * T

    wi_ = w_in[0]
    wq, wk, wv, wu_, wqi, wki, wwi = jnp.split(wi_, [512, 576, 640, 1152, 1664, 1728], axis=1)
    w_all = jnp.concatenate(
        [wq, wqi, wu_, wk, wk, wki, wki, wv, wv, wwi, jnp.zeros((D_MODEL, LANES - IDX_HEADS), F32)],
        axis=1).astype(BF16)
    g_mix = norm_mix[0][None, :]
    g_mem = norm_mem[0][None, :]
    g_ffn = norm_ffn[0][None, :]
    g_fin = norm_final[None, :]
    pw = pool_w[0].astype(BF16)
    ps = pool_scale[0][None, :]
    wo = w_o[0].astype(BF16)
    wmq = w_mq[0].reshape(D_MODEL, MEM_WIDTH).astype(BF16)
    wkv = jnp.concatenate([w_mk[0].reshape(D_MODEL, MEM_WIDTH), w_mv[0].reshape(D_MODEL, MEM_WIDTH)],
                          axis=1).astype(BF16)
    wmo = w_mo[0].reshape(MEM_WIDTH, D_MODEL).astype(BF16)
    rw = router_w[0].astype(BF16)
    rb = router_b[0]
    wup = w_up[0].astype(BF16)
    wdn = w_down[0].astype(BF16)
    bup = b_up[0][:, None, :]
    bdn = b_down[0][:, None, :]

    xp = x_prompt.reshape(NP, D_MODEL)
    tm_in = _pick_tile(T, 1024)
    cos_p, sin_p = _rope_tables(jnp.arange(T, dtype=jnp.int32))
    q, qi, u, kk, kiki, vvt, k_pt, ki_pt, v_pt, wi_p = _in_project(xp, g_mix, w_all, cos_p, sin_p, tm_in,
                                                                   T // tm_in, batch=B)
    attn = _dsa_prompt(q, qi, wi_p, kk, kiki, vvt, B, T)
    mk_p, mv_p = _mem_kv(mem_prompt.reshape(B * N_MEM, D_MODEL), wkv)
    tm_moe = min(MOE_TILE, NP)
    n_tiles = -(-(NP + SAMPLE_ROWS) // tm_moe)
    assert (n_tiles - 1) * tm_moe < NP and NP + SAMPLE_ROWS <= n_tiles * tm_moe
    h2_all, tok_all = _mid_prompt(xp, attn, u, mk_p.reshape(B, N_MEM, MEM_WIDTH), mv_p.reshape(B, N_MEM, MEM_WIDTH),
                                  pw, ps, wo, g_mem, wmq, wmo, g_ffn, B, T, _pick_tile(T, 1024), n_tiles * tm_moe)

    xs = x_sample.reshape(DB, D_MODEL)
    cos_s, sin_s = _rope_tables(jnp.full((DB,), past, jnp.int32))
    q_s, qi_s, u_s, kk_s, kiki_s, vv_s, k_s, ki_s, v_s, wi_s = _in_project(xs, g_mix, w_all, cos_s, sin_s, DB, 1)
    mask, take_new = _dsa_sample_mask(
        page_table, qi_s.reshape(DB, IDX_HEADS, IDX_DIM), wi_s[:, :IDX_HEADS, None],
        kiki_s[:, None, :IDX_DIM], jnp.swapaxes(cache_idx_k, 2, 3))
    attn_s = _dsa_sample_attend(
        page_table, q_s.reshape(DB, ATTN_HEADS, HEAD_DIM), kk_s[:, None, :HEAD_DIM], vv_s[:, None, :HEAD_DIM],
        mask, take_new, jnp.swapaxes(cache_k, 2, 3), jnp.swapaxes(cache_v, 2, 3))
    attn_s = attn_s.reshape(DB, ATTN_WIDTH).astype(BF16)
    h1_s, qm_s = _mid_sample_a(xs, attn_s, u_s, jnp.transpose(state_pool[0], (1, 0, 2)), pw, ps, wo, g_mem, wmq, past)
    o_s = _mem_sample(qm_s, cache_mem_k.reshape(DB, N_MEM * MEM_HEADS, MEM_HEAD_DIM),
                      cache_mem_v.reshape(DB, N_MEM * MEM_HEADS, MEM_HEAD_DIM))
    h2_all, tok_all = _sample_join(h1_s, o_s.reshape(DB, MEM_WIDTH), wmo, g_ffn, h2_all, tok_all, NP)

    rank, gate, cnt = _router(tok_all, rw.T, rb[:, None], tm_moe, NP + DB)
    y_p, y_s = _moe(tok_all, h2_all, rank, gate, cnt.reshape(-1), wup, bup, wdn, bdn, g_fin, tm_moe, NP)
    y_prompt = y_p.reshape(B, T, D_MODEL)
    y_sample = y_s[:DB].reshape(DB, 1, D_MODEL)

    new_pool_p = u.reshape(B, T, POOL_WIDTH)[:, T - (POOL_MAX - 1):][None]
    new_pool_s = jnp.concatenate([state_pool[0][:, 1:], u_s[:, None, :]], axis=1)[None]
    return (y_prompt, y_sample,
            jnp.swapaxes(k_pt, 1, 2)[None], jnp.swapaxes(v_pt, 1, 2)[None], jnp.swapaxes(ki_pt, 1, 2)[None],
            new_pool_p,
            mk_p.reshape(1, B, N_MEM, MEM_HEADS, MEM_HEAD_DIM), mv_p.reshape(1, B, N_MEM, MEM_HEADS, MEM_HEAD_DIM),
            k_s.reshape(1, DB, 1, HEAD_DIM), v_s.reshape(1, DB, 1, HEAD_DIM), ki_s.reshape(1, DB, 1, IDX_DIM),
            new_pool_s)
```

```python
import functools

import jax
import jax.numpy as jnp
from jax import lax
from jax.experimental import pallas as pl
from jax.experimental.pallas import tpu as pltpu

D_MODEL = 1024
HEAD_DIM = 64
ATTN_WIDTH = 512
ATTN_HEADS = 8
IDX_HEADS = 8
IDX_DIM = 64
TOPK_MAX = 256
Q_BLOCK = 128
PAGE_SIZE = 128
POOL_WIDTH = 512
POOL_WINDOWS = (2, 4, 8, 16)
POOL_CG = 128
POOL_MAX = 16
N_MEM = 256
MEM_HEADS = 4
MEM_HEAD_DIM = 128
MEM_WIDTH = MEM_HEADS * MEM_HEAD_DIM
N_EXPERTS = 32
TOP_K = 4
D_FF = 1024
SWIGLU_LIMIT = 7.0
SWIGLU_ALPHA = 1.702
ROPE_THETA = 10000.0
EPS = 1e-6
NEG = -1e30

LANES = 128
MOE_TILE = 13 * LANES
MOE_FIRST_ROWS = (192, 208, 224, 240, 256)
MOE_TAIL_ROWS = 128
SAMPLE_ROWS = 128
DSA_Q = 256
DSA_KCHUNK = 256
IN_COLS = 3 * 512 + 4 * LANES
VMEM_LIMIT = 56 * 1024 * 1024

BF16 = jnp.bfloat16
F32 = jnp.float32


def _dot(a, b):
    return jnp.dot(a, b, preferred_element_type=F32)


def _dot_nt(a, b):
    return lax.dot_general(a, b, (((1,), (1,)), ((), ())), preferred_element_type=F32)


def _dot_tn(a, b):
    return lax.dot_general(a, b, (((0,), (0,)), ((), ())), preferred_element_type=F32)


def _rms(x, g):
    return x * lax.rsqrt(jnp.mean(x * x, axis=-1, keepdims=True) + EPS) * g


def _params(*sem):
    return pltpu.CompilerParams(dimension_semantics=sem, vmem_limit_bytes=VMEM_LIMIT)


def _in_kernel(x_ref, g_ref, w_ref, cos_ref, sin_ref,
               q_ref, qi_ref, u_ref, kk_ref, kiki_ref, vv_ref, k_ref, ki_ref, v_ref, wi_ref, *, feature_major):
    xn = _rms(x_ref[...], g_ref[...]).astype(BF16)
    z = _dot(xn, w_ref[...])
    cos = cos_ref[...]
    sin = sin_ref[...]
    lane = lax.broadcasted_iota(jnp.int32, cos.shape, 1)
    first_half = (lane % HEAD_DIM) < (HEAD_DIM // 2)

    def rope(zc):
        swapped = jnp.where(first_half,
                            pltpu.roll(zc, LANES - HEAD_DIM // 2, 1),
                            pltpu.roll(zc, HEAD_DIM // 2, 1))
        return zc * cos + swapped * sin

    for c in range(4):
        q_c = rope(z[:, c * LANES:(c + 1) * LANES]) * (HEAD_DIM ** -0.5)
        qi_c = rope(z[:, 512 + c * LANES:512 + (c + 1) * LANES]) * (IDX_DIM ** -0.5)
        q_ref[:, c * LANES:(c + 1) * LANES] = q_c.astype(BF16)
        qi_ref[:, c * LANES:(c + 1) * LANES] = qi_c.astype(BF16)
    u_ref[...] = z[:, 1024:1536]
    kk = rope(z[:, 1536:1664])
    kiki = rope(z[:, 1664:1792])
    vv = z[:, 1792:1920]
    kk_ref[...] = kk.astype(BF16)
    kiki_ref[...] = kiki.astype(BF16)
    wi_ref[...] = z[:, 1920:2048]
    if feature_major:
        kk_t = jnp.transpose(kk)
        kiki_t = jnp.transpose(kiki)
        vv_t = jnp.transpose(vv)
        vv_ref[0] = vv_t.astype(BF16)
        k_ref[0] = kk_t[0:HEAD_DIM, :]
        ki_ref[0] = kiki_t[0:IDX_DIM, :]
        v_ref[0] = vv_t[0:HEAD_DIM, :]
    else:
        vv_ref[...] = vv.astype(BF16)
        k_ref[...] = kk[:, :HEAD_DIM]
        ki_ref[...] = kiki[:, :IDX_DIM]
        v_ref[...] = vv[:, :HEAD_DIM]


def _in_project(x, g, w_all, cos, sin, tm, table_blocks, batch=None):
    n = x.shape[0]
    row = lambda i: (i, 0)
    tab = lambda i: (i % table_blocks, 0)
    fixed = lambda i: (0, 0)
    feature_major = batch is not None
    if feature_major:
        t_len = n // batch
        assert t_len == table_blocks * tm
        fm = lambda feat, dt: jax.ShapeDtypeStruct((batch, feat, t_len), dt)
        fm_spec = lambda feat: pl.BlockSpec((1, feat, tm), lambda i: (i // table_blocks, 0, i % table_blocks))
    else:
        fm = lambda feat, dt: jax.ShapeDtypeStruct((n, feat), dt)
        fm_spec = lambda feat: pl.BlockSpec((tm, feat), row)
    outs = [
        jax.ShapeDtypeStruct((n, 512), BF16),
        jax.ShapeDtypeStruct((n, 512), BF16),
        jax.ShapeDtypeStruct((n, 512), F32),
        jax.ShapeDtypeStruct((n, LANES), BF16),
        jax.ShapeDtypeStruct((n, LANES), BF16),
        fm(LANES, BF16),
        fm(HEAD_DIM, F32),
        fm(IDX_DIM, F32),
        fm(HEAD_DIM, F32),
        jax.ShapeDtypeStruct((n, LANES), F32),
    ]
    out_specs = ([pl.BlockSpec((tm, s.shape[1]), row) for s in outs[:5]]
                 + [fm_spec(LANES), fm_spec(HEAD_DIM), fm_spec(IDX_DIM), fm_spec(HEAD_DIM)]
                 + [pl.BlockSpec((tm, LANES), row)])
    return pl.pallas_call(
        functools.partial(_in_kernel, feature_major=feature_major),
        grid=(n // tm,),
        in_specs=[pl.BlockSpec((tm, D_MODEL), row),
                  pl.BlockSpec((1, D_MODEL), fixed),
                  pl.BlockSpec((D_MODEL, IN_COLS), fixed),
                  pl.BlockSpec((tm, LANES), tab),
                  pl.BlockSpec((tm, LANES), tab)],
        out_specs=out_specs,
        out_shape=outs,
        compiler_params=_params("parallel"),
        name="in_project",
    )(x, g, w_all, cos, sin)


def _f32_to_key(x):
    bits = lax.bitcast_convert_type(x, jnp.int32)
    return jnp.where(bits < 0, bits ^ jnp.int32(0x7FFFFFFF), bits)


def _key_to_f32(key):
    bits = jnp.where(key < 0, key ^ jnp.int32(0x7FFFFFFF), key)
    return lax.bitcast_convert_type(bits, F32)


def _kth_largest(count_ge, k, shape):
    def body(it, tau):
        cand = tau + jnp.left_shift(jnp.int32(1), jnp.int32(31) - it)
        return jnp.where(count_ge(_key_to_f32(cand)) >= k, cand, tau)

    return _key_to_f32(lax.fori_loop(0, 32, body, jnp.full(shape, -2 ** 31, jnp.int32)))


def _kth_largest_between(count_ge, k, lo, hi):
    key_lo = _f32_to_key(lo)
    span = _f32_to_key(hi) - key_lo
    n_bits = jnp.max(32 - lax.clz(span))
    flip = jnp.int32(-2 ** 31)

    def body(it, off):
        cand = off | jnp.left_shift(jnp.int32(1), n_bits - 1 - it)
        ok = jnp.logical_and(count_ge(_key_to_f32(key_lo + cand)) >= k, (cand ^ flip) <= (span ^ flip))
        return jnp.where(ok, cand, off)

    return _key_to_f32(key_lo + lax.fori_loop(0, n_bits, body, jnp.zeros_like(key_lo)))


def _strict_triangle(n, lower):
    r = lax.broadcasted_iota(jnp.int32, (n, n), 0)
    c = lax.broadcasted_iota(jnp.int32, (n, n), 1)
    return jnp.where((r > c) if lower else (r < c), 1.0, 0.0).astype(BF16)


def _take_first_ties(eq, need_f, tied, axis):
    n = eq.shape[axis]
    cshape = (1, eq.shape[1]) if axis == 0 else (eq.shape[0], 1)

    def ranked():
        tri = _strict_triangle(LANES, lower=(axis == 0))
        carry = jnp.zeros(cshape, F32)
        parts = []
        for c in range(n // LANES):
            sl = slice(c * LANES, (c + 1) * LANES)
            eq_f = jnp.where(eq[sl, :] if axis == 0 else eq[:, sl], 1.0, 0.0)
            eq_b = eq_f.astype(BF16)
            rank = (_dot(tri, eq_b) if axis == 0 else _dot(eq_b, tri)) + carry
            parts.append(jnp.where(rank < need_f, eq_f, 0.0))
            carry = carry + jnp.sum(eq_f, axis=axis, keepdims=True)
        return jnp.concatenate(parts, axis=axis), carry

    def plain():
        return jnp.where(eq, 1.0, 0.0), jnp.zeros(cshape, F32)

    return lax.cond(jnp.max(jnp.where(tied, 1, 0)) > 0, ranked, plain)


def _dsa_prompt_kernel(q_ref, qi_ref, wi_ref, kk_ref, kiki_ref, vvt_ref, o_ref, sm_ref, bias_ref, *, topk, block):
    i = block
    qb = q_ref.shape[0]
    lk = kk_ref.shape[1]
    select = lk > topk
    ck = min(DSA_KCHUNK, lk)
    chunks = [slice(c * ck, (c + 1) * ck) for c in range(lk // ck)]
    upper_rows = lax.broadcasted_iota(jnp.int32, (LANES, qb), 0) < HEAD_DIM

    def head_operand(ref, h):
        pair_t = jnp.transpose(ref[:, (h // 2) * LANES:(h // 2 + 1) * LANES].astype(F32))
        keep = upper_rows if h % 2 == 0 else jnp.logical_not(upper_rows)
        return jnp.where(keep, pair_t, 0.0).astype(BF16)

    pos_q = i * qb + lax.broadcasted_iota(jnp.int32, (1, qb), 1)

    def causal(sl):
        return sl.start + lax.broadcasted_iota(jnp.int32, (ck, 1), 0) <= pos_q

    if select:
        w_t = jnp.transpose(wi_ref[...])[0:IDX_HEADS, :] * (IDX_HEADS ** -0.5)
        qi_ops = [head_operand(qi_ref, h) for h in range(IDX_HEADS)]
        for sl in chunks:
            keys = kiki_ref[0, sl, :]
            score = jnp.zeros((ck, qb), F32)
            for h in range(IDX_HEADS):
                score = score + jnp.maximum(_dot(keys, qi_ops[h]), 0.0) * w_t[h:h + 1, :]
            sm_ref[sl, :] = jnp.where(causal(sl), score, -jnp.inf)

        def count_ge(cand):
            return jnp.sum(jnp.where(sm_ref[0:lk, :] >= cand, 1, 0), axis=0, keepdims=True)

        if ck == topk:
            class_max = sm_ref[chunks[0], :]
            for sl in chunks[1:]:
                class_max = jnp.maximum(class_max, sm_ref[sl, :])
            tau = _kth_largest_between(count_ge, topk, jnp.min(class_max, axis=0, keepdims=True),
                                       jnp.max(class_max, axis=0, keepdims=True))
        else:
            tau = _kth_largest(count_ge, topk, (1, qb))
        sm = sm_ref[0:lk, :]
        gt = sm > tau
        eq = sm == tau
        n_gt = jnp.sum(jnp.where(gt, 1, 0), axis=0, keepdims=True)
        n_eq = jnp.sum(jnp.where(eq, 1, 0), axis=0, keepdims=True)
        tied = n_gt + n_eq > topk
        need_f = jnp.where(tied, topk - n_gt, lk + 1).astype(F32)
        take_eq, _ = _take_first_ties(eq, need_f, tied, axis=0)
        bias_ref[0:lk, :] = jnp.where(jnp.logical_or(gt, take_eq > 0.5), 0.0, NEG)

    def chunk_bias(sl):
        if select:
            return bias_ref[sl, :]
        return jnp.where(causal(sl), 0.0, NEG)

    def fold8(x, op):
        return op(x.reshape(ck // 8, 8, qb), axis=0)

    for j in range(ATTN_HEADS // 2):
        outs = []
        for h in (2 * j, 2 * j + 1):
            q_op = head_operand(q_ref, h)
            mx8 = jnp.full((8, qb), -jnp.inf, F32)
            for sl in chunks:
                logits = _dot(kk_ref[0, sl, :], q_op) + chunk_bias(sl)
                sm_ref[sl, :] = logits
                mx8 = jnp.maximum(mx8, fold8(logits, jnp.max))
            m = jnp.max(mx8, axis=0, keepdims=True)
            den8 = jnp.zeros((8, qb), F32)
            acc = jnp.zeros((LANES, qb), F32)
            for sl in chunks:
                p = jnp.exp(sm_ref[sl, :] - m)
                den8 = den8 + fold8(p, jnp.sum)
                acc = acc + _dot(vvt_ref[0, :, sl], p.astype(BF16))
            outs.append(acc * (1.0 / jnp.sum(den8, axis=0, keepdims=True)))
        pair_t = jnp.where(upper_rows, outs[0], outs[1])
        o_ref[0, :, j * LANES:(j + 1) * LANES] = jnp.transpose(pair_t).astype(BF16)


def _dsa_prompt_block(q, qi, wi, kk, kiki, vvt, batch, t_len, block, topk):
    nb = t_len // DSA_Q
    lk = (block + 1) * DSA_Q
    blk = lambda b: (b * nb + block, 0)
    seq = lambda b: (b, 0, 0)
    return pl.pallas_call(
        functools.partial(_dsa_prompt_kernel, topk=topk, block=block),
        grid=(batch,),
        in_specs=[pl.BlockSpec((DSA_Q, 512), blk),
                  pl.BlockSpec((DSA_Q, 512), blk),
                  pl.BlockSpec((DSA_Q, LANES), blk),
                  pl.BlockSpec((1, lk, LANES), seq),
                  pl.BlockSpec((1, lk, LANES), seq),
                  pl.BlockSpec((1, LANES, lk), seq)],
        out_specs=pl.BlockSpec((1, DSA_Q, 512), seq),
        out_shape=jax.ShapeDtypeStruct((batch, DSA_Q, 512), BF16),
        scratch_shapes=[pltpu.VMEM((lk, DSA_Q), F32), pltpu.VMEM((lk, DSA_Q), F32)],
        compiler_params=_params("parallel"),
        name="dsa_prompt_%d" % lk,
    )(q, qi, wi, kk, kiki, vvt)


def _dsa_prompt(q, qi, wi, kk, kiki, vvt, batch, t_len):
    nb = t_len // DSA_Q
    topk = min(TOPK_MAX, t_len // 4)
    assert topk % DSA_Q == 0 and t_len % DSA_Q == 0
    kk, kiki = (a.reshape(batch, t_len, LANES) for a in (kk, kiki))
    parts = [_dsa_prompt_block(q, qi, wi, kk, kiki, vvt, batch, t_len, block, topk) for block in range(nb)]
    return jnp.concatenate(parts, axis=1).reshape(batch * t_len, 512)


def _mem_kv_kernel(m_ref, w_ref, k_ref, v_ref):
    z = _dot(m_ref[...].astype(BF16), w_ref[...])
    k_ref[...] = z[:, :MEM_WIDTH]
    v_ref[...] = z[:, MEM_WIDTH:]


def _mem_kv(mem, w_kv):
    n = mem.shape[0]
    tm = min(n, 512)
    row = lambda i: (i, 0)
    return pl.pallas_call(
        _mem_kv_kernel,
        grid=(n // tm,),
        in_specs=[pl.BlockSpec((tm, D_MODEL), row),
                  pl.BlockSpec((D_MODEL, 2 * MEM_WIDTH), lambda i: (0, 0))],
        out_specs=[pl.BlockSpec((tm, MEM_WIDTH), row), pl.BlockSpec((tm, MEM_WIDTH), row)],
        out_shape=[jax.ShapeDtypeStruct((n, MEM_WIDTH), F32)] * 2,
        compiler_params=_params("parallel"),
        name="mem_kv",
    )(mem, w_kv)


def _pool_project(window_sums, u, cnts, pw_ref, ps_ref):
    ys = []
    for g in range(len(POOL_WINDOWS)):
        sl = slice(g * POOL_CG, (g + 1) * POOL_CG)
        pooled = window_sums[g] / cnts[g] - u[:, sl]
        ys.append(_dot(pooled.astype(BF16), pw_ref[g]) * ps_ref[:, sl])
    return jnp.concatenate(ys, axis=1)


def _mid_prompt_kernel(x_ref, attn_ref, u_ref, halo_ref, mk_ref, mv_ref, pw_ref, ps_ref, wo_ref,
                       gm_ref, wmq_ref, wmo_ref, gf_ref, h2_ref, tok_ref, e_ref, *, nt, n_real):
    step = pl.program_id(0)

    @pl.when(step >= n_real)
    def _():
        h2_ref[...] = jnp.zeros_like(h2_ref)
        tok_ref[...] = jnp.zeros_like(tok_ref)

    @pl.when(step < n_real)
    def _():
        _mid_prompt_rows(x_ref, attn_ref, u_ref, halo_ref, mk_ref, mv_ref, pw_ref, ps_ref, wo_ref,
                         gm_ref, wmq_ref, wmo_ref, gf_ref, h2_ref, tok_ref, e_ref, step % nt)


def _mid_prompt_rows(x_ref, attn_ref, u_ref, halo_ref, mk_ref, mv_ref, pw_ref, ps_ref, wo_ref,
                     gm_ref, wmq_ref, wmo_ref, gf_ref, h2_ref, tok_ref, e_ref, i):
    tm = x_ref.shape[0]
    u = u_ref[...]
    halo = halo_ref[...]
    e_ref[0:POOL_MAX, :] = jnp.where(i == 0, jnp.zeros_like(halo), halo)
    e_ref[POOL_MAX:POOL_MAX + tm, :] = u
    pos = i * tm + lax.broadcasted_iota(jnp.int32, (tm, 1), 0)
    sums, cnts = [], []
    for g, w in enumerate(POOL_WINDOWS):
        sl = slice(g * POOL_CG, (g + 1) * POOL_CG)
        s = u[:, sl]
        for j in range(1, w):
            s = s + e_ref[POOL_MAX - j:POOL_MAX - j + tm, sl]
        sums.append(s)
        cnts.append(jnp.minimum(w, pos + 1).astype(F32))
    pool = _pool_project(sums, u, cnts, pw_ref, ps_ref)
    h1 = x_ref[...] + _dot(attn_ref[...], wo_ref[0:ATTN_WIDTH, :]) \
        + _dot(pool.astype(BF16), wo_ref[ATTN_WIDTH:, :])

    qm = _dot(_rms(h1, gm_ref[...]).astype(BF16), wmq_ref[...]).astype(BF16)
    mk = mk_ref[0].astype(BF16)
    mv = mv_ref[0].astype(BF16)
    outs = []
    for h in range(MEM_HEADS):
        sl = slice(h * MEM_HEAD_DIM, (h + 1) * MEM_HEAD_DIM)
        logits = _dot_nt(qm[:, sl], mk[:, sl]) * (MEM_HEAD_DIM ** -0.5)
        m = jnp.max(logits, axis=1, keepdims=True)
        p = jnp.exp(logits - m)
        outs.append(_dot(p.astype(BF16), mv[:, sl]) * (1.0 / jnp.sum(p, axis=1, keepdims=True)))
    o = jnp.concatenate(outs, axis=1)
    h2 = h1 + _dot(o.astype(BF16), wmo_ref[...])
    h2_ref[...] = h2
    tok_ref[...] = _rms(h2, gf_ref[...]).astype(BF16)


def _mid_prompt(x, attn, u, mk, mv, pool_w, pool_scale, w_o, g_mem, w_mq, w_mo, g_ffn, batch, t_len, tm, n_rows):
    nt = t_len // tm
    hb = tm // POOL_MAX
    n_real = batch * nt
    n_steps = -(-n_rows // tm)
    last = n_real - 1
    row = lambda s: (jnp.minimum(s, last), 0)
    halo = lambda s: (jnp.maximum(jnp.minimum(s, last) * hb - 1, 0), 0)
    fixed2 = lambda s: (0, 0)
    fixed3 = lambda s: (0, 0, 0)
    perb = lambda s: (jnp.minimum(s, last) // nt, 0, 0)
    out_row = lambda s: (s, 0)
    n = n_rows
    return pl.pallas_call(
        functools.partial(_mid_prompt_kernel, nt=nt, n_real=n_real),
        grid=(n_steps,),
        in_specs=[pl.BlockSpec((tm, D_MODEL), row),
                  pl.BlockSpec((tm, ATTN_WIDTH), row),
                  pl.BlockSpec((tm, POOL_WIDTH), row),
                  pl.BlockSpec((POOL_MAX, POOL_WIDTH), halo),
                  pl.BlockSpec((1, N_MEM, MEM_WIDTH), perb),
                  pl.BlockSpec((1, N_MEM, MEM_WIDTH), perb),
                  pl.BlockSpec((len(POOL_WINDOWS), POOL_CG, POOL_CG), fixed3),
                  pl.BlockSpec((1, POOL_WIDTH), fixed2),
                  pl.BlockSpec((D_MODEL, D_MODEL), fixed2),
                  pl.BlockSpec((1, D_MODEL), fixed2),
                  pl.BlockSpec((D_MODEL, MEM_WIDTH), fixed2),
                  pl.BlockSpec((MEM_WIDTH, D_MODEL), fixed2),
                  pl.BlockSpec((1, D_MODEL), fixed2)],
        out_specs=[pl.BlockSpec((tm, D_MODEL), out_row), pl.BlockSpec((tm, D_MODEL), out_row)],
        out_shape=[jax.ShapeDtypeStruct((n, D_MODEL), F32), jax.ShapeDtypeStruct((n, D_MODEL), BF16)],
        scratch_shapes=[pltpu.VMEM((POOL_MAX + tm, POOL_WIDTH), F32)],
        compiler_params=_params("arbitrary"),
        name="mid_prompt",
    )(x, attn, u, u, mk, mv, pool_w, pool_scale, w_o, g_mem, w_mq, w_mo, g_ffn)


def _router_kernel(tok_ref, rw_ref, rb_ref, rank_ref, gate_ref, cnt_ref, *, n_valid):
    tm = tok_ref.shape[0]
    token = pl.program_id(0) * tm + lax.broadcasted_iota(jnp.int32, (1, tm), 1)
    valid = token < n_valid
    logits = _dot_nt(rw_ref[...], tok_ref[...]) + rb_ref[...]
    eid = lax.broadcasted_iota(jnp.int32, (N_EXPERTS, tm), 0)
    vals, hots = [], []
    cur = logits
    for _ in range(TOP_K):
        m = jnp.max(cur, axis=0, keepdims=True)
        idx = jnp.min(jnp.where(cur == m, eid, N_EXPERTS), axis=0, keepdims=True)
        hot = eid == idx
        vals.append(m)
        hots.append(hot)
        cur = jnp.where(hot, -jnp.inf, cur)
    exps = [jnp.exp(v - vals[0]) for v in vals]
    den = exps[0] + exps[1] + exps[2] + exps[3]
    gate = jnp.zeros((N_EXPERTS, tm), F32)
    sel = jnp.zeros((N_EXPERTS, tm), jnp.bool_)
    for hot, ex in zip(hots, exps):
        gate = jnp.where(hot, ex / den, gate)
        sel = jnp.logical_or(sel, hot)
    sel = jnp.logical_and(sel, valid)
    gate_ref[0] = gate

    upper = _strict_triangle(LANES, lower=False)
    carry = jnp.zeros((N_EXPERTS, 1), F32)
    for c in range(tm // LANES):
        sl = slice(c * LANES, (c + 1) * LANES)
        sel_c = sel[:, sl]
        sel_b = jnp.where(sel_c, 1.0, 0.0).astype(BF16)
        rank = _dot(sel_b, upper) + carry
        rank_ref[0, :, sl] = jnp.where(sel_c, rank, -1.0).astype(jnp.int32)
        carry = carry + jnp.sum(sel_b.astype(F32), axis=1, keepdims=True)
    cnt_ref[0] = carry.astype(jnp.int32)


def _router(tok, rw_t, rb, tm, n_valid):
    n = tok.shape[0]
    nt = n // tm
    t3 = lambda i: (i, 0, 0)
    return pl.pallas_call(
        functools.partial(_router_kernel, n_valid=n_valid),
        grid=(nt,),
        in_specs=[pl.BlockSpec((tm, D_MODEL), lambda i: (i, 0)),
                  pl.BlockSpec((N_EXPERTS, D_MODEL), lambda i: (0, 0)),
                  pl.BlockSpec((N_EXPERTS, 1), lambda i: (0, 0))],
        out_specs=[pl.BlockSpec((1, N_EXPERTS, tm), t3),
                   pl.BlockSpec((1, N_EXPERTS, tm), t3),
                   pl.BlockSpec((1, N_EXPERTS, 1), t3)],
        out_shape=[jax.ShapeDtypeStruct((nt, N_EXPERTS, tm), jnp.int32),
                   jax.ShapeDtypeStruct((nt, N_EXPERTS, tm), F32),
                   jax.ShapeDtypeStruct((nt, N_EXPERTS, 1), jnp.int32)],
        compiler_params=_params("parallel"),
        name="router",
    )(tok, rw_t, rb)


def _swiglu(h):
    gate = jnp.minimum(h[:, :D_FF], SWIGLU_LIMIT)
    up = jnp.clip(h[:, D_FF:], -SWIGLU_LIMIT, SWIGLU_LIMIT)
    return gate * jax.nn.sigmoid(SWIGLU_ALPHA * gate) * (up + 1.0)


def _moe_kernel(cnt_ref, x_ref, h2_ref, rank_ref, gate_ref, wu_ref, bu_ref, wd_ref, bd_ref, gfin_ref,
                o_ref, os_ref, *, sample_row):
    i = pl.program_id(0)
    e = pl.program_id(1)
    tm = x_ref.shape[0]

    @pl.when(e == 0)
    def _():
        o_ref[...] = jnp.zeros_like(o_ref)

    n_routed = cnt_ref[i * N_EXPERTS + e]
    rank = rank_ref[0, pl.ds(e, 1), :]
    gate = gate_ref[0, pl.ds(e, 1), :]

    def run_slots(first, size):
        slot = first + lax.broadcasted_iota(jnp.int32, (size, tm), 0)
        hit = rank == slot
        onehot = jnp.where(hit, 1.0, 0.0).astype(BF16)
        xg = _dot(onehot, x_ref[...]).astype(BF16)
        act = _swiglu(_dot(xg, wu_ref[0]) + bu_ref[0])
        out = _dot(act.astype(BF16), wd_ref[0]) + bd_ref[0]
        g_slot = jnp.sum(jnp.where(hit, gate, 0.0), axis=1, keepdims=True)
        o_ref[...] += _dot_tn(onehot, (out * g_slot).astype(BF16))

    largest = MOE_FIRST_ROWS[-1]
    for lo, size in zip((0,) + MOE_FIRST_ROWS[:-1], MOE_FIRST_ROWS):
        fits = n_routed <= size if size < largest else n_routed > lo
        @pl.when(jnp.logical_and(n_routed > lo, fits))
        def _():
            run_slots(0, size)

    def body(c, carry):
        run_slots(largest + c * MOE_TAIL_ROWS, MOE_TAIL_ROWS)
        return carry

    lax.fori_loop(0, (jnp.maximum(n_routed - largest, 0) + MOE_TAIL_ROWS - 1) // MOE_TAIL_ROWS, body, 0)

    @pl.when(e == pl.num_programs(1) - 1)
    def _():
        o_ref[...] = _rms(h2_ref[...] + o_ref[...], gfin_ref[...])

        @pl.when(i == pl.num_programs(0) - 1)
        def _():
            os_ref[...] = o_ref[sample_row:sample_row + os_ref.shape[0], :]


def _moe(tok, h2, rank, gate, cnt, wu, bu, wd, bd, g_final, tm, n_prompt):
    nt = tok.shape[0] // tm
    tile = lambda i, e, c: (i, 0)
    tile3 = lambda i, e, c: (i, 0, 0)
    expert = lambda i, e, c: (e, 0, 0)
    grid_spec = pltpu.PrefetchScalarGridSpec(
        num_scalar_prefetch=1,
        grid=(nt, N_EXPERTS),
        in_specs=[pl.BlockSpec((tm, D_MODEL), tile),
                  pl.BlockSpec((tm, D_MODEL), tile),
                  pl.BlockSpec((1, N_EXPERTS, tm), tile3),
                  pl.BlockSpec((1, N_EXPERTS, tm), tile3),
                  pl.BlockSpec((1, D_MODEL, 2 * D_FF), expert),
                  pl.BlockSpec((1, 1, 2 * D_FF), expert),
                  pl.BlockSpec((1, D_FF, D_MODEL), expert),
                  pl.BlockSpec((1, 1, D_MODEL), expert),
                  pl.BlockSpec((1, D_MODEL), lambda i, e, c: (0, 0))],
        out_specs=[pl.BlockSpec((tm, D_MODEL), tile),
                   pl.BlockSpec((SAMPLE_ROWS, D_MODEL), lambda i, e, c: (0, 0))],
    )
    return pl.pallas_call(
        functools.partial(_moe_kernel, sample_row=n_prompt - (nt - 1) * tm),
        grid_spec=grid_spec,
        out_shape=[jax.ShapeDtypeStruct((n_prompt, D_MODEL), F32),
                   jax.ShapeDtypeStruct((SAMPLE_ROWS, D_MODEL), F32)],
        compiler_params=_params("arbitrary", "arbitrary"),
        name="moe",
    )(cnt, tok, h2, rank, gate, wu, bu, wd, bd, g_final)


def _page_copy(cache_ref, buf_ref, sem_ref, pt_ref, b, p, slot):
    dst = buf_ref.at[slot, :, pl.ds(pl.multiple_of(p * PAGE_SIZE, PAGE_SIZE), PAGE_SIZE)]
    return pltpu.make_async_copy(cache_ref.at[0, pt_ref[b, p]], dst, sem_ref.at[slot])


def _fetch_pages(cache_ref, buf_ref, sem_ref, pt_ref, b, slot, n_pages):
    def body(p, c):
        _page_copy(cache_ref, buf_ref, sem_ref, pt_ref, b, p, slot).start()
        return c
    lax.fori_loop(0, n_pages, body, 0, unroll=8)


def _wait_pages(cache_ref, buf_ref, sem_ref, pt_ref, b, slot, n_pages):
    def body(p, c):
        _page_copy(cache_ref, buf_ref, sem_ref, pt_ref, b, p, slot).wait()
        return c
    lax.fori_loop(0, n_pages, body, 0, unroll=8)


def _dsa_sample_score_kernel(pt_ref, qi_ref, wi_ref, kin_ref, cache_ref, score_ref, new_ref, buf_ref, sem_ref):
    b = pl.program_id(0)
    nb = pl.num_programs(0)
    past = buf_ref.shape[2]
    n_pages = past // PAGE_SIZE
    slot = b % 2

    @pl.when(b == 0)
    def _():
        _fetch_pages(cache_ref, buf_ref, sem_ref, pt_ref, b, slot, n_pages)

    @pl.when(b + 1 < nb)
    def _():
        _fetch_pages(cache_ref, buf_ref, sem_ref, pt_ref, b + 1, 1 - slot, n_pages)

    _wait_pages(cache_ref, buf_ref, sem_ref, pt_ref, b, slot, n_pages)

    qi = qi_ref[0]
    w = wi_ref[0] * (IDX_HEADS ** -0.5)
    keys_t = buf_ref[slot].astype(BF16)
    score_ref[0] = jnp.sum(jnp.maximum(_dot(qi, keys_t), 0.0) * w, axis=0, keepdims=True)
    s_new = jnp.sum(qi.astype(F32) * kin_ref[0].astype(F32), axis=1, keepdims=True)
    new_ref[0] = jnp.sum(jnp.maximum(s_new, 0.0) * w, axis=0, keepdims=True)


def _dsa_sample_select_kernel(score_ref, new_ref, mask_ref, take_new_ref, *, topk):
    score = score_ref[...]
    score_new = new_ref[...]
    past = score.shape[1]

    def count_ge(cand):
        return (jnp.sum(jnp.where(score >= cand, 1, 0), axis=1, keepdims=True)
                + jnp.where(score_new >= cand, 1, 0))

    tau = _kth_largest(count_ge, topk, score_new.shape)
    gt = score > tau
    eq = score == tau
    n_gt = jnp.sum(jnp.where(gt, 1, 0), axis=1, keepdims=True) + jnp.where(score_new > tau, 1, 0)
    n_eq_all = jnp.sum(jnp.where(eq, 1, 0), axis=1, keepdims=True) + jnp.where(score_new == tau, 1, 0)
    tied = n_gt + n_eq_all > topk
    need_f = jnp.where(tied, topk - n_gt, past + 2).astype(F32)
    take_eq, n_eq = _take_first_ties(eq, need_f, tied, axis=1)
    mask_ref[...] = jnp.where(jnp.logical_or(gt, take_eq > 0.5), 0.0, NEG)
    take_new = jnp.logical_or(score_new > tau, jnp.logical_and(score_new == tau, n_eq < need_f))
    take_new_ref[...] = jnp.where(take_new, 0.0, NEG)


def _dsa_sample_select(score, score_new, topk):
    db, past = score.shape
    full = lambda shape: pl.BlockSpec(shape, lambda i: (0, 0))
    return pl.pallas_call(
        functools.partial(_dsa_sample_select_kernel, topk=topk),
        grid=(1,),
        in_specs=[full((db, past)), full((db, 1))],
        out_specs=[full((db, past)), full((db, 1))],
        out_shape=[jax.ShapeDtypeStruct((db, past), F32), jax.ShapeDtypeStruct((db, 1), F32)],
        compiler_params=_params("arbitrary"),
        name="dsa_sample_select",
    )(score, score_new)


def _dsa_sample_mask(page_table, qi_heads, wi_col, ki_new, cache_idx_k):
    db, n_pages = page_table.shape
    past = n_pages * PAGE_SIZE
    topk = min(TOPK_MAX, (past + 1) // 4)
    per = lambda b, pt: (b, 0, 0)
    grid_spec = pltpu.PrefetchScalarGridSpec(
        num_scalar_prefetch=1,
        grid=(db,),
        in_specs=[pl.BlockSpec((1, IDX_HEADS, IDX_DIM), per),
                  pl.BlockSpec((1, IDX_HEADS, 1), per),
                  pl.BlockSpec((1, 1, IDX_DIM), per),
                  pl.BlockSpec(memory_space=pl.ANY)],
        out_specs=[pl.BlockSpec((1, 1, past), per), pl.BlockSpec((1, 1, 1), per)],
        scratch_shapes=[pltpu.VMEM((2, IDX_DIM, past), F32),
                        pltpu.SemaphoreType.DMA((2,))],
    )
    score, score_new = pl.pallas_call(
        _dsa_sample_score_kernel,
        grid_spec=grid_spec,
        out_shape=[jax.ShapeDtypeStruct((db, 1, past), F32), jax.ShapeDtypeStruct((db, 1, 1), F32)],
        compiler_params=_params("arbitrary"),
        name="dsa_sample_score",
    )(page_table, qi_heads, wi_col, ki_new, cache_idx_k)
    mask, take_new = _dsa_sample_select(score.reshape(db, past), score_new.reshape(db, 1), topk)
    return mask.reshape(db, 1, past), take_new.reshape(db, 1, 1)


def _dsa_sample_attend_kernel(pt_ref, q_ref, kn_ref, vn_ref, mask_ref, new_ref, ck_ref, cv_ref, o_ref,
                              kbuf_ref, vbuf_ref, ksem_ref, vsem_ref):
    b = pl.program_id(0)
    nb = pl.num_programs(0)
    past = kbuf_ref.shape[2]
    n_pages = past // PAGE_SIZE
    slot = b % 2

    def fetch(bb, sl):
        _fetch_pages(ck_ref, kbuf_ref, ksem_ref, pt_ref, bb, sl, n_pages)
        _fetch_pages(cv_ref, vbuf_ref, vsem_ref, pt_ref, bb, sl, n_pages)

    @pl.when(b == 0)
    def _():
        fetch(b, slot)

    @pl.when(b + 1 < nb)
    def _():
        fetch(b + 1, 1 - slot)

    _wait_pages(ck_ref, kbuf_ref, ksem_ref, pt_ref, b, slot, n_pages)
    _wait_pages(cv_ref, vbuf_ref, vsem_ref, pt_ref, b, slot, n_pages)

    q = q_ref[0]
    keys_t = kbuf_ref[slot].astype(BF16)
    vals_t = vbuf_ref[slot].astype(BF16)
    logits = _dot(q, keys_t) + mask_ref[0]
    l_new = jnp.sum(q.astype(F32) * kn_ref[0].astype(F32), axis=1, keepdims=True) + new_ref[0]
    m = jnp.maximum(jnp.max(logits, axis=1, keepdims=True), l_new)
    p = jnp.exp(logits - m)
    p_new = jnp.exp(l_new - m)
    den = jnp.sum(p, axis=1, keepdims=True) + p_new
    p = p / den
    p_new = p_new / den
    o_ref[0] = _dot_nt(p.astype(BF16), vals_t) + p_new * vn_ref[0].astype(F32)


def _dsa_sample_attend(page_table, q_heads, k_new, v_new, mask, take_new, cache_k, cache_v):
    db, n_pages = page_table.shape
    past = n_pages * PAGE_SIZE
    per = lambda b, pt: (b, 0, 0)
    grid_spec = pltpu.PrefetchScalarGridSpec(
        num_scalar_prefetch=1,
        grid=(db,),
        in_specs=[pl.BlockSpec((1, ATTN_HEADS, HEAD_DIM), per),
                  pl.BlockSpec((1, 1, HEAD_DIM), per),
                  pl.BlockSpec((1, 1, HEAD_DIM), per),
                  pl.BlockSpec((1, 1, past), per),
                  pl.BlockSpec((1, 1, 1), per),
                  pl.BlockSpec(memory_space=pl.ANY),
                  pl.BlockSpec(memory_space=pl.ANY)],
        out_specs=pl.BlockSpec((1, ATTN_HEADS, HEAD_DIM), per),
        scratch_shapes=[pltpu.VMEM((2, HEAD_DIM, past), F32),
                        pltpu.VMEM((2, HEAD_DIM, past), F32),
                        pltpu.SemaphoreType.DMA((2,)),
                        pltpu.SemaphoreType.DMA((2,))],
    )
    return pl.pallas_call(
        _dsa_sample_attend_kernel,
        grid_spec=grid_spec,
        out_shape=jax.ShapeDtypeStruct((db, ATTN_HEADS, HEAD_DIM), F32),
        compiler_params=_params("arbitrary"),
        name="dsa_sample_attend",
    )(page_table, q_heads, k_new, v_new, mask, take_new, cache_k, cache_v)


def _mid_sample_a_kernel(x_ref, attn_ref, u_ref, st_ref, pw_ref, ps_ref, wo_ref, gm_ref, wmq_ref,
                         h1_ref, qm_ref, *, past):
    u = u_ref[...]
    rows = u.shape[0]
    sums, cnts = [], []
    for g, w in enumerate(POOL_WINDOWS):
        sl = slice(g * POOL_CG, (g + 1) * POOL_CG)
        s = u[:, sl]
        for j in range(1, w):
            s = s + st_ref[POOL_MAX - 1 - j][:, sl]
        sums.append(s)
        cnts.append(jnp.full((rows, 1), float(min(w, past + 1)), F32))
    pool = _pool_project(sums, u, cnts, pw_ref, ps_ref)
    h1 = x_ref[...] + _dot(attn_ref[...], wo_ref[0:ATTN_WIDTH, :]) \
        + _dot(pool.astype(BF16), wo_ref[ATTN_WIDTH:, :])
    h1_ref[...] = h1
    qm_ref[...] = _dot(_rms(h1, gm_ref[...]).astype(BF16), wmq_ref[...]).astype(BF16)


def _mid_sample_a(x, attn, u, state_t, pool_w, pool_scale, w_o, g_mem, w_mq, past):
    db = x.shape[0]
    full = lambda a: pl.BlockSpec(a.shape, lambda i, nd=a.ndim: (0,) * nd)
    args = (x, attn, u, state_t, pool_w, pool_scale, w_o, g_mem, w_mq)
    return pl.pallas_call(
        functools.partial(_mid_sample_a_kernel, past=past),
        grid=(1,),
        in_specs=[full(a) for a in args],
        out_specs=[pl.BlockSpec((db, D_MODEL), lambda i: (0, 0)),
                   pl.BlockSpec((db, MEM_WIDTH), lambda i: (0, 0))],
        out_shape=[jax.ShapeDtypeStruct((db, D_MODEL), F32), jax.ShapeDtypeStruct((db, MEM_WIDTH), BF16)],
        compiler_params=_params("arbitrary"),
        name="mid_sample_a",
    )(*args)


def _mem_sample_kernel(q_ref, mk_ref, mv_ref, o_ref):
    rows = 2 * MEM_HEADS
    n = mk_ref.shape[1]
    q_all = q_ref[0].astype(F32)
    q = jnp.concatenate([q_all[:, (r % MEM_HEADS) * MEM_HEAD_DIM:(r % MEM_HEADS + 1) * MEM_HEAD_DIM]
                         for r in range(rows)], axis=0).astype(BF16)
    logits = _dot_nt(q, mk_ref[0].astype(BF16)) * (MEM_HEAD_DIM ** -0.5)
    own = (lax.broadcasted_iota(jnp.int32, (rows, n), 1) % MEM_HEADS
           == lax.broadcasted_iota(jnp.int32, (rows, n), 0) % MEM_HEADS)
    logits = jnp.where(own, logits, -jnp.inf)
    m = jnp.max(logits, axis=1, keepdims=True)
    p = jnp.exp(logits - m)
    p = p / jnp.sum(p, axis=1, keepdims=True)
    o = _dot(p.astype(BF16), mv_ref[0].astype(BF16))
    for h in range(MEM_HEADS):
        o_ref[0, :, h * MEM_HEAD_DIM:(h + 1) * MEM_HEAD_DIM] = o[h:h + 1, :]


def _mem_sample(qm, mk, mv):
    db = qm.shape[0]
    per = lambda b: (b, 0, 0)
    return pl.pallas_call(
        _mem_sample_kernel,
        grid=(db,),
        in_specs=[pl.BlockSpec((1, 1, MEM_WIDTH), per),
                  pl.BlockSpec((1, N_MEM * MEM_HEADS, MEM_HEAD_DIM), per),
                  pl.BlockSpec((1, N_MEM * MEM_HEADS, MEM_HEAD_DIM), per)],
        out_specs=pl.BlockSpec((1, 1, MEM_WIDTH), per),
        out_shape=jax.ShapeDtypeStruct((db, 1, MEM_WIDTH), F32),
        compiler_params=_params("parallel"),
        name="mem_sample",
    )(qm.reshape(db, 1, MEM_WIDTH), mk, mv)


def _sample_join_kernel(h1_ref, o_ref, wmo_ref, gf_ref, h2_any, tok_any, h2_ref, tok_ref):
    del h2_any, tok_any
    db = h1_ref.shape[0]
    h2_ref[...] = jnp.zeros_like(h2_ref)
    tok_ref[...] = jnp.zeros_like(tok_ref)

    @pl.when(pl.program_id(0) == 0)
    def _():
        h2 = h1_ref[...] + _dot(o_ref[...].astype(BF16), wmo_ref[...])
        h2_ref[0:db, :] = h2
        tok_ref[0:db, :] = _rms(h2, gf_ref[...]).astype(BF16)


def _sample_join(h1, o, w_mo, g_ffn, h2_all, tok_all, n_prompt):
    db = h1.shape[0]
    n_rows = h2_all.shape[0]
    assert db <= SAMPLE_ROWS and n_prompt % SAMPLE_ROWS == 0 and n_rows % SAMPLE_ROWS == 0
    first = n_prompt // SAMPLE_ROWS
    f2 = lambda j: (0, 0)
    tail = lambda j: (first + j, 0)
    return pl.pallas_call(
        _sample_join_kernel,
        grid=((n_rows - n_prompt) // SAMPLE_ROWS,),
        in_specs=[pl.BlockSpec((db, D_MODEL), f2),
                  pl.BlockSpec((db, MEM_WIDTH), f2),
                  pl.BlockSpec((MEM_WIDTH, D_MODEL), f2),
                  pl.BlockSpec((1, D_MODEL), f2),
                  pl.BlockSpec(memory_space=pl.ANY),
                  pl.BlockSpec(memory_space=pl.ANY)],
        out_specs=[pl.BlockSpec((SAMPLE_ROWS, D_MODEL), tail), pl.BlockSpec((SAMPLE_ROWS, D_MODEL), tail)],
        out_shape=[jax.ShapeDtypeStruct(h2_all.shape, F32), jax.ShapeDtypeStruct(tok_all.shape, BF16)],
        input_output_aliases={4: 0, 5: 1},
        compiler_params=_params("arbitrary"),
        name="sample_join",
    )(h1, o, w_mo, g_ffn, h2_all, tok_all)


def _rope_tables(pos):
    half = HEAD_DIM // 2
    inv = ROPE_THETA ** (-jnp.arange(half, dtype=F32) / half)
    ang = pos.astype(F32)[:, None] * inv[None, :]
    cos = jnp.cos(ang)
    sin = jnp.sin(ang)
    return jnp.tile(cos, (1, 4)), jnp.concatenate([-sin, sin, -sin, sin], axis=1)


def _pick_tile(n, pref):
    t = min(n, pref)
    while n % t:
        t //= 2
    return t


def kernel(x_prompt, x_sample, mem_prompt, cache_k, cache_v, cache_idx_k, state_pool, cache_mem_k, cache_mem_v, page_table, norm_mix, w_in, pool_w, pool_scale, w_o, norm_mem, w_mq, w_mk, w_mv, w_mo, norm_ffn, router_w, router_b, w_up, b_up, w_down, b_down, norm_final):
    B, T = x_prompt.shape[:2]
    DB, DS = x_sample.shape[:2]
    assert DS == 1 and w_in.shape[0] == 1
    n_pages = page_table.shape[1]
    past = n_pages * PAGE_SIZE
    NP = B * T

    wi_ = w_in[0]
    wq, wk, wv, wu_, wqi, wki, wwi = jnp.split(wi_, [512, 576, 640, 1152, 1664, 1728], axis=1)
    w_all = jnp.concatenate(
        [wq, wqi, wu_, wk, wk, wki, wki, wv, wv, wwi, jnp.zeros((D_MODEL, LANES - IDX_HEADS), F32)],
        axis=1).astype(BF16)
    g_mix = norm_mix[0][None, :]
    g_mem = norm_mem[0][None, :]
    g_ffn = norm_ffn[0][None, :]
    g_fin = norm_final[None, :]
    pw = pool_w[0].astype(BF16)
    ps = pool_scale[0][None, :]
    wo = w_o[0].astype(BF16)
    wmq = w_mq[0].reshape(D_MODEL, MEM_WIDTH).astype(BF16)
    wkv = jnp.concatenate([w_mk[0].reshape(D_MODEL, MEM_WIDTH), w_mv[0].reshape(D_MODEL, MEM_WIDTH)],
                          axis=1).astype(BF16)
    wmo = w_mo[0].reshape(MEM_WIDTH, D_MODEL).astype(BF16)
    rw = router_w[0].astype(BF16)
    rb = router_b[0]
    wup = w_up[0].astype(BF16)
    wdn = w_down[0].astype(BF16)
    bup = b_up[0][:, None, :]
    bdn = b_down[0][:, None, :]

    xp = x_prompt.reshape(NP, D_MODEL)
    tm_in = _pick_tile(T, 1024)
    cos_p, sin_p = _rope_tables(jnp.arange(T, dtype=jnp.int32))
    q, qi, u, kk, kiki, vvt, k_pt, ki_pt, v_pt, wi_p = _in_project(xp, g_mix, w_all, cos_p, sin_p, tm_in,
                                                                   T // tm_in, batch=B)
    attn = _dsa_prompt(q, qi, wi_p, kk, kiki, vvt, B, T)
    mk_p, mv_p = _mem_kv(mem_prompt.reshape(B * N_MEM, D_MODEL), wkv)
    tm_moe = min(MOE_TILE, NP)
    n_tiles = -(-(NP + SAMPLE_ROWS) // tm_moe)
    assert (n_tiles - 1) * tm_moe < NP and NP + SAMPLE_ROWS <= n_tiles * tm_moe
    h2_all, tok_all = _mid_prompt(xp, attn, u, mk_p.reshape(B, N_MEM, MEM_WIDTH), mv_p.reshape(B, N_MEM, MEM_WIDTH),
                                  pw, ps, wo, g_mem, wmq, wmo, g_ffn, B, T, _pick_tile(T, 1024), n_tiles * tm_moe)

    xs = x_sample.reshape(DB, D_MODEL)
    cos_s, sin_s = _rope_tables(jnp.full((DB,), past, jnp.int32))
    q_s, qi_s, u_s, kk_s, kiki_s, vv_s, k_s, ki_s, v_s, wi_s = _in_project(xs, g_mix, w_all, cos_s, sin_s, DB, 1)
    mask, take_new = _dsa_sample_mask(
        page_table, qi_s.reshape(DB, IDX_HEADS, IDX_DIM), wi_s[:, :IDX_HEADS, None],
        kiki_s[:, None, :IDX_DIM], jnp.swapaxes(cache_idx_k, 2, 3))
    attn_s = _dsa_sample_attend(
        page_table, q_s.reshape(DB, ATTN_HEADS, HEAD_DIM), kk_s[:, None, :HEAD_DIM], vv_s[:, None, :HEAD_DIM],
        mask, take_new, jnp.swapaxes(cache_k, 2, 3), jnp.swapaxes(cache_v, 2, 3))
    attn_s = attn_s.reshape(DB, ATTN_WIDTH).astype(BF16)
    h1_s, qm_s = _mid_sample_a(xs, attn_s, u_s, jnp.transpose(state_pool[0], (1, 0, 2)), pw, ps, wo, g_mem, wmq, past)
    o_s = _mem_sample(qm_s, cache_mem_k.reshape(DB, N_MEM * MEM_HEADS, MEM_HEAD_DIM),
                      cache_mem_v.reshape(DB, N_MEM * MEM_HEADS, MEM_HEAD_DIM))
    h2_all, tok_all = _sample_join(h1_s, o_s.reshape(DB, MEM_WIDTH), wmo, g_ffn, h2_all, tok_all, NP)

    rank, gate, cnt = _router(tok_all, rw.T, rb[:, None], tm_moe, NP + DB)
    y_p, y_s = _moe(tok_all, h2_all, rank, gate, cnt.reshape(-1), wup, bup, wdn, bdn, g_fin, tm_moe, NP)
    y_prompt = y_p.reshape(B, T, D_MODEL)
    y_sample = y_s[:DB].reshape(DB, 1, D_MODEL)

    new_pool_p = u.reshape(B, T, POOL_WIDTH)[:, T - (POOL_MAX - 1):][None]
    new_pool_s = jnp.concatenate([state_pool[0][:, 1:], u_s[:, None, :]], axis=1)[None]
    return (y_prompt, y_sample,
            jnp.swapaxes(k_pt, 1, 2)[None], jnp.swapaxes(v_pt, 1, 2)[None], jnp.swapaxes(ki_pt, 1, 2)[None],
            new_pool_p,
            mk_p.reshape(1, B, N_MEM, MEM_HEADS, MEM_HEAD_DIM), mv_p.reshape(1, B, N_MEM, MEM_HEADS, MEM_HEAD_DIM),
            k_s.reshape(1, DB, 1, HEAD_DIM), v_s.reshape(1, DB, 1, HEAD_DIM), ki_s.reshape(1, DB, 1, IDX_DIM),
            new_pool_s)
```
